```python
import math
import jax, jax.numpy as jnp
from jax import lax
import numpy as np

D_MODEL = 2048
BATCH = 2
SEQ = 4096
DEPTH = 2
DEC_BATCH = 32
DEC_SEQ = 4
PAST_LEN = 16384
PAGE_SIZE = 128

GROUP_W = D_MODEL // 4
D_MIX = 4 * GROUP_W
CONV_W = 3
CONV_GROUPS = 4
GLA_HEADS = 4
GLA_DK = GROUP_W // 2 // GLA_HEADS
GLA_DV = GROUP_W // GLA_HEADS
GLA_RANK = 16
GLA_TAU = 16.0
GLA_CHUNK = 64
SWA_HEADS = 8
SWA_KV_HEADS = 2
SWA_HD = GROUP_W // SWA_HEADS
WINDOW = 128
SWA_BLOCK = WINDOW
N_MEM = 256
MEM_HEADS = 4
MEM_HD = GROUP_W // MEM_HEADS
EPS = 1e-6

IN_SPLITS = (GROUP_W, GROUP_W, GROUP_W, GROUP_W,
             GLA_HEADS * GLA_DK, GLA_HEADS * GLA_DK, GLA_HEADS * GLA_DV, GLA_RANK, GROUP_W,
             SWA_HEADS * SWA_HD, SWA_KV_HEADS * SWA_HD, SWA_KV_HEADS * SWA_HD, GROUP_W,
             MEM_HEADS * MEM_HD, GROUP_W)
IN_OFFSETS = tuple(sum(IN_SPLITS[:i + 1]) for i in range(len(IN_SPLITS) - 1))
D_IN = sum(IN_SPLITS)

kernel_name = "hymba_conv_gla_swa_memory_step"


def rmsnorm(x, g):
    xf = x.astype(jnp.float32)
    y = xf * lax.rsqrt(jnp.mean(xf * xf, axis=-1, keepdims=True) + EPS)
    return (y * g.astype(jnp.float32)).astype(x.dtype)


def short_conv(u, prev, w):
    L = u.shape[1]
    ext = jnp.concatenate([prev.astype(u.dtype), u], axis=1)
    out = w[0] * ext[:, 0:L]
    for j in range(1, CONV_W):
        out = out + w[j] * ext[:, j:j + L]
    return out, ext[:, -(CONV_W - 1):]


def gla_chunked(q, k, v, log_a, s0, chunk):
    b, L, H, _ = q.shape
    n = L // chunk

    def to_chunks(t):
        return t.reshape(b, n, chunk, H, t.shape[-1]).transpose(1, 0, 3, 2, 4)

    qc, kc, vc, ac = to_chunks(q), to_chunks(k), to_chunks(v), to_chunks(log_a)
    mask = jnp.tril(jnp.ones((chunk, chunk), dtype=bool))

    def step(S, inp):
        qi, ki, vi, ai = inp
        cum = jnp.cumsum(ai, axis=2)
        total = cum[:, :, -1:]
        q_dec = qi * jnp.exp(cum)
        k_dec = ki * jnp.exp(-cum)
        attn = jnp.where(mask, jnp.einsum('bhtd,bhsd->bhts', q_dec, k_dec), 0.0)
        o = jnp.einsum('bhts,bhsv->bhtv', attn, vi) + jnp.einsum('bhtd,bhdv->bhtv', q_dec, S)
        k_tail = ki * jnp.exp(total - cum)
        S_new = jnp.exp(total)[:, :, 0, :, None] * S + jnp.einsum('bhsd,bhsv->bhdv', k_tail, vi)
        return S_new, o

    S, o = lax.scan(step, s0, (qc, kc, vc, ac))
    o = o.transpose(1, 0, 3, 2, 4).reshape(b, L, H, -1)
    return o, S


def sink_softmax(s, valid, sinks):
    sink = sinks.astype(jnp.float32).reshape(SWA_KV_HEADS, -1, 1, 1)
    s = jnp.where(valid, s, -jnp.inf)
    m = jnp.maximum(jnp.max(s, axis=-1, keepdims=True), sink)
    e = jnp.exp(s - m)
    return e / (jnp.sum(e, axis=-1, keepdims=True) + jnp.exp(sink - m))


def swa_banded(q, k, v, sinks):
    b, L, H, hd = q.shape
    n = L // SWA_BLOCK
    G = H // SWA_KV_HEADS
    qb = q.reshape(b, n, SWA_BLOCK, SWA_KV_HEADS, G, hd)

    def band(t):
        prev = jnp.concatenate([jnp.zeros_like(t[:, :SWA_BLOCK]), t[:, :-SWA_BLOCK]], axis=1)
        return jnp.concatenate([prev.reshape(b, n, SWA_BLOCK, SWA_KV_HEADS, hd),
                                t.reshape(b, n, SWA_BLOCK, SWA_KV_HEADS, hd)], axis=2)

    kb, vb = band(k), band(v)
    s = jnp.einsum('bnqkgd,bnskd->bnkgqs', qb, kb).astype(jnp.float32) * (hd ** -0.5)
    blk = jnp.arange(n)[:, None, None]
    qi = jnp.arange(SWA_BLOCK)[None, :, None]
    kj = jnp.arange(2 * SWA_BLOCK)[None, None, :]
    dist = qi + SWA_BLOCK - kj
    valid = (dist >= 0) & (dist < WINDOW) & (blk * SWA_BLOCK + kj - SWA_BLOCK >= 0)
    p = sink_softmax(s, valid[:, None, None], sinks)
    o = jnp.einsum('bnkgqs,bnskd->bnqkgd', p.astype(v.dtype), vb)
    return o.reshape(b, L, H, hd)


def swa_decode(q, k_all, v_all, sinks):
    b, T, H, hd = q.shape
    G = H // SWA_KV_HEADS
    S = k_all.shape[1]
    qg = q.reshape(b, T, SWA_KV_HEADS, G, hd)
    s = jnp.einsum('btkgd,bskd->bkgts', qg, k_all).astype(jnp.float32) * (hd ** -0.5)
    dist = jnp.arange(T)[:, None] + (S - T) - jnp.arange(S)[None, :]
    valid = (dist >= 0) & (dist < WINDOW)
    p = sink_softmax(s, valid, sinks)
    o = jnp.einsum('bkgts,bskd->btkgd', p.astype(v_all.dtype), v_all)
    return o.reshape(b, T, H, hd)


def memory_kv(mem, g_mem, w_mem_kv, g_mem_k):
    b, m, _ = mem.shape
    kv = rmsnorm(mem, g_mem) @ w_mem_kv
    k, v = jnp.split(kv, 2, axis=-1)
    k = rmsnorm(k.reshape(b, m, MEM_HEADS, MEM_HD), g_mem_k)
    return k, v.reshape(b, m, MEM_HEADS, MEM_HD)


def mixer_layer(x, mem_k, mem_v, conv_prev, gla_s0, swa_k_prev, swa_v_prev,
                g_norm, w_in, conv_w, w_gla_a_up, b_gla_a, g_gla_o, g_swa_q, g_swa_k,
                swa_sinks, g_mem_q, w_out):
    b, L, _ = x.shape
    f32 = jnp.float32
    hn = rmsnorm(x, g_norm)
    proj = hn @ w_in
    (a_b, a_c, a_h, a_z, g_q, g_k, g_v, g_a, g_z,
     s_q, s_k, s_v, s_z, m_q, m_z) = jnp.split(proj, IN_OFFSETS, axis=-1)

    if conv_prev is None:
        conv_prev = jnp.zeros((b, CONV_W - 1, GROUP_W), x.dtype)
    conv_out, conv_state = short_conv(a_c * a_h, conv_prev, conv_w)
    y_a = a_b * conv_out * jax.nn.silu(a_z)

    q = g_q.reshape(b, L, GLA_HEADS, GLA_DK).astype(f32) * (GLA_DK ** -0.5)
    k = g_k.reshape(b, L, GLA_HEADS, GLA_DK).astype(f32)
    v = g_v.reshape(b, L, GLA_HEADS, GLA_DV).astype(f32)
    log_a = jax.nn.log_sigmoid((g_a @ w_gla_a_up + b_gla_a).astype(f32)).reshape(b, L, GLA_HEADS, GLA_DK) / GLA_TAU
    if gla_s0 is None:
        s0 = jnp.zeros((b, GLA_HEADS, GLA_DK, GLA_DV), f32)
    else:
        s0 = gla_s0.astype(f32)
    chunk = GLA_CHUNK if L % GLA_CHUNK == 0 else L
    o_gla, gla_state = gla_chunked(q, k, v, log_a, s0, chunk)
    o_gla = rmsnorm(o_gla, g_gla_o).astype(x.dtype)
    y_b = o_gla.reshape(b, L, GROUP_W) * jax.nn.silu(g_z)

    q = rmsnorm(s_q.reshape(b, L, SWA_HEADS, SWA_HD), g_swa_q)
    k = rmsnorm(s_k.reshape(b, L, SWA_KV_HEADS, SWA_HD), g_swa_k)
    v = s_v.reshape(b, L, SWA_KV_HEADS, SWA_HD)
    if swa_k_prev is None:
        o_swa = swa_banded(q, k, v, swa_sinks)
        k_buf, v_buf = k[:, -WINDOW:], v[:, -WINDOW:]
    else:
        k_all = jnp.concatenate([swa_k_prev.astype(k.dtype), k], axis=1)
        v_all = jnp.concatenate([swa_v_prev.astype(v.dtype), v], axis=1)
        o_swa = swa_decode(q, k_all, v_all, swa_sinks)
        k_buf, v_buf = k_all[:, -WINDOW:], v_all[:, -WINDOW:]
    y_c = o_swa.reshape(b, L, GROUP_W) * jax.nn.silu(s_z)

    q = rmsnorm(m_q.reshape(b, L, MEM_HEADS, MEM_HD), g_mem_q)
    s = jnp.einsum('blhd,bmhd->bhlm', q, mem_k.astype(q.dtype)).astype(f32) * (MEM_HD ** -0.5)
    p = jax.nn.softmax(s, axis=-1).astype(x.dtype)
    o_mem = jnp.einsum('bhlm,bmhd->blhd', p, mem_v.astype(x.dtype))
    y_d = o_mem.reshape(b, L, GROUP_W) * jax.nn.silu(m_z)

    mix = jnp.concatenate([y_a, y_b, y_c, y_d], axis=-1)
    y = x + mix @ w_out
    return y, conv_state, gla_state.astype(x.dtype), k_buf, v_buf


def setup_inputs(seed: int = 0) -> dict:
    key = jax.random.key(seed)
    ks = jax.random.split(key, 32)

    def nrm(k, shape, scale=1.0):
        return jax.random.normal(k, shape, jnp.float32) * scale

    def gain(k, shape):
        return 1.0 + 0.02 * jax.random.normal(k, shape, jnp.float32)

    return {
        "x_prompt": nrm(ks[0], (BATCH, SEQ, D_MODEL)),
        "x_sample": nrm(ks[1], (DEC_BATCH, DEC_SEQ, D_MODEL)),
        "mem_prompt": nrm(ks[2], (BATCH, N_MEM, D_MODEL)),
        "state_conv": nrm(ks[3], (DEPTH, DEC_BATCH, CONV_W - 1, GROUP_W)),
        "state_gla": nrm(ks[4], (DEPTH, DEC_BATCH, GLA_HEADS, GLA_DK, GLA_DV), 0.3),
        "cache_swa_k": nrm(ks[5], (DEPTH, DEC_BATCH, WINDOW, SWA_KV_HEADS, SWA_HD)),
        "cache_swa_v": nrm(ks[6], (DEPTH, DEC_BATCH, WINDOW, SWA_KV_HEADS, SWA_HD)),
        "cache_mem_k": nrm(ks[7], (DEPTH, DEC_BATCH, N_MEM, MEM_HEADS, MEM_HD)),
        "cache_mem_v": nrm(ks[8], (DEPTH, DEC_BATCH, N_MEM, MEM_HEADS, MEM_HD)),
        "g_norm": gain(ks[9], (DEPTH, D_MODEL)),
        "w_in": nrm(ks[10], (DEPTH, D_MODEL, D_IN), D_MODEL ** -0.5),
        "conv_w": nrm(ks[11], (DEPTH, CONV_W, GROUP_W), CONV_W ** -0.5),
        "w_gla_a_up": nrm(ks[12], (DEPTH, GLA_RANK, GLA_HEADS * GLA_DK), GLA_RANK ** -0.5),
        "b_gla_a": nrm(ks[13], (DEPTH, GLA_HEADS * GLA_DK), 0.1),
        "g_gla_o": gain(ks[14], (DEPTH, GLA_DV)),
        "g_swa_q": gain(ks[15], (DEPTH, SWA_HD)),
        "g_swa_k": gain(ks[16], (DEPTH, SWA_HD)),
        "swa_sinks": nrm(ks[17], (DEPTH, SWA_HEADS), 0.5),
        "g_mem": gain(ks[18], (DEPTH, D_MODEL)),
        "w_mem_kv": nrm(ks[19], (DEPTH, D_MODEL, 2 * MEM_HEADS * MEM_HD), D_MODEL ** -0.5),
        "g_mem_q": gain(ks[20], (DEPTH, MEM_HD)),
        "g_mem_k": gain(ks[21], (DEPTH, MEM_HD)),
        "w_out": nrm(ks[22], (DEPTH, D_MIX, D_MODEL), D_MIX ** -0.5),
    }


def reference(x_prompt, x_sample, mem_prompt, state_conv, state_gla, cache_swa_k, cache_swa_v,
              cache_mem_k, cache_mem_v, g_norm, w_in, conv_w, w_gla_a_up, b_gla_a, g_gla_o,
              g_swa_q, g_swa_k, swa_sinks, g_mem, w_mem_kv, g_mem_q, g_mem_k, w_out):
    hp, hs = x_prompt, x_sample
    conv_p, gla_p, swk_p, swv_p, mk_p, mv_p = [], [], [], [], [], []
    conv_s, gla_s, swk_s, swv_s = [], [], [], []
    for l in range(DEPTH):
        lw = (g_norm[l], w_in[l], conv_w[l], w_gla_a_up[l], b_gla_a[l], g_gla_o[l],
              g_swa_q[l], g_swa_k[l], swa_sinks[l], g_mem_q[l], w_out[l])
        mk, mv = memory_kv(mem_prompt, g_mem[l], w_mem_kv[l], g_mem_k[l])
        hp, c, s, kb, vb = mixer_layer(hp, mk, mv, None, None, None, None, *lw)
        conv_p.append(c); gla_p.append(s); swk_p.append(kb); swv_p.append(vb)
        mk_p.append(mk); mv_p.append(mv)
        hs, c, s, kb, vb = mixer_layer(hs, cache_mem_k[l], cache_mem_v[l], state_conv[l], state_gla[l],
                                       cache_swa_k[l], cache_swa_v[l], *lw)
        conv_s.append(c); gla_s.append(s); swk_s.append(kb); swv_s.append(vb)
    return (hp, hs,
            jnp.stack(conv_p), jnp.stack(gla_p), jnp.stack(swk_p), jnp.stack(swv_p),
            jnp.stack(mk_p), jnp.stack(mv_p),
            jnp.stack(conv_s), jnp.stack(gla_s), jnp.stack(swk_s), jnp.stack(swv_s))
```

```python
import functools

import jax
import jax.numpy as jnp
from jax import lax
from jax.experimental import pallas as pl
from jax.experimental.pallas import tpu as pltpu

f32 = jnp.float32
bf16 = jnp.bfloat16

D_MODEL = 2048
GROUP_W = 512
GLA_HEADS = 4
GLA_DK = 64
GLA_DV = 128
GLA_RANK = 16
GLA_TAU = 16.0
GLA_CHUNK = 64
SWA_HEADS = 8
SWA_KV_HEADS = 2
SWA_HD = 64
SWA_GROUP = SWA_HEADS // SWA_KV_HEADS
WINDOW = 128
N_MEM = 256
MEM_HEADS = 4
MEM_HD = 128
CONV_W = 3
EPS = 1e-6

LANES = 128

D_IN = 5904
A_LOW_OFF = 3072
OFF_AB, OFF_AC, OFF_AH, OFF_AZ = 0, 512, 1024, 1536
OFF_GQ, OFF_GK, OFF_GV, OFF_GZ = 2048, 2304, 2560, 3072
OFF_SQ, OFF_SK, OFF_SV, OFF_SZ = 3584, 4096, 4224, 4352
OFF_MQ, OFF_MZ = 4864, 5376
OFF_GA = 5888
D_IN_PAD = 6144

VMEM_LIMIT = 52 * 1024 * 1024


def _dot(a, b):
    return jnp.dot(a, b, preferred_element_type=f32)


def _dot_nt(a, b):
    return lax.dot_general(a, b, (((1,), (1,)), ((), ())), preferred_element_type=f32)


def _dot_tn(a, b):
    return lax.dot_general(a, b, (((0,), (0,)), ((), ())), preferred_element_type=f32)


def _split3(x):
    hi = x.astype(bf16)
    r = x - hi.astype(f32)
    mid = r.astype(bf16)
    lo = (r - mid.astype(f32)).astype(bf16)
    return hi, mid, lo


def _silu(x):
    return x * jax.nn.sigmoid(x)


def _log_sigmoid(x):
    return jnp.minimum(x, 0.0) - jnp.log1p(jnp.exp(-jnp.abs(x)))


def _norm_matmul_kernel(x_ref, g_ref, w_ref, o_ref, hn_ref):
    @pl.when(pl.program_id(1) == 0)
    def _():
        x = x_ref[...]
        y = x * lax.rsqrt(jnp.mean(x * x, axis=-1, keepdims=True) + EPS)
        hn_ref[...] = (y * g_ref[...]).astype(bf16)

    o_ref[...] = _dot(hn_ref[...], w_ref[...])


def _norm_matmul(x, g, w, tm, tn):
    m, k = x.shape
    n = w.shape[1]
    return pl.pallas_call(
        _norm_matmul_kernel,
        grid=(m // tm, n // tn),
        in_specs=[
            pl.BlockSpec((tm, k), lambda i, j: (i, 0)),
            pl.BlockSpec((1, k), lambda i, j: (0, 0)),
            pl.BlockSpec((k, tn), lambda i, j: (0, j)),
        ],
        out_specs=pl.BlockSpec((tm, tn), lambda i, j: (i, j)),
        out_shape=jax.ShapeDtypeStruct((m, n), f32),
        scratch_shapes=[pltpu.VMEM((tm, k), bf16)],
        compiler_params=pltpu.CompilerParams(
            dimension_semantics=("arbitrary", "arbitrary"), vmem_limit_bytes=VMEM_LIMIT),
        name="norm_in_proj",
    )(x, g, w)


def _memory_kv_kernel(x_ref, g_ref, w_ref, gk_ref, k_ref, v_ref):
    x = x_ref[...]
    y = x * lax.rsqrt(jnp.mean(x * x, axis=-1, keepdims=True) + EPS)
    kv = _dot((y * g_ref[...]).astype(bf16), w_ref[...])
    for h in range(MEM_HEADS):
        kh = kv[:, h * MEM_HD:(h + 1) * MEM_HD]
        kh = kh * lax.rsqrt(jnp.mean(kh * kh, axis=-1, keepdims=True) + EPS)
        k_ref[:, h * MEM_HD:(h + 1) * MEM_HD] = kh * gk_ref[...]
    v_ref[...] = kv[:, GROUP_W:]


def _memory_kv(mem, g, w, gk):
    b = mem.shape[0]
    out = jax.ShapeDtypeStruct((b, N_MEM, GROUP_W), f32)
    return pl.pallas_call(
        _memory_kv_kernel,
        grid=(b,),
        in_specs=[
            pl.BlockSpec((None, N_MEM, D_MODEL), lambda i: (i, 0, 0)),
            pl.BlockSpec((1, D_MODEL), lambda i: (0, 0)),
            pl.BlockSpec((D_MODEL, 2 * GROUP_W), lambda i: (0, 0)),
            pl.BlockSpec((1, MEM_HD), lambda i: (0, 0)),
        ],
        out_specs=[pl.BlockSpec((None, N_MEM, GROUP_W), lambda i: (i, 0, 0))] * 2,
        out_shape=[out, out],
        compiler_params=pltpu.CompilerParams(
            dimension_semantics=("arbitrary",), vmem_limit_bytes=VMEM_LIMIT),
        name="memory_kv",
    )(mem, g, w, gk)


CONV_PAD = 8


def _mixer_kernel(*refs, tile, decode):
    if decode:
        (p_ref, mk_ref, mv_ref, conv_in_ref, gla_in_ref, kc_ref, vc_ref,
         convw_ref, wup_ref, bga_ref, ggo_ref, gsq_ref, gsk_ref, bd_ref, sinks_ref, gmq_ref,
         mix_ref, conv_out_ref, gla_out_ref, kbuf_ref, vbuf_ref,
         ext_ref, s_ref, kprev_ref, vprev_ref) = refs
    else:
        (p_ref, mk_ref, mv_ref,
         convw_ref, wup_ref, bga_ref, ggo_ref, gsq_ref, gsk_ref, bd_ref, sinks_ref, gmq_ref,
         mix_ref, conv_out_ref, gla_out_ref, kbuf_ref, vbuf_ref,
         ext_ref, s_ref, kprev_ref, vprev_ref) = refs

    T = tile
    t = pl.program_id(1)
    last = t == pl.num_programs(1) - 1

    @pl.when(t == 0)
    def _():
        ext_ref[0:CONV_PAD, :] = jnp.zeros((CONV_PAD, GROUP_W), f32)
        if decode:
            ext_ref[CONV_PAD - (CONV_W - 1):CONV_PAD, :] = conv_in_ref[...]
            s_ref[...] = gla_in_ref[...]
            kprev_ref[...] = kc_ref[...]
            vprev_ref[...] = vc_ref[...]
        else:
            s_ref[...] = jnp.zeros_like(s_ref)
            kprev_ref[...] = jnp.zeros_like(kprev_ref)
            vprev_ref[...] = jnp.zeros_like(vprev_ref)

    row = lax.broadcasted_iota(jnp.int32, (T, T), 0)
    col = lax.broadcasted_iota(jnp.int32, (T, T), 1)

    u = p_ref[:, OFF_AC:OFF_AC + GROUP_W] * p_ref[:, OFF_AH:OFF_AH + GROUP_W]
    ext_ref[CONV_PAD:CONV_PAD + T, :] = u
    conv = (convw_ref[0:1, :] * ext_ref[CONV_PAD - 2:CONV_PAD - 2 + T, :]
            + convw_ref[1:2, :] * ext_ref[CONV_PAD - 1:CONV_PAD - 1 + T, :]
            + convw_ref[2:3, :] * u)
    mix_ref[:, 0:GROUP_W] = (p_ref[:, OFF_AB:OFF_AB + GROUP_W] * conv
                             * _silu(p_ref[:, OFF_AZ:OFF_AZ + GROUP_W])).astype(mix_ref.dtype)
    conv_state = ext_ref[CONV_PAD + T - 2:CONV_PAD + T, :]
    ext_ref[CONV_PAD - 2:CONV_PAD, :] = conv_state

    @pl.when(last)
    def _():
        conv_out_ref[...] = conv_state

    C = min(GLA_CHUNK, T)
    n_chunk = T // C
    g_k = p_ref[:, OFF_GK:OFF_GK + GLA_HEADS * GLA_DK]
    g_a = p_ref[:, OFF_GA:OFF_GA + LANES].astype(bf16)
    log_a = _log_sigmoid(_dot(g_a, wup_ref[...]) + bga_ref[...]) * (1.0 / GLA_TAU)
    la3 = _split3(log_a)
    tril = jnp.where((row // C == col // C) & (row >= col), 1.0, 0.0).astype(bf16)
    in_chunk = jnp.where(lax.broadcasted_iota(jnp.int32, (T, LANES), 0) // C
                         == lax.broadcasted_iota(jnp.int32, (T, LANES), 1), 1.0, 0.0).astype(bf16)
    cum = _dot(tril, la3[0]) + _dot(tril, la3[1]) + _dot(tril, la3[2])
    tot_t = (_dot_tn(la3[0], in_chunk) + _dot_tn(la3[1], in_chunk)
             + _dot_tn(la3[2], in_chunk))
    decay_t = jnp.exp(tot_t)
    q_dec = (p_ref[:, OFF_GQ:OFF_GQ + GLA_HEADS * GLA_DK] * (GLA_DK ** -0.5)) * jnp.exp(cum)
    k_dec = g_k * jnp.exp(-cum)
    k_tail = jnp.concatenate(
        [g_k[c * C:(c + 1) * C] * jnp.exp(cum[(c + 1) * C - 1:(c + 1) * C] - cum[c * C:(c + 1) * C])
         for c in range(n_chunk)], axis=0) if n_chunk > 1 else g_k * jnp.exp(cum[T - 1:T] - cum)
    causal = (row // C == col // C) & (row >= col)
    for h in range(GLA_HEADS):
        ks = slice(h * GLA_DK, (h + 1) * GLA_DK)
        vs = slice(OFF_GV + h * GLA_DV, OFF_GV + (h + 1) * GLA_DV)
        qd = q_dec[:, ks].astype(bf16)
        kd = k_dec[:, ks].astype(bf16)
        kt = k_tail[:, ks].astype(bf16)
        v_h = p_ref[:, vs].astype(bf16)
        attn = jnp.where(causal, _dot_nt(qd, kd), 0.0).astype(bf16)
        o_intra = _dot(attn, v_h)
        s_h = s_ref[h]
        outs = []
        for c in range(n_chunk):
            rs = slice(c * C, (c + 1) * C)
            outs.append(o_intra[rs] + _dot(qd[rs], s_h.astype(bf16)))
            s_h = decay_t[ks, c:c + 1] * s_h + _dot_tn(kt[rs], v_h[rs])
        s_ref[h] = s_h
        o_h = jnp.concatenate(outs, axis=0) if n_chunk > 1 else outs[0]
        o_h = o_h * lax.rsqrt(jnp.mean(o_h * o_h, axis=-1, keepdims=True) + EPS) * ggo_ref[...]
        zs = slice(OFF_GZ + h * GLA_DV, OFF_GZ + (h + 1) * GLA_DV)
        mix_ref[:, GROUP_W + h * GLA_DV:GROUP_W + (h + 1) * GLA_DV] = (
            o_h * _silu(p_ref[:, zs])).astype(mix_ref.dtype)

    @pl.when(last)
    def _():
        gla_out_ref[...] = s_ref[...]

    def head_norm(x, g, n_lanes):
        sq = x * x
        hi = sq.astype(bf16)
        lo = (sq - hi.astype(f32)).astype(bf16)
        bd = bd_ref[0:n_lanes, 0:n_lanes]
        ms = (_dot(hi, bd) + _dot(lo, bd)) * (1.0 / SWA_HD)
        return x * lax.rsqrt(ms + EPS) * g

    q_n = head_norm(p_ref[:, OFF_SQ:OFF_SQ + GROUP_W], gsq_ref[...], GROUP_W)
    k_n = head_norm(p_ref[:, OFF_SK:OFF_SK + LANES], gsk_ref[...], LANES)
    v_n = p_ref[:, OFF_SV:OFF_SV + LANES]

    BQ = min(WINDOW, T)
    n_blk = T // BQ
    stack = SWA_GROUP if BQ % 8 == 0 else 1
    nk = WINDOW + BQ
    qi = lax.broadcasted_iota(jnp.int32, (stack * BQ, nk), 0) % BQ
    kj = lax.broadcasted_iota(jnp.int32, (stack * BQ, nk), 1)
    dist = qi + WINDOW - kj
    band = (dist >= 0) & (dist < WINDOW)
    srow = lax.broadcasted_iota(jnp.int32, (stack * BQ, 1), 0) // BQ
    for blk in range(n_blk):
        rs = slice(blk * BQ, (blk + 1) * BQ)
        if blk == 0:
            k_prev, v_prev = kprev_ref[...], vprev_ref[...]
            valid = band if decode else band & ((kj >= WINDOW) | (t > 0))
        else:
            ps = slice((blk - 1) * BQ, blk * BQ)
            k_prev, v_prev = k_n[ps], v_n[ps]
            valid = band
        k_cat = jnp.concatenate([k_prev, k_n[rs]], axis=0)
        v_cat = jnp.concatenate([v_prev, v_n[rs]], axis=0)
        for g in range(SWA_KV_HEADS):
            kg = k_cat[:, g * SWA_HD:(g + 1) * SWA_HD].astype(bf16)
            vg = v_cat[:, g * SWA_HD:(g + 1) * SWA_HD].astype(bf16)
            for j0 in range(0, SWA_GROUP, stack):
                heads = [g * SWA_GROUP + j0 + j for j in range(stack)]
                qs = [q_n[rs, hd * SWA_HD:(hd + 1) * SWA_HD] for hd in heads]
                qg = (jnp.concatenate(qs, axis=0) if stack > 1 else qs[0]).astype(bf16)
                sink = jnp.full((stack * BQ, 1), sinks_ref[heads[0]], f32)
                for j in range(1, stack):
                    sink = jnp.where(srow == j, sinks_ref[heads[j]], sink)
                s = _dot_nt(qg, kg) * (SWA_HD ** -0.5)
                s = jnp.where(valid, s, -jnp.inf)
                m = jnp.maximum(jnp.max(s, axis=-1, keepdims=True), sink)
                e = jnp.exp(s - m)
                prob = e / (jnp.sum(e, axis=-1, keepdims=True) + jnp.exp(sink - m))
                o = _dot(prob.astype(bf16), vg)
                for j, hd in enumerate(heads):
                    zs = slice(OFF_SZ + hd * SWA_HD, OFF_SZ + (hd + 1) * SWA_HD)
                    mix_ref[rs, 2 * GROUP_W + hd * SWA_HD:2 * GROUP_W + (hd + 1) * SWA_HD] = (
                        o[j * BQ:(j + 1) * BQ] * _silu(p_ref[rs, zs])).astype(mix_ref.dtype)

    if decode:
        kbuf_ref[0:WINDOW - T, :] = kc_ref[T:WINDOW, :]
        kbuf_ref[WINDOW - T:WINDOW, :] = k_n
        vbuf_ref[0:WINDOW - T, :] = vc_ref[T:WINDOW, :]
        vbuf_ref[WINDOW - T:WINDOW, :] = v_n
    else:
        kprev_ref[...] = k_n[T - WINDOW:T]
        vprev_ref[...] = v_n[T - WINDOW:T]

        @pl.when(last)
        def _():
            kbuf_ref[...] = k_n[T - WINDOW:T]
            vbuf_ref[...] = v_n[T - WINDOW:T]

    for h in range(MEM_HEADS):
        hs = slice(h * MEM_HD, (h + 1) * MEM_HD)
        qh = p_ref[:, OFF_MQ + h * MEM_HD:OFF_MQ + (h + 1) * MEM_HD]
        qh = qh * lax.rsqrt(jnp.mean(qh * qh, axis=-1, keepdims=True) + EPS) * gmq_ref[...]
        s = _dot_nt(qh.astype(bf16), mk_ref[:, hs].astype(bf16)) * (MEM_HD ** -0.5)
        e = jnp.exp(s - jnp.max(s, axis=-1, keepdims=True))
        prob = e / jnp.sum(e, axis=-1, keepdims=True)
        o = _dot(prob.astype(bf16), mv_ref[:, hs].astype(bf16))
        zs = slice(OFF_MZ + h * MEM_HD, OFF_MZ + (h + 1) * MEM_HD)
        mix_ref[:, 3 * GROUP_W + h * MEM_HD:3 * GROUP_W + (h + 1) * MEM_HD] = (
            o * _silu(p_ref[:, zs])).astype(mix_ref.dtype)


def _mixer(proj, mem_k, mem_v, state, lw, tile, decode):
    b, L, _ = proj.shape
    nt = L // tile
    conv_w, w_up, b_ga, g_go, g_sq, g_sk, bd, sinks, g_mq = lw

    def tok(width):
        return pl.BlockSpec((None, tile, width), lambda i, t: (i, t, 0))

    def per_seq(*shape):
        return pl.BlockSpec((None,) + shape, lambda i, t: (i,) + (0,) * len(shape))

    def const(a):
        return pl.BlockSpec(a.shape, lambda i, t: (0,) * a.ndim)

    state_shapes = [(CONV_W - 1, GROUP_W), (GLA_HEADS, GLA_DK, GLA_DV),
                    (WINDOW, SWA_KV_HEADS * SWA_HD), (WINDOW, SWA_KV_HEADS * SWA_HD)]
    in_specs = [tok(D_IN_PAD), per_seq(N_MEM, GROUP_W), per_seq(N_MEM, GROUP_W)]
    args = [proj, mem_k, mem_v]
    if decode:
        in_specs += [per_seq(*s) for s in state_shapes]
        args += list(state)
    in_specs += [const(conv_w), const(w_up), const(b_ga), const(g_go), const(g_sq), const(g_sk),
                 const(bd), pl.BlockSpec(memory_space=pltpu.SMEM), const(g_mq)]
    args += [conv_w, w_up, b_ga, g_go, g_sq, g_sk, bd, sinks, g_mq]
    mix_dtype = f32 if decode else bf16
    out_shape = [jax.ShapeDtypeStruct((b, L, 4 * GROUP_W), mix_dtype)]
    out_shape += [jax.ShapeDtypeStruct((b,) + s, f32) for s in state_shapes]
    out_specs = [tok(4 * GROUP_W)] + [per_seq(*s) for s in state_shapes]
    return pl.pallas_call(
        functools.partial(_mixer_kernel, tile=tile, decode=decode),
        grid=(b, nt),
        in_specs=in_specs,
        out_specs=out_specs,
        out_shape=out_shape,
        scratch_shapes=[
            pltpu.VMEM((CONV_PAD + tile, GROUP_W), f32),
            pltpu.VMEM((GLA_HEADS, GLA_DK, GLA_DV), f32),
            pltpu.VMEM((WINDOW, SWA_KV_HEADS * SWA_HD), f32),
            pltpu.VMEM((WINDOW, SWA_KV_HEADS * SWA_HD), f32),
        ],
        compiler_params=pltpu.CompilerParams(
            dimension_semantics=("arbitrary", "arbitrary"), vmem_limit_bytes=VMEM_LIMIT),
        name="mixer_decode" if decode else "mixer_prompt",
    )(*args)


def _out_proj_kernel(mix_ref, w_ref, x_ref, y_ref):
    y_ref[...] = x_ref[...] + _dot(mix_ref[...].astype(bf16), w_ref[...])


def _out_proj(mix, w, x, tm):
    m, k = mix.shape
    n = w.shape[1]
    return pl.pallas_call(
        _out_proj_kernel,
        grid=(m // tm,),
        in_specs=[
            pl.BlockSpec((tm, k), lambda i: (i, 0)),
            pl.BlockSpec((k, n), lambda i: (0, 0)),
            pl.BlockSpec((tm, n), lambda i: (i, 0)),
        ],
        out_specs=pl.BlockSpec((tm, n), lambda i: (i, 0)),
        out_shape=jax.ShapeDtypeStruct((m, n), f32),
        compiler_params=pltpu.CompilerParams(
            dimension_semantics=("arbitrary",), vmem_limit_bytes=VMEM_LIMIT),
        name="out_proj",
    )(mix, w, x)


PROMPT_TILE = 256
PROJ_TM, PROJ_TN = 1024, 768
OUT_TM = 512


def _layer_weights(l, w_in, conv_w, w_gla_a_up, b_gla_a, g_gla_o, g_swa_q, g_swa_k, swa_sinks,
                   g_mem_q):
    w = w_in[l]
    w_r = jnp.concatenate(
        [w[:, :A_LOW_OFF], w[:, A_LOW_OFF + GLA_RANK:], w[:, A_LOW_OFF:A_LOW_OFF + GLA_RANK],
         jnp.zeros((D_MODEL, D_IN_PAD - D_IN), w.dtype)], axis=1).astype(bf16)
    w_up = jnp.concatenate(
        [w_gla_a_up[l], jnp.zeros((LANES - GLA_RANK, GLA_HEADS * GLA_DK), f32)], axis=0).astype(bf16)
    lane = jnp.arange(GROUP_W)
    bd = (lane[:, None] // SWA_HD == lane[None, :] // SWA_HD).astype(bf16)
    mixer_w = (conv_w[l], w_up, b_gla_a[l][None, :], g_gla_o[l][None, :],
               jnp.tile(g_swa_q[l], SWA_HEADS)[None, :], jnp.tile(g_swa_k[l], SWA_KV_HEADS)[None, :],
               bd, swa_sinks[l], g_mem_q[l][None, :])
    return w_r, mixer_w


def kernel(x_prompt, x_sample, mem_prompt, state_conv, state_gla, cache_swa_k, cache_swa_v,
           cache_mem_k, cache_mem_v, g_norm, w_in, conv_w, w_gla_a_up, b_gla_a, g_gla_o,
           g_swa_q, g_swa_k, swa_sinks, g_mem, w_mem_kv, g_mem_q, g_mem_k, w_out):
    depth = w_in.shape[0]
    bp, lp, _ = x_prompt.shape
    bs, ls, _ = x_sample.shape
    hp = x_prompt.reshape(bp * lp, D_MODEL)
    hs = x_sample.reshape(bs * ls, D_MODEL)
    outs = [[] for _ in range(10)]
    for l in range(depth):
        w_r, mixer_w = _layer_weights(l, w_in, conv_w, w_gla_a_up, b_gla_a, g_gla_o, g_swa_q,
                                      g_swa_k, swa_sinks, g_mem_q)
        w_o = w_out[l].astype(bf16)
        g_n = g_norm[l][None, :]

        mk, mv = _memory_kv(mem_prompt, g_mem[l][None, :], w_mem_kv[l].astype(bf16),
                            g_mem_k[l][None, :])
        proj = _norm_matmul(hp, g_n, w_r, PROJ_TM, PROJ_TN).reshape(bp, lp, D_IN_PAD)
        mix, c, s, kb, vb = _mixer(proj, mk, mv, None, mixer_w, PROMPT_TILE, decode=False)
        hp = _out_proj(mix.reshape(bp * lp, 4 * GROUP_W), w_o, hp, OUT_TM)
        for lst, a in zip(outs[:6], (
                c, s, kb.reshape(bp, WINDOW, SWA_KV_HEADS, SWA_HD),
                vb.reshape(bp, WINDOW, SWA_KV_HEADS, SWA_HD),
                mk.reshape(bp, N_MEM, MEM_HEADS, MEM_HD), mv.reshape(bp, N_MEM, MEM_HEADS, MEM_HD))):
            lst.append(a)

        proj = _norm_matmul(hs, g_n, w_r, bs * ls, PROJ_TN).reshape(bs, ls, D_IN_PAD)
        state = (state_conv[l], state_gla[l],
                 cache_swa_k[l].reshape(bs, WINDOW, SWA_KV_HEADS * SWA_HD),
                 cache_swa_v[l].reshape(bs, WINDOW, SWA_KV_HEADS * SWA_HD))
        mix, c, s, kb, vb = _mixer(proj, cache_mem_k[l].reshape(bs, N_MEM, GROUP_W),
                                   cache_mem_v[l].reshape(bs, N_MEM, GROUP_W), state, mixer_w,
                                   ls, decode=True)
        hs = _out_proj(mix.reshape(bs * ls, 4 * GROUP_W), w_o, hs, bs * ls)
        for lst, a in zip(outs[6:], (
                c, s, kb.reshape(bs, WINDOW, SWA_KV_HEADS, SWA_HD),
                vb.reshape(bs, WINDOW, SWA_KV_HEADS, SWA_HD))):
            lst.append(a)

    return (hp.reshape(bp, lp, D_MODEL), hs.reshape(bs, ls, D_MODEL),
            *[jnp.stack(o) for o in outs])
```

```python
import functools

import jax
import jax.numpy as jnp
import numpy as np
from jax import lax
from jax.experimental import pallas as pl
from jax.experimental.pallas import tpu as pltpu

f32 = jnp.float32
bf16 = jnp.bfloat16

D_MODEL = 2048
GROUP_W = 512
GLA_HEADS = 4
GLA_DK = 64
GLA_DV = 128
GLA_RANK = 16
GLA_TAU = 16.0
GLA_CHUNK = 64
SWA_HEADS = 8
SWA_KV_HEADS = 2
SWA_HD = 64
SWA_GROUP = SWA_HEADS // SWA_KV_HEADS
WINDOW = 128
N_MEM = 256
MEM_HEADS = 4
MEM_HD = 128
CONV_W = 3
EPS = 1e-6

LANES = 128

D_IN = 5904
OFF_AB, OFF_AC, OFF_AH, OFF_AZ = 0, 512, 1024, 1536
OFF_GQ, OFF_GK, OFF_GV, OFF_GA, OFF_GZ = 2048, 2304, 2560, 3072, 3088
OFF_SQ, OFF_SK, OFF_SV, OFF_SZ = 3600, 4112, 4240, 4368
OFF_MQ, OFF_MZ = 4880, 5392

VMEM_LIMIT = 52 * 1024 * 1024


def _dot(a, b):
    return jnp.dot(a, b, preferred_element_type=f32)


def _dot_nt(a, b):
    return lax.dot_general(a, b, (((1,), (1,)), ((), ())), preferred_element_type=f32)


def _dot_tn(a, b):
    return lax.dot_general(a, b, (((0,), (0,)), ((), ())), preferred_element_type=f32)


def _split3(x):
    hi = x.astype(bf16)
    r = x - hi.astype(f32)
    mid = r.astype(bf16)
    lo = (r - mid.astype(f32)).astype(bf16)
    return hi, mid, lo


def _silu(x):
    return x * jax.nn.sigmoid(x)


def _log_sigmoid(x):
    return jnp.minimum(x, 0.0) - jnp.log1p(jnp.exp(-jnp.abs(x)))


def _norm_matmul_kernel(x_ref, g_ref, w_ref, o_ref, hn_ref):
    @pl.when(pl.program_id(1) == 0)
    def _():
        x = x_ref[...]
        y = x * lax.rsqrt(jnp.mean(x * x, axis=-1, keepdims=True) + EPS)
        hn_ref[...] = (y * g_ref[...]).astype(bf16)

    o_ref[...] = _dot(hn_ref[...], w_ref[...].astype(bf16))


def _norm_matmul(x, g, w, l, tm, tn):
    m, k = x.shape
    n = w.shape[2]
    return pl.pallas_call(
        _norm_matmul_kernel,
        grid=(m // tm, pl.cdiv(n, tn)),
        in_specs=[
            pl.BlockSpec((tm, k), lambda i, j: (i, 0)),
            pl.BlockSpec((None, 1, k), lambda i, j: (l, 0, 0)),
            pl.BlockSpec((None, k, tn), lambda i, j: (l, 0, j)),
        ],
        out_specs=pl.BlockSpec((tm, tn), lambda i, j: (i, j)),
        out_shape=jax.ShapeDtypeStruct((m, n), f32),
        scratch_shapes=[pltpu.VMEM((tm, k), bf16)],
        compiler_params=pltpu.CompilerParams(
            dimension_semantics=("arbitrary", "arbitrary"), vmem_limit_bytes=VMEM_LIMIT),
        name="norm_in_proj",
    )(x, g, w)


def _memory_kv_kernel(x_ref, g_ref, w_ref, gk_ref, k_ref, v_ref, wb_ref):
    @pl.when(pl.program_id(0) == 0)
    def _():
        wb_ref[...] = w_ref[...].astype(bf16)

    x = x_ref[...]
    y = x * lax.rsqrt(jnp.mean(x * x, axis=-1, keepdims=True) + EPS)
    kv = _dot((y * g_ref[...]).astype(bf16), wb_ref[...])
    for h in range(MEM_HEADS):
        kh = kv[:, h * MEM_HD:(h + 1) * MEM_HD]
        kh = kh * lax.rsqrt(jnp.mean(kh * kh, axis=-1, keepdims=True) + EPS)
        k_ref[:, h * MEM_HD:(h + 1) * MEM_HD] = kh * gk_ref[...]
    v_ref[...] = kv[:, GROUP_W:]


def _memory_kv(mem, g, w, gk, l):
    b = mem.shape[0]
    out = jax.ShapeDtypeStruct((b, N_MEM, GROUP_W), f32)
    return pl.pallas_call(
        _memory_kv_kernel,
        grid=(b,),
        in_specs=[
            pl.BlockSpec((None, N_MEM, D_MODEL), lambda i: (i, 0, 0)),
            pl.BlockSpec((None, 1, D_MODEL), lambda i: (l, 0, 0)),
            pl.BlockSpec((None, D_MODEL, 2 * GROUP_W), lambda i: (l, 0, 0)),
            pl.BlockSpec((None, 1, MEM_HD), lambda i: (l, 0, 0)),
        ],
        out_specs=[pl.BlockSpec((None, N_MEM, GROUP_W), lambda i: (i, 0, 0))] * 2,
        out_shape=[out, out],
        scratch_shapes=[pltpu.VMEM((D_MODEL, 2 * GROUP_W), bf16)],
        compiler_params=pltpu.CompilerParams(
            dimension_semantics=("arbitrary",), vmem_limit_bytes=VMEM_LIMIT),
        name="memory_kv",
    )(mem, g, w, gk)


CONV_PAD = 8


def _mixer_kernel(*refs, tile, decode, layer):
    if decode:
        (p_ref, mk_ref, mv_ref, conv_in_ref, gla_in_ref, kc_ref, vc_ref,
         convw_ref, wup_ref, bga_ref, ggo_ref, gsq_ref, gsk_ref, bd_ref, sinks_ref, gmq_ref,
         mix_ref, conv_out_ref, gla_out_ref, kbuf_ref, vbuf_ref,
         ext_ref, s_ref, kprev_ref, vprev_ref, tail_ref) = refs
    else:
        (p_ref, mk_ref, mv_ref,
         convw_ref, wup_ref, bga_ref, ggo_ref, gsq_ref, gsk_ref, bd_ref, sinks_ref, gmq_ref,
         mix_ref, conv_out_ref, gla_out_ref, kbuf_ref, vbuf_ref,
         ext_ref, s_ref, kprev_ref, vprev_ref, tail_ref) = refs

    T = tile
    t = pl.program_id(1)
    last = t == pl.num_programs(1) - 1

    @pl.when(t == 0)
    def _():
        ext_ref[0:CONV_PAD, :] = jnp.zeros((CONV_PAD, GROUP_W), f32)
        if decode:
            ext_ref[CONV_PAD - (CONV_W - 1):CONV_PAD, :] = conv_in_ref[...]
            s_ref[...] = gla_in_ref[...]
            kprev_ref[...] = kc_ref[...]
            vprev_ref[...] = vc_ref[...]
        else:
            s_ref[...] = jnp.zeros_like(s_ref)
            kprev_ref[...] = jnp.zeros_like(kprev_ref)
            vprev_ref[...] = jnp.zeros_like(vprev_ref)

    tail_ref[...] = p_ref[:, OFF_GZ:D_IN]

    def tail(rows, off, width):
        return tail_ref[rows, off - OFF_GZ:off - OFF_GZ + width]

    all_rows = slice(None)
    row = lax.broadcasted_iota(jnp.int32, (T, T), 0)
    col = lax.broadcasted_iota(jnp.int32, (T, T), 1)

    u = p_ref[:, OFF_AC:OFF_AC + GROUP_W] * p_ref[:, OFF_AH:OFF_AH + GROUP_W]
    ext_ref[CONV_PAD:CONV_PAD + T, :] = u
    conv = (convw_ref[0:1, :] * ext_ref[CONV_PAD - 2:CONV_PAD - 2 + T, :]
            + convw_ref[1:2, :] * ext_ref[CONV_PAD - 1:CONV_PAD - 1 + T, :]
            + convw_ref[2:3, :] * u)
    mix_ref[:, 0:GROUP_W] = (p_ref[:, OFF_AB:OFF_AB + GROUP_W] * conv
                             * _silu(p_ref[:, OFF_AZ:OFF_AZ + GROUP_W])).astype(mix_ref.dtype)
    conv_state = ext_ref[CONV_PAD + T - 2:CONV_PAD + T, :]
    ext_ref[CONV_PAD - 2:CONV_PAD, :] = conv_state

    @pl.when(last)
    def _():
        conv_out_ref[...] = conv_state

    C = min(GLA_CHUNK, T)
    n_chunk = T // C
    g_k = p_ref[:, OFF_GK:OFF_GK + GLA_HEADS * GLA_DK]
    g_a = p_ref[:, OFF_GA:OFF_GA + GLA_RANK].astype(bf16)
    log_a = _log_sigmoid(_dot(g_a, wup_ref[...].astype(bf16)) + bga_ref[...]) * (1.0 / GLA_TAU)
    la3 = _split3(log_a)
    tril = jnp.where((row // C == col // C) & (row >= col), 1.0, 0.0).astype(bf16)
    in_chunk = jnp.where(lax.broadcasted_iota(jnp.int32, (T, LANES), 0) // C
                         == lax.broadcasted_iota(jnp.int32, (T, LANES), 1), 1.0, 0.0).astype(bf16)
    cum = _dot(tril, la3[0]) + _dot(tril, la3[1]) + _dot(tril, la3[2])
    tot_t = (_dot_tn(la3[0], in_chunk) + _dot_tn(la3[1], in_chunk)
             + _dot_tn(la3[2], in_chunk))
    decay_t = jnp.exp(tot_t)
    q_dec = (p_ref[:, OFF_GQ:OFF_GQ + GLA_HEADS * GLA_DK] * (GLA_DK ** -0.5)) * jnp.exp(cum)
    k_dec = g_k * jnp.exp(-cum)
    k_tail = jnp.concatenate(
        [g_k[c * C:(c + 1) * C] * jnp.exp(cum[(c + 1) * C - 1:(c + 1) * C] - cum[c * C:(c + 1) * C])
         for c in range(n_chunk)], axis=0) if n_chunk > 1 else g_k * jnp.exp(cum[T - 1:T] - cum)
    causal = (row // C == col // C) & (row >= col)
    for h in range(GLA_HEADS):
        ks = slice(h * GLA_DK, (h + 1) * GLA_DK)
        vs = slice(OFF_GV + h * GLA_DV, OFF_GV + (h + 1) * GLA_DV)
        qd = q_dec[:, ks].astype(bf16)
        kd = k_dec[:, ks].astype(bf16)
        kt = k_tail[:, ks].astype(bf16)
        v_h = p_ref[:, vs].astype(bf16)
        attn = jnp.where(causal, _dot_nt(qd, kd), 0.0).astype(bf16)
        o_intra = _dot(attn, v_h)
        s_h = s_ref[h]
        outs = []
        for c in range(n_chunk):
            rs = slice(c * C, (c + 1) * C)
            outs.append(o_intra[rs] + _dot(qd[rs], s_h.astype(bf16)))
            s_h = decay_t[ks, c:c + 1] * s_h + _dot_tn(kt[rs], v_h[rs])
        s_ref[h] = s_h
        o_h = jnp.concatenate(outs, axis=0) if n_chunk > 1 else outs[0]
        o_h = o_h * lax.rsqrt(jnp.mean(o_h * o_h, axis=-1, keepdims=True) + EPS) * ggo_ref[...]
        mix_ref[:, GROUP_W + h * GLA_DV:GROUP_W + (h + 1) * GLA_DV] = (
            o_h * _silu(tail(all_rows, OFF_GZ + h * GLA_DV, GLA_DV))).astype(mix_ref.dtype)

    @pl.when(last)
    def _():
        gla_out_ref[...] = s_ref[...]

    def head_norm(x, g, n_lanes):
        sq = x * x
        hi = sq.astype(bf16)
        lo = (sq - hi.astype(f32)).astype(bf16)
        bd = bd_ref[0:n_lanes, 0:n_lanes]
        ms = (_dot(hi, bd) + _dot(lo, bd)) * (1.0 / SWA_HD)
        return x * lax.rsqrt(ms + EPS) * g

    q_n = head_norm(tail(all_rows, OFF_SQ, GROUP_W), gsq_ref[...], GROUP_W)
    k_n = head_norm(tail(all_rows, OFF_SK, LANES), gsk_ref[...], LANES)
    v_n = tail(all_rows, OFF_SV, LANES)

    BQ = min(WINDOW, T)
    n_blk = T // BQ
    stack = SWA_GROUP if BQ % 8 == 0 else 1
    nk = WINDOW + BQ
    qi = lax.broadcasted_iota(jnp.int32, (stack * BQ, nk), 0) % BQ
    kj = lax.broadcasted_iota(jnp.int32, (stack * BQ, nk), 1)
    dist = qi + WINDOW - kj
    band = (dist >= 0) & (dist < WINDOW)
    srow = lax.broadcasted_iota(jnp.int32, (stack * BQ, 1), 0) // BQ
    for blk in range(n_blk):
        rs = slice(blk * BQ, (blk + 1) * BQ)
        if blk == 0:
            k_prev, v_prev = kprev_ref[...], vprev_ref[...]
            valid = band if decode else band & ((kj >= WINDOW) | (t > 0))
        else:
            ps = slice((blk - 1) * BQ, blk * BQ)
            k_prev, v_prev = k_n[ps], v_n[ps]
            valid = band
        k_cat = jnp.concatenate([k_prev, k_n[rs]], axis=0)
        v_cat = jnp.concatenate([v_prev, v_n[rs]], axis=0)
        for g in range(SWA_KV_HEADS):
            kg = k_cat[:, g * SWA_HD:(g + 1) * SWA_HD].astype(bf16)
            vg = v_cat[:, g * SWA_HD:(g + 1) * SWA_HD].astype(bf16)
            for j0 in range(0, SWA_GROUP, stack):
                heads = [g * SWA_GROUP + j0 + j for j in range(stack)]
                qs = [q_n[rs, hd * SWA_HD:(hd + 1) * SWA_HD] for hd in heads]
                qg = (jnp.concatenate(qs, axis=0) if stack > 1 else qs[0]).astype(bf16)
                sink = jnp.full((stack * BQ, 1), sinks_ref[layer, heads[0]], f32)
                for j in range(1, stack):
                    sink = jnp.where(srow == j, sinks_ref[layer, heads[j]], sink)
                s = _dot_nt(qg, kg) * (SWA_HD ** -0.5)
                s = jnp.where(valid, s, -jnp.inf)
                m = jnp.maximum(jnp.max(s, axis=-1, keepdims=True), sink)
                e = jnp.exp(s - m)
                prob = e / (jnp.sum(e, axis=-1, keepdims=True) + jnp.exp(sink - m))
                o = _dot(prob.astype(bf16), vg)
                for j, hd in enumerate(heads):
                    z = tail(rs, OFF_SZ + hd * SWA_HD, SWA_HD)
                    mix_ref[rs, 2 * GROUP_W + hd * SWA_HD:2 * GROUP_W + (hd + 1) * SWA_HD] = (
                        o[j * BQ:(j + 1) * BQ] * _silu(z)).astype(mix_ref.dtype)

    if decode:
        kbuf_ref[0:WINDOW - T, :] = kc_ref[T:WINDOW, :]
        kbuf_ref[WINDOW - T:WINDOW, :] = k_n
        vbuf_ref[0:WINDOW - T, :] = vc_ref[T:WINDOW, :]
        vbuf_ref[WINDOW - T:WINDOW, :] = v_n
    else:
        kprev_ref[...] = k_n[T - WINDOW:T]
        vprev_ref[...] = v_n[T - WINDOW:T]

        @pl.when(last)
        def _():
            kbuf_ref[...] = k_n[T - WINDOW:T]
            vbuf_ref[...] = v_n[T - WINDOW:T]

    for h in range(MEM_HEADS):
        hs = slice(h * MEM_HD, (h + 1) * MEM_HD)
        qh = tail(all_rows, OFF_MQ + h * MEM_HD, MEM_HD)
        qh = qh * lax.rsqrt(jnp.mean(qh * qh, axis=-1, keepdims=True) + EPS) * gmq_ref[...]
        s = _dot_nt(qh.astype(bf16), mk_ref[:, hs].astype(bf16)) * (MEM_HD ** -0.5)
        e = jnp.exp(s - jnp.max(s, axis=-1, keepdims=True))
        prob = e / jnp.sum(e, axis=-1, keepdims=True)
        o = _dot(prob.astype(bf16), mv_ref[:, hs].astype(bf16))
        mix_ref[:, 3 * GROUP_W + h * MEM_HD:3 * GROUP_W + (h + 1) * MEM_HD] = (
            o * _silu(tail(all_rows, OFF_MZ + h * MEM_HD, MEM_HD))).astype(mix_ref.dtype)


def _mixer(proj, mem_k, mem_v, mem_layer, state, params, layer, tile, decode):
    b, L, _ = proj.shape
    nt = L // tile
    conv_w, w_up, b_ga, g_go, g_sq, g_sk, bd, sinks, g_mq = params

    def tok(width):
        return pl.BlockSpec((None, tile, width), lambda i, t: (i, t, 0))

    def per_seq(*shape):
        return pl.BlockSpec((None,) + shape, lambda i, t: (i,) + (0,) * len(shape))

    def per_seq_at(lyr, *shape):
        return pl.BlockSpec((None, None) + shape, lambda i, t: (lyr, i) + (0,) * len(shape))

    def param(a):
        return pl.BlockSpec((None,) + a.shape[1:], lambda i, t: (layer,) + (0,) * (a.ndim - 1))

    state_shapes = [(CONV_W - 1, GROUP_W), (GLA_HEADS, GLA_DK, GLA_DV),
                    (WINDOW, SWA_KV_HEADS * SWA_HD), (WINDOW, SWA_KV_HEADS * SWA_HD)]
    in_specs = [tok(D_IN), per_seq_at(mem_layer, N_MEM, GROUP_W),
                per_seq_at(mem_layer, N_MEM, GROUP_W)]
    args = [proj, mem_k, mem_v]
    if decode:
        in_specs += [per_seq_at(layer, *s) for s in state_shapes]
        args += list(state)
    in_specs += [param(conv_w), param(w_up), param(b_ga), param(g_go), param(g_sq), param(g_sk),
                 pl.BlockSpec(bd.shape, lambda i, t: (0, 0)),
                 pl.BlockSpec(memory_space=pltpu.SMEM), param(g_mq)]
    args += [conv_w, w_up, b_ga, g_go, g_sq, g_sk, bd, sinks, g_mq]
    mix_dtype = f32 if decode else bf16
    out_shape = [jax.ShapeDtypeStruct((b, L, 4 * GROUP_W), mix_dtype)]
    out_shape += [jax.ShapeDtypeStruct((b,) + s, f32) for s in state_shapes]
    out_specs = [tok(4 * GROUP_W)] + [per_seq(*s) for s in state_shapes]
    return pl.pallas_call(
        functools.partial(_mixer_kernel, tile=tile, decode=decode, layer=layer),
        grid=(b, nt),
        in_specs=in_specs,
        out_specs=out_specs,
        out_shape=out_shape,
        scratch_shapes=[
            pltpu.VMEM((CONV_PAD + tile, GROUP_W), f32),
            pltpu.VMEM((GLA_HEADS, GLA_DK, GLA_DV), f32),
            pltpu.VMEM((WINDOW, SWA_KV_HEADS * SWA_HD), f32),
            pltpu.VMEM((WINDOW, SWA_KV_HEADS * SWA_HD), f32),
            pltpu.VMEM((tile, D_IN - OFF_GZ), f32),
        ],
        compiler_params=pltpu.CompilerParams(
            dimension_semantics=("arbitrary", "arbitrary"), vmem_limit_bytes=VMEM_LIMIT),
        name="mixer_decode" if decode else "mixer_prompt",
    )(*args)


def _out_proj_kernel(mix_ref, w_ref, x_ref, y_ref, wb_ref):
    @pl.when(pl.program_id(1) == 0)
    def _():
        wb_ref[...] = w_ref[...].astype(bf16)

    y_ref[...] = x_ref[...] + _dot(mix_ref[...].astype(bf16), wb_ref[...])


def _out_proj(mix, w, x, l, tm, tn):
    m, k = mix.shape
    n = w.shape[2]
    return pl.pallas_call(
        _out_proj_kernel,
        grid=(n // tn, m // tm),
        in_specs=[
            pl.BlockSpec((tm, k), lambda j, i: (i, 0)),
            pl.BlockSpec((None, k, tn), lambda j, i: (l, 0, j)),
            pl.BlockSpec((tm, tn), lambda j, i: (i, j)),
        ],
        out_specs=pl.BlockSpec((tm, tn), lambda j, i: (i, j)),
        out_shape=jax.ShapeDtypeStruct((m, n), f32),
        scratch_shapes=[pltpu.VMEM((k, tn), bf16)],
        compiler_params=pltpu.CompilerParams(
            dimension_semantics=("arbitrary", "arbitrary"), vmem_limit_bytes=VMEM_LIMIT),
        name="out_proj",
    )(mix, w, x)


PROMPT_TILE = 256
PROJ_TM, PROJ_TN = 1024, 768
OUT_TM, OUT_TN = 1024, 1024

_LANE = np.arange(GROUP_W)
HEAD_BLOCK_DIAG = _LANE[:, None] // SWA_HD == _LANE[None, :] // SWA_HD


def kernel(x_prompt, x_sample, mem_prompt, state_conv, state_gla, cache_swa_k, cache_swa_v,
           cache_mem_k, cache_mem_v, g_norm, w_in, conv_w, w_gla_a_up, b_gla_a, g_gla_o,
           g_swa_q, g_swa_k, swa_sinks, g_mem, w_mem_kv, g_mem_q, g_mem_k, w_out):
    depth = w_in.shape[0]
    bp, lp, _ = x_prompt.shape
    bs, ls, _ = x_sample.shape
    hp = x_prompt.reshape(bp * lp, D_MODEL)
    hs = x_sample.reshape(bs * ls, D_MODEL)

    def row(a):
        return a[:, None, :]

    params = (conv_w, w_gla_a_up, row(b_gla_a), row(g_gla_o),
              row(jnp.tile(g_swa_q, (1, SWA_HEADS))), row(jnp.tile(g_swa_k, (1, SWA_KV_HEADS))),
              jnp.asarray(HEAD_BLOCK_DIAG, bf16), swa_sinks, row(g_mem_q))
    g_n, g_m, g_mk = row(g_norm), row(g_mem), row(g_mem_k)
    kv_w = SWA_KV_HEADS * SWA_HD
    state = (state_conv, state_gla, cache_swa_k.reshape(depth, bs, WINDOW, kv_w),
             cache_swa_v.reshape(depth, bs, WINDOW, kv_w))
    mem_k_s = cache_mem_k.reshape(depth, bs, N_MEM, GROUP_W)
    mem_v_s = cache_mem_v.reshape(depth, bs, N_MEM, GROUP_W)

    outs = [[] for _ in range(10)]
    for l in range(depth):
        mk, mv = _memory_kv(mem_prompt, g_m, w_mem_kv, g_mk, l)
        proj = _norm_matmul(hp, g_n, w_in, l, PROJ_TM, PROJ_TN).reshape(bp, lp, D_IN)
        mix, c, s, kb, vb = _mixer(proj, mk[None], mv[None], 0, None, params, l, PROMPT_TILE,
                                   decode=False)
        hp = _out_proj(mix.reshape(bp * lp, 4 * GROUP_W), w_out, hp, l, OUT_TM, OUT_TN)
        for lst, a in zip(outs[:6], (
                c, s, kb.reshape(bp, WINDOW, SWA_KV_HEADS, SWA_HD),
                vb.reshape(bp, WINDOW, SWA_KV_HEADS, SWA_HD),
                mk.reshape(bp, N_MEM, MEM_HEADS, MEM_HD), mv.reshape(bp, N_MEM, MEM_HEADS, MEM_HD))):
            lst.append(a)

        proj = _norm_matmul(hs, g_n, w_in, l, bs * ls, PROJ_TN).reshape(bs, ls, D_IN)
        mix, c, s, kb, vb = _mixer(proj, mem_k_s, mem_v_s, l, state, params, l, ls, decode=True)
        hs = _out_proj(mix.reshape(bs * ls, 4 * GROUP_W), w_out, hs, l, bs * ls, OUT_TN)
        for lst, a in zip(outs[6:], (
                c, s, kb.reshape(bs, WINDOW, SWA_KV_HEADS, SWA_HD),
                vb.reshape(bs, WINDOW, SWA_KV_HEADS, SWA_HD))):
            lst.append(a)

    return (hp.reshape(bp, lp, D_MODEL), hs.reshape(bs, ls, D_MODEL),
            *[jnp.stack(o) for o in outs])
```

```python
import functools

import jax
import jax.numpy as jnp
import numpy as np
from jax import lax
from jax.experimental import pallas as pl
from jax.experimental.pallas import tpu as pltpu

f32 = jnp.float32
bf16 = jnp.bfloat16

D_MODEL = 2048
GROUP_W = 512
GLA_HEADS = 4
GLA_DK = 64
GLA_DV = 128
GLA_RANK = 16
GLA_TAU = 16.0
GLA_CHUNK = 64
SWA_HEADS = 8
SWA_KV_HEADS = 2
SWA_HD = 64
SWA_GROUP = SWA_HEADS // SWA_KV_HEADS
WINDOW = 128
N_MEM = 256
MEM_HEADS = 4
MEM_HD = 128
CONV_W = 3
EPS = 1e-6

LANES = 128

D_IN = 5904
OFF_AB, OFF_AC, OFF_AH, OFF_AZ = 0, 512, 1024, 1536
OFF_GQ, OFF_GK, OFF_GV, OFF_GA, OFF_GZ = 2048, 2304, 2560, 3072, 3088
OFF_SQ, OFF_SK, OFF_SV, OFF_SZ = 3600, 4112, 4240, 4368
OFF_MQ, OFF_MZ = 4880, 5392

VMEM_LIMIT = 52 * 1024 * 1024


def _dot(a, b):
    return jnp.dot(a, b, preferred_element_type=f32)


def _dot_nt(a, b):
    return lax.dot_general(a, b, (((1,), (1,)), ((), ())), preferred_element_type=f32)


def _dot_tn(a, b):
    return lax.dot_general(a, b, (((0,), (0,)), ((), ())), preferred_element_type=f32)


def _split3(x):
    hi = x.astype(bf16)
    r = x - hi.astype(f32)
    mid = r.astype(bf16)
    lo = (r - mid.astype(f32)).astype(bf16)
    return hi, mid, lo


def _silu(x):
    return x * jax.nn.sigmoid(x)


def _log_sigmoid(x):
    return jnp.minimum(x, 0.0) - jnp.log1p(jnp.exp(-jnp.abs(x)))


def _norm_matmul_kernel(x_ref, g_ref, wt_ref, o_ref, hn_ref):
    @pl.when(pl.program_id(1) == 0)
    def _():
        x = x_ref[...]
        y = x * lax.rsqrt(jnp.mean(x * x, axis=-1, keepdims=True) + EPS)
        hn_ref[...] = (y * g_ref[...]).astype(bf16)

    o_ref[...] = _dot_nt(hn_ref[...], wt_ref[...].astype(bf16))


def _norm_matmul(x, g, wt, l, tm, tn):
    m, k = x.shape
    n = wt.shape[1]
    return pl.pallas_call(
        _norm_matmul_kernel,
        grid=(m // tm, pl.cdiv(n, tn)),
        in_specs=[
            pl.BlockSpec((tm, k), lambda i, j: (i, 0)),
            pl.BlockSpec((None, 1, k), lambda i, j: (l, 0, 0)),
            pl.BlockSpec((None, tn, k), lambda i, j: (l, j, 0)),
        ],
        out_specs=pl.BlockSpec((tm, tn), lambda i, j: (i, j)),
        out_shape=jax.ShapeDtypeStruct((m, n), f32),
        scratch_shapes=[pltpu.VMEM((tm, k), bf16)],
        compiler_params=pltpu.CompilerParams(
            dimension_semantics=("arbitrary", "arbitrary"), vmem_limit_bytes=VMEM_LIMIT),
        name="norm_in_proj",
    )(x, g, wt)


def _memory_kv_kernel(x_ref, g_ref, w_ref, gk_ref, k_ref, v_ref, wb_ref):
    @pl.when(pl.program_id(0) == 0)
    def _():
        wb_ref[...] = w_ref[...].astype(bf16)

    x = x_ref[...]
    y = x * lax.rsqrt(jnp.mean(x * x, axis=-1, keepdims=True) + EPS)
    kv = _dot((y * g_ref[...]).astype(bf16), wb_ref[...])
    for h in range(MEM_HEADS):
        kh = kv[:, h * MEM_HD:(h + 1) * MEM_HD]
        kh = kh * lax.rsqrt(jnp.mean(kh * kh, axis=-1, keepdims=True) + EPS)
        k_ref[:, h * MEM_HD:(h + 1) * MEM_HD] = kh * gk_ref[...]
    v_ref[...] = kv[:, GROUP_W:]


def _memory_kv(mem, g, w, gk, l):
    b = mem.shape[0]
    out = jax.ShapeDtypeStruct((b, N_MEM, GROUP_W), f32)
    return pl.pallas_call(
        _memory_kv_kernel,
        grid=(b,),
        in_specs=[
            pl.BlockSpec((None, N_MEM, D_MODEL), lambda i: (i, 0, 0)),
            pl.BlockSpec((None, 1, D_MODEL), lambda i: (l, 0, 0)),
            pl.BlockSpec((None, D_MODEL, 2 * GROUP_W), lambda i: (l, 0, 0)),
            pl.BlockSpec((None, 1, MEM_HD), lambda i: (l, 0, 0)),
        ],
        out_specs=[pl.BlockSpec((None, N_MEM, GROUP_W), lambda i: (i, 0, 0))] * 2,
        out_shape=[out, out],
        scratch_shapes=[pltpu.VMEM((D_MODEL, 2 * GROUP_W), bf16)],
        compiler_params=pltpu.CompilerParams(
            dimension_semantics=("arbitrary",), vmem_limit_bytes=VMEM_LIMIT),
        name="memory_kv",
    )(mem, g, w, gk)


CONV_PAD = 8


def _mixer_kernel(*refs, tile, decode, layer):
    if decode:
        (p_ref, mk_ref, mv_ref, conv_in_ref, gla_in_ref, kc_ref, vc_ref,
         convw_ref, wup_ref, bga_ref, ggo_ref, gsq_ref, gsk_ref, bd_ref, sinks_ref, gmq_ref,
         mix_ref, conv_out_ref, gla_out_ref, kbuf_ref, vbuf_ref,
         ext_ref, s_ref, kprev_ref, vprev_ref, tail_ref) = refs
    else:
        (p_ref, mk_ref, mv_ref,
         convw_ref, wup_ref, bga_ref, ggo_ref, gsq_ref, gsk_ref, bd_ref, sinks_ref, gmq_ref,
         mix_ref, conv_out_ref, gla_out_ref, kbuf_ref, vbuf_ref,
         ext_ref, s_ref, kprev_ref, vprev_ref, tail_ref) = refs

    T = tile
    t = pl.program_id(1)
    last = t == pl.num_programs(1) - 1

    @pl.when(t == 0)
    def _():
        ext_ref[0:CONV_PAD, :] = jnp.zeros((CONV_PAD, GROUP_W), f32)
        if decode:
            ext_ref[CONV_PAD - (CONV_W - 1):CONV_PAD, :] = conv_in_ref[...]
            s_ref[...] = gla_in_ref[...]
            kprev_ref[...] = kc_ref[...]
            vprev_ref[...] = vc_ref[...]
        else:
            s_ref[...] = jnp.zeros_like(s_ref)
            kprev_ref[...] = jnp.zeros_like(kprev_ref)
            vprev_ref[...] = jnp.zeros_like(vprev_ref)

    tail_ref[...] = p_ref[:, OFF_GZ:D_IN]

    def tail(rows, off, width):
        return tail_ref[rows, off - OFF_GZ:off - OFF_GZ + width]

    all_rows = slice(None)
    row = lax.broadcasted_iota(jnp.int32, (T, T), 0)
    col = lax.broadcasted_iota(jnp.int32, (T, T), 1)

    u = p_ref[:, OFF_AC:OFF_AC + GROUP_W] * p_ref[:, OFF_AH:OFF_AH + GROUP_W]
    ext_ref[CONV_PAD:CONV_PAD + T, :] = u
    conv = (convw_ref[0:1, :] * ext_ref[CONV_PAD - 2:CONV_PAD - 2 + T, :]
            + convw_ref[1:2, :] * ext_ref[CONV_PAD - 1:CONV_PAD - 1 + T, :]
            + convw_ref[2:3, :] * u)
    mix_ref[:, 0:GROUP_W] = (p_ref[:, OFF_AB:OFF_AB + GROUP_W] * conv
                             * _silu(p_ref[:, OFF_AZ:OFF_AZ + GROUP_W])).astype(mix_ref.dtype)
    conv_state = ext_ref[CONV_PAD + T - 2:CONV_PAD + T, :]
    ext_ref[CONV_PAD - 2:CONV_PAD, :] = conv_state

    @pl.when(last)
    def _():
        conv_out_ref[...] = conv_state

    C = min(GLA_CHUNK, T)
    n_chunk = T // C
    g_k = p_ref[:, OFF_GK:OFF_GK + GLA_HEADS * GLA_DK]
    g_a = p_ref[:, OFF_GA:OFF_GA + GLA_RANK].astype(bf16)
    log_a = _log_sigmoid(_dot(g_a, wup_ref[...].astype(bf16)) + bga_ref[...]) * (1.0 / GLA_TAU)
    la3 = _split3(log_a)
    tril = jnp.where((row // C == col // C) & (row >= col), 1.0, 0.0).astype(bf16)
    in_chunk = jnp.where(lax.broadcasted_iota(jnp.int32, (T, LANES), 0) // C
                         == lax.broadcasted_iota(jnp.int32, (T, LANES), 1), 1.0, 0.0).astype(bf16)
    cum = _dot(tril, la3[0]) + _dot(tril, la3[1]) + _dot(tril, la3[2])
    tot_t = (_dot_tn(la3[0], in_chunk) + _dot_tn(la3[1], in_chunk)
             + _dot_tn(la3[2], in_chunk))
    decay_t = jnp.exp(tot_t)
    q_dec = (p_ref[:, OFF_GQ:OFF_GQ + GLA_HEADS * GLA_DK] * (GLA_DK ** -0.5)) * jnp.exp(cum)
    k_dec = g_k * jnp.exp(-cum)
    k_tail = jnp.concatenate(
        [g_k[c * C:(c + 1) * C] * jnp.exp(cum[(c + 1) * C - 1:(c + 1) * C] - cum[c * C:(c + 1) * C])
         for c in range(n_chunk)], axis=0) if n_chunk > 1 else g_k * jnp.exp(cum[T - 1:T] - cum)
    causal = (row // C == col // C) & (row >= col)
    for h in range(GLA_HEADS):
        ks = slice(h * GLA_DK, (h + 1) * GLA_DK)
        vs = slice(OFF_GV + h * GLA_DV, OFF_GV + (h + 1) * GLA_DV)
        qd = q_dec[:, ks].astype(bf16)
        kd = k_dec[:, ks].astype(bf16)
        kt = k_tail[:, ks].astype(bf16)
        v_h = p_ref[:, vs].astype(bf16)
        attn = jnp.where(causal, _dot_nt(qd, kd), 0.0).astype(bf16)
        o_intra = _dot(attn, v_h)
        s_h = s_ref[h]
        outs = []
        for c in range(n_chunk):
            rs = slice(c * C, (c + 1) * C)
            outs.append(o_intra[rs] + _dot(qd[rs], s_h.astype(bf16)))
            s_h = decay_t[ks, c:c + 1] * s_h + _dot_tn(kt[rs], v_h[rs])
        s_ref[h] = s_h
        o_h = jnp.concatenate(outs, axis=0) if n_chunk > 1 else outs[0]
        o_h = o_h * lax.rsqrt(jnp.mean(o_h * o_h, axis=-1, keepdims=True) + EPS) * ggo_ref[...]
        mix_ref[:, GROUP_W + h * GLA_DV:GROUP_W + (h + 1) * GLA_DV] = (
            o_h * _silu(tail(all_rows, OFF_GZ + h * GLA_DV, GLA_DV))).astype(mix_ref.dtype)

    @pl.when(last)
    def _():
        gla_out_ref[...] = s_ref[...]

    def head_norm(x, g, n_lanes):
        sq = x * x
        hi = sq.astype(bf16)
        lo = (sq - hi.astype(f32)).astype(bf16)
        bd = bd_ref[0:n_lanes, 0:n_lanes]
        ms = (_dot(hi, bd) + _dot(lo, bd)) * (1.0 / SWA_HD)
        return x * lax.rsqrt(ms + EPS) * g

    q_n = head_norm(tail(all_rows, OFF_SQ, GROUP_W), gsq_ref[...], GROUP_W)
    k_n = head_norm(tail(all_rows, OFF_SK, LANES), gsk_ref[...], LANES)
    v_n = tail(all_rows, OFF_SV, LANES)

    BQ = min(WINDOW, T)
    n_blk = T // BQ
    stack = SWA_GROUP if BQ % 8 == 0 else 1
    nk = WINDOW + BQ
    qi = lax.broadcasted_iota(jnp.int32, (stack * BQ, nk), 0) % BQ
    kj = lax.broadcasted_iota(jnp.int32, (stack * BQ, nk), 1)
    dist = qi + WINDOW - kj
    band = (dist >= 0) & (dist < WINDOW)
    srow = lax.broadcasted_iota(jnp.int32, (stack * BQ, 1), 0) // BQ
    for blk in range(n_blk):
        rs = slice(blk * BQ, (blk + 1) * BQ)
        if blk == 0:
            k_prev, v_prev = kprev_ref[...], vprev_ref[...]
            valid = band if decode else band & ((kj >= WINDOW) | (t > 0))
        else:
            ps = slice((blk - 1) * BQ, blk * BQ)
            k_prev, v_prev = k_n[ps], v_n[ps]
            valid = band
        k_cat = jnp.concatenate([k_prev, k_n[rs]], axis=0)
        v_cat = jnp.concatenate([v_prev, v_n[rs]], axis=0)
        for g in range(SWA_KV_HEADS):
            kg = k_cat[:, g * SWA_HD:(g + 1) * SWA_HD].astype(bf16)
            vg = v_cat[:, g * SWA_HD:(g + 1) * SWA_HD].astype(bf16)
            for j0 in range(0, SWA_GROUP, stack):
                heads = [g * SWA_GROUP + j0 + j for j in range(stack)]
                qs = [q_n[rs, hd * SWA_HD:(hd + 1) * SWA_HD] for hd in heads]
                qg = (jnp.concatenate(qs, axis=0) if stack > 1 else qs[0]).astype(bf16)
                sink = jnp.full((stack * BQ, 1), sinks_ref[layer, heads[0]], f32)
                for j in range(1, stack):
                    sink = jnp.where(srow == j, sinks_ref[layer, heads[j]], sink)
                s = _dot_nt(qg, kg) * (SWA_HD ** -0.5)
                s = jnp.where(valid, s, -jnp.inf)
                m = jnp.maximum(jnp.max(s, axis=-1, keepdims=True), sink)
                e = jnp.exp(s - m)
                prob = e / (jnp.sum(e, axis=-1, keepdims=True) + jnp.exp(sink - m))
                o = _dot(prob.astype(bf16), vg)
                for j, hd in enumerate(heads):
                    z = tail(rs, OFF_SZ + hd * SWA_HD, SWA_HD)
                    mix_ref[rs, 2 * GROUP_W + hd * SWA_HD:2 * GROUP_W + (hd + 1) * SWA_HD] = (
                        o[j * BQ:(j + 1) * BQ] * _silu(z)).astype(mix_ref.dtype)

    if decode:
        kbuf_ref[0:WINDOW - T, :] = kc_ref[T:WINDOW, :]
        kbuf_ref[WINDOW - T:WINDOW, :] = k_n
        vbuf_ref[0:WINDOW - T, :] = vc_ref[T:WINDOW, :]
        vbuf_ref[WINDOW - T:WINDOW, :] = v_n
    else:
        kprev_ref[...] = k_n[T - WINDOW:T]
        vprev_ref[...] = v_n[T - WINDOW:T]

        @pl.when(last)
        def _():
            kbuf_ref[...] = k_n[T - WINDOW:T]
            vbuf_ref[...] = v_n[T - WINDOW:T]

    for h in range(MEM_HEADS):
        hs = slice(h * MEM_HD, (h + 1) * MEM_HD)
        qh = tail(all_rows, OFF_MQ + h * MEM_HD, MEM_HD)
        qh = qh * lax.rsqrt(jnp.mean(qh * qh, axis=-1, keepdims=True) + EPS) * gmq_ref[...]
        s = _dot_nt(qh.astype(bf16), mk_ref[:, hs].astype(bf16)) * (MEM_HD ** -0.5)
        e = jnp.exp(s - jnp.max(s, axis=-1, keepdims=True))
        prob = e / jnp.sum(e, axis=-1, keepdims=True)
        o = _dot(prob.astype(bf16), mv_ref[:, hs].astype(bf16))
        mix_ref[:, 3 * GROUP_W + h * MEM_HD:3 * GROUP_W + (h + 1) * MEM_HD] = (
            o * _silu(tail(all_rows, OFF_MZ + h * MEM_HD, MEM_HD))).astype(mix_ref.dtype)


def _mixer(proj, mem_k, mem_v, mem_layer, state, params, layer, tile, decode):
    b, L, _ = proj.shape
    nt = L // tile
    conv_w, w_up, b_ga, g_go, g_sq, g_sk, bd, sinks, g_mq = params

    def tok(width):
        return pl.BlockSpec((None, tile, width), lambda i, t: (i, t, 0))

    def per_seq(*shape):
        return pl.BlockSpec((None,) + shape, lambda i, t: (i,) + (0,) * len(shape))

    def per_seq_at(lyr, *shape):
        return pl.BlockSpec((None, None) + shape, lambda i, t: (lyr, i) + (0,) * len(shape))

    def param(a):
        return pl.BlockSpec((None,) + a.shape[1:], lambda i, t: (layer,) + (0,) * (a.ndim - 1))

    state_shapes = [(CONV_W - 1, GROUP_W), (GLA_HEADS, GLA_DK, GLA_DV),
                    (WINDOW, SWA_KV_HEADS * SWA_HD), (WINDOW, SWA_KV_HEADS * SWA_HD)]
    in_specs = [tok(D_IN), per_seq_at(mem_layer, N_MEM, GROUP_W),
                per_seq_at(mem_layer, N_MEM, GROUP_W)]
    args = [proj, mem_k, mem_v]
    if decode:
        in_specs += [per_seq_at(layer, *s) for s in state_shapes]
        args += list(state)
    in_specs += [param(conv_w), param(w_up), param(b_ga), param(g_go), param(g_sq), param(g_sk),
                 pl.BlockSpec(bd.shape, lambda i, t: (0, 0)),
                 pl.BlockSpec(memory_space=pltpu.SMEM), param(g_mq)]
    args += [conv_w, w_up, b_ga, g_go, g_sq, g_sk, bd, sinks, g_mq]
    mix_dtype = f32 if decode else bf16
    out_shape = [jax.ShapeDtypeStruct((b, L, 4 * GROUP_W), mix_dtype)]
    out_shape += [jax.ShapeDtypeStruct((b,) + s, f32) for s in state_shapes]
    out_specs = [tok(4 * GROUP_W)] + [per_seq(*s) for s in state_shapes]
    return pl.pallas_call(
        functools.partial(_mixer_kernel, tile=tile, decode=decode, layer=layer),
        grid=(b, nt),
        in_specs=in_specs,
        out_specs=out_specs,
        out_shape=out_shape,
        scratch_shapes=[
            pltpu.VMEM((CONV_PAD + tile, GROUP_W), f32),
            pltpu.VMEM((GLA_HEADS, GLA_DK, GLA_DV), f32),
            pltpu.VMEM((WINDOW, SWA_KV_HEADS * SWA_HD), f32),
            pltpu.VMEM((WINDOW, SWA_KV_HEADS * SWA_HD), f32),
            pltpu.VMEM((tile, D_IN - OFF_GZ), f32),
        ],
        compiler_params=pltpu.CompilerParams(
            dimension_semantics=("arbitrary", "arbitrary"), vmem_limit_bytes=VMEM_LIMIT),
        name="mixer_decode" if decode else "mixer_prompt",
    )(*args)


def _out_proj_kernel(mix_ref, w_ref, x_ref, y_ref, wb_ref):
    @pl.when(pl.program_id(1) == 0)
    def _():
        wb_ref[...] = w_ref[...].astype(bf16)

    y_ref[...] = x_ref[...] + _dot(mix_ref[...].astype(bf16), wb_ref[...])


def _out_proj(mix, w, x, l, tm, tn):
    m, k = mix.shape
    n = w.shape[2]
    return pl.pallas_call(
        _out_proj_kernel,
        grid=(n // tn, m // tm),
        in_specs=[
            pl.BlockSpec((tm, k), lambda j, i: (i, 0)),
            pl.BlockSpec((None, k, tn), lambda j, i: (l, 0, j)),
            pl.BlockSpec((tm, tn), lambda j, i: (i, j)),
        ],
        out_specs=pl.BlockSpec((tm, tn), lambda j, i: (i, j)),
        out_shape=jax.ShapeDtypeStruct((m, n), f32),
        scratch_shapes=[pltpu.VMEM((k, tn), bf16)],
        compiler_params=pltpu.CompilerParams(
            dimension_semantics=("arbitrary", "arbitrary"), vmem_limit_bytes=VMEM_LIMIT),
        name="out_proj",
    )(mix, w, x)


PROMPT_TILE = 256
PROJ_TM, PROJ_TN = 1024, 768
OUT_TM, OUT_TN = 1024, 1024

_LANE = np.arange(GROUP_W)
HEAD_BLOCK_DIAG = _LANE[:, None] // SWA_HD == _LANE[None, :] // SWA_HD


def kernel(x_prompt, x_sample, mem_prompt, state_conv, state_gla, cache_swa_k, cache_swa_v,
           cache_mem_k, cache_mem_v, g_norm, w_in, conv_w, w_gla_a_up, b_gla_a, g_gla_o,
           g_swa_q, g_swa_k, swa_sinks, g_mem, w_mem_kv, g_mem_q, g_mem_k, w_out):
    depth = w_in.shape[0]
    bp, lp, _ = x_prompt.shape
    bs, ls, _ = x_sample.shape
    hp = x_prompt.reshape(bp * lp, D_MODEL)
    hs = x_sample.reshape(bs * ls, D_MODEL)

    def row(a):
        return a[:, None, :]

    params = (conv_w, w_gla_a_up, row(b_gla_a), row(g_gla_o),
              row(jnp.tile(g_swa_q, (1, SWA_HEADS))), row(jnp.tile(g_swa_k, (1, SWA_KV_HEADS))),
              jnp.asarray(HEAD_BLOCK_DIAG, bf16), swa_sinks, row(g_mem_q))
    g_n, g_m, g_mk = row(g_norm), row(g_mem), row(g_mem_k)
    w_in_t = jnp.swapaxes(w_in, 1, 2)
    kv_w = SWA_KV_HEADS * SWA_HD
    state = (state_conv, state_gla, cache_swa_k.reshape(depth, bs, WINDOW, kv_w),
             cache_swa_v.reshape(depth, bs, WINDOW, kv_w))
    mem_k_s = cache_mem_k.reshape(depth, bs, N_MEM, GROUP_W)
    mem_v_s = cache_mem_v.reshape(depth, bs, N_MEM, GROUP_W)

    outs = [[] for _ in range(10)]
    for l in range(depth):
        mk, mv = _memory_kv(mem_prompt, g_m, w_mem_kv, g_mk, l)
        proj = _norm_matmul(hp, g_n, w_in_t, l, PROJ_TM, PROJ_TN).reshape(bp, lp, D_IN)
        mix, c, s, kb, vb = _mixer(proj, mk[None], mv[None], 0, None, params, l, PROMPT_TILE,
                                   decode=False)
        hp = _out_proj(mix.reshape(bp * lp, 4 * GROUP_W), w_out, hp, l, OUT_TM, OUT_TN)
        for lst, a in zip(outs[:6], (
                c, s, kb.reshape(bp, WINDOW, SWA_KV_HEADS, SWA_HD),
                vb.reshape(bp, WINDOW, SWA_KV_HEADS, SWA_HD),
                mk.reshape(bp, N_MEM, MEM_HEADS, MEM_HD), mv.reshape(bp, N_MEM, MEM_HEADS, MEM_HD))):
            lst.append(a)

        proj = _norm_matmul(hs, g_n, w_in_t, l, bs * ls, PROJ_TN).reshape(bs, ls, D_IN)
        mix, c, s, kb, vb = _mixer(proj, mem_k_s, mem_v_s, l, state, params, l, ls, decode=True)
        hs = _out_proj(mix.reshape(bs * ls, 4 * GROUP_W), w_out, hs, l, bs * ls, OUT_TN)
        for lst, a in zip(outs[6:], (
                c, s, kb.reshape(bs, WINDOW, SWA_KV_HEADS, SWA_HD),
                vb.reshape(bs, WINDOW, SWA_KV_HEADS, SWA_HD))):
            lst.append(a)

    return (hp.reshape(bp, lp, D_MODEL), hs.reshape(bs, ls, D_MODEL),
            *[jnp.stack(o) for o in outs])
```

```python
import functools

import jax
import jax.numpy as jnp
import numpy as np
from jax import lax
from jax.experimental import pallas as pl
from jax.experimental.pallas import tpu as pltpu

f32 = jnp.float32
bf16 = jnp.bfloat16

D_MODEL = 2048
GROUP_W = 512
GLA_HEADS = 4
GLA_DK = 64
GLA_DV = 128
GLA_RANK = 16
GLA_TAU = 16.0
GLA_CHUNK = 64
SWA_HEADS = 8
SWA_KV_HEADS = 2
SWA_HD = 64
SWA_GROUP = SWA_HEADS // SWA_KV_HEADS
WINDOW = 128
N_MEM = 256
MEM_HEADS = 4
MEM_HD = 128
CONV_W = 3
EPS = 1e-6

LANES = 128

D_IN = 5904
OFF_AB, OFF_AC, OFF_AH, OFF_AZ = 0, 512, 1024, 1536
OFF_GQ, OFF_GK, OFF_GV, OFF_GA, OFF_GZ = 2048, 2304, 2560, 3072, 3088
OFF_SQ, OFF_SK, OFF_SV, OFF_SZ = 3600, 4112, 4240, 4368
OFF_MQ, OFF_MZ = 4880, 5392

VMEM_LIMIT = 52 * 1024 * 1024


def _dot(a, b):
    return jnp.dot(a, b, preferred_element_type=f32)


def _dot_nt(a, b):
    return lax.dot_general(a, b, (((1,), (1,)), ((), ())), preferred_element_type=f32)


def _dot_tn(a, b):
    return lax.dot_general(a, b, (((0,), (0,)), ((), ())), preferred_element_type=f32)


def _split3(x):
    hi = x.astype(bf16)
    r = x - hi.astype(f32)
    mid = r.astype(bf16)
    lo = (r - mid.astype(f32)).astype(bf16)
    return hi, mid, lo


def _silu(x):
    return x * jax.nn.sigmoid(x)


def _log_sigmoid(x):
    return jnp.minimum(x, 0.0) - jnp.log1p(jnp.exp(-jnp.abs(x)))


def _norm_matmul_kernel(x_ref, g_ref, wt_ref, o_ref, hn_ref):
    @pl.when(pl.program_id(1) == 0)
    def _():
        x = x_ref[...]
        y = x * lax.rsqrt(jnp.mean(x * x, axis=-1, keepdims=True) + EPS)
        hn_ref[...] = (y * g_ref[...]).astype(bf16)

    o_ref[...] = _dot_nt(hn_ref[...], wt_ref[...].astype(bf16))


def _norm_matmul(x, g, wt, l, tm, tn):
    m, k = x.shape
    n = wt.shape[1]
    return pl.pallas_call(
        _norm_matmul_kernel,
        grid=(m // tm, pl.cdiv(n, tn)),
        in_specs=[
            pl.BlockSpec((tm, k), lambda i, j: (i, 0)),
            pl.BlockSpec((None, 1, k), lambda i, j: (l, 0, 0)),
            pl.BlockSpec((None, tn, k), lambda i, j: (l, j, 0)),
        ],
        out_specs=pl.BlockSpec((tm, tn), lambda i, j: (i, j)),
        out_shape=jax.ShapeDtypeStruct((m, n), f32),
        scratch_shapes=[pltpu.VMEM((tm, k), bf16)],
        compiler_params=pltpu.CompilerParams(
            dimension_semantics=("arbitrary", "arbitrary"), vmem_limit_bytes=VMEM_LIMIT),
        name="norm_in_proj",
    )(x, g, wt)


def _memory_kv_kernel(x_ref, g_ref, w_ref, gk_ref, k_ref, v_ref, wb_ref):
    @pl.when(pl.program_id(0) == 0)
    def _():
        wb_ref[...] = w_ref[...].astype(bf16)

    x = x_ref[...]
    y = x * lax.rsqrt(jnp.mean(x * x, axis=-1, keepdims=True) + EPS)
    kv = _dot((y * g_ref[...]).astype(bf16), wb_ref[...])
    for h in range(MEM_HEADS):
        kh = kv[:, h * MEM_HD:(h + 1) * MEM_HD]
        kh = kh * lax.rsqrt(jnp.mean(kh * kh, axis=-1, keepdims=True) + EPS)
        k_ref[:, h * MEM_HD:(h + 1) * MEM_HD] = kh * gk_ref[...]
    v_ref[...] = kv[:, GROUP_W:]


def _memory_kv(mem, g, w, gk, l):
    b = mem.shape[0]
    out = jax.ShapeDtypeStruct((b, N_MEM, GROUP_W), f32)
    return pl.pallas_call(
        _memory_kv_kernel,
        grid=(b,),
        in_specs=[
            pl.BlockSpec((None, N_MEM, D_MODEL), lambda i: (i, 0, 0)),
            pl.BlockSpec((None, 1, D_MODEL), lambda i: (l, 0, 0)),
            pl.BlockSpec((None, D_MODEL, 2 * GROUP_W), lambda i: (l, 0, 0)),
            pl.BlockSpec((None, 1, MEM_HD), lambda i: (l, 0, 0)),
        ],
        out_specs=[pl.BlockSpec((None, N_MEM, GROUP_W), lambda i: (i, 0, 0))] * 2,
        out_shape=[out, out],
        scratch_shapes=[pltpu.VMEM((D_MODEL, 2 * GROUP_W), bf16)],
        compiler_params=pltpu.CompilerParams(
            dimension_semantics=("arbitrary",), vmem_limit_bytes=VMEM_LIMIT),
        name="memory_kv",
    )(mem, g, w, gk)


CONV_PAD = 8
N_STACK_SLOTS = 2 + 2 * SWA_KV_HEADS
N_SEQ_IN_PROMPT, N_SEQ_IN_DECODE, N_PARAMS, N_OUT = 3, 7, 9, 5


def _mixer_kernel(*refs, tile, decode, layer, bb):
    n_seq = N_SEQ_IN_DECODE if decode else N_SEQ_IN_PROMPT
    seq_in = refs[:n_seq]
    params = refs[n_seq:n_seq + N_PARAMS]
    outs = refs[n_seq + N_PARAMS:n_seq + N_PARAMS + N_OUT]
    scratch = refs[n_seq + N_PARAMS + N_OUT:]
    for s in range(bb):
        _mixer_seq([r.at[s] for r in seq_in], params, [r.at[s] for r in outs],
                   [r.at[s] for r in scratch], tile=tile, decode=decode, layer=layer)


def _mixer_seq(seq_in, params, outs, scratch, *, tile, decode, layer):
    if decode:
        p_ref, mk_ref, mv_ref, conv_in_ref, gla_in_ref, kc_ref, vc_ref = seq_in
    else:
        p_ref, mk_ref, mv_ref = seq_in
    convw_ref, wup_ref, bga_ref, ggo_ref, gsq_ref, gsk_ref, bd_ref, sinks_ref, gmq_ref = params
    mix_ref, conv_out_ref, gla_out_ref, kbuf_ref, vbuf_ref = outs
    ext_ref, s_ref, kprev_ref, vprev_ref, tail_ref, stk_ref = scratch

    T = tile
    t = pl.program_id(1)
    last = t == pl.num_programs(1) - 1

    @pl.when(t == 0)
    def _():
        ext_ref[0:CONV_PAD, :] = jnp.zeros((CONV_PAD, GROUP_W), f32)
        if decode:
            ext_ref[CONV_PAD - (CONV_W - 1):CONV_PAD, :] = conv_in_ref[...]
            s_ref[...] = gla_in_ref[...]
            kprev_ref[...] = kc_ref[...]
            vprev_ref[...] = vc_ref[...]
        else:
            s_ref[...] = jnp.zeros_like(s_ref)
            kprev_ref[...] = jnp.zeros_like(kprev_ref)
            vprev_ref[...] = jnp.zeros_like(vprev_ref)

    tail_ref[...] = p_ref[:, OFF_GZ:D_IN]

    def tail(rows, off, width):
        return tail_ref[rows, off - OFF_GZ:off - OFF_GZ + width]

    def stack_rows(pieces, slot):
        r, w = pieces[0].shape
        if r % 8 == 0:
            return jnp.concatenate(pieces, axis=0)
        for j, piece in enumerate(pieces):
            stk_ref[slot, j * r:(j + 1) * r, 0:w] = piece
        return stk_ref[slot, 0:len(pieces) * r, 0:w]

    def unstack_rows(x, n, slot):
        r, w = x.shape[0] // n, x.shape[1]
        if r % 8 == 0:
            return [x[j * r:(j + 1) * r] for j in range(n)]
        stk_ref[slot, 0:n * r, 0:w] = x
        return [stk_ref[slot, j * r:(j + 1) * r, 0:w] for j in range(n)]

    all_rows = slice(None)
    row = lax.broadcasted_iota(jnp.int32, (T, T), 0)
    col = lax.broadcasted_iota(jnp.int32, (T, T), 1)

    u = p_ref[:, OFF_AC:OFF_AC + GROUP_W] * p_ref[:, OFF_AH:OFF_AH + GROUP_W]
    ext_ref[CONV_PAD:CONV_PAD + T, :] = u
    conv = (convw_ref[0:1, :] * ext_ref[CONV_PAD - 2:CONV_PAD - 2 + T, :]
            + convw_ref[1:2, :] * ext_ref[CONV_PAD - 1:CONV_PAD - 1 + T, :]
            + convw_ref[2:3, :] * u)
    mix_ref[:, 0:GROUP_W] = (p_ref[:, OFF_AB:OFF_AB + GROUP_W] * conv
                             * _silu(p_ref[:, OFF_AZ:OFF_AZ + GROUP_W])).astype(mix_ref.dtype)
    conv_state = ext_ref[CONV_PAD + T - 2:CONV_PAD + T, :]
    ext_ref[CONV_PAD - 2:CONV_PAD, :] = conv_state

    @pl.when(last)
    def _():
        conv_out_ref[...] = conv_state

    C = min(GLA_CHUNK, T)
    n_chunk = T // C
    g_k = p_ref[:, OFF_GK:OFF_GK + GLA_HEADS * GLA_DK]
    g_a = p_ref[:, OFF_GA:OFF_GA + GLA_RANK].astype(bf16)
    log_a = _log_sigmoid(_dot(g_a, wup_ref[...].astype(bf16)) + bga_ref[...]) * (1.0 / GLA_TAU)
    la3 = _split3(log_a)
    tril = jnp.where((row // C == col // C) & (row >= col), 1.0, 0.0).astype(bf16)
    in_chunk = jnp.where(lax.broadcasted_iota(jnp.int32, (T, LANES), 0) // C
                         == lax.broadcasted_iota(jnp.int32, (T, LANES), 1), 1.0, 0.0).astype(bf16)
    cum = _dot(tril, la3[0]) + _dot(tril, la3[1]) + _dot(tril, la3[2])
    tot_t = (_dot_tn(la3[0], in_chunk) + _dot_tn(la3[1], in_chunk)
             + _dot_tn(la3[2], in_chunk))
    decay_t = jnp.exp(tot_t)
    q_dec = (p_ref[:, OFF_GQ:OFF_GQ + GLA_HEADS * GLA_DK] * (GLA_DK ** -0.5)) * jnp.exp(cum)
    k_dec = g_k * jnp.exp(-cum)
    k_tail = jnp.concatenate(
        [g_k[c * C:(c + 1) * C] * jnp.exp(cum[(c + 1) * C - 1:(c + 1) * C] - cum[c * C:(c + 1) * C])
         for c in range(n_chunk)], axis=0) if n_chunk > 1 else g_k * jnp.exp(cum[T - 1:T] - cum)
    causal = (row // C == col // C) & (row >= col)
    for h in range(GLA_HEADS):
        ks = slice(h * GLA_DK, (h + 1) * GLA_DK)
        vs = slice(OFF_GV + h * GLA_DV, OFF_GV + (h + 1) * GLA_DV)
        qd = q_dec[:, ks].astype(bf16)
        kd = k_dec[:, ks].astype(bf16)
        kt = k_tail[:, ks].astype(bf16)
        v_h = p_ref[:, vs].astype(bf16)
        attn = jnp.where(causal, _dot_nt(qd, kd), 0.0).astype(bf16)
        o_intra = _dot(attn, v_h)
        s_h = s_ref[h]
        o_chunks = []
        for c in range(n_chunk):
            rs = slice(c * C, (c + 1) * C)
            o_chunks.append(o_intra[rs] + _dot(qd[rs], s_h.astype(bf16)))
            s_h = decay_t[ks, c:c + 1] * s_h + _dot_tn(kt[rs], v_h[rs])
        s_ref[h] = s_h
        o_h = jnp.concatenate(o_chunks, axis=0) if n_chunk > 1 else o_chunks[0]
        o_h = o_h * lax.rsqrt(jnp.mean(o_h * o_h, axis=-1, keepdims=True) + EPS) * ggo_ref[...]
        mix_ref[:, GROUP_W + h * GLA_DV:GROUP_W + (h + 1) * GLA_DV] = (
            o_h * _silu(tail(all_rows, OFF_GZ + h * GLA_DV, GLA_DV))).astype(mix_ref.dtype)

    @pl.when(last)
    def _():
        gla_out_ref[...] = s_ref[...]

    def head_norm(x, g, n_lanes):
        sq = x * x
        hi = sq.astype(bf16)
        lo = (sq - hi.astype(f32)).astype(bf16)
        bd = bd_ref[0:n_lanes, 0:n_lanes]
        ms = (_dot(hi, bd) + _dot(lo, bd)) * (1.0 / SWA_HD)
        return x * lax.rsqrt(ms + EPS) * g

    q_n = head_norm(tail(all_rows, OFF_SQ, GROUP_W), gsq_ref[...], GROUP_W)
    k_n = head_norm(tail(all_rows, OFF_SK, LANES), gsk_ref[...], LANES)
    v_n = tail(all_rows, OFF_SV, LANES)

    BQ = min(WINDOW, T)
    n_blk = T // BQ
    stack = SWA_GROUP
    nk = WINDOW + BQ
    qi = lax.broadcasted_iota(jnp.int32, (stack * BQ, nk), 0) % BQ
    kj = lax.broadcasted_iota(jnp.int32, (stack * BQ, nk), 1)
    dist = qi + WINDOW - kj
    band = (dist >= 0) & (dist < WINDOW)
    srow = lax.broadcasted_iota(jnp.int32, (stack * BQ, 1), 0) // BQ
    for blk in range(n_blk):
        rs = slice(blk * BQ, (blk + 1) * BQ)
        if blk == 0:
            k_prev, v_prev = kprev_ref[...], vprev_ref[...]
            valid = band if decode else band & ((kj >= WINDOW) | (t > 0))
        else:
            ps = slice((blk - 1) * BQ, blk * BQ)
            k_prev, v_prev = k_n[ps], v_n[ps]
            valid = band
        k_cat = jnp.concatenate([k_prev, k_n[rs]], axis=0)
        v_cat = jnp.concatenate([v_prev, v_n[rs]], axis=0)
        for g in range(SWA_KV_HEADS):
            kg = k_cat[:, g * SWA_HD:(g + 1) * SWA_HD].astype(bf16)
            vg = v_cat[:, g * SWA_HD:(g + 1) * SWA_HD].astype(bf16)
            heads = [g * SWA_GROUP + j for j in range(stack)]
            qg = stack_rows([q_n[rs, hd * SWA_HD:(hd + 1) * SWA_HD] for hd in heads],
                            2 + 2 * g).astype(bf16)
            sink = jnp.full((stack * BQ, 1), sinks_ref[layer, heads[0]], f32)
            for j in range(1, stack):
                sink = jnp.where(srow == j, sinks_ref[layer, heads[j]], sink)
            s = _dot_nt(qg, kg) * (SWA_HD ** -0.5)
            s = jnp.where(valid, s, -jnp.inf)
            m = jnp.maximum(jnp.max(s, axis=-1, keepdims=True), sink)
            e = jnp.exp(s - m)
            prob = e / (jnp.sum(e, axis=-1, keepdims=True) + jnp.exp(sink - m))
            o = _dot(prob.astype(bf16), vg)
            for hd, o_hd in zip(heads, unstack_rows(o, stack, 3 + 2 * g)):
                z = tail(rs, OFF_SZ + hd * SWA_HD, SWA_HD)
                mix_ref[rs, 2 * GROUP_W + hd * SWA_HD:2 * GROUP_W + (hd + 1) * SWA_HD] = (
                    o_hd * _silu(z)).astype(mix_ref.dtype)

    if decode:
        kbuf_ref[0:WINDOW - T, :] = kc_ref[T:WINDOW, :]
        kbuf_ref[WINDOW - T:WINDOW, :] = k_n
        vbuf_ref[0:WINDOW - T, :] = vc_ref[T:WINDOW, :]
        vbuf_ref[WINDOW - T:WINDOW, :] = v_n
    else:
        kprev_ref[...] = k_n[T - WINDOW:T]
        vprev_ref[...] = v_n[T - WINDOW:T]

        @pl.when(last)
        def _():
            kbuf_ref[...] = k_n[T - WINDOW:T]
            vbuf_ref[...] = v_n[T - WINDOW:T]

    def mem_q(h):
        qh = tail(all_rows, OFF_MQ + h * MEM_HD, MEM_HD)
        return qh * lax.rsqrt(jnp.mean(qh * qh, axis=-1, keepdims=True) + EPS) * gmq_ref[...]

    def softmax(s):
        e = jnp.exp(s - jnp.max(s, axis=-1, keepdims=True))
        return e / jnp.sum(e, axis=-1, keepdims=True)

    if decode:
        qs = stack_rows([mem_q(h) for h in range(MEM_HEADS)], 0).astype(bf16)
        s = _dot_nt(qs, mk_ref[...].astype(bf16)) * (MEM_HD ** -0.5)
        shape = (MEM_HEADS * T, MEM_HEADS * N_MEM)
        same_head = (lax.broadcasted_iota(jnp.int32, shape, 0) // T
                     == lax.broadcasted_iota(jnp.int32, shape, 1) % MEM_HEADS)
        prob = softmax(jnp.where(same_head, s, -jnp.inf))
        o_all = unstack_rows(_dot(prob.astype(bf16), mv_ref[...].astype(bf16)), MEM_HEADS, 1)
    else:
        o_all = []
        for h in range(MEM_HEADS):
            hs = slice(h * MEM_HD, (h + 1) * MEM_HD)
            s = _dot_nt(mem_q(h).astype(bf16), mk_ref[:, hs].astype(bf16)) * (MEM_HD ** -0.5)
            o_all.append(_dot(softmax(s).astype(bf16), mv_ref[:, hs].astype(bf16)))
    for h in range(MEM_HEADS):
        mix_ref[:, 3 * GROUP_W + h * MEM_HD:3 * GROUP_W + (h + 1) * MEM_HD] = (
            o_all[h] * _silu(tail(all_rows, OFF_MZ + h * MEM_HD, MEM_HD))).astype(mix_ref.dtype)


def _mixer(proj, mem_k, mem_v, mem_layer, state, params, layer, tile, bb, decode):
    b, L, _ = proj.shape
    nt = L // tile
    conv_w, w_up, b_ga, g_go, g_sq, g_sk, bd, sinks, g_mq = params

    def tok(width):
        return pl.BlockSpec((bb, tile, width), lambda i, t: (i, t, 0))

    def per_seq(*shape):
        return pl.BlockSpec((bb,) + shape, lambda i, t: (i,) + (0,) * len(shape))

    def per_seq_at(lyr, *shape):
        return pl.BlockSpec((None, bb) + shape, lambda i, t: (lyr, i) + (0,) * len(shape))

    def param(a):
        return pl.BlockSpec((None,) + a.shape[1:], lambda i, t: (layer,) + (0,) * (a.ndim - 1))

    kv_w = SWA_KV_HEADS * SWA_HD
    state_shapes = [(CONV_W - 1, GROUP_W), (GLA_HEADS, GLA_DK, GLA_DV), (WINDOW, kv_w),
                    (WINDOW, kv_w)]
    in_specs = [tok(D_IN), per_seq_at(mem_layer, *mem_k.shape[2:]),
                per_seq_at(mem_layer, *mem_v.shape[2:])]
    args = [proj, mem_k, mem_v]
    if decode:
        in_specs += [per_seq_at(layer, *s) for s in state_shapes]
        args += list(state)
    in_specs += [param(conv_w), param(w_up), param(b_ga), param(g_go), param(g_sq), param(g_sk),
                 pl.BlockSpec(bd.shape, lambda i, t: (0, 0)),
                 pl.BlockSpec(memory_space=pltpu.SMEM), param(g_mq)]
    args += [conv_w, w_up, b_ga, g_go, g_sq, g_sk, bd, sinks, g_mq]
    mix_dtype = f32 if decode else bf16
    out_shape = [jax.ShapeDtypeStruct((b, L, 4 * GROUP_W), mix_dtype)]
    out_shape += [jax.ShapeDtypeStruct((b,) + s, f32) for s in state_shapes]
    out_specs = [tok(4 * GROUP_W)] + [per_seq(*s) for s in state_shapes]
    return pl.pallas_call(
        functools.partial(_mixer_kernel, tile=tile, decode=decode, layer=layer, bb=bb),
        grid=(b // bb, nt),
        in_specs=in_specs,
        out_specs=out_specs,
        out_shape=out_shape,
        scratch_shapes=[
            pltpu.VMEM((bb, CONV_PAD + tile, GROUP_W), f32),
            pltpu.VMEM((bb, GLA_HEADS, GLA_DK, GLA_DV), f32),
            pltpu.VMEM((bb, WINDOW, kv_w), f32),
            pltpu.VMEM((bb, WINDOW, kv_w), f32),
            pltpu.VMEM((bb, tile, D_IN - OFF_GZ), f32),
            pltpu.VMEM((bb, N_STACK_SLOTS, MEM_HEADS * min(tile, 8), LANES), f32),
        ],
        compiler_params=pltpu.CompilerParams(
            dimension_semantics=("arbitrary", "arbitrary"), vmem_limit_bytes=VMEM_LIMIT),
        name="mixer_decode" if decode else "mixer_prompt",
    )(*args)


def _out_proj_kernel(mix_ref, w_ref, x_ref, y_ref, wb_ref):
    @pl.when(pl.program_id(1) == 0)
    def _():
        wb_ref[...] = w_ref[...].astype(bf16)

    y_ref[...] = x_ref[...] + _dot(mix_ref[...].astype(bf16), wb_ref[...])


def _out_proj(mix, w, x, l, tm, tn):
    m, k = mix.shape
    n = w.shape[2]
    return pl.pallas_call(
        _out_proj_kernel,
        grid=(n // tn, m // tm),
        in_specs=[
            pl.BlockSpec((tm, k), lambda j, i: (i, 0)),
            pl.BlockSpec((None, k, tn), lambda j, i: (l, 0, j)),
            pl.BlockSpec((tm, tn), lambda j, i: (i, j)),
        ],
        out_specs=pl.BlockSpec((tm, tn), lambda j, i: (i, j)),
        out_shape=jax.ShapeDtypeStruct((m, n), f32),
        scratch_shapes=[pltpu.VMEM((k, tn), bf16)],
        compiler_params=pltpu.CompilerParams(
            dimension_semantics=("arbitrary", "arbitrary"), vmem_limit_bytes=VMEM_LIMIT),
        name="out_proj",
    )(mix, w, x)


PROMPT_TILE = 256
DECODE_SEQS_PER_STEP = 8
PROJ_TM, PROJ_TN = 1024, 768
OUT_TM, OUT_TN = 1024, 1024

_LANE = np.arange(GROUP_W)
HEAD_BLOCK_DIAG = _LANE[:, None] // SWA_HD == _LANE[None, :] // SWA_HD


def kernel(x_prompt, x_sample, mem_prompt, state_conv, state_gla, cache_swa_k, cache_swa_v,
           cache_mem_k, cache_mem_v, g_norm, w_in, conv_w, w_gla_a_up, b_gla_a, g_gla_o,
           g_swa_q, g_swa_k, swa_sinks, g_mem, w_mem_kv, g_mem_q, g_mem_k, w_out):
    depth = w_in.shape[0]
    bp, lp, _ = x_prompt.shape
    bs, ls, _ = x_sample.shape
    hp = x_prompt.reshape(bp * lp, D_MODEL)
    hs = x_sample.reshape(bs * ls, D_MODEL)

    def row(a):
        return a[:, None, :]

    params = (conv_w, w_gla_a_up, row(b_gla_a), row(g_gla_o),
              row(jnp.tile(g_swa_q, (1, SWA_HEADS))), row(jnp.tile(g_swa_k, (1, SWA_KV_HEADS))),
              jnp.asarray(HEAD_BLOCK_DIAG, bf16), swa_sinks, row(g_mem_q))
    g_n, g_m, g_mk = row(g_norm), row(g_mem), row(g_mem_k)
    w_in_t = jnp.swapaxes(w_in, 1, 2)
    kv_w = SWA_KV_HEADS * SWA_HD
    state = (state_conv, state_gla, cache_swa_k.reshape(depth, bs, WINDOW, kv_w),
             cache_swa_v.reshape(depth, bs, WINDOW, kv_w))
    mem_k_s = cache_mem_k.reshape(depth, bs, N_MEM * MEM_HEADS, MEM_HD)
    mem_v_s = cache_mem_v.reshape(depth, bs, N_MEM * MEM_HEADS, MEM_HD)

    outs = [[] for _ in range(10)]
    for l in range(depth):
        mk, mv = _memory_kv(mem_prompt, g_m, w_mem_kv, g_mk, l)
        proj = _norm_matmul(hp, g_n, w_in_t, l, PROJ_TM, PROJ_TN).reshape(bp, lp, D_IN)
        mix, c, s, kb, vb = _mixer(proj, mk[None], mv[None], 0, None, params, l, PROMPT_TILE, 1,
                                   decode=False)
        hp = _out_proj(mix.reshape(bp * lp, 4 * GROUP_W), w_out, hp, l, OUT_TM, OUT_TN)
        for lst, a in zip(outs[:6], (
                c, s, kb.reshape(bp, WINDOW, SWA_KV_HEADS, SWA_HD),
                vb.reshape(bp, WINDOW, SWA_KV_HEADS, SWA_HD),
                mk.reshape(bp, N_MEM, MEM_HEADS, MEM_HD), mv.reshape(bp, N_MEM, MEM_HEADS, MEM_HD))):
            lst.append(a)

        proj = _norm_matmul(hs, g_n, w_in_t, l, bs * ls, PROJ_TN).reshape(bs, ls, D_IN)
        mix, c, s, kb, vb = _mixer(proj, mem_k_s, mem_v_s, l, state, params, l, ls,
                                   DECODE_SEQS_PER_STEP, decode=True)
        hs = _out_proj(mix.reshape(bs * ls, 4 * GROUP_W), w_out, hs, l, bs * ls, OUT_TN)
        for lst, a in zip(outs[6:], (
                c, s, kb.reshape(bs, WINDOW, SWA_KV_HEADS, SWA_HD),
                vb.reshape(bs, WINDOW, SWA_KV_HEADS, SWA_HD))):
            lst.append(a)

    return (hp.reshape(bp, lp, D_MODEL), hs.reshape(bs, ls, D_MODEL),
            *[jnp.stack(o) for o in outs])
```

```python
import functools

import jax
import jax.numpy as jnp
import numpy as np
from jax import lax
from jax.experimental import pallas as pl
from jax.experimental.pallas import tpu as pltpu

f32 = jnp.float32
bf16 = jnp.bfloat16

D_MODEL = 2048
GROUP_W = 512
GLA_HEADS = 4
GLA_DK = 64
GLA_DV = 128
GLA_RANK = 16
GLA_TAU = 16.0
GLA_CHUNK = 64
SWA_HEADS = 8
SWA_KV_HEADS = 2
SWA_HD = 64
SWA_GROUP = SWA_HEADS // SWA_KV_HEADS
WINDOW = 128
N_MEM = 256
MEM_HEADS = 4
MEM_HD = 128
CONV_W = 3
EPS = 1e-6

LANES = 128

D_IN = 5904
OFF_AB, OFF_AC, OFF_AH, OFF_AZ = 0, 512, 1024, 1536
OFF_GQ, OFF_GK, OFF_GV, OFF_GA, OFF_GZ = 2048, 2304, 2560, 3072, 3088
OFF_SQ, OFF_SK, OFF_SV, OFF_SZ = 3600, 4112, 4240, 4368
OFF_MQ, OFF_MZ = 4880, 5392

VMEM_LIMIT = 52 * 1024 * 1024


def _dot(a, b):
    return jnp.dot(a, b, preferred_element_type=f32)


def _dot_nt(a, b):
    return lax.dot_general(a, b, (((1,), (1,)), ((), ())), preferred_element_type=f32)


def _dot_tn(a, b):
    return lax.dot_general(a, b, (((0,), (0,)), ((), ())), preferred_element_type=f32)


def _split3(x):
    hi = x.astype(bf16)
    r = x - hi.astype(f32)
    mid = r.astype(bf16)
    lo = (r - mid.astype(f32)).astype(bf16)
    return hi, mid, lo


def _silu(x):
    return x * jax.nn.sigmoid(x)


def _log_sigmoid(x):
    return jnp.minimum(x, 0.0) - jnp.log1p(jnp.exp(-jnp.abs(x)))


def _norm_matmul_kernel(x_ref, g_ref, wt_ref, o_ref, hn_ref):
    @pl.when(pl.program_id(1) == 0)
    def _():
        x = x_ref[...]
        y = x * lax.rsqrt(jnp.mean(x * x, axis=-1, keepdims=True) + EPS)
        hn_ref[...] = (y * g_ref[...]).astype(bf16)

    o_ref[...] = _dot_nt(hn_ref[...], wt_ref[...].astype(bf16))


def _norm_matmul(x, g, wt, l, tm, tn):
    m, k = x.shape
    n = wt.shape[1]
    return pl.pallas_call(
        _norm_matmul_kernel,
        grid=(m // tm, pl.cdiv(n, tn)),
        in_specs=[
            pl.BlockSpec((tm, k), lambda i, j: (i, 0)),
            pl.BlockSpec((None, 1, k), lambda i, j: (l, 0, 0)),
            pl.BlockSpec((None, tn, k), lambda i, j: (l, j, 0)),
        ],
        out_specs=pl.BlockSpec((tm, tn), lambda i, j: (i, j)),
        out_shape=jax.ShapeDtypeStruct((m, n), f32),
        scratch_shapes=[pltpu.VMEM((tm, k), bf16)],
        compiler_params=pltpu.CompilerParams(
            dimension_semantics=("arbitrary", "arbitrary"), vmem_limit_bytes=VMEM_LIMIT),
        name="norm_in_proj",
    )(x, g, wt)


def _memory_kv_kernel(x_ref, g_ref, w_ref, gk_ref, k_ref, v_ref, wb_ref):
    @pl.when(pl.program_id(0) == 0)
    def _():
        wb_ref[...] = w_ref[...].astype(bf16)

    x = x_ref[...]
    y = x * lax.rsqrt(jnp.mean(x * x, axis=-1, keepdims=True) + EPS)
    kv = _dot((y * g_ref[...]).astype(bf16), wb_ref[...])
    for h in range(MEM_HEADS):
        kh = kv[:, h * MEM_HD:(h + 1) * MEM_HD]
        kh = kh * lax.rsqrt(jnp.mean(kh * kh, axis=-1, keepdims=True) + EPS)
        k_ref[:, h * MEM_HD:(h + 1) * MEM_HD] = kh * gk_ref[...]
    v_ref[...] = kv[:, GROUP_W:]


def _memory_kv(mem, g, w, gk, l):
    b = mem.shape[0]
    out = jax.ShapeDtypeStruct((b, N_MEM, GROUP_W), f32)
    return pl.pallas_call(
        _memory_kv_kernel,
        grid=(b,),
        in_specs=[
            pl.BlockSpec((None, N_MEM, D_MODEL), lambda i: (i, 0, 0)),
            pl.BlockSpec((None, 1, D_MODEL), lambda i: (l, 0, 0)),
            pl.BlockSpec((None, D_MODEL, 2 * GROUP_W), lambda i: (l, 0, 0)),
            pl.BlockSpec((None, 1, MEM_HD), lambda i: (l, 0, 0)),
        ],
        out_specs=[pl.BlockSpec((None, N_MEM, GROUP_W), lambda i: (i, 0, 0))] * 2,
        out_shape=[out, out],
        scratch_shapes=[pltpu.VMEM((D_MODEL, 2 * GROUP_W), bf16)],
        compiler_params=pltpu.CompilerParams(
            dimension_semantics=("arbitrary",), vmem_limit_bytes=VMEM_LIMIT),
        name="memory_kv",
    )(mem, g, w, gk)


CONV_PAD = 8
N_STACK_SLOTS = 2 + 2 * SWA_KV_HEADS
N_SEQ_IN_PROMPT, N_SEQ_IN_DECODE, N_PARAMS_PROMPT, N_PARAMS_DECODE, N_OUT = 3, 7, 11, 9, 5


def _mixer_kernel(*refs, tile, decode, layer, bb):
    n_seq = N_SEQ_IN_DECODE if decode else N_SEQ_IN_PROMPT
    n_par = N_PARAMS_DECODE if decode else N_PARAMS_PROMPT
    seq_in = refs[:n_seq]
    params = refs[n_seq:n_seq + n_par]
    outs = refs[n_seq + n_par:n_seq + n_par + N_OUT]
    scratch = refs[n_seq + n_par + N_OUT:]
    for s in range(bb):
        _mixer_seq([r.at[s] for r in seq_in], params, [r.at[s] for r in outs],
                   [r.at[s] for r in scratch], tile=tile, decode=decode, layer=layer)


def _mixer_seq(seq_in, params, outs, scratch, *, tile, decode, layer):
    if decode:
        p_ref, mk_ref, mv_ref, conv_in_ref, gla_in_ref, kc_ref, vc_ref = seq_in
    else:
        x_ref, mk_ref, mv_ref = seq_in
        gn_ref, wt_ref = params[:2]
    (convw_ref, wup_ref, bga_ref, ggo_ref, gsq_ref, gsk_ref, bd_ref, sinks_ref,
     gmq_ref) = params[-N_PARAMS_DECODE:]
    mix_ref, conv_out_ref, gla_out_ref, kbuf_ref, vbuf_ref = outs
    ext_ref, s_ref, kprev_ref, vprev_ref, tail_ref, stk_ref = scratch

    T = tile
    t = pl.program_id(1)

    @pl.when(t == 0)
    def _():
        ext_ref[0:CONV_PAD, :] = jnp.zeros((CONV_PAD, GROUP_W), f32)
        if decode:
            ext_ref[CONV_PAD - (CONV_W - 1):CONV_PAD, :] = conv_in_ref[...]
            s_ref[...] = gla_in_ref[...]
            kprev_ref[...] = kc_ref[...]
            vprev_ref[...] = vc_ref[...]
        else:
            s_ref[...] = jnp.zeros_like(s_ref)
            kprev_ref[...] = jnp.zeros_like(kprev_ref)
            vprev_ref[...] = jnp.zeros_like(vprev_ref)

    if decode:
        tail_ref[...] = p_ref[:, OFF_GZ:D_IN]

        def seg(off, width):
            if off < OFF_GZ:
                return p_ref[:, off:off + width]
            return tail_ref[:, off - OFF_GZ:off - OFF_GZ + width]
    else:
        x = x_ref[...]
        hn = (x * lax.rsqrt(jnp.mean(x * x, axis=-1, keepdims=True) + EPS) * gn_ref[...]).astype(bf16)

        def seg(off, width):
            return _dot_nt(hn, wt_ref[off:off + width, :])

    def stack_rows(pieces, slot):
        r, w = pieces[0].shape
        if r % 8 == 0:
            return jnp.concatenate(pieces, axis=0)
        for j, piece in enumerate(pieces):
            stk_ref[slot, j * r:(j + 1) * r, 0:w] = piece
        return stk_ref[slot, 0:len(pieces) * r, 0:w]

    def unstack_rows(x, n, slot):
        r, w = x.shape[0] // n, x.shape[1]
        if r % 8 == 0:
            return [x[j * r:(j + 1) * r] for j in range(n)]
        stk_ref[slot, 0:n * r, 0:w] = x
        return [stk_ref[slot, j * r:(j + 1) * r, 0:w] for j in range(n)]

    row = lax.broadcasted_iota(jnp.int32, (T, T), 0)
    col = lax.broadcasted_iota(jnp.int32, (T, T), 1)

    u = seg(OFF_AC, GROUP_W) * seg(OFF_AH, GROUP_W)
    ext_ref[CONV_PAD:CONV_PAD + T, :] = u
    conv = (convw_ref[0:1, :] * ext_ref[CONV_PAD - 2:CONV_PAD - 2 + T, :]
            + convw_ref[1:2, :] * ext_ref[CONV_PAD - 1:CONV_PAD - 1 + T, :]
            + convw_ref[2:3, :] * u)
    mix_ref[:, 0:GROUP_W] = (seg(OFF_AB, GROUP_W) * conv
                             * _silu(seg(OFF_AZ, GROUP_W))).astype(mix_ref.dtype)
    conv_state = ext_ref[CONV_PAD + T - 2:CONV_PAD + T, :]
    ext_ref[CONV_PAD - 2:CONV_PAD, :] = conv_state
    conv_out_ref[...] = conv_state

    C = min(GLA_CHUNK, T)
    n_chunk = T // C
    g_k = seg(OFF_GK, GLA_HEADS * GLA_DK)
    g_v = seg(OFF_GV, GLA_HEADS * GLA_DV)
    g_z = seg(OFF_GZ, GROUP_W)
    g_a = seg(OFF_GA, GLA_RANK).astype(bf16)
    log_a = _log_sigmoid(_dot(g_a, wup_ref[...].astype(bf16)) + bga_ref[...]) * (1.0 / GLA_TAU)
    la3 = _split3(log_a)
    tril = jnp.where((row // C == col // C) & (row >= col), 1.0, 0.0).astype(bf16)
    in_chunk = jnp.where(lax.broadcasted_iota(jnp.int32, (T, LANES), 0) // C
                         == lax.broadcasted_iota(jnp.int32, (T, LANES), 1), 1.0, 0.0).astype(bf16)
    cum = _dot(tril, la3[0]) + _dot(tril, la3[1]) + _dot(tril, la3[2])
    tot_t = (_dot_tn(la3[0], in_chunk) + _dot_tn(la3[1], in_chunk)
             + _dot_tn(la3[2], in_chunk))
    decay_t = jnp.exp(tot_t)
    q_dec = (seg(OFF_GQ, GLA_HEADS * GLA_DK) * (GLA_DK ** -0.5)) * jnp.exp(cum)
    k_dec = g_k * jnp.exp(-cum)
    k_tail = jnp.concatenate(
        [g_k[c * C:(c + 1) * C] * jnp.exp(cum[(c + 1) * C - 1:(c + 1) * C] - cum[c * C:(c + 1) * C])
         for c in range(n_chunk)], axis=0) if n_chunk > 1 else g_k * jnp.exp(cum[T - 1:T] - cum)
    causal = (row // C == col // C) & (row >= col)
    for h in range(GLA_HEADS):
        ks = slice(h * GLA_DK, (h + 1) * GLA_DK)
        vs = slice(h * GLA_DV, (h + 1) * GLA_DV)
        qd = q_dec[:, ks].astype(bf16)
        kd = k_dec[:, ks].astype(bf16)
        kt = k_tail[:, ks].astype(bf16)
        v_h = g_v[:, vs].astype(bf16)
        attn = jnp.where(causal, _dot_nt(qd, kd), 0.0).astype(bf16)
        o_intra = _dot(attn, v_h)
        s_h = s_ref[h]
        o_chunks = []
        for c in range(n_chunk):
            rs = slice(c * C, (c + 1) * C)
            o_chunks.append(o_intra[rs] + _dot(qd[rs], s_h.astype(bf16)))
            s_h = decay_t[ks, c:c + 1] * s_h + _dot_tn(kt[rs], v_h[rs])
        s_ref[h] = s_h
        o_h = jnp.concatenate(o_chunks, axis=0) if n_chunk > 1 else o_chunks[0]
        o_h = o_h * lax.rsqrt(jnp.mean(o_h * o_h, axis=-1, keepdims=True) + EPS) * ggo_ref[...]
        mix_ref[:, GROUP_W + h * GLA_DV:GROUP_W + (h + 1) * GLA_DV] = (
            o_h * _silu(g_z[:, vs])).astype(mix_ref.dtype)
        gla_out_ref[h] = s_h

    def head_norm(x, g, n_lanes):
        sq = x * x
        hi = sq.astype(bf16)
        lo = (sq - hi.astype(f32)).astype(bf16)
        bd = bd_ref[0:n_lanes, 0:n_lanes]
        ms = (_dot(hi, bd) + _dot(lo, bd)) * (1.0 / SWA_HD)
        return x * lax.rsqrt(ms + EPS) * g

    q_n = head_norm(seg(OFF_SQ, GROUP_W), gsq_ref[...], GROUP_W)
    k_n = head_norm(seg(OFF_SK, LANES), gsk_ref[...], LANES)
    v_n = seg(OFF_SV, LANES)
    s_z = seg(OFF_SZ, GROUP_W)

    BQ = min(WINDOW, T)
    n_blk = T // BQ
    stack = SWA_GROUP
    nk = WINDOW + BQ
    qi = lax.broadcasted_iota(jnp.int32, (stack * BQ, nk), 0) % BQ
    kj = lax.broadcasted_iota(jnp.int32, (stack * BQ, nk), 1)
    dist = qi + WINDOW - kj
    band = (dist >= 0) & (dist < WINDOW)
    srow = lax.broadcasted_iota(jnp.int32, (stack * BQ, 1), 0) // BQ
    for blk in range(n_blk):
        rs = slice(blk * BQ, (blk + 1) * BQ)
        if blk == 0:
            k_prev, v_prev = kprev_ref[...], vprev_ref[...]
            valid = band if decode else band & ((kj >= WINDOW) | (t > 0))
        else:
            ps = slice((blk - 1) * BQ, blk * BQ)
            k_prev, v_prev = k_n[ps], v_n[ps]
            valid = band
        k_cat = jnp.concatenate([k_prev, k_n[rs]], axis=0)
        v_cat = jnp.concatenate([v_prev, v_n[rs]], axis=0)
        for g in range(SWA_KV_HEADS):
            kg = k_cat[:, g * SWA_HD:(g + 1) * SWA_HD].astype(bf16)
            vg = v_cat[:, g * SWA_HD:(g + 1) * SWA_HD].astype(bf16)
            heads = [g * SWA_GROUP + j for j in range(stack)]
            qg = stack_rows([q_n[rs, hd * SWA_HD:(hd + 1) * SWA_HD] for hd in heads],
                            2 + 2 * g).astype(bf16)
            sink = jnp.full((stack * BQ, 1), sinks_ref[layer, heads[0]], f32)
            for j in range(1, stack):
                sink = jnp.where(srow == j, sinks_ref[layer, heads[j]], sink)
            s = _dot_nt(qg, kg) * (SWA_HD ** -0.5)
            s = jnp.where(valid, s, -jnp.inf)
            m = jnp.maximum(jnp.max(s, axis=-1, keepdims=True), sink)
            e = jnp.exp(s - m)
            prob = e / (jnp.sum(e, axis=-1, keepdims=True) + jnp.exp(sink - m))
            o = _dot(prob.astype(bf16), vg)
            for hd, o_hd in zip(heads, unstack_rows(o, stack, 3 + 2 * g)):
                z = s_z[rs, hd * SWA_HD:(hd + 1) * SWA_HD]
                mix_ref[rs, 2 * GROUP_W + hd * SWA_HD:2 * GROUP_W + (hd + 1) * SWA_HD] = (
                    o_hd * _silu(z)).astype(mix_ref.dtype)

    if decode:
        kbuf_ref[0:WINDOW - T, :] = kc_ref[T:WINDOW, :]
        kbuf_ref[WINDOW - T:WINDOW, :] = k_n
        vbuf_ref[0:WINDOW - T, :] = vc_ref[T:WINDOW, :]
        vbuf_ref[WINDOW - T:WINDOW, :] = v_n
    else:
        kprev_ref[...] = k_n[T - WINDOW:T]
        vprev_ref[...] = v_n[T - WINDOW:T]
        kbuf_ref[...] = k_n[T - WINDOW:T]
        vbuf_ref[...] = v_n[T - WINDOW:T]

    m_q = seg(OFF_MQ, GROUP_W)
    m_z = seg(OFF_MZ, GROUP_W)

    def mem_q(h):
        qh = m_q[:, h * MEM_HD:(h + 1) * MEM_HD]
        return qh * lax.rsqrt(jnp.mean(qh * qh, axis=-1, keepdims=True) + EPS) * gmq_ref[...]

    def softmax(s):
        e = jnp.exp(s - jnp.max(s, axis=-1, keepdims=True))
        return e / jnp.sum(e, axis=-1, keepdims=True)

    if decode:
        qs = stack_rows([mem_q(h) for h in range(MEM_HEADS)], 0).astype(bf16)
        s = _dot_nt(qs, mk_ref[...].astype(bf16)) * (MEM_HD ** -0.5)
        shape = (MEM_HEADS * T, MEM_HEADS * N_MEM)
        same_head = (lax.broadcasted_iota(jnp.int32, shape, 0) // T
                     == lax.broadcasted_iota(jnp.int32, shape, 1) % MEM_HEADS)
        prob = softmax(jnp.where(same_head, s, -jnp.inf))
        o_all = unstack_rows(_dot(prob.astype(bf16), mv_ref[...].astype(bf16)), MEM_HEADS, 1)
    else:
        o_all = []
        for h in range(MEM_HEADS):
            hs = slice(h * MEM_HD, (h + 1) * MEM_HD)
            s = _dot_nt(mem_q(h).astype(bf16), mk_ref[:, hs].astype(bf16)) * (MEM_HD ** -0.5)
            o_all.append(_dot(softmax(s).astype(bf16), mv_ref[:, hs].astype(bf16)))
    for h in range(MEM_HEADS):
        mix_ref[:, 3 * GROUP_W + h * MEM_HD:3 * GROUP_W + (h + 1) * MEM_HD] = (
            o_all[h] * _silu(m_z[:, h * MEM_HD:(h + 1) * MEM_HD])).astype(mix_ref.dtype)


def _mixer(tokens, norm_w, mem_k, mem_v, mem_layer, state, params, layer, tile, bb, decode):
    b, L, width = tokens.shape
    nt = L // tile
    conv_w, w_up, b_ga, g_go, g_sq, g_sk, bd, sinks, g_mq = params

    def tok(width):
        return pl.BlockSpec((bb, tile, width), lambda i, t: (i, t, 0))

    def per_seq(*shape):
        return pl.BlockSpec((bb,) + shape, lambda i, t: (i,) + (0,) * len(shape))

    def per_seq_at(lyr, *shape):
        return pl.BlockSpec((None, bb) + shape, lambda i, t: (lyr, i) + (0,) * len(shape))

    def param(a):
        return pl.BlockSpec((None,) + a.shape[1:], lambda i, t: (layer,) + (0,) * (a.ndim - 1))

    kv_w = SWA_KV_HEADS * SWA_HD
    state_shapes = [(CONV_W - 1, GROUP_W), (GLA_HEADS, GLA_DK, GLA_DV), (WINDOW, kv_w),
                    (WINDOW, kv_w)]
    in_specs = [tok(width), per_seq_at(mem_layer, *mem_k.shape[2:]),
                per_seq_at(mem_layer, *mem_v.shape[2:])]
    args = [tokens, mem_k, mem_v]
    if decode:
        in_specs += [per_seq_at(layer, *s) for s in state_shapes]
        args += list(state)
    else:
        g_n, w_t = norm_w
        in_specs += [param(g_n), pl.BlockSpec((None,) + w_t.shape[1:], lambda i, t: (layer, 0, 0),
                                              pipeline_mode=pl.Buffered(1))]
        args += [g_n, w_t]
    in_specs += [param(conv_w), param(w_up), param(b_ga), param(g_go), param(g_sq), param(g_sk),
                 pl.BlockSpec(bd.shape, lambda i, t: (0, 0)),
                 pl.BlockSpec(memory_space=pltpu.SMEM), param(g_mq)]
    args += [conv_w, w_up, b_ga, g_go, g_sq, g_sk, bd, sinks, g_mq]
    mix_dtype = f32 if decode else bf16
    out_shape = [jax.ShapeDtypeStruct((b, L, 4 * GROUP_W), mix_dtype)]
    out_shape += [jax.ShapeDtypeStruct((b,) + s, f32) for s in state_shapes]
    out_specs = [tok(4 * GROUP_W)] + [per_seq(*s) for s in state_shapes]
    return pl.pallas_call(
        functools.partial(_mixer_kernel, tile=tile, decode=decode, layer=layer, bb=bb),
        grid=(b // bb, nt),
        in_specs=in_specs,
        out_specs=out_specs,
        out_shape=out_shape,
        scratch_shapes=[
            pltpu.VMEM((bb, CONV_PAD + tile, GROUP_W), f32),
            pltpu.VMEM((bb, GLA_HEADS, GLA_DK, GLA_DV), f32),
            pltpu.VMEM((bb, WINDOW, kv_w), f32),
            pltpu.VMEM((bb, WINDOW, kv_w), f32),
            pltpu.VMEM((bb, tile if decode else 8, D_IN - OFF_GZ), f32),
            pltpu.VMEM((bb, N_STACK_SLOTS, MEM_HEADS * min(tile, 8), LANES), f32),
        ],
        compiler_params=pltpu.CompilerParams(
            dimension_semantics=("arbitrary", "arbitrary"), vmem_limit_bytes=VMEM_LIMIT),
        name="mixer_decode" if decode else "mixer_prompt",
    )(*args)


def _out_proj_kernel(mix_ref, w_ref, x_ref, y_ref, wb_ref):
    @pl.when(pl.program_id(1) == 0)
    def _():
        wb_ref[...] = w_ref[...].astype(bf16)

    y_ref[...] = x_ref[...] + _dot(mix_ref[...].astype(bf16), wb_ref[...])


def _out_proj(mix, w, x, l, tm, tn):
    m, k = mix.shape
    n = w.shape[2]
    return pl.pallas_call(
        _out_proj_kernel,
        grid=(n // tn, m // tm),
        in_specs=[
            pl.BlockSpec((tm, k), lambda j, i: (i, 0)),
            pl.BlockSpec((None, k, tn), lambda j, i: (l, 0, j)),
            pl.BlockSpec((tm, tn), lambda j, i: (i, j)),
        ],
        out_specs=pl.BlockSpec((tm, tn), lambda j, i: (i, j)),
        out_shape=jax.ShapeDtypeStruct((m, n), f32),
        scratch_shapes=[pltpu.VMEM((k, tn), bf16)],
        compiler_params=pltpu.CompilerParams(
            dimension_semantics=("arbitrary", "arbitrary"), vmem_limit_bytes=VMEM_LIMIT),
        name="out_proj",
    )(mix, w, x)


PROMPT_TILE = 256
DECODE_SEQS_PER_STEP = 8
PROJ_TN = 768
OUT_TM, OUT_TN = 1024, 1024

_LANE = np.arange(GROUP_W)
HEAD_BLOCK_DIAG = _LANE[:, None] // SWA_HD == _LANE[None, :] // SWA_HD


def kernel(x_prompt, x_sample, mem_prompt, state_conv, state_gla, cache_swa_k, cache_swa_v,
           cache_mem_k, cache_mem_v, g_norm, w_in, conv_w, w_gla_a_up, b_gla_a, g_gla_o,
           g_swa_q, g_swa_k, swa_sinks, g_mem, w_mem_kv, g_mem_q, g_mem_k, w_out):
    depth = w_in.shape[0]
    bp, lp, _ = x_prompt.shape
    bs, ls, _ = x_sample.shape
    hp = x_prompt.reshape(bp * lp, D_MODEL)
    hs = x_sample.reshape(bs * ls, D_MODEL)

    def row(a):
        return a[:, None, :]

    params = (conv_w, w_gla_a_up, row(b_gla_a), row(g_gla_o),
              row(jnp.tile(g_swa_q, (1, SWA_HEADS))), row(jnp.tile(g_swa_k, (1, SWA_KV_HEADS))),
              jnp.asarray(HEAD_BLOCK_DIAG, bf16), swa_sinks, row(g_mem_q))
    g_n, g_m, g_mk = row(g_norm), row(g_mem), row(g_mem_k)
    w_in_t = jnp.swapaxes(w_in, 1, 2).astype(bf16)
    kv_w = SWA_KV_HEADS * SWA_HD
    state = (state_conv, state_gla, cache_swa_k.reshape(depth, bs, WINDOW, kv_w),
             cache_swa_v.reshape(depth, bs, WINDOW, kv_w))
    mem_k_s = cache_mem_k.reshape(depth, bs, N_MEM * MEM_HEADS, MEM_HD)
    mem_v_s = cache_mem_v.reshape(depth, bs, N_MEM * MEM_HEADS, MEM_HD)

    outs = [[] for _ in range(10)]
    for l in range(depth):
        mk, mv = _memory_kv(mem_prompt, g_m, w_mem_kv, g_mk, l)
        mix, c, s, kb, vb = _mixer(hp.reshape(bp, lp, D_MODEL), (g_n, w_in_t), mk[None], mv[None], 0,
                                   None, params, l, PROMPT_TILE, 1, decode=False)
        hp = _out_proj(mix.reshape(bp * lp, 4 * GROUP_W), w_out, hp, l, OUT_TM, OUT_TN)
        for lst, a in zip(outs[:6], (
                c, s, kb.reshape(bp, WINDOW, SWA_KV_HEADS, SWA_HD),
                vb.reshape(bp, WINDOW, SWA_KV_HEADS, SWA_HD),
                mk.reshape(bp, N_MEM, MEM_HEADS, MEM_HD), mv.reshape(bp, N_MEM, MEM_HEADS, MEM_HD))):
            lst.append(a)

        proj = _norm_matmul(hs, g_n, w_in_t, l, bs * ls, PROJ_TN).reshape(bs, ls, D_IN)
        mix, c, s, kb, vb = _mixer(proj, None, mem_k_s, mem_v_s, l, state, params, l, ls,
                                   DECODE_SEQS_PER_STEP, decode=True)
        hs = _out_proj(mix.reshape(bs * ls, 4 * GROUP_W), w_out, hs, l, bs * ls, OUT_TN)
        for lst, a in zip(outs[6:], (
                c, s, kb.reshape(bs, WINDOW, SWA_KV_HEADS, SWA_HD),
                vb.reshape(bs, WINDOW, SWA_KV_HEADS, SWA_HD))):
            lst.append(a)

    return (hp.reshape(bp, lp, D_MODEL), hs.reshape(bs, ls, D_MODEL),
            *[jnp.stack(o) for o in outs])
```

```python
import functools

import jax
import jax.numpy as jnp
import numpy as np
from jax import lax
from jax.experimental import pallas as pl
from jax.experimental.pallas import tpu as pltpu

f32 = jnp.float32
bf16 = jnp.bfloat16

D_MODEL = 2048
GROUP_W = 512
GLA_HEADS = 4
GLA_DK = 64
GLA_DV = 128
GLA_RANK = 16
GLA_TAU = 16.0
GLA_CHUNK = 64
SWA_HEADS = 8
SWA_KV_HEADS = 2
SWA_HD = 64
SWA_GROUP = SWA_HEADS // SWA_KV_HEADS
WINDOW = 128
N_MEM = 256
MEM_HEADS = 4
MEM_HD = 128
CONV_W = 3
EPS = 1e-6

LANES = 128

D_IN = 5904
OFF_AB, OFF_AC, OFF_AH, OFF_AZ = 0, 512, 1024, 1536
OFF_GQ, OFF_GK, OFF_GV, OFF_GA, OFF_GZ = 2048, 2304, 2560, 3072, 3088
OFF_SQ, OFF_SK, OFF_SV, OFF_SZ = 3600, 4112, 4240, 4368
OFF_MQ, OFF_MZ = 4880, 5392

VMEM_LIMIT = 56 * 1024 * 1024


def _dot(a, b):
    return jnp.dot(a, b, preferred_element_type=f32)


def _dot_nt(a, b):
    return lax.dot_general(a, b, (((1,), (1,)), ((), ())), preferred_element_type=f32)


def _dot_tn(a, b):
    return lax.dot_general(a, b, (((0,), (0,)), ((), ())), preferred_element_type=f32)


def _split3(x):
    hi = x.astype(bf16)
    r = x - hi.astype(f32)
    mid = r.astype(bf16)
    lo = (r - mid.astype(f32)).astype(bf16)
    return hi, mid, lo


def _silu(x):
    return x * jax.nn.sigmoid(x)


def _log_sigmoid(x):
    return jnp.minimum(x, 0.0) - jnp.log1p(jnp.exp(-jnp.abs(x)))


def _norm_matmul_kernel(x_ref, g_ref, wt_ref, o_ref, hn_ref):
    @pl.when(pl.program_id(1) == 0)
    def _():
        x = x_ref[...]
        y = x * lax.rsqrt(jnp.mean(x * x, axis=-1, keepdims=True) + EPS)
        hn_ref[...] = (y * g_ref[...]).astype(bf16)

    o_ref[...] = _dot_nt(hn_ref[...], wt_ref[...].astype(bf16))


def _norm_matmul(x, g, wt, l, tm, tn):
    m, k = x.shape
    n = wt.shape[1]
    return pl.pallas_call(
        _norm_matmul_kernel,
        grid=(m // tm, pl.cdiv(n, tn)),
        in_specs=[
            pl.BlockSpec((tm, k), lambda i, j: (i, 0)),
            pl.BlockSpec((None, 1, k), lambda i, j: (l, 0, 0)),
            pl.BlockSpec((None, tn, k), lambda i, j: (l, j, 0)),
        ],
        out_specs=pl.BlockSpec((tm, tn), lambda i, j: (i, j)),
        out_shape=jax.ShapeDtypeStruct((m, n), f32),
        scratch_shapes=[pltpu.VMEM((tm, k), bf16)],
        compiler_params=pltpu.CompilerParams(
            dimension_semantics=("arbitrary", "arbitrary"), vmem_limit_bytes=VMEM_LIMIT),
        name="norm_in_proj",
    )(x, g, wt)


def _memory_kv_kernel(x_ref, g_ref, w_ref, gk_ref, k_ref, v_ref, wb_ref):
    @pl.when(pl.program_id(0) == 0)
    def _():
        wb_ref[...] = w_ref[...].astype(bf16)

    x = x_ref[...]
    y = x * lax.rsqrt(jnp.mean(x * x, axis=-1, keepdims=True) + EPS)
    kv = _dot((y * g_ref[...]).astype(bf16), wb_ref[...])
    for h in range(MEM_HEADS):
        kh = kv[:, h * MEM_HD:(h + 1) * MEM_HD]
        kh = kh * lax.rsqrt(jnp.mean(kh * kh, axis=-1, keepdims=True) + EPS)
        k_ref[:, h * MEM_HD:(h + 1) * MEM_HD] = kh * gk_ref[...]
    v_ref[...] = kv[:, GROUP_W:]


def _memory_kv(mem, g, w, gk, l):
    b = mem.shape[0]
    out = jax.ShapeDtypeStruct((b, N_MEM, GROUP_W), f32)
    return pl.pallas_call(
        _memory_kv_kernel,
        grid=(b,),
        in_specs=[
            pl.BlockSpec((None, N_MEM, D_MODEL), lambda i: (i, 0, 0)),
            pl.BlockSpec((None, 1, D_MODEL), lambda i: (l, 0, 0)),
            pl.BlockSpec((None, D_MODEL, 2 * GROUP_W), lambda i: (l, 0, 0)),
            pl.BlockSpec((None, 1, MEM_HD), lambda i: (l, 0, 0)),
        ],
        out_specs=[pl.BlockSpec((None, N_MEM, GROUP_W), lambda i: (i, 0, 0))] * 2,
        out_shape=[out, out],
        scratch_shapes=[pltpu.VMEM((D_MODEL, 2 * GROUP_W), bf16)],
        compiler_params=pltpu.CompilerParams(
            dimension_semantics=("arbitrary",), vmem_limit_bytes=VMEM_LIMIT),
        name="memory_kv",
    )(mem, g, w, gk)


CONV_PAD = 8
N_STACK_SLOTS = 2 + 2 * SWA_KV_HEADS
N_SEQ_IN_PROMPT, N_SEQ_IN_DECODE, N_PARAMS_PROMPT, N_PARAMS_DECODE, N_OUT = 3, 7, 11, 9, 5


def _mixer_kernel(*refs, tile, decode, layer, bb):
    n_seq = N_SEQ_IN_DECODE if decode else N_SEQ_IN_PROMPT
    n_par = N_PARAMS_DECODE if decode else N_PARAMS_PROMPT
    seq_in = refs[:n_seq]
    params = refs[n_seq:n_seq + n_par]
    outs = refs[n_seq + n_par:n_seq + n_par + N_OUT]
    scratch = refs[n_seq + n_par + N_OUT:]
    for s in range(bb):
        _mixer_seq([r.at[s] for r in seq_in], params, [r.at[s] for r in outs],
                   [r.at[s] for r in scratch], tile=tile, decode=decode, layer=layer)


def _mixer_seq(seq_in, params, outs, scratch, *, tile, decode, layer):
    if decode:
        p_ref, mk_ref, mv_ref, conv_in_ref, gla_in_ref, kc_ref, vc_ref = seq_in
    else:
        x_ref, mk_ref, mv_ref = seq_in
        gn_ref, wt_ref = params[:2]
    (convw_ref, wup_ref, bga_ref, ggo_ref, gsq_ref, gsk_ref, bd_ref, sinks_ref,
     gmq_ref) = params[-N_PARAMS_DECODE:]
    mix_ref, conv_out_ref, gla_out_ref, kbuf_ref, vbuf_ref = outs
    ext_ref, s_ref, kprev_ref, vprev_ref, tail_ref, stk_ref = scratch

    T = tile
    t = pl.program_id(1)

    @pl.when(t == 0)
    def _():
        ext_ref[0:CONV_PAD, :] = jnp.zeros((CONV_PAD, GROUP_W), f32)
        if decode:
            ext_ref[CONV_PAD - (CONV_W - 1):CONV_PAD, :] = conv_in_ref[...]
            s_ref[...] = gla_in_ref[...]
            kprev_ref[...] = kc_ref[...]
            vprev_ref[...] = vc_ref[...]
        else:
            s_ref[...] = jnp.zeros_like(s_ref)
            kprev_ref[...] = jnp.zeros_like(kprev_ref)
            vprev_ref[...] = jnp.zeros_like(vprev_ref)

    if decode:
        tail_ref[...] = p_ref[:, OFF_GZ:D_IN]

        def seg(off, width):
            if off < OFF_GZ:
                return p_ref[:, off:off + width]
            return tail_ref[:, off - OFF_GZ:off - OFF_GZ + width]
    else:
        x = x_ref[...]
        hn = (x * lax.rsqrt(jnp.mean(x * x, axis=-1, keepdims=True) + EPS) * gn_ref[...]).astype(bf16)

        def seg(off, width):
            return _dot_nt(hn, wt_ref[off:off + width, :])

    def stack_rows(pieces, slot):
        r, w = pieces[0].shape
        if r % 8 == 0:
            return jnp.concatenate(pieces, axis=0)
        for j, piece in enumerate(pieces):
            stk_ref[slot, j * r:(j + 1) * r, 0:w] = piece
        return stk_ref[slot, 0:len(pieces) * r, 0:w]

    def unstack_rows(x, n, slot):
        r, w = x.shape[0] // n, x.shape[1]
        if r % 8 == 0:
            return [x[j * r:(j + 1) * r] for j in range(n)]
        stk_ref[slot, 0:n * r, 0:w] = x
        return [stk_ref[slot, j * r:(j + 1) * r, 0:w] for j in range(n)]

    row = lax.broadcasted_iota(jnp.int32, (T, T), 0)
    col = lax.broadcasted_iota(jnp.int32, (T, T), 1)

    u = seg(OFF_AC, GROUP_W) * seg(OFF_AH, GROUP_W)
    ext_ref[CONV_PAD:CONV_PAD + T, :] = u
    conv = (convw_ref[0:1, :] * ext_ref[CONV_PAD - 2:CONV_PAD - 2 + T, :]
            + convw_ref[1:2, :] * ext_ref[CONV_PAD - 1:CONV_PAD - 1 + T, :]
            + convw_ref[2:3, :] * u)
    mix_ref[:, 0:GROUP_W] = (seg(OFF_AB, GROUP_W) * conv
                             * _silu(seg(OFF_AZ, GROUP_W))).astype(mix_ref.dtype)
    conv_state = ext_ref[CONV_PAD + T - 2:CONV_PAD + T, :]
    ext_ref[CONV_PAD - 2:CONV_PAD, :] = conv_state
    conv_out_ref[...] = conv_state

    C = min(GLA_CHUNK, T)
    n_chunk = T // C
    g_k = seg(OFF_GK, GLA_HEADS * GLA_DK)
    g_v = seg(OFF_GV, GLA_HEADS * GLA_DV)
    g_z = seg(OFF_GZ, GROUP_W)
    g_a = seg(OFF_GA, GLA_RANK).astype(bf16)
    log_a = _log_sigmoid(_dot(g_a, wup_ref[...].astype(bf16)) + bga_ref[...]) * (1.0 / GLA_TAU)
    la3 = _split3(log_a)
    tril = jnp.where((row // C == col // C) & (row >= col), 1.0, 0.0).astype(bf16)
    in_chunk = jnp.where(lax.broadcasted_iota(jnp.int32, (T, LANES), 0) // C
                         == lax.broadcasted_iota(jnp.int32, (T, LANES), 1), 1.0, 0.0).astype(bf16)
    cum = _dot(tril, la3[0]) + _dot(tril, la3[1]) + _dot(tril, la3[2])
    tot_t = (_dot_tn(la3[0], in_chunk) + _dot_tn(la3[1], in_chunk)
             + _dot_tn(la3[2], in_chunk))
    decay_t = jnp.exp(tot_t)
    q_dec = (seg(OFF_GQ, GLA_HEADS * GLA_DK) * (GLA_DK ** -0.5)) * jnp.exp(cum)
    k_dec = g_k * jnp.exp(-cum)
    k_tail = jnp.concatenate(
        [g_k[c * C:(c + 1) * C] * jnp.exp(cum[(c + 1) * C - 1:(c + 1) * C] - cum[c * C:(c + 1) * C])
         for c in range(n_chunk)], axis=0) if n_chunk > 1 else g_k * jnp.exp(cum[T - 1:T] - cum)
    causal = (row // C == col // C) & (row >= col)
    for h in range(GLA_HEADS):
        ks = slice(h * GLA_DK, (h + 1) * GLA_DK)
        vs = slice(h * GLA_DV, (h + 1) * GLA_DV)
        qd = q_dec[:, ks].astype(bf16)
        kd = k_dec[:, ks].astype(bf16)
        kt = k_tail[:, ks].astype(bf16)
        v_h = g_v[:, vs].astype(bf16)
        attn = jnp.where(causal, _dot_nt(qd, kd), 0.0).astype(bf16)
        o_intra = _dot(attn, v_h)
        s_h = s_ref[h]
        o_chunks = []
        for c in range(n_chunk):
            rs = slice(c * C, (c + 1) * C)
            o_chunks.append(o_intra[rs] + _dot(qd[rs], s_h.astype(bf16)))
            s_h = decay_t[ks, c:c + 1] * s_h + _dot_tn(kt[rs], v_h[rs])
        s_ref[h] = s_h
        o_h = jnp.concatenate(o_chunks, axis=0) if n_chunk > 1 else o_chunks[0]
        o_h = o_h * lax.rsqrt(jnp.mean(o_h * o_h, axis=-1, keepdims=True) + EPS) * ggo_ref[...]
        mix_ref[:, GROUP_W + h * GLA_DV:GROUP_W + (h + 1) * GLA_DV] = (
            o_h * _silu(g_z[:, vs])).astype(mix_ref.dtype)
        gla_out_ref[h] = s_h

    def head_norm(x, g, n_lanes):
        sq = x * x
        hi = sq.astype(bf16)
        lo = (sq - hi.astype(f32)).astype(bf16)
        bd = bd_ref[0:n_lanes, 0:n_lanes]
        ms = (_dot(hi, bd) + _dot(lo, bd)) * (1.0 / SWA_HD)
        return x * lax.rsqrt(ms + EPS) * g

    q_n = head_norm(seg(OFF_SQ, GROUP_W), gsq_ref[...], GROUP_W)
    k_n = head_norm(seg(OFF_SK, LANES), gsk_ref[...], LANES)
    v_n = seg(OFF_SV, LANES)
    s_z = seg(OFF_SZ, GROUP_W)

    BQ = min(WINDOW, T)
    n_blk = T // BQ
    stack = SWA_GROUP
    nk = WINDOW + BQ
    qi = lax.broadcasted_iota(jnp.int32, (stack * BQ, nk), 0) % BQ
    kj = lax.broadcasted_iota(jnp.int32, (stack * BQ, nk), 1)
    dist = qi + WINDOW - kj
    band = (dist >= 0) & (dist < WINDOW)
    srow = lax.broadcasted_iota(jnp.int32, (stack * BQ, 1), 0) // BQ
    for blk in range(n_blk):
        rs = slice(blk * BQ, (blk + 1) * BQ)
        if blk == 0:
            k_prev, v_prev = kprev_ref[...], vprev_ref[...]
            valid = band if decode else band & ((kj >= WINDOW) | (t > 0))
        else:
            ps = slice((blk - 1) * BQ, blk * BQ)
            k_prev, v_prev = k_n[ps], v_n[ps]
            valid = band
        k_cat = jnp.concatenate([k_prev, k_n[rs]], axis=0)
        v_cat = jnp.concatenate([v_prev, v_n[rs]], axis=0)
        for g in range(SWA_KV_HEADS):
            kg = k_cat[:, g * SWA_HD:(g + 1) * SWA_HD].astype(bf16)
            vg = v_cat[:, g * SWA_HD:(g + 1) * SWA_HD].astype(bf16)
            heads = [g * SWA_GROUP + j for j in range(stack)]
            qg = stack_rows([q_n[rs, hd * SWA_HD:(hd + 1) * SWA_HD] for hd in heads],
                            2 + 2 * g).astype(bf16)
            sink = jnp.full((stack * BQ, 1), sinks_ref[layer, heads[0]], f32)
            for j in range(1, stack):
                sink = jnp.where(srow == j, sinks_ref[layer, heads[j]], sink)
            s = _dot_nt(qg, kg) * (SWA_HD ** -0.5)
            s = jnp.where(valid, s, -jnp.inf)
            m = jnp.maximum(jnp.max(s, axis=-1, keepdims=True), sink)
            e = jnp.exp(s - m)
            prob = e / (jnp.sum(e, axis=-1, keepdims=True) + jnp.exp(sink - m))
            o = _dot(prob.astype(bf16), vg)
            for hd, o_hd in zip(heads, unstack_rows(o, stack, 3 + 2 * g)):
                z = s_z[rs, hd * SWA_HD:(hd + 1) * SWA_HD]
                mix_ref[rs, 2 * GROUP_W + hd * SWA_HD:2 * GROUP_W + (hd + 1) * SWA_HD] = (
                    o_hd * _silu(z)).astype(mix_ref.dtype)

    if decode:
        kbuf_ref[0:WINDOW - T, :] = kc_ref[T:WINDOW, :]
        kbuf_ref[WINDOW - T:WINDOW, :] = k_n
        vbuf_ref[0:WINDOW - T, :] = vc_ref[T:WINDOW, :]
        vbuf_ref[WINDOW - T:WINDOW, :] = v_n
    else:
        kprev_ref[...] = k_n[T - WINDOW:T]
        vprev_ref[...] = v_n[T - WINDOW:T]
        kbuf_ref[...] = k_n[T - WINDOW:T]
        vbuf_ref[...] = v_n[T - WINDOW:T]

    m_q = seg(OFF_MQ, GROUP_W)
    m_z = seg(OFF_MZ, GROUP_W)

    def mem_q(h):
        qh = m_q[:, h * MEM_HD:(h + 1) * MEM_HD]
        return qh * lax.rsqrt(jnp.mean(qh * qh, axis=-1, keepdims=True) + EPS) * gmq_ref[...]

    def softmax(s):
        e = jnp.exp(s - jnp.max(s, axis=-1, keepdims=True))
        return e / jnp.sum(e, axis=-1, keepdims=True)

    if decode:
        qs = stack_rows([mem_q(h) for h in range(MEM_HEADS)], 0).astype(bf16)
        s = _dot_nt(qs, mk_ref[...].astype(bf16)) * (MEM_HD ** -0.5)
        shape = (MEM_HEADS * T, MEM_HEADS * N_MEM)
        same_head = (lax.broadcasted_iota(jnp.int32, shape, 0) // T
                     == lax.broadcasted_iota(jnp.int32, shape, 1) % MEM_HEADS)
        prob = softmax(jnp.where(same_head, s, -jnp.inf))
        o_all = unstack_rows(_dot(prob.astype(bf16), mv_ref[...].astype(bf16)), MEM_HEADS, 1)
    else:
        o_all = []
        for h in range(MEM_HEADS):
            hs = slice(h * MEM_HD, (h + 1) * MEM_HD)
            s = _dot_nt(mem_q(h).astype(bf16), mk_ref[:, hs].astype(bf16)) * (MEM_HD ** -0.5)
            o_all.append(_dot(softmax(s).astype(bf16), mv_ref[:, hs].astype(bf16)))
    for h in range(MEM_HEADS):
        mix_ref[:, 3 * GROUP_W + h * MEM_HD:3 * GROUP_W + (h + 1) * MEM_HD] = (
            o_all[h] * _silu(m_z[:, h * MEM_HD:(h + 1) * MEM_HD])).astype(mix_ref.dtype)


def _mixer(tokens, norm_w, mem_k, mem_v, mem_layer, state, params, layer, tile, bb, decode):
    b, L, width = tokens.shape
    nt = L // tile
    conv_w, w_up, b_ga, g_go, g_sq, g_sk, bd, sinks, g_mq = params

    def tok(width):
        return pl.BlockSpec((bb, tile, width), lambda i, t: (i, t, 0))

    def per_seq(*shape):
        return pl.BlockSpec((bb,) + shape, lambda i, t: (i,) + (0,) * len(shape))

    def per_seq_at(lyr, *shape):
        return pl.BlockSpec((None, bb) + shape, lambda i, t: (lyr, i) + (0,) * len(shape))

    def param(a):
        return pl.BlockSpec((None,) + a.shape[1:], lambda i, t: (layer,) + (0,) * (a.ndim - 1))

    kv_w = SWA_KV_HEADS * SWA_HD
    state_shapes = [(CONV_W - 1, GROUP_W), (GLA_HEADS, GLA_DK, GLA_DV), (WINDOW, kv_w),
                    (WINDOW, kv_w)]
    in_specs = [tok(width), per_seq_at(mem_layer, *mem_k.shape[2:]),
                per_seq_at(mem_layer, *mem_v.shape[2:])]
    args = [tokens, mem_k, mem_v]
    if decode:
        in_specs += [per_seq_at(layer, *s) for s in state_shapes]
        args += list(state)
    else:
        g_n, w_t = norm_w
        in_specs += [param(g_n), pl.BlockSpec((None,) + w_t.shape[1:], lambda i, t: (layer, 0, 0),
                                              pipeline_mode=pl.Buffered(1))]
        args += [g_n, w_t]
    in_specs += [param(conv_w), param(w_up), param(b_ga), param(g_go), param(g_sq), param(g_sk),
                 pl.BlockSpec(bd.shape, lambda i, t: (0, 0)),
                 pl.BlockSpec(memory_space=pltpu.SMEM), param(g_mq)]
    args += [conv_w, w_up, b_ga, g_go, g_sq, g_sk, bd, sinks, g_mq]
    mix_dtype = f32 if decode else bf16
    out_shape = [jax.ShapeDtypeStruct((b, L, 4 * GROUP_W), mix_dtype)]
    out_shape += [jax.ShapeDtypeStruct((b,) + s, f32) for s in state_shapes]
    out_specs = [tok(4 * GROUP_W)] + [per_seq(*s) for s in state_shapes]
    return pl.pallas_call(
        functools.partial(_mixer_kernel, tile=tile, decode=decode, layer=layer, bb=bb),
        grid=(b // bb, nt),
        in_specs=in_specs,
        out_specs=out_specs,
        out_shape=out_shape,
        scratch_shapes=[
            pltpu.VMEM((bb, CONV_PAD + tile, GROUP_W), f32),
            pltpu.VMEM((bb, GLA_HEADS, GLA_DK, GLA_DV), f32),
            pltpu.VMEM((bb, WINDOW, kv_w), f32),
            pltpu.VMEM((bb, WINDOW, kv_w), f32),
            pltpu.VMEM((bb, tile if decode else 8, D_IN - OFF_GZ), f32),
            pltpu.VMEM((bb, N_STACK_SLOTS, MEM_HEADS * min(tile, 8), LANES), f32),
        ],
        compiler_params=pltpu.CompilerParams(
            dimension_semantics=("arbitrary", "arbitrary"), vmem_limit_bytes=VMEM_LIMIT),
        name="mixer_decode" if decode else "mixer_prompt",
    )(*args)


def _out_proj_kernel(mix_ref, w_ref, x_ref, y_ref, wb_ref):
    @pl.when(pl.program_id(1) == 0)
    def _():
        wb_ref[...] = w_ref[...].astype(bf16)

    y_ref[...] = x_ref[...] + _dot(mix_ref[...].astype(bf16), wb_ref[...])


def _out_proj(mix, w, x, l, tm, tn):
    m, k = mix.shape
    n = w.shape[2]
    return pl.pallas_call(
        _out_proj_kernel,
        grid=(n // tn, m // tm),
        in_specs=[
            pl.BlockSpec((tm, k), lambda j, i: (i, 0)),
            pl.BlockSpec((None, k, tn), lambda j, i: (l, 0, j)),
            pl.BlockSpec((tm, tn), lambda j, i: (i, j)),
        ],
        out_specs=pl.BlockSpec((tm, tn), lambda j, i: (i, j)),
        out_shape=jax.ShapeDtypeStruct((m, n), f32),
        scratch_shapes=[pltpu.VMEM((k, tn), bf16)],
        compiler_params=pltpu.CompilerParams(
            dimension_semantics=("arbitrary", "arbitrary"), vmem_limit_bytes=VMEM_LIMIT),
        name="out_proj",
    )(mix, w, x)


PROMPT_TILE = 512
DECODE_SEQS_PER_STEP = 8
PROJ_TN = 768
OUT_TM, OUT_TN = 1024, 1024

_LANE = np.arange(GROUP_W)
HEAD_BLOCK_DIAG = _LANE[:, None] // SWA_HD == _LANE[None, :] // SWA_HD


def kernel(x_prompt, x_sample, mem_prompt, state_conv, state_gla, cache_swa_k, cache_swa_v,
           cache_mem_k, cache_mem_v, g_norm, w_in, conv_w, w_gla_a_up, b_gla_a, g_gla_o,
           g_swa_q, g_swa_k, swa_sinks, g_mem, w_mem_kv, g_mem_q, g_mem_k, w_out):
    depth = w_in.shape[0]
    bp, lp, _ = x_prompt.shape
    bs, ls, _ = x_sample.shape
    hp = x_prompt.reshape(bp * lp, D_MODEL)
    hs = x_sample.reshape(bs * ls, D_MODEL)

    def row(a):
        return a[:, None, :]

    params = (conv_w, w_gla_a_up, row(b_gla_a), row(g_gla_o),
              row(jnp.tile(g_swa_q, (1, SWA_HEADS))), row(jnp.tile(g_swa_k, (1, SWA_KV_HEADS))),
              jnp.asarray(HEAD_BLOCK_DIAG, bf16), swa_sinks, row(g_mem_q))
    g_n, g_m, g_mk = row(g_norm), row(g_mem), row(g_mem_k)
    w_in_t = jnp.swapaxes(w_in, 1, 2).astype(bf16)
    kv_w = SWA_KV_HEADS * SWA_HD
    state = (state_conv, state_gla, cache_swa_k.reshape(depth, bs, WINDOW, kv_w),
             cache_swa_v.reshape(depth, bs, WINDOW, kv_w))
    mem_k_s = cache_mem_k.reshape(depth, bs, N_MEM * MEM_HEADS, MEM_HD)
    mem_v_s = cache_mem_v.reshape(depth, bs, N_MEM * MEM_HEADS, MEM_HD)

    outs = [[] for _ in range(10)]
    for l in range(depth):
        mk, mv = _memory_kv(mem_prompt, g_m, w_mem_kv, g_mk, l)
        mix, c, s, kb, vb = _mixer(hp.reshape(bp, lp, D_MODEL), (g_n, w_in_t), mk[None], mv[None], 0,
                                   None, params, l, PROMPT_TILE, 1, decode=False)
        hp = _out_proj(mix.reshape(bp * lp, 4 * GROUP_W), w_out, hp, l, OUT_TM, OUT_TN)
        for lst, a in zip(outs[:6], (
                c, s, kb.reshape(bp, WINDOW, SWA_KV_HEADS, SWA_HD),
                vb.reshape(bp, WINDOW, SWA_KV_HEADS, SWA_HD),
                mk.reshape(bp, N_MEM, MEM_HEADS, MEM_HD), mv.reshape(bp, N_MEM, MEM_HEADS, MEM_HD))):
            lst.append(a)

        proj = _norm_matmul(hs, g_n, w_in_t, l, bs * ls, PROJ_TN).reshape(bs, ls, D_IN)
        mix, c, s, kb, vb = _mixer(proj, None, mem_k_s, mem_v_s, l, state, params, l, ls,
                                   DECODE_SEQS_PER_STEP, decode=True)
        hs = _out_proj(mix.reshape(bs * ls, 4 * GROUP_W), w_out, hs, l, bs * ls, OUT_TN)
        for lst, a in zip(outs[6:], (
                c, s, kb.reshape(bs, WINDOW, SWA_KV_HEADS, SWA_HD),
                vb.reshape(bs, WINDOW, SWA_KV_HEADS, SWA_HD))):
            lst.append(a)

    return (hp.reshape(bp, lp, D_MODEL), hs.reshape(bs, ls, D_MODEL),
            *[jnp.stack(o) for o in outs])
```

```python
import functools
import itertools

import jax
import jax.numpy as jnp
import numpy as np
from jax import lax
from jax.experimental import pallas as pl
from jax.experimental.pallas import tpu as pltpu

f32 = jnp.float32
bf16 = jnp.bfloat16

D_MODEL = 2048
GROUP_W = 512
GLA_HEADS = 4
GLA_DK = 64
GLA_DV = 128
GLA_RANK = 16
GLA_TAU = 16.0
GLA_CHUNK = 64
SWA_HEADS = 8
SWA_KV_HEADS = 2
SWA_HD = 64
SWA_GROUP = SWA_HEADS // SWA_KV_HEADS
WINDOW = 128
N_MEM = 256
MEM_HEADS = 4
MEM_HD = 128
CONV_W = 3
EPS = 1e-6

LANES = 128

D_IN = 5904
OFF_AB, OFF_AC, OFF_AH, OFF_AZ = 0, 512, 1024, 1536
OFF_GQ, OFF_GK, OFF_GV, OFF_GA, OFF_GZ = 2048, 2304, 2560, 3072, 3088
OFF_SQ, OFF_SK, OFF_SV, OFF_SZ = 3600, 4112, 4240, 4368
OFF_MQ, OFF_MZ = 4880, 5392

VMEM_LIMIT = 56 * 1024 * 1024


def _dot(a, b):
    return jnp.dot(a, b, preferred_element_type=f32)


def _dot_nt(a, b):
    return lax.dot_general(a, b, (((1,), (1,)), ((), ())), preferred_element_type=f32)


def _dot_tn(a, b):
    return lax.dot_general(a, b, (((0,), (0,)), ((), ())), preferred_element_type=f32)


def _split3(x):
    hi = x.astype(bf16)
    r = x - hi.astype(f32)
    mid = r.astype(bf16)
    lo = (r - mid.astype(f32)).astype(bf16)
    return hi, mid, lo


def _silu(x):
    return x * jax.nn.sigmoid(x)


def _log_sigmoid(x):
    return jnp.minimum(x, 0.0) - jnp.log1p(jnp.exp(-jnp.abs(x)))


def _norm_matmul_kernel(x_ref, g_ref, wt_ref, o_ref, hn_ref):
    @pl.when(pl.program_id(1) == 0)
    def _():
        x = x_ref[...]
        y = x * lax.rsqrt(jnp.mean(x * x, axis=-1, keepdims=True) + EPS)
        hn_ref[...] = (y * g_ref[...]).astype(bf16)

    o_ref[...] = _dot_nt(hn_ref[...], wt_ref[...].astype(bf16))


def _norm_matmul(x, g, wt, l, tm, tn):
    m, k = x.shape
    n = wt.shape[1]
    return pl.pallas_call(
        _norm_matmul_kernel,
        grid=(m // tm, pl.cdiv(n, tn)),
        in_specs=[
            pl.BlockSpec((tm, k), lambda i, j: (i, 0)),
            pl.BlockSpec((None, 1, k), lambda i, j: (l, 0, 0)),
            pl.BlockSpec((None, tn, k), lambda i, j: (l, j, 0)),
        ],
        out_specs=pl.BlockSpec((tm, tn), lambda i, j: (i, j)),
        out_shape=jax.ShapeDtypeStruct((m, n), f32),
        scratch_shapes=[pltpu.VMEM((tm, k), bf16)],
        compiler_params=pltpu.CompilerParams(
            dimension_semantics=("arbitrary", "arbitrary"), vmem_limit_bytes=VMEM_LIMIT),
        name="norm_in_proj",
    )(x, g, wt)


def _memory_kv_kernel(x_ref, g_ref, w_ref, gk_ref, k_ref, v_ref, wb_ref):
    @pl.when(pl.program_id(0) == 0)
    def _():
        wb_ref[...] = w_ref[...].astype(bf16)

    x = x_ref[...]
    y = x * lax.rsqrt(jnp.mean(x * x, axis=-1, keepdims=True) + EPS)
    kv = _dot((y * g_ref[...]).astype(bf16), wb_ref[...])
    for h in range(MEM_HEADS):
        kh = kv[:, h * MEM_HD:(h + 1) * MEM_HD]
        kh = kh * lax.rsqrt(jnp.mean(kh * kh, axis=-1, keepdims=True) + EPS)
        k_ref[:, h * MEM_HD:(h + 1) * MEM_HD] = kh * gk_ref[...]
    v_ref[...] = kv[:, GROUP_W:]


def _memory_kv(mem, g, w, gk, l):
    b = mem.shape[0]
    out = jax.ShapeDtypeStruct((b, N_MEM, GROUP_W), f32)
    return pl.pallas_call(
        _memory_kv_kernel,
        grid=(b,),
        in_specs=[
            pl.BlockSpec((None, N_MEM, D_MODEL), lambda i: (i, 0, 0)),
            pl.BlockSpec((None, 1, D_MODEL), lambda i: (l, 0, 0)),
            pl.BlockSpec((None, D_MODEL, 2 * GROUP_W), lambda i: (l, 0, 0)),
            pl.BlockSpec((None, 1, MEM_HD), lambda i: (l, 0, 0)),
        ],
        out_specs=[pl.BlockSpec((None, N_MEM, GROUP_W), lambda i: (i, 0, 0))] * 2,
        out_shape=[out, out],
        scratch_shapes=[pltpu.VMEM((D_MODEL, 2 * GROUP_W), bf16)],
        compiler_params=pltpu.CompilerParams(
            dimension_semantics=("arbitrary",), vmem_limit_bytes=VMEM_LIMIT),
        name="memory_kv",
    )(mem, g, w, gk)


CONV_PAD = 8
N_STACK_SLOTS = 2 + 2 * SWA_KV_HEADS
N_SEQ_IN_PROMPT, N_SEQ_IN_DECODE, N_PARAMS_PROMPT, N_PARAMS_DECODE, N_OUT = 3, 7, 11, 9, 5


def _mixer_kernel(*refs, tile, decode, layer, bb):
    n_seq = N_SEQ_IN_DECODE if decode else N_SEQ_IN_PROMPT
    n_par = N_PARAMS_DECODE if decode else N_PARAMS_PROMPT
    seq_in = refs[:n_seq]
    params = refs[n_seq:n_seq + n_par]
    outs = refs[n_seq + n_par:n_seq + n_par + N_OUT]
    scratch = refs[n_seq + n_par + N_OUT:]
    stages = [_mixer_seq([r.at[s] for r in seq_in], params, [r.at[s] for r in outs],
                         [r.at[s] for r in scratch], tile=tile, decode=decode, layer=layer)
              for s in range(bb)]
    for _ in itertools.zip_longest(*stages):
        pass


def _mixer_seq(seq_in, params, outs, scratch, *, tile, decode, layer):
    if decode:
        p_ref, mk_ref, mv_ref, conv_in_ref, gla_in_ref, kc_ref, vc_ref = seq_in
    else:
        x_ref, mk_ref, mv_ref = seq_in
        gn_ref, wt_ref = params[:2]
    (convw_ref, wup_ref, bga_ref, ggo_ref, gsq_ref, gsk_ref, bd_ref, sinks_ref,
     gmq_ref) = params[-N_PARAMS_DECODE:]
    mix_ref, conv_out_ref, gla_out_ref, kbuf_ref, vbuf_ref = outs
    ext_ref, s_ref, kprev_ref, vprev_ref, tail_ref, stk_ref = scratch

    T = tile
    t = pl.program_id(1)

    def init_state():
        ext_ref[0:CONV_PAD, :] = jnp.zeros((CONV_PAD, GROUP_W), f32)
        if decode:
            ext_ref[CONV_PAD - (CONV_W - 1):CONV_PAD, :] = conv_in_ref[...]
            s_ref[...] = gla_in_ref[...]
            kprev_ref[...] = kc_ref[...]
            vprev_ref[...] = vc_ref[...]
        else:
            s_ref[...] = jnp.zeros_like(s_ref)
            kprev_ref[...] = jnp.zeros_like(kprev_ref)
            vprev_ref[...] = jnp.zeros_like(vprev_ref)

    if decode:
        init_state()
    else:
        pl.when(t == 0)(init_state)
    yield

    if decode:
        tail_ref[...] = p_ref[:, OFF_GZ:D_IN]

        def seg(off, width):
            if off < OFF_GZ:
                return p_ref[:, off:off + width]
            return tail_ref[:, off - OFF_GZ:off - OFF_GZ + width]
    else:
        x = x_ref[...]
        hn = (x * lax.rsqrt(jnp.mean(x * x, axis=-1, keepdims=True) + EPS) * gn_ref[...]).astype(bf16)

        def seg(off, width):
            return _dot_nt(hn, wt_ref[off:off + width, :])

    def stack_rows(pieces, slot):
        r, w = pieces[0].shape
        if r % 8 == 0:
            return jnp.concatenate(pieces, axis=0)
        for j, piece in enumerate(pieces):
            stk_ref[slot, j * r:(j + 1) * r, 0:w] = piece
        return stk_ref[slot, 0:len(pieces) * r, 0:w]

    def unstack_rows(x, n, slot):
        r, w = x.shape[0] // n, x.shape[1]
        if r % 8 == 0:
            return [x[j * r:(j + 1) * r] for j in range(n)]
        stk_ref[slot, 0:n * r, 0:w] = x
        return [stk_ref[slot, j * r:(j + 1) * r, 0:w] for j in range(n)]

    row = lax.broadcasted_iota(jnp.int32, (T, T), 0)
    col = lax.broadcasted_iota(jnp.int32, (T, T), 1)

    u = seg(OFF_AC, GROUP_W) * seg(OFF_AH, GROUP_W)
    ext_ref[CONV_PAD:CONV_PAD + T, :] = u
    conv = (convw_ref[0:1, :] * ext_ref[CONV_PAD - 2:CONV_PAD - 2 + T, :]
            + convw_ref[1:2, :] * ext_ref[CONV_PAD - 1:CONV_PAD - 1 + T, :]
            + convw_ref[2:3, :] * u)
    mix_ref[:, 0:GROUP_W] = (seg(OFF_AB, GROUP_W) * conv
                             * _silu(seg(OFF_AZ, GROUP_W))).astype(mix_ref.dtype)
    conv_state = ext_ref[CONV_PAD + T - 2:CONV_PAD + T, :]
    ext_ref[CONV_PAD - 2:CONV_PAD, :] = conv_state
    conv_out_ref[...] = conv_state
    yield

    C = min(GLA_CHUNK, T)
    n_chunk = T // C
    g_a = seg(OFF_GA, GLA_RANK).astype(bf16)
    log_a = _log_sigmoid(_dot(g_a, wup_ref[...].astype(bf16)) + bga_ref[...]) * (1.0 / GLA_TAU)
    la3 = _split3(log_a)
    yield
    tril = jnp.where((row // C == col // C) & (row >= col), 1.0, 0.0).astype(bf16)
    in_chunk = jnp.where(lax.broadcasted_iota(jnp.int32, (T, LANES), 0) // C
                         == lax.broadcasted_iota(jnp.int32, (T, LANES), 1), 1.0, 0.0).astype(bf16)
    cum = _dot(tril, la3[0]) + _dot(tril, la3[1]) + _dot(tril, la3[2])
    tot_t = (_dot_tn(la3[0], in_chunk) + _dot_tn(la3[1], in_chunk)
             + _dot_tn(la3[2], in_chunk))
    decay_t = jnp.exp(tot_t)
    yield
    g_k = seg(OFF_GK, GLA_HEADS * GLA_DK)
    q_dec = (seg(OFF_GQ, GLA_HEADS * GLA_DK) * (GLA_DK ** -0.5)) * jnp.exp(cum)
    k_dec = g_k * jnp.exp(-cum)
    k_tail = jnp.concatenate(
        [g_k[c * C:(c + 1) * C] * jnp.exp(cum[(c + 1) * C - 1:(c + 1) * C] - cum[c * C:(c + 1) * C])
         for c in range(n_chunk)], axis=0) if n_chunk > 1 else g_k * jnp.exp(cum[T - 1:T] - cum)
    causal = (row // C == col // C) & (row >= col)
    g_v = seg(OFF_GV, GLA_HEADS * GLA_DV)
    g_z = seg(OFF_GZ, GROUP_W)
    yield
    for h in range(GLA_HEADS):
        ks = slice(h * GLA_DK, (h + 1) * GLA_DK)
        vs = slice(h * GLA_DV, (h + 1) * GLA_DV)
        qd = q_dec[:, ks].astype(bf16)
        kd = k_dec[:, ks].astype(bf16)
        kt = k_tail[:, ks].astype(bf16)
        v_h = g_v[:, vs].astype(bf16)
        attn = jnp.where(causal, _dot_nt(qd, kd), 0.0).astype(bf16)
        o_intra = _dot(attn, v_h)
        s_h = s_ref[h]
        o_chunks = []
        for c in range(n_chunk):
            rs = slice(c * C, (c + 1) * C)
            o_chunks.append(o_intra[rs] + _dot(qd[rs], s_h.astype(bf16)))
            s_h = decay_t[ks, c:c + 1] * s_h + _dot_tn(kt[rs], v_h[rs])
        s_ref[h] = s_h
        o_h = jnp.concatenate(o_chunks, axis=0) if n_chunk > 1 else o_chunks[0]
        o_h = o_h * lax.rsqrt(jnp.mean(o_h * o_h, axis=-1, keepdims=True) + EPS) * ggo_ref[...]
        mix_ref[:, GROUP_W + h * GLA_DV:GROUP_W + (h + 1) * GLA_DV] = (
            o_h * _silu(g_z[:, vs])).astype(mix_ref.dtype)
        gla_out_ref[h] = s_h
        yield

    def head_norm(x, g, n_lanes):
        sq = x * x
        hi = sq.astype(bf16)
        lo = (sq - hi.astype(f32)).astype(bf16)
        bd = bd_ref[0:n_lanes, 0:n_lanes]
        ms = (_dot(hi, bd) + _dot(lo, bd)) * (1.0 / SWA_HD)
        return x * lax.rsqrt(ms + EPS) * g

    s_q = seg(OFF_SQ, GROUP_W)
    s_kv = seg(OFF_SK, 2 * LANES)
    s_z = seg(OFF_SZ, GROUP_W)
    q_n = head_norm(s_q, gsq_ref[...], GROUP_W)
    k_n = head_norm(s_kv[:, 0:LANES], gsk_ref[...], LANES)
    v_n = s_kv[:, LANES:2 * LANES]
    yield

    BQ = min(WINDOW, T)
    n_blk = T // BQ
    stack = SWA_GROUP
    nk = WINDOW + BQ
    qi = lax.broadcasted_iota(jnp.int32, (stack * BQ, nk), 0) % BQ
    kj = lax.broadcasted_iota(jnp.int32, (stack * BQ, nk), 1)
    dist = qi + WINDOW - kj
    band = (dist >= 0) & (dist < WINDOW)
    srow = lax.broadcasted_iota(jnp.int32, (stack * BQ, 1), 0) // BQ
    for blk in range(n_blk):
        rs = slice(blk * BQ, (blk + 1) * BQ)
        if blk == 0:
            k_prev, v_prev = kprev_ref[...], vprev_ref[...]
            valid = band if decode else band & ((kj >= WINDOW) | (t > 0))
        else:
            ps = slice((blk - 1) * BQ, blk * BQ)
            k_prev, v_prev = k_n[ps], v_n[ps]
            valid = band
        k_cat = jnp.concatenate([k_prev, k_n[rs]], axis=0)
        v_cat = jnp.concatenate([v_prev, v_n[rs]], axis=0)
        for g in range(SWA_KV_HEADS):
            kg = k_cat[:, g * SWA_HD:(g + 1) * SWA_HD].astype(bf16)
            vg = v_cat[:, g * SWA_HD:(g + 1) * SWA_HD].astype(bf16)
            heads = [g * SWA_GROUP + j for j in range(stack)]
            qg = stack_rows([q_n[rs, hd * SWA_HD:(hd + 1) * SWA_HD] for hd in heads],
                            2 + 2 * g).astype(bf16)
            sink = jnp.full((stack * BQ, 1), sinks_ref[layer, heads[0]], f32)
            for j in range(1, stack):
                sink = jnp.where(srow == j, sinks_ref[layer, heads[j]], sink)
            s = _dot_nt(qg, kg) * (SWA_HD ** -0.5)
            s = jnp.where(valid, s, -jnp.inf)
            m = jnp.maximum(jnp.max(s, axis=-1, keepdims=True), sink)
            e = jnp.exp(s - m)
            prob = e / (jnp.sum(e, axis=-1, keepdims=True) + jnp.exp(sink - m))
            o = _dot(prob.astype(bf16), vg)
            for hd, o_hd in zip(heads, unstack_rows(o, stack, 3 + 2 * g)):
                z = s_z[rs, hd * SWA_HD:(hd + 1) * SWA_HD]
                mix_ref[rs, 2 * GROUP_W + hd * SWA_HD:2 * GROUP_W + (hd + 1) * SWA_HD] = (
                    o_hd * _silu(z)).astype(mix_ref.dtype)
            yield

    if decode:
        kbuf_ref[0:WINDOW - T, :] = kc_ref[T:WINDOW, :]
        kbuf_ref[WINDOW - T:WINDOW, :] = k_n
        vbuf_ref[0:WINDOW - T, :] = vc_ref[T:WINDOW, :]
        vbuf_ref[WINDOW - T:WINDOW, :] = v_n
    else:
        kprev_ref[...] = k_n[T - WINDOW:T]
        vprev_ref[...] = v_n[T - WINDOW:T]
        kbuf_ref[...] = k_n[T - WINDOW:T]
        vbuf_ref[...] = v_n[T - WINDOW:T]
    yield

    m_q = seg(OFF_MQ, GROUP_W)
    m_z = seg(OFF_MZ, GROUP_W)
    yield

    def mem_q(h):
        qh = m_q[:, h * MEM_HD:(h + 1) * MEM_HD]
        return qh * lax.rsqrt(jnp.mean(qh * qh, axis=-1, keepdims=True) + EPS) * gmq_ref[...]

    def softmax(s):
        e = jnp.exp(s - jnp.max(s, axis=-1, keepdims=True))
        return e / jnp.sum(e, axis=-1, keepdims=True)

    if decode:
        qs = stack_rows([mem_q(h) for h in range(MEM_HEADS)], 0).astype(bf16)
        s = _dot_nt(qs, mk_ref[...].astype(bf16)) * (MEM_HD ** -0.5)
        shape = (MEM_HEADS * T, MEM_HEADS * N_MEM)
        same_head = (lax.broadcasted_iota(jnp.int32, shape, 0) // T
                     == lax.broadcasted_iota(jnp.int32, shape, 1) % MEM_HEADS)
        yield
        prob = softmax(jnp.where(same_head, s, -jnp.inf))
        o_all = unstack_rows(_dot(prob.astype(bf16), mv_ref[...].astype(bf16)), MEM_HEADS, 1)
    else:
        o_all = []
        for h in range(MEM_HEADS):
            hs = slice(h * MEM_HD, (h + 1) * MEM_HD)
            s = _dot_nt(mem_q(h).astype(bf16), mk_ref[:, hs].astype(bf16)) * (MEM_HD ** -0.5)
            o_all.append(_dot(softmax(s).astype(bf16), mv_ref[:, hs].astype(bf16)))
    for h in range(MEM_HEADS):
        mix_ref[:, 3 * GROUP_W + h * MEM_HD:3 * GROUP_W + (h + 1) * MEM_HD] = (
            o_all[h] * _silu(m_z[:, h * MEM_HD:(h + 1) * MEM_HD])).astype(mix_ref.dtype)


def _mixer(tokens, norm_w, mem_k, mem_v, mem_layer, state, params, layer, tile, bb, decode):
    b, L, width = tokens.shape
    nt = L // tile
    assert nt == 1 or not decode, "a decode call covers each sequence with a single tile"
    conv_w, w_up, b_ga, g_go, g_sq, g_sk, bd, sinks, g_mq = params

    def tok(width):
        return pl.BlockSpec((bb, tile, width), lambda i, t: (i, t, 0))

    def per_seq(*shape):
        return pl.BlockSpec((bb,) + shape, lambda i, t: (i,) + (0,) * len(shape))

    def per_seq_at(lyr, *shape):
        return pl.BlockSpec((None, bb) + shape, lambda i, t: (lyr, i) + (0,) * len(shape))

    def param(a):
        return pl.BlockSpec((None,) + a.shape[1:], lambda i, t: (layer,) + (0,) * (a.ndim - 1))

    kv_w = SWA_KV_HEADS * SWA_HD
    state_shapes = [(CONV_W - 1, GROUP_W), (GLA_HEADS, GLA_DK, GLA_DV), (WINDOW, kv_w),
                    (WINDOW, kv_w)]
    in_specs = [tok(width), per_seq_at(mem_layer, *mem_k.shape[2:]),
                per_seq_at(mem_layer, *mem_v.shape[2:])]
    args = [tokens, mem_k, mem_v]
    if decode:
        in_specs += [per_seq_at(layer, *s) for s in state_shapes]
        args += list(state)
    else:
        g_n, w_t = norm_w
        in_specs += [param(g_n), pl.BlockSpec((None,) + w_t.shape[1:], lambda i, t: (layer, 0, 0),
                                              pipeline_mode=pl.Buffered(1))]
        args += [g_n, w_t]
    in_specs += [param(conv_w), param(w_up), param(b_ga), param(g_go), param(g_sq), param(g_sk),
                 pl.BlockSpec(bd.shape, lambda i, t: (0, 0)),
                 pl.BlockSpec(memory_space=pltpu.SMEM), param(g_mq)]
    args += [conv_w, w_up, b_ga, g_go, g_sq, g_sk, bd, sinks, g_mq]
    mix_dtype = f32 if decode else bf16
    out_shape = [jax.ShapeDtypeStruct((b, L, 4 * GROUP_W), mix_dtype)]
    out_shape += [jax.ShapeDtypeStruct((b,) + s, f32) for s in state_shapes]
    out_specs = [tok(4 * GROUP_W)] + [per_seq(*s) for s in state_shapes]
    return pl.pallas_call(
        functools.partial(_mixer_kernel, tile=tile, decode=decode, layer=layer, bb=bb),
        grid=(b // bb, nt),
        in_specs=in_specs,
        out_specs=out_specs,
        out_shape=out_shape,
        scratch_shapes=[
            pltpu.VMEM((bb, CONV_PAD + tile, GROUP_W), f32),
            pltpu.VMEM((bb, GLA_HEADS, GLA_DK, GLA_DV), f32),
            pltpu.VMEM((bb, WINDOW, kv_w), f32),
            pltpu.VMEM((bb, WINDOW, kv_w), f32),
            pltpu.VMEM((bb, tile if decode else 8, D_IN - OFF_GZ), f32),
            pltpu.VMEM((bb, N_STACK_SLOTS, MEM_HEADS * min(tile, 8), LANES), f32),
        ],
        compiler_params=pltpu.CompilerParams(
            dimension_semantics=("arbitrary", "arbitrary"), vmem_limit_bytes=VMEM_LIMIT),
        name="mixer_decode" if decode else "mixer_prompt",
    )(*args)


def _out_proj_kernel(mix_ref, w_ref, x_ref, y_ref, wb_ref):
    @pl.when(pl.program_id(1) == 0)
    def _():
        wb_ref[...] = w_ref[...].astype(bf16)

    y_ref[...] = x_ref[...] + _dot(mix_ref[...].astype(bf16), wb_ref[...])


def _out_proj(mix, w, x, l, tm, tn):
    m, k = mix.shape
    n = w.shape[2]
    return pl.pallas_call(
        _out_proj_kernel,
        grid=(n // tn, m // tm),
        in_specs=[
            pl.BlockSpec((tm, k), lambda j, i: (i, 0)),
            pl.BlockSpec((None, k, tn), lambda j, i: (l, 0, j)),
            pl.BlockSpec((tm, tn), lambda j, i: (i, j)),
        ],
        out_specs=pl.BlockSpec((tm, tn), lambda j, i: (i, j)),
        out_shape=jax.ShapeDtypeStruct((m, n), f32),
        scratch_shapes=[pltpu.VMEM((k, tn), bf16)],
        compiler_params=pltpu.CompilerParams(
            dimension_semantics=("arbitrary", "arbitrary"), vmem_limit_bytes=VMEM_LIMIT),
        name="out_proj",
    )(mix, w, x)


PROMPT_TILE = 512
DECODE_SEQS_PER_STEP = 8
PROJ_TN = 768
OUT_TM, OUT_TN = 1024, 1024

_LANE = np.arange(GROUP_W)
HEAD_BLOCK_DIAG = _LANE[:, None] // SWA_HD == _LANE[None, :] // SWA_HD


def kernel(x_prompt, x_sample, mem_prompt, state_conv, state_gla, cache_swa_k, cache_swa_v,
           cache_mem_k, cache_mem_v, g_norm, w_in, conv_w, w_gla_a_up, b_gla_a, g_gla_o,
           g_swa_q, g_swa_k, swa_sinks, g_mem, w_mem_kv, g_mem_q, g_mem_k, w_out):
    depth = w_in.shape[0]
    bp, lp, _ = x_prompt.shape
    bs, ls, _ = x_sample.shape
    hp = x_prompt.reshape(bp * lp, D_MODEL)
    hs = x_sample.reshape(bs * ls, D_MODEL)

    def row(a):
        return a[:, None, :]

    params = (conv_w, w_gla_a_up, row(b_gla_a), row(g_gla_o),
              row(jnp.tile(g_swa_q, (1, SWA_HEADS))), row(jnp.tile(g_swa_k, (1, SWA_KV_HEADS))),
              jnp.asarray(HEAD_BLOCK_DIAG, bf16), swa_sinks, row(g_mem_q))
    g_n, g_m, g_mk = row(g_norm), row(g_mem), row(g_mem_k)
    w_in_t = jnp.swapaxes(w_in, 1, 2).astype(bf16)
    kv_w = SWA_KV_HEADS * SWA_HD
    state = (state_conv, state_gla, cache_swa_k.reshape(depth, bs, WINDOW, kv_w),
             cache_swa_v.reshape(depth, bs, WINDOW, kv_w))
    mem_k_s = cache_mem_k.reshape(depth, bs, N_MEM * MEM_HEADS, MEM_HD)
    mem_v_s = cache_mem_v.reshape(depth, bs, N_MEM * MEM_HEADS, MEM_HD)

    outs = [[] for _ in range(10)]
    for l in range(depth):
        mk, mv = _memory_kv(mem_prompt, g_m, w_mem_kv, g_mk, l)
        mix, c, s, kb, vb = _mixer(hp.reshape(bp, lp, D_MODEL), (g_n, w_in_t), mk[None], mv[None], 0,
                                   None, params, l, PROMPT_TILE, 1, decode=False)
        hp = _out_proj(mix.reshape(bp * lp, 4 * GROUP_W), w_out, hp, l, OUT_TM, OUT_TN)
        for lst, a in zip(outs[:6], (
                c, s, kb.reshape(bp, WINDOW, SWA_KV_HEADS, SWA_HD),
                vb.reshape(bp, WINDOW, SWA_KV_HEADS, SWA_HD),
                mk.reshape(bp, N_MEM, MEM_HEADS, MEM_HD), mv.reshape(bp, N_MEM, MEM_HEADS, MEM_HD))):
            lst.append(a)

        proj = _norm_matmul(hs, g_n, w_in_t, l, bs * ls, PROJ_TN).reshape(bs, ls, D_IN)
        mix, c, s, kb, vb = _mixer(proj, None, mem_k_s, mem_v_s, l, state, params, l, ls,
                                   DECODE_SEQS_PER_STEP, decode=True)
        hs = _out_proj(mix.reshape(bs * ls, 4 * GROUP_W), w_out, hs, l, bs * ls, OUT_TN)
        for lst, a in zip(outs[6:], (
                c, s, kb.reshape(bs, WINDOW, SWA_KV_HEADS, SWA_HD),
                vb.reshape(bs, WINDOW, SWA_KV_HEADS, SWA_HD))):
            lst.append(a)

    return (hp.reshape(bp, lp, D_MODEL), hs.reshape(bs, ls, D_MODEL),
            *[jnp.stack(o) for o in outs])
```

```python
import functools
import itertools

import jax
import jax.numpy as jnp
import numpy as np
from jax import lax
from jax.experimental import pallas as pl
from jax.experimental.pallas import tpu as pltpu

f32 = jnp.float32
bf16 = jnp.bfloat16

D_MODEL = 2048
GROUP_W = 512
GLA_HEADS = 4
GLA_DK = 64
GLA_DV = 128
GLA_RANK = 16
GLA_TAU = 16.0
GLA_CHUNK = 64
SWA_HEADS = 8
SWA_KV_HEADS = 2
SWA_HD = 64
SWA_GROUP = SWA_HEADS // SWA_KV_HEADS
WINDOW = 128
N_MEM = 256
MEM_HEADS = 4
MEM_HD = 128
CONV_W = 3
EPS = 1e-6

LANES = 128

D_IN = 5904
OFF_AB, OFF_AC, OFF_AH, OFF_AZ = 0, 512, 1024, 1536
OFF_GQ, OFF_GK, OFF_GV, OFF_GA, OFF_GZ = 2048, 2304, 2560, 3072, 3088
OFF_SQ, OFF_SK, OFF_SV, OFF_SZ = 3600, 4112, 4240, 4368
OFF_MQ, OFF_MZ = 4880, 5392

VMEM_LIMIT = 56 * 1024 * 1024


def _dot(a, b):
    return jnp.dot(a, b, preferred_element_type=f32)


def _dot_nt(a, b):
    return lax.dot_general(a, b, (((1,), (1,)), ((), ())), preferred_element_type=f32)


def _dot_tn(a, b):
    return lax.dot_general(a, b, (((0,), (0,)), ((), ())), preferred_element_type=f32)


def _split3(x):
    hi = x.astype(bf16)
    r = x - hi.astype(f32)
    mid = r.astype(bf16)
    lo = (r - mid.astype(f32)).astype(bf16)
    return hi, mid, lo


def _silu(x):
    return x * jax.nn.sigmoid(x)


def _log_sigmoid(x):
    return jnp.minimum(x, 0.0) - jnp.log1p(jnp.exp(-jnp.abs(x)))


def _norm_matmul_kernel(x_ref, g_ref, wt_ref, o_ref, hn_ref):
    @pl.when(pl.program_id(1) == 0)
    def _():
        x = x_ref[...]
        y = x * lax.rsqrt(jnp.mean(x * x, axis=-1, keepdims=True) + EPS)
        hn_ref[...] = (y * g_ref[...]).astype(bf16)

    o_ref[...] = _dot_nt(hn_ref[...], wt_ref[...].astype(bf16))


def _norm_matmul(x, g, wt, l, tm, tn):
    m, k = x.shape
    n = wt.shape[1]
    return pl.pallas_call(
        _norm_matmul_kernel,
        grid=(m // tm, pl.cdiv(n, tn)),
        in_specs=[
            pl.BlockSpec((tm, k), lambda i, j: (i, 0)),
            pl.BlockSpec((None, 1, k), lambda i, j: (l, 0, 0)),
            pl.BlockSpec((None, tn, k), lambda i, j: (l, j, 0)),
        ],
        out_specs=pl.BlockSpec((tm, tn), lambda i, j: (i, j)),
        out_shape=jax.ShapeDtypeStruct((m, n), f32),
        scratch_shapes=[pltpu.VMEM((tm, k), bf16)],
        compiler_params=pltpu.CompilerParams(
            dimension_semantics=("arbitrary", "arbitrary"), vmem_limit_bytes=VMEM_LIMIT),
        name="norm_in_proj",
    )(x, g, wt)


def _memory_kv_kernel(x_ref, g_ref, w_ref, gk_ref, k_ref, v_ref, wb_ref):
    @pl.when(pl.program_id(0) == 0)
    def _():
        wb_ref[...] = w_ref[...].astype(bf16)

    x = x_ref[...]
    y = x * lax.rsqrt(jnp.mean(x * x, axis=-1, keepdims=True) + EPS)
    kv = _dot((y * g_ref[...]).astype(bf16), wb_ref[...])
    for h in range(MEM_HEADS):
        kh = kv[:, h * MEM_HD:(h + 1) * MEM_HD]
        kh = kh * lax.rsqrt(jnp.mean(kh * kh, axis=-1, keepdims=True) + EPS)
        k_ref[:, h * MEM_HD:(h + 1) * MEM_HD] = kh * gk_ref[...]
    v_ref[...] = kv[:, GROUP_W:]


def _memory_kv(mem, g, w, gk, l):
    b = mem.shape[0]
    out = jax.ShapeDtypeStruct((b, N_MEM, GROUP_W), f32)
    return pl.pallas_call(
        _memory_kv_kernel,
        grid=(b,),
        in_specs=[
            pl.BlockSpec((None, N_MEM, D_MODEL), lambda i: (i, 0, 0)),
            pl.BlockSpec((None, 1, D_MODEL), lambda i: (l, 0, 0)),
            pl.BlockSpec((None, D_MODEL, 2 * GROUP_W), lambda i: (l, 0, 0)),
            pl.BlockSpec((None, 1, MEM_HD), lambda i: (l, 0, 0)),
        ],
        out_specs=[pl.BlockSpec((None, N_MEM, GROUP_W), lambda i: (i, 0, 0))] * 2,
        out_shape=[out, out],
        scratch_shapes=[pltpu.VMEM((D_MODEL, 2 * GROUP_W), bf16)],
        compiler_params=pltpu.CompilerParams(
            dimension_semantics=("arbitrary",), vmem_limit_bytes=VMEM_LIMIT),
        name="memory_kv",
    )(mem, g, w, gk)


CONV_PAD = 8
N_STACK_SLOTS = 2 + 2 * SWA_KV_HEADS
N_SEQ_IN_PROMPT, N_SEQ_IN_DECODE, N_PARAMS_PROMPT, N_PARAMS_DECODE, N_OUT = 3, 7, 11, 9, 5


def _mixer_kernel(*refs, tile, decode, layer, bb):
    n_seq = N_SEQ_IN_DECODE if decode else N_SEQ_IN_PROMPT
    n_par = N_PARAMS_DECODE if decode else N_PARAMS_PROMPT
    seq_in = refs[:n_seq]
    params = refs[n_seq:n_seq + n_par]
    outs = refs[n_seq + n_par:n_seq + n_par + N_OUT]
    scratch = refs[n_seq + n_par + N_OUT:]

    def view(ref, s):
        if decode and ref.ndim == 2:
            return _RowWindow(ref, s * tile, tile)
        return ref.at[s]

    if decode:
        p_ref, tail_ref, qkn_ref = seq_in[0], scratch[4], scratch[6]
        bd_ref, gsq_ref, gsk_ref = params[-3], params[-5], params[-4]
        tail_ref[...] = p_ref[:, OFF_GZ:D_IN]
        sq0, sk0 = OFF_SQ - OFF_GZ, OFF_SK - OFF_GZ
        qkn_ref[:, 0:GROUP_W] = _head_norm(tail_ref[:, sq0:sq0 + GROUP_W], gsq_ref[...], bd_ref)
        qkn_ref[:, GROUP_W:GROUP_W + LANES] = _head_norm(tail_ref[:, sk0:sk0 + LANES], gsk_ref[...],
                                                         bd_ref)

        def seg_rows(off, width):
            if off < OFF_GZ:
                return p_ref[:, off:off + width]
            return tail_ref[:, off - OFF_GZ:off - OFF_GZ + width]

        gla_in_ref, gla_out_ref = seq_in[4], outs[2]
        _gla_group(seg_rows, bb * tile, tile, params[-8:-5], outs[0],
                   lambda c: _gla_block_diag(gla_in_ref.at[c]),
                   lambda c, state: _store_gla_state(gla_out_ref.at[c], state), carry=False)

    stages = [_mixer_seq([view(r, s) for r in seq_in], params, [view(r, s) for r in outs],
                         [view(r, s) for r in scratch], tile=tile, decode=decode, layer=layer)
              for s in range(bb)]
    for _ in itertools.zip_longest(*stages):
        pass


class _RowWindow:
    def __init__(self, ref, start, size):
        self.ref, self.start, self.size, self.dtype = ref, start, size, ref.dtype

    def _index(self, idx):
        rows, cols = (slice(None), slice(None)) if idx is Ellipsis else idx
        lo, hi, _ = rows.indices(self.size)
        return slice(self.start + lo, self.start + hi), cols

    def __getitem__(self, idx):
        return self.ref[self._index(idx)]

    def __setitem__(self, idx, value):
        self.ref[self._index(idx)] = value


GLA_K_W = GLA_HEADS * GLA_DK
GLA_V_W = GLA_HEADS * GLA_DV


def _gla_block_diag(state_ref):
    rows = []
    for h in range(GLA_HEADS):
        blocks = [state_ref[h] if j == h else jnp.zeros((GLA_DK, GLA_DV), f32)
                  for j in range(GLA_HEADS)]
        rows.append(jnp.concatenate(blocks, axis=1))
    return jnp.concatenate(rows, axis=0)


def _store_gla_state(state_ref, state):
    for h in range(GLA_HEADS):
        state_ref[h] = state[h * GLA_DK:(h + 1) * GLA_DK, h * GLA_DV:(h + 1) * GLA_DV]


def _gla_group(seg, T, C, params, mix_ref, state_in, state_out, carry):
    wup_ref, bga_ref, ggo_ref = params
    n_chunk = T // C
    row = lax.broadcasted_iota(jnp.int32, (T, T), 0)
    col = lax.broadcasted_iota(jnp.int32, (T, T), 1)
    causal = (row // C == col // C) & (row >= col)
    g_a = seg(OFF_GA, GLA_RANK).astype(bf16)
    log_a = _log_sigmoid(_dot(g_a, wup_ref[...].astype(bf16)) + bga_ref[...]) * (1.0 / GLA_TAU)
    la3 = _split3(log_a)
    tril = jnp.where(causal, 1.0, 0.0).astype(bf16)
    in_chunk = jnp.where(lax.broadcasted_iota(jnp.int32, (T, LANES), 0) // C
                         == lax.broadcasted_iota(jnp.int32, (T, LANES), 1), 1.0, 0.0).astype(bf16)
    cum = _dot(tril, la3[0]) + _dot(tril, la3[1]) + _dot(tril, la3[2])
    tot_t = (_dot_tn(la3[0], in_chunk) + _dot_tn(la3[1], in_chunk)
             + _dot_tn(la3[2], in_chunk))
    decay_t = jnp.exp(tot_t)
    g_k = seg(OFF_GK, GLA_K_W)
    qd = ((seg(OFF_GQ, GLA_K_W) * (GLA_DK ** -0.5)) * jnp.exp(cum)).astype(bf16)
    kd = (g_k * jnp.exp(-cum)).astype(bf16)
    k_tail = jnp.concatenate(
        [g_k[c * C:(c + 1) * C] * jnp.exp(cum[(c + 1) * C - 1:(c + 1) * C] - cum[c * C:(c + 1) * C])
         for c in range(n_chunk)], axis=0) if n_chunk > 1 else g_k * jnp.exp(cum[T - 1:T] - cum)
    kt = k_tail.astype(bf16)
    v_b = seg(OFF_GV, GLA_V_W).astype(bf16)
    g_z = seg(OFF_GZ, GROUP_W)

    o_intra = []
    for h in range(GLA_HEADS):
        ks = slice(h * GLA_DK, (h + 1) * GLA_DK)
        attn = jnp.where(causal, _dot_nt(qd[:, ks], kd[:, ks]), 0.0).astype(bf16)
        o_intra.append(_dot(attn, v_b[:, h * GLA_DV:(h + 1) * GLA_DV]))
    o_intra = jnp.concatenate(o_intra, axis=1)

    shape = (GLA_K_W, GLA_V_W)
    on_diag = (lax.broadcasted_iota(jnp.int32, shape, 0) // GLA_DK
               == lax.broadcasted_iota(jnp.int32, shape, 1) // GLA_DV)
    o_chunks = []
    state = None
    for c in range(n_chunk):
        rs = slice(c * C, (c + 1) * C)
        if c == 0 or not carry:
            state = state_in(c)
        o_chunks.append(o_intra[rs] + _dot(qd[rs], state.astype(bf16)))
        update = jnp.where(on_diag, _dot_tn(kt[rs], v_b[rs]), 0.0)
        state = decay_t[:, c:c + 1] * state + update
        state_out(c, state)
    o = jnp.concatenate(o_chunks, axis=0) if n_chunk > 1 else o_chunks[0]
    for h in range(GLA_HEADS):
        vs = slice(h * GLA_DV, (h + 1) * GLA_DV)
        o_h = o[:, vs]
        o_h = o_h * lax.rsqrt(jnp.mean(o_h * o_h, axis=-1, keepdims=True) + EPS) * ggo_ref[...]
        mix_ref[:, GROUP_W + h * GLA_DV:GROUP_W + (h + 1) * GLA_DV] = (
            o_h * _silu(g_z[:, vs])).astype(mix_ref.dtype)


def _head_norm(x, g, bd_ref):
    n_lanes = x.shape[1]
    sq = x * x
    hi = sq.astype(bf16)
    lo = (sq - hi.astype(f32)).astype(bf16)
    bd = bd_ref[0:n_lanes, 0:n_lanes]
    ms = (_dot(hi, bd) + _dot(lo, bd)) * (1.0 / SWA_HD)
    return x * lax.rsqrt(ms + EPS) * g


def _mixer_seq(seq_in, params, outs, scratch, *, tile, decode, layer):
    if decode:
        p_ref, mk_ref, mv_ref, conv_in_ref, gla_in_ref, kc_ref, vc_ref = seq_in
    else:
        x_ref, mk_ref, mv_ref = seq_in
        gn_ref, wt_ref = params[:2]
    (convw_ref, wup_ref, bga_ref, ggo_ref, gsq_ref, gsk_ref, bd_ref, sinks_ref,
     gmq_ref) = params[-N_PARAMS_DECODE:]
    mix_ref, conv_out_ref, gla_out_ref, kbuf_ref, vbuf_ref = outs
    ext_ref, s_ref, kprev_ref, vprev_ref, tail_ref, stk_ref, qkn_ref = scratch

    T = tile
    t = pl.program_id(1)

    def init_state():
        ext_ref[0:CONV_PAD, :] = jnp.zeros((CONV_PAD, GROUP_W), f32)
        if decode:
            ext_ref[CONV_PAD - (CONV_W - 1):CONV_PAD, :] = conv_in_ref[...]
            kprev_ref[...] = kc_ref[...]
            vprev_ref[...] = vc_ref[...]
        else:
            s_ref[...] = jnp.zeros_like(s_ref)
            kprev_ref[...] = jnp.zeros_like(kprev_ref)
            vprev_ref[...] = jnp.zeros_like(vprev_ref)

    if decode:
        init_state()
    else:
        pl.when(t == 0)(init_state)
    yield

    if decode:
        def seg(off, width):
            if off < OFF_GZ:
                return p_ref[:, off:off + width]
            return tail_ref[:, off - OFF_GZ:off - OFF_GZ + width]
    else:
        x = x_ref[...]
        hn = (x * lax.rsqrt(jnp.mean(x * x, axis=-1, keepdims=True) + EPS) * gn_ref[...]).astype(bf16)

        def seg(off, width):
            return _dot_nt(hn, wt_ref[off:off + width, :])

    def stack_rows(pieces, slot):
        r, w = pieces[0].shape
        if r % 8 == 0:
            return jnp.concatenate(pieces, axis=0)
        for j, piece in enumerate(pieces):
            stk_ref[slot, j * r:(j + 1) * r, 0:w] = piece
        return stk_ref[slot, 0:len(pieces) * r, 0:w]

    def unstack_rows(x, n, slot):
        r, w = x.shape[0] // n, x.shape[1]
        if r % 8 == 0:
            return [x[j * r:(j + 1) * r] for j in range(n)]
        stk_ref[slot, 0:n * r, 0:w] = x
        return [stk_ref[slot, j * r:(j + 1) * r, 0:w] for j in range(n)]

    u = seg(OFF_AC, GROUP_W) * seg(OFF_AH, GROUP_W)
    ext_ref[CONV_PAD:CONV_PAD + T, :] = u
    conv = (convw_ref[0:1, :] * ext_ref[CONV_PAD - 2:CONV_PAD - 2 + T, :]
            + convw_ref[1:2, :] * ext_ref[CONV_PAD - 1:CONV_PAD - 1 + T, :]
            + convw_ref[2:3, :] * u)
    mix_ref[:, 0:GROUP_W] = (seg(OFF_AB, GROUP_W) * conv
                             * _silu(seg(OFF_AZ, GROUP_W))).astype(mix_ref.dtype)
    conv_state = ext_ref[CONV_PAD + T - 2:CONV_PAD + T, :]
    ext_ref[CONV_PAD - 2:CONV_PAD, :] = conv_state
    conv_out_ref[...] = conv_state
    yield

    if not decode:
        def keep_state(c, state):
            if c == T // GLA_CHUNK - 1:
                s_ref[...] = state
                _store_gla_state(gla_out_ref, state)

        _gla_group(seg, T, GLA_CHUNK, (wup_ref, bga_ref, ggo_ref), mix_ref,
                   lambda c: s_ref[...], keep_state, carry=True)
        yield

    s_z = seg(OFF_SZ, GROUP_W)
    if decode:
        q_n, k_n = qkn_ref[:, 0:GROUP_W], qkn_ref[:, GROUP_W:GROUP_W + LANES]
        v_n = seg(OFF_SV, LANES)
    else:
        s_q = seg(OFF_SQ, GROUP_W)
        s_kv = seg(OFF_SK, 2 * LANES)
        q_n = _head_norm(s_q, gsq_ref[...], bd_ref)
        k_n = _head_norm(s_kv[:, 0:LANES], gsk_ref[...], bd_ref)
        v_n = s_kv[:, LANES:2 * LANES]
    yield

    BQ = min(WINDOW, T)
    n_blk = T // BQ
    stack = SWA_GROUP
    nk = WINDOW + BQ
    qi = lax.broadcasted_iota(jnp.int32, (stack * BQ, nk), 0) % BQ
    kj = lax.broadcasted_iota(jnp.int32, (stack * BQ, nk), 1)
    dist = qi + WINDOW - kj
    band = (dist >= 0) & (dist < WINDOW)
    srow = lax.broadcasted_iota(jnp.int32, (stack * BQ, 1), 0) // BQ
    for blk in range(n_blk):
        rs = slice(blk * BQ, (blk + 1) * BQ)
        if blk == 0:
            k_prev, v_prev = kprev_ref[...], vprev_ref[...]
            valid = band if decode else band & ((kj >= WINDOW) | (t > 0))
        else:
            ps = slice((blk - 1) * BQ, blk * BQ)
            k_prev, v_prev = k_n[ps], v_n[ps]
            valid = band
        k_cat = jnp.concatenate([k_prev, k_n[rs]], axis=0)
        v_cat = jnp.concatenate([v_prev, v_n[rs]], axis=0)
        for g in range(SWA_KV_HEADS):
            kg = k_cat[:, g * SWA_HD:(g + 1) * SWA_HD].astype(bf16)
            vg = v_cat[:, g * SWA_HD:(g + 1) * SWA_HD].astype(bf16)
            heads = [g * SWA_GROUP + j for j in range(stack)]
            qg = stack_rows([q_n[rs, hd * SWA_HD:(hd + 1) * SWA_HD] for hd in heads],
                            2 + 2 * g).astype(bf16)
            sink = jnp.full((stack * BQ, 1), sinks_ref[layer, heads[0]], f32)
            for j in range(1, stack):
                sink = jnp.where(srow == j, sinks_ref[layer, heads[j]], sink)
            s = _dot_nt(qg, kg) * (SWA_HD ** -0.5)
            s = jnp.where(valid, s, -jnp.inf)
            m = jnp.maximum(jnp.max(s, axis=-1, keepdims=True), sink)
            e = jnp.exp(s - m)
            prob = e / (jnp.sum(e, axis=-1, keepdims=True) + jnp.exp(sink - m))
            o = _dot(prob.astype(bf16), vg)
            for hd, o_hd in zip(heads, unstack_rows(o, stack, 3 + 2 * g)):
                z = s_z[rs, hd * SWA_HD:(hd + 1) * SWA_HD]
                mix_ref[rs, 2 * GROUP_W + hd * SWA_HD:2 * GROUP_W + (hd + 1) * SWA_HD] = (
                    o_hd * _silu(z)).astype(mix_ref.dtype)
            yield

    if decode:
        kbuf_ref[0:WINDOW - T, :] = kc_ref[T:WINDOW, :]
        kbuf_ref[WINDOW - T:WINDOW, :] = k_n
        vbuf_ref[0:WINDOW - T, :] = vc_ref[T:WINDOW, :]
        vbuf_ref[WINDOW - T:WINDOW, :] = v_n
    else:
        kprev_ref[...] = k_n[T - WINDOW:T]
        vprev_ref[...] = v_n[T - WINDOW:T]
        kbuf_ref[...] = k_n[T - WINDOW:T]
        vbuf_ref[...] = v_n[T - WINDOW:T]
    yield

    m_q = seg(OFF_MQ, GROUP_W)
    m_z = seg(OFF_MZ, GROUP_W)
    yield

    def mem_q(h):
        qh = m_q[:, h * MEM_HD:(h + 1) * MEM_HD]
        return qh * lax.rsqrt(jnp.mean(qh * qh, axis=-1, keepdims=True) + EPS) * gmq_ref[...]

    def softmax(s):
        e = jnp.exp(s - jnp.max(s, axis=-1, keepdims=True))
        return e / jnp.sum(e, axis=-1, keepdims=True)

    if decode:
        qs = stack_rows([mem_q(h) for h in range(MEM_HEADS)], 0).astype(bf16)
        s = _dot_nt(qs, mk_ref[...].astype(bf16)) * (MEM_HD ** -0.5)
        shape = (MEM_HEADS * T, MEM_HEADS * N_MEM)
        same_head = (lax.broadcasted_iota(jnp.int32, shape, 0) // T
                     == lax.broadcasted_iota(jnp.int32, shape, 1) % MEM_HEADS)
        yield
        prob = softmax(jnp.where(same_head, s, -jnp.inf))
        o_all = unstack_rows(_dot(prob.astype(bf16), mv_ref[...].astype(bf16)), MEM_HEADS, 1)
    else:
        o_all = []
        for h in range(MEM_HEADS):
            hs = slice(h * MEM_HD, (h + 1) * MEM_HD)
            s = _dot_nt(mem_q(h).astype(bf16), mk_ref[:, hs].astype(bf16)) * (MEM_HD ** -0.5)
            o_all.append(_dot(softmax(s).astype(bf16), mv_ref[:, hs].astype(bf16)))
    for h in range(MEM_HEADS):
        mix_ref[:, 3 * GROUP_W + h * MEM_HD:3 * GROUP_W + (h + 1) * MEM_HD] = (
            o_all[h] * _silu(m_z[:, h * MEM_HD:(h + 1) * MEM_HD])).astype(mix_ref.dtype)


def _mixer(tokens, norm_w, mem_k, mem_v, mem_layer, state, params, layer, tile, bb, decode):
    if decode:
        b, width = state[0].shape[1], tokens.shape[1]
        L = tokens.shape[0] // b
    else:
        b, L, width = tokens.shape
    nt = L // tile
    assert nt == 1 or not decode, "a decode call covers each sequence with a single tile"
    conv_w, w_up, b_ga, g_go, g_sq, g_sk, bd, sinks, g_mq = params

    def tok(width):
        if decode:
            return pl.BlockSpec((bb * tile, width), lambda i, t: (i, 0))
        return pl.BlockSpec((bb, tile, width), lambda i, t: (i, t, 0))

    def per_seq(*shape):
        return pl.BlockSpec((bb,) + shape, lambda i, t: (i,) + (0,) * len(shape))

    def per_seq_at(lyr, *shape):
        return pl.BlockSpec((None, bb) + shape, lambda i, t: (lyr, i) + (0,) * len(shape))

    def param(a):
        return pl.BlockSpec((None,) + a.shape[1:], lambda i, t: (layer,) + (0,) * (a.ndim - 1))

    kv_w = SWA_KV_HEADS * SWA_HD
    state_shapes = [(CONV_W - 1, GROUP_W), (GLA_HEADS, GLA_DK, GLA_DV), (WINDOW, kv_w),
                    (WINDOW, kv_w)]
    in_specs = [tok(width), per_seq_at(mem_layer, *mem_k.shape[2:]),
                per_seq_at(mem_layer, *mem_v.shape[2:])]
    args = [tokens, mem_k, mem_v]
    if decode:
        in_specs += [per_seq_at(layer, *s) for s in state_shapes]
        args += list(state)
    else:
        g_n, w_t = norm_w
        in_specs += [param(g_n), pl.BlockSpec((None,) + w_t.shape[1:], lambda i, t: (layer, 0, 0),
                                              pipeline_mode=pl.Buffered(1))]
        args += [g_n, w_t]
    in_specs += [param(conv_w), param(w_up), param(b_ga), param(g_go), param(g_sq), param(g_sk),
                 pl.BlockSpec(bd.shape, lambda i, t: (0, 0)),
                 pl.BlockSpec(memory_space=pltpu.SMEM), param(g_mq)]
    args += [conv_w, w_up, b_ga, g_go, g_sq, g_sk, bd, sinks, g_mq]
    if decode:
        out_shape = [jax.ShapeDtypeStruct((b * L, 4 * GROUP_W), f32)]
        row_scratch = [pltpu.VMEM((bb * tile, D_IN - OFF_GZ), f32),
                       pltpu.VMEM((bb * tile, GROUP_W + LANES), f32)]
    else:
        out_shape = [jax.ShapeDtypeStruct((b, L, 4 * GROUP_W), bf16)]
        row_scratch = [pltpu.VMEM((bb, 8, LANES), f32)] * 2
    out_shape += [jax.ShapeDtypeStruct((b,) + s, f32) for s in state_shapes]
    out_specs = [tok(4 * GROUP_W)] + [per_seq(*s) for s in state_shapes]
    return pl.pallas_call(
        functools.partial(_mixer_kernel, tile=tile, decode=decode, layer=layer, bb=bb),
        grid=(b // bb, nt),
        in_specs=in_specs,
        out_specs=out_specs,
        out_shape=out_shape,
        scratch_shapes=[
            pltpu.VMEM((bb, CONV_PAD + tile, GROUP_W), f32),
            pltpu.VMEM((bb, 8, LANES) if decode else (bb, GLA_K_W, GLA_V_W), f32),
            pltpu.VMEM((bb, WINDOW, kv_w), f32),
            pltpu.VMEM((bb, WINDOW, kv_w), f32),
            row_scratch[0],
            pltpu.VMEM((bb, N_STACK_SLOTS, MEM_HEADS * min(tile, 8), LANES), f32),
            row_scratch[1],
        ],
        compiler_params=pltpu.CompilerParams(
            dimension_semantics=("arbitrary", "arbitrary"), vmem_limit_bytes=VMEM_LIMIT),
        name="mixer_decode" if decode else "mixer_prompt",
    )(*args)


def _out_proj_kernel(mix_ref, w_ref, x_ref, y_ref, wb_ref):
    @pl.when(pl.program_id(1) == 0)
    def _():
        wb_ref[...] = w_ref[...].astype(bf16)

    y_ref[...] = x_ref[...] + _dot(mix_ref[...].astype(bf16), wb_ref[...])


def _out_proj(mix, w, x, l, tm, tn):
    m, k = mix.shape
    n = w.shape[2]
    return pl.pallas_call(
        _out_proj_kernel,
        grid=(n // tn, m // tm),
        in_specs=[
            pl.BlockSpec((tm, k), lambda j, i: (i, 0)),
            pl.BlockSpec((None, k, tn), lambda j, i: (l, 0, j)),
            pl.BlockSpec((tm, tn), lambda j, i: (i, j)),
        ],
        out_specs=pl.BlockSpec((tm, tn), lambda j, i: (i, j)),
        out_shape=jax.ShapeDtypeStruct((m, n), f32),
        scratch_shapes=[pltpu.VMEM((k, tn), bf16)],
        compiler_params=pltpu.CompilerParams(
            dimension_semantics=("arbitrary", "arbitrary"), vmem_limit_bytes=VMEM_LIMIT),
        name="out_proj",
    )(mix, w, x)


PROMPT_TILE = 512
DECODE_SEQS_PER_STEP = 8
PROJ_TN = 768
OUT_TM, OUT_TN = 1024, 1024

_LANE = np.arange(GROUP_W)
HEAD_BLOCK_DIAG = _LANE[:, None] // SWA_HD == _LANE[None, :] // SWA_HD


def kernel(x_prompt, x_sample, mem_prompt, state_conv, state_gla, cache_swa_k, cache_swa_v,
           cache_mem_k, cache_mem_v, g_norm, w_in, conv_w, w_gla_a_up, b_gla_a, g_gla_o,
           g_swa_q, g_swa_k, swa_sinks, g_mem, w_mem_kv, g_mem_q, g_mem_k, w_out):
    depth = w_in.shape[0]
    bp, lp, _ = x_prompt.shape
    bs, ls, _ = x_sample.shape
    hp = x_prompt.reshape(bp * lp, D_MODEL)
    hs = x_sample.reshape(bs * ls, D_MODEL)

    def row(a):
        return a[:, None, :]

    params = (conv_w, w_gla_a_up, row(b_gla_a), row(g_gla_o),
              row(jnp.tile(g_swa_q, (1, SWA_HEADS))), row(jnp.tile(g_swa_k, (1, SWA_KV_HEADS))),
              jnp.asarray(HEAD_BLOCK_DIAG, bf16), swa_sinks, row(g_mem_q))
    g_n, g_m, g_mk = row(g_norm), row(g_mem), row(g_mem_k)
    w_in_t = jnp.swapaxes(w_in, 1, 2).astype(bf16)
    kv_w = SWA_KV_HEADS * SWA_HD
    state = (state_conv, state_gla, cache_swa_k.reshape(depth, bs, WINDOW, kv_w),
             cache_swa_v.reshape(depth, bs, WINDOW, kv_w))
    mem_k_s = cache_mem_k.reshape(depth, bs, N_MEM * MEM_HEADS, MEM_HD)
    mem_v_s = cache_mem_v.reshape(depth, bs, N_MEM * MEM_HEADS, MEM_HD)

    outs = [[] for _ in range(10)]
    for l in range(depth):
        mk, mv = _memory_kv(mem_prompt, g_m, w_mem_kv, g_mk, l)
        mix, c, s, kb, vb = _mixer(hp.reshape(bp, lp, D_MODEL), (g_n, w_in_t), mk[None], mv[None], 0,
                                   None, params, l, PROMPT_TILE, 1, decode=False)
        hp = _out_proj(mix.reshape(bp * lp, 4 * GROUP_W), w_out, hp, l, OUT_TM, OUT_TN)
        for lst, a in zip(outs[:6], (
                c, s, kb.reshape(bp, WINDOW, SWA_KV_HEADS, SWA_HD),
                vb.reshape(bp, WINDOW, SWA_KV_HEADS, SWA_HD),
                mk.reshape(bp, N_MEM, MEM_HEADS, MEM_HD), mv.reshape(bp, N_MEM, MEM_HEADS, MEM_HD))):
            lst.append(a)

        proj = _norm_matmul(hs, g_n, w_in_t, l, bs * ls, PROJ_TN)
        mix, c, s, kb, vb = _mixer(proj, None, mem_k_s, mem_v_s, l, state, params, l, ls,
                                   DECODE_SEQS_PER_STEP, decode=True)
        hs = _out_proj(mix, w_out, hs, l, bs * ls, OUT_TN)
        for lst, a in zip(outs[6:], (
                c, s, kb.reshape(bs, WINDOW, SWA_KV_HEADS, SWA_HD),
                vb.reshape(bs, WINDOW, SWA_KV_HEADS, SWA_HD))):
            lst.append(a)

    return (hp.reshape(bp, lp, D_MODEL), hs.reshape(bs, ls, D_MODEL),
            *[jnp.stack(o) for o in outs])
```

```python
import functools
import itertools

import jax
import jax.numpy as jnp
import numpy as np
from jax import lax
from jax.experimental import pallas as pl
from jax.experimental.pallas import tpu as pltpu

f32 = jnp.float32
bf16 = jnp.bfloat16

D_MODEL = 2048
GROUP_W = 512
GLA_HEADS = 4
GLA_DK = 64
GLA_DV = 128
GLA_RANK = 16
GLA_TAU = 16.0
GLA_CHUNK = 64
SWA_HEADS = 8
SWA_KV_HEADS = 2
SWA_HD = 64
SWA_GROUP = SWA_HEADS // SWA_KV_HEADS
WINDOW = 128
N_MEM = 256
MEM_HEADS = 4
MEM_HD = 128
CONV_W = 3
EPS = 1e-6

LANES = 128
MXU_CHUNK = 256

D_IN = 5904
OFF_AB, OFF_AC, OFF_AH, OFF_AZ = 0, 512, 1024, 1536
OFF_GQ, OFF_GK, OFF_GV, OFF_GA, OFF_GZ = 2048, 2304, 2560, 3072, 3088
OFF_SQ, OFF_SK, OFF_SV, OFF_SZ = 3600, 4112, 4240, 4368
OFF_MQ, OFF_MZ = 4880, 5392

VMEM_LIMIT = 56 * 1024 * 1024


def _dot(a, b):
    return jnp.dot(a, b, preferred_element_type=f32)


def _dot_nt(a, b):
    return lax.dot_general(a, b, (((1,), (1,)), ((), ())), preferred_element_type=f32)


def _dot_tn(a, b):
    return lax.dot_general(a, b, (((0,), (0,)), ((), ())), preferred_element_type=f32)


def _split3(x):
    hi = x.astype(bf16)
    r = x - hi.astype(f32)
    mid = r.astype(bf16)
    lo = (r - mid.astype(f32)).astype(bf16)
    return hi, mid, lo


def _silu(x):
    return x * jax.nn.sigmoid(x)


def _log_sigmoid(x):
    return jnp.minimum(x, 0.0) - jnp.log1p(jnp.exp(-jnp.abs(x)))


def _norm_matmul_kernel(x_ref, g_ref, wt_ref, o_ref, hn_ref):
    @pl.when(pl.program_id(1) == 0)
    def _():
        x = x_ref[...]
        y = x * lax.rsqrt(jnp.mean(x * x, axis=-1, keepdims=True) + EPS)
        hn_ref[...] = (y * g_ref[...]).astype(bf16)

    o_ref[...] = _dot_nt(hn_ref[...], wt_ref[...].astype(bf16))


def _norm_matmul(x, g, wt, l, tm, tn):
    m, k = x.shape
    n = wt.shape[1]
    return pl.pallas_call(
        _norm_matmul_kernel,
        grid=(m // tm, pl.cdiv(n, tn)),
        in_specs=[
            pl.BlockSpec((tm, k), lambda i, j: (i, 0)),
            pl.BlockSpec((None, 1, k), lambda i, j: (l, 0, 0)),
            pl.BlockSpec((None, tn, k), lambda i, j: (l, j, 0)),
        ],
        out_specs=pl.BlockSpec((tm, tn), lambda i, j: (i, j)),
        out_shape=jax.ShapeDtypeStruct((m, n), f32),
        scratch_shapes=[pltpu.VMEM((tm, k), bf16)],
        compiler_params=pltpu.CompilerParams(
            dimension_semantics=("arbitrary", "arbitrary"), vmem_limit_bytes=VMEM_LIMIT),
        name="norm_in_proj",
    )(x, g, wt)


def _memory_kv_kernel(x_ref, g_ref, w_ref, gk_ref, k_ref, v_ref, wb_ref):
    @pl.when(pl.program_id(0) == 0)
    def _():
        wb_ref[...] = w_ref[...].astype(bf16)

    x = x_ref[...]
    y = x * lax.rsqrt(jnp.mean(x * x, axis=-1, keepdims=True) + EPS)
    kv = _dot((y * g_ref[...]).astype(bf16), wb_ref[...])
    for h in range(MEM_HEADS):
        kh = kv[:, h * MEM_HD:(h + 1) * MEM_HD]
        kh = kh * lax.rsqrt(jnp.mean(kh * kh, axis=-1, keepdims=True) + EPS)
        k_ref[:, h * MEM_HD:(h + 1) * MEM_HD] = kh * gk_ref[...]
    v_ref[...] = kv[:, GROUP_W:]


def _memory_kv(mem, g, w, gk, l):
    b = mem.shape[0]
    out = jax.ShapeDtypeStruct((b, N_MEM, GROUP_W), f32)
    return pl.pallas_call(
        _memory_kv_kernel,
        grid=(b,),
        in_specs=[
            pl.BlockSpec((None, N_MEM, D_MODEL), lambda i: (i, 0, 0)),
            pl.BlockSpec((None, 1, D_MODEL), lambda i: (l, 0, 0)),
            pl.BlockSpec((None, D_MODEL, 2 * GROUP_W), lambda i: (l, 0, 0)),
            pl.BlockSpec((None, 1, MEM_HD), lambda i: (l, 0, 0)),
        ],
        out_specs=[pl.BlockSpec((None, N_MEM, GROUP_W), lambda i: (i, 0, 0))] * 2,
        out_shape=[out, out],
        scratch_shapes=[pltpu.VMEM((D_MODEL, 2 * GROUP_W), bf16)],
        compiler_params=pltpu.CompilerParams(
            dimension_semantics=("arbitrary",), vmem_limit_bytes=VMEM_LIMIT),
        name="memory_kv",
    )(mem, g, w, gk)


CONV_PAD = 8
N_STACK_SLOTS = 2 + 2 * SWA_KV_HEADS
N_SEQ_IN_PROMPT, N_SEQ_IN_DECODE, N_PARAMS_PROMPT, N_PARAMS_DECODE, N_OUT = 3, 7, 11, 9, 5


def _mixer_kernel(*refs, tile, decode, layer, bb):
    n_seq = N_SEQ_IN_DECODE if decode else N_SEQ_IN_PROMPT
    n_par = N_PARAMS_DECODE if decode else N_PARAMS_PROMPT
    seq_in = refs[:n_seq]
    params = refs[n_seq:n_seq + n_par]
    outs = refs[n_seq + n_par:n_seq + n_par + N_OUT]
    scratch = refs[n_seq + n_par + N_OUT:]

    def view(ref, s):
        if decode and ref.ndim == 2:
            return _RowWindow(ref, s * tile, tile)
        return ref.at[s]

    if decode:
        p_ref, tail_ref, qkn_ref = seq_in[0], scratch[4], scratch[6]
        bd_ref, gsq_ref, gsk_ref = params[-3], params[-5], params[-4]
        tail_ref[...] = p_ref[:, OFF_GZ:D_IN]
        sq0, sk0 = OFF_SQ - OFF_GZ, OFF_SK - OFF_GZ
        qkn_ref[:, 0:GROUP_W] = _head_norm(tail_ref[:, sq0:sq0 + GROUP_W], gsq_ref[...], bd_ref)
        qkn_ref[:, GROUP_W:GROUP_W + LANES] = _head_norm(tail_ref[:, sk0:sk0 + LANES], gsk_ref[...],
                                                         bd_ref)

        def seg_rows(off, width):
            if off < OFF_GZ:
                return p_ref[:, off:off + width]
            return tail_ref[:, off - OFF_GZ:off - OFF_GZ + width]

        gla_in_ref, gla_out_ref = seq_in[4], outs[2]
        _gla_group(seg_rows, bb * tile, tile, params[-8:-5], outs[0],
                   lambda c: _gla_block_diag(gla_in_ref.at[c]),
                   lambda c, state: _store_gla_state(gla_out_ref.at[c], state), carry=False)

    stages = [_mixer_seq([view(r, s) for r in seq_in], params, [view(r, s) for r in outs],
                         [view(r, s) for r in scratch], tile=tile, decode=decode, layer=layer)
              for s in range(bb)]
    for _ in itertools.zip_longest(*stages):
        pass


class _RowWindow:
    def __init__(self, ref, start, size):
        self.ref, self.start, self.size, self.dtype = ref, start, size, ref.dtype

    def _index(self, idx):
        rows, cols = (slice(None), slice(None)) if idx is Ellipsis else idx
        lo, hi, _ = rows.indices(self.size)
        return slice(self.start + lo, self.start + hi), cols

    def __getitem__(self, idx):
        return self.ref[self._index(idx)]

    def __setitem__(self, idx, value):
        self.ref[self._index(idx)] = value


GLA_INTRA_ROWS = MXU_CHUNK
GLA_K_W = GLA_HEADS * GLA_DK
GLA_V_W = GLA_HEADS * GLA_DV


def _gla_block_diag(state_ref):
    rows = []
    for h in range(GLA_HEADS):
        blocks = [state_ref[h] if j == h else jnp.zeros((GLA_DK, GLA_DV), f32)
                  for j in range(GLA_HEADS)]
        rows.append(jnp.concatenate(blocks, axis=1))
    return jnp.concatenate(rows, axis=0)


def _store_gla_state(state_ref, state):
    for h in range(GLA_HEADS):
        state_ref[h] = state[h * GLA_DK:(h + 1) * GLA_DK, h * GLA_DV:(h + 1) * GLA_DV]


def _gla_group(seg, T, C, params, mix_ref, state_in, state_out, carry):
    wup_ref, bga_ref, ggo_ref = params
    n_chunk = T // C
    G = min(T, GLA_INTRA_ROWS)
    groups = [slice(i * G, (i + 1) * G) for i in range(T // G)]
    row = lax.broadcasted_iota(jnp.int32, (G, G), 0)
    col = lax.broadcasted_iota(jnp.int32, (G, G), 1)
    causal = (row // C == col // C) & (row >= col)
    g_a = seg(OFF_GA, GLA_RANK).astype(bf16)
    log_a = _log_sigmoid(_dot(g_a, wup_ref[...].astype(bf16)) + bga_ref[...]) * (1.0 / GLA_TAU)
    la3 = _split3(log_a)
    tril = jnp.where(causal, 1.0, 0.0).astype(bf16)
    in_chunk = jnp.where(lax.broadcasted_iota(jnp.int32, (T, LANES), 0) // C
                         == lax.broadcasted_iota(jnp.int32, (T, LANES), 1), 1.0, 0.0).astype(bf16)
    cum = jnp.concatenate(
        [_dot(tril, la3[0][r]) + _dot(tril, la3[1][r]) + _dot(tril, la3[2][r]) for r in groups],
        axis=0)
    tot_t = (_dot_tn(la3[0], in_chunk) + _dot_tn(la3[1], in_chunk)
             + _dot_tn(la3[2], in_chunk))
    decay_t = jnp.exp(tot_t)
    g_k = seg(OFF_GK, GLA_K_W)
    qd = ((seg(OFF_GQ, GLA_K_W) * (GLA_DK ** -0.5)) * jnp.exp(cum)).astype(bf16)
    kd = (g_k * jnp.exp(-cum)).astype(bf16)
    k_tail = jnp.concatenate(
        [g_k[c * C:(c + 1) * C] * jnp.exp(cum[(c + 1) * C - 1:(c + 1) * C] - cum[c * C:(c + 1) * C])
         for c in range(n_chunk)], axis=0) if n_chunk > 1 else g_k * jnp.exp(cum[T - 1:T] - cum)
    kt = k_tail.astype(bf16)
    v_b = seg(OFF_GV, GLA_V_W).astype(bf16)
    g_z = seg(OFF_GZ, GROUP_W)

    o_intra = []
    for r in groups:
        o_heads = []
        for h in range(GLA_HEADS):
            ks = slice(h * GLA_DK, (h + 1) * GLA_DK)
            attn = jnp.where(causal, _dot_nt(qd[r, ks], kd[r, ks]), 0.0).astype(bf16)
            o_heads.append(_dot(attn, v_b[r, h * GLA_DV:(h + 1) * GLA_DV]))
        o_intra.append(jnp.concatenate(o_heads, axis=1))
    o_intra = jnp.concatenate(o_intra, axis=0)

    shape = (GLA_K_W, GLA_V_W)
    on_diag = (lax.broadcasted_iota(jnp.int32, shape, 0) // GLA_DK
               == lax.broadcasted_iota(jnp.int32, shape, 1) // GLA_DV)
    o_chunks = []
    state = None
    for c in range(n_chunk):
        rs = slice(c * C, (c + 1) * C)
        if c == 0 or not carry:
            state = state_in(c)
        o_chunks.append(o_intra[rs] + _dot(qd[rs], state.astype(bf16)))
        update = jnp.where(on_diag, _dot_tn(kt[rs], v_b[rs]), 0.0)
        state = decay_t[:, c:c + 1] * state + update
        state_out(c, state)
    o = jnp.concatenate(o_chunks, axis=0) if n_chunk > 1 else o_chunks[0]
    for h in range(GLA_HEADS):
        vs = slice(h * GLA_DV, (h + 1) * GLA_DV)
        o_h = o[:, vs]
        o_h = o_h * lax.rsqrt(jnp.mean(o_h * o_h, axis=-1, keepdims=True) + EPS) * ggo_ref[...]
        mix_ref[:, GROUP_W + h * GLA_DV:GROUP_W + (h + 1) * GLA_DV] = (
            o_h * _silu(g_z[:, vs])).astype(mix_ref.dtype)


def _head_norm(x, g, bd_ref):
    rows, n_lanes = x.shape
    w = min(n_lanes, MXU_CHUNK)
    pieces = n_lanes // w
    bd = bd_ref[0:w, 0:w]

    def head_sums(v):
        stacked = jnp.concatenate([v[:, i * w:(i + 1) * w] for i in range(pieces)], axis=0)
        r = _dot(stacked, bd)
        return jnp.concatenate([r[i * rows:(i + 1) * rows] for i in range(pieces)], axis=1)

    sq = x * x
    hi = sq.astype(bf16)
    lo = (sq - hi.astype(f32)).astype(bf16)
    ms = (head_sums(hi) + head_sums(lo)) * (1.0 / SWA_HD)
    return x * lax.rsqrt(ms + EPS) * g


def _mixer_seq(seq_in, params, outs, scratch, *, tile, decode, layer):
    if decode:
        p_ref, mk_ref, mv_ref, conv_in_ref, gla_in_ref, kc_ref, vc_ref = seq_in
    else:
        x_ref, mk_ref, mv_ref = seq_in
        gn_ref, wt_ref = params[:2]
    (convw_ref, wup_ref, bga_ref, ggo_ref, gsq_ref, gsk_ref, bd_ref, sinks_ref,
     gmq_ref) = params[-N_PARAMS_DECODE:]
    mix_ref, conv_out_ref, gla_out_ref, kbuf_ref, vbuf_ref = outs
    ext_ref, s_ref, kprev_ref, vprev_ref, tail_ref, stk_ref, qkn_ref = scratch

    T = tile
    t = pl.program_id(1)

    def init_state():
        ext_ref[0:CONV_PAD, :] = jnp.zeros((CONV_PAD, GROUP_W), f32)
        if decode:
            ext_ref[CONV_PAD - (CONV_W - 1):CONV_PAD, :] = conv_in_ref[...]
            kprev_ref[...] = kc_ref[...]
            vprev_ref[...] = vc_ref[...]
        else:
            s_ref[...] = jnp.zeros_like(s_ref)
            kprev_ref[...] = jnp.zeros_like(kprev_ref)
            vprev_ref[...] = jnp.zeros_like(vprev_ref)

    if decode:
        init_state()
    else:
        pl.when(t == 0)(init_state)
    yield

    if decode:
        def seg(off, width):
            if off < OFF_GZ:
                return p_ref[:, off:off + width]
            return tail_ref[:, off - OFF_GZ:off - OFF_GZ + width]
    else:
        x = x_ref[...]
        hn = (x * lax.rsqrt(jnp.mean(x * x, axis=-1, keepdims=True) + EPS) * gn_ref[...]).astype(bf16)

        def seg(off, width):
            return _dot_nt(hn, wt_ref[off:off + width, :])

    def stack_rows(pieces, slot):
        r, w = pieces[0].shape
        if r % 8 == 0:
            return jnp.concatenate(pieces, axis=0)
        for j, piece in enumerate(pieces):
            stk_ref[slot, j * r:(j + 1) * r, 0:w] = piece
        return stk_ref[slot, 0:len(pieces) * r, 0:w]

    def unstack_rows(x, n, slot):
        r, w = x.shape[0] // n, x.shape[1]
        if r % 8 == 0:
            return [x[j * r:(j + 1) * r] for j in range(n)]
        stk_ref[slot, 0:n * r, 0:w] = x
        return [stk_ref[slot, j * r:(j + 1) * r, 0:w] for j in range(n)]

    u = seg(OFF_AC, GROUP_W) * seg(OFF_AH, GROUP_W)
    ext_ref[CONV_PAD:CONV_PAD + T, :] = u
    conv = (convw_ref[0:1, :] * ext_ref[CONV_PAD - 2:CONV_PAD - 2 + T, :]
            + convw_ref[1:2, :] * ext_ref[CONV_PAD - 1:CONV_PAD - 1 + T, :]
            + convw_ref[2:3, :] * u)
    mix_ref[:, 0:GROUP_W] = (seg(OFF_AB, GROUP_W) * conv
                             * _silu(seg(OFF_AZ, GROUP_W))).astype(mix_ref.dtype)
    conv_state = ext_ref[CONV_PAD + T - 2:CONV_PAD + T, :]
    ext_ref[CONV_PAD - 2:CONV_PAD, :] = conv_state
    conv_out_ref[...] = conv_state
    yield

    if not decode:
        def keep_state(c, state):
            if c == T // GLA_CHUNK - 1:
                s_ref[...] = state
                _store_gla_state(gla_out_ref, state)

        _gla_group(seg, T, GLA_CHUNK, (wup_ref, bga_ref, ggo_ref), mix_ref,
                   lambda c: s_ref[...], keep_state, carry=True)
        yield

    s_z = seg(OFF_SZ, GROUP_W)
    if decode:
        q_n, k_n = qkn_ref[:, 0:GROUP_W], qkn_ref[:, GROUP_W:GROUP_W + LANES]
        v_n = seg(OFF_SV, LANES)
    else:
        s_q = seg(OFF_SQ, GROUP_W)
        s_kv = seg(OFF_SK, 2 * LANES)
        q_n = _head_norm(s_q, gsq_ref[...], bd_ref)
        k_n = _head_norm(s_kv[:, 0:LANES], gsk_ref[...], bd_ref)
        v_n = s_kv[:, LANES:2 * LANES]
    yield

    BQ = min(WINDOW, T)
    n_blk = T // BQ
    stack = SWA_GROUP
    nk = WINDOW + BQ
    qi = lax.broadcasted_iota(jnp.int32, (stack * BQ, nk), 0) % BQ
    kj = lax.broadcasted_iota(jnp.int32, (stack * BQ, nk), 1)
    dist = qi + WINDOW - kj
    band = (dist >= 0) & (dist < WINDOW)
    srow = lax.broadcasted_iota(jnp.int32, (stack * BQ, 1), 0) // BQ
    for blk in range(n_blk):
        rs = slice(blk * BQ, (blk + 1) * BQ)
        if blk == 0:
            k_prev, v_prev = kprev_ref[...], vprev_ref[...]
            valid = band if decode else band & ((kj >= WINDOW) | (t > 0))
        else:
            ps = slice((blk - 1) * BQ, blk * BQ)
            k_prev, v_prev = k_n[ps], v_n[ps]
            valid = band
        k_cat = jnp.concatenate([k_prev, k_n[rs]], axis=0)
        v_cat = jnp.concatenate([v_prev, v_n[rs]], axis=0)
        for g in range(SWA_KV_HEADS):
            kg = k_cat[:, g * SWA_HD:(g + 1) * SWA_HD].astype(bf16)
            vg = v_cat[:, g * SWA_HD:(g + 1) * SWA_HD].astype(bf16)
            heads = [g * SWA_GROUP + j for j in range(stack)]
            qg = stack_rows([q_n[rs, hd * SWA_HD:(hd + 1) * SWA_HD] for hd in heads],
                            2 + 2 * g).astype(bf16)
            sink = jnp.full((stack * BQ, 1), sinks_ref[layer, heads[0]], f32)
            for j in range(1, stack):
                sink = jnp.where(srow == j, sinks_ref[layer, heads[j]], sink)
            s = _dot_nt(qg, kg) * (SWA_HD ** -0.5)
            s = jnp.where(valid, s, -jnp.inf)
            m = jnp.maximum(jnp.max(s, axis=-1, keepdims=True), sink)
            e = jnp.exp(s - m)
            prob = e / (jnp.sum(e, axis=-1, keepdims=True) + jnp.exp(sink - m))
            o = _dot(prob.astype(bf16), vg)
            for hd, o_hd in zip(heads, unstack_rows(o, stack, 3 + 2 * g)):
                z = s_z[rs, hd * SWA_HD:(hd + 1) * SWA_HD]
                mix_ref[rs, 2 * GROUP_W + hd * SWA_HD:2 * GROUP_W + (hd + 1) * SWA_HD] = (
                    o_hd * _silu(z)).astype(mix_ref.dtype)
            yield

    if decode:
        kbuf_ref[0:WINDOW - T, :] = kc_ref[T:WINDOW, :]
        kbuf_ref[WINDOW - T:WINDOW, :] = k_n
        vbuf_ref[0:WINDOW - T, :] = vc_ref[T:WINDOW, :]
        vbuf_ref[WINDOW - T:WINDOW, :] = v_n
    else:
        kprev_ref[...] = k_n[T - WINDOW:T]
        vprev_ref[...] = v_n[T - WINDOW:T]
        kbuf_ref[...] = k_n[T - WINDOW:T]
        vbuf_ref[...] = v_n[T - WINDOW:T]
    yield

    m_q = seg(OFF_MQ, GROUP_W)
    m_z = seg(OFF_MZ, GROUP_W)
    yield

    def mem_q(h):
        qh = m_q[:, h * MEM_HD:(h + 1) * MEM_HD]
        return qh * lax.rsqrt(jnp.mean(qh * qh, axis=-1, keepdims=True) + EPS) * gmq_ref[...]

    def softmax(s):
        e = jnp.exp(s - jnp.max(s, axis=-1, keepdims=True))
        return e / jnp.sum(e, axis=-1, keepdims=True)

    if decode:
        qs = stack_rows([mem_q(h) for h in range(MEM_HEADS)], 0).astype(bf16)
        s = _dot_nt(qs, mk_ref[...].astype(bf16)) * (MEM_HD ** -0.5)
        shape = (MEM_HEADS * T, MEM_HEADS * N_MEM)
        same_head = (lax.broadcasted_iota(jnp.int32, shape, 0) // T
                     == lax.broadcasted_iota(jnp.int32, shape, 1) % MEM_HEADS)
        yield
        prob = softmax(jnp.where(same_head, s, -jnp.inf))
        o_all = unstack_rows(_dot(prob.astype(bf16), mv_ref[...].astype(bf16)), MEM_HEADS, 1)
    else:
        o_all = []
        for h in range(MEM_HEADS):
            hs = slice(h * MEM_HD, (h + 1) * MEM_HD)
            s = _dot_nt(mem_q(h).astype(bf16), mk_ref[:, hs].astype(bf16)) * (MEM_HD ** -0.5)
            o_all.append(_dot(softmax(s).astype(bf16), mv_ref[:, hs].astype(bf16)))
    for h in range(MEM_HEADS):
        mix_ref[:, 3 * GROUP_W + h * MEM_HD:3 * GROUP_W + (h + 1) * MEM_HD] = (
            o_all[h] * _silu(m_z[:, h * MEM_HD:(h + 1) * MEM_HD])).astype(mix_ref.dtype)


def _mixer(tokens, norm_w, mem_k, mem_v, mem_layer, state, params, layer, tile, bb, decode):
    if decode:
        b, width = state[0].shape[1], tokens.shape[1]
        L = tokens.shape[0] // b
    else:
        b, L, width = tokens.shape
    nt = L // tile
    assert nt == 1 or not decode, "a decode call covers each sequence with a single tile"
    conv_w, w_up, b_ga, g_go, g_sq, g_sk, bd, sinks, g_mq = params

    def tok(width):
        if decode:
            return pl.BlockSpec((bb * tile, width), lambda i, t: (i, 0))
        return pl.BlockSpec((bb, tile, width), lambda i, t: (i, t, 0))

    def per_seq(*shape):
        return pl.BlockSpec((bb,) + shape, lambda i, t: (i,) + (0,) * len(shape))

    def per_seq_at(lyr, *shape):
        return pl.BlockSpec((None, bb) + shape, lambda i, t: (lyr, i) + (0,) * len(shape))

    def param(a):
        return pl.BlockSpec((None,) + a.shape[1:], lambda i, t: (layer,) + (0,) * (a.ndim - 1))

    kv_w = SWA_KV_HEADS * SWA_HD
    state_shapes = [(CONV_W - 1, GROUP_W), (GLA_HEADS, GLA_DK, GLA_DV), (WINDOW, kv_w),
                    (WINDOW, kv_w)]
    in_specs = [tok(width), per_seq_at(mem_layer, *mem_k.shape[2:]),
                per_seq_at(mem_layer, *mem_v.shape[2:])]
    args = [tokens, mem_k, mem_v]
    if decode:
        in_specs += [per_seq_at(layer, *s) for s in state_shapes]
        args += list(state)
    else:
        g_n, w_t = norm_w
        in_specs += [param(g_n), pl.BlockSpec((None,) + w_t.shape[1:], lambda i, t: (layer, 0, 0),
                                              pipeline_mode=pl.Buffered(1))]
        args += [g_n, w_t]
    in_specs += [param(conv_w), param(w_up), param(b_ga), param(g_go), param(g_sq), param(g_sk),
                 pl.BlockSpec(bd.shape, lambda i, t: (0, 0)),
                 pl.BlockSpec(memory_space=pltpu.SMEM), param(g_mq)]
    args += [conv_w, w_up, b_ga, g_go, g_sq, g_sk, bd, sinks, g_mq]
    if decode:
        out_shape = [jax.ShapeDtypeStruct((b * L, 4 * GROUP_W), f32)]
        row_scratch = [pltpu.VMEM((bb * tile, D_IN - OFF_GZ), f32),
                       pltpu.VMEM((bb * tile, GROUP_W + LANES), f32)]
    else:
        out_shape = [jax.ShapeDtypeStruct((b, L, 4 * GROUP_W), bf16)]
        row_scratch = [pltpu.VMEM((bb, 8, LANES), f32)] * 2
    out_shape += [jax.ShapeDtypeStruct((b,) + s, f32) for s in state_shapes]
    out_specs = [tok(4 * GROUP_W)] + [per_seq(*s) for s in state_shapes]
    return pl.pallas_call(
        functools.partial(_mixer_kernel, tile=tile, decode=decode, layer=layer, bb=bb),
        grid=(b // bb, nt),
        in_specs=in_specs,
        out_specs=out_specs,
        out_shape=out_shape,
        scratch_shapes=[
            pltpu.VMEM((bb, CONV_PAD + tile, GROUP_W), f32),
            pltpu.VMEM((bb, 8, LANES) if decode else (bb, GLA_K_W, GLA_V_W), f32),
            pltpu.VMEM((bb, WINDOW, kv_w), f32),
            pltpu.VMEM((bb, WINDOW, kv_w), f32),
            row_scratch[0],
            pltpu.VMEM((bb, N_STACK_SLOTS, MEM_HEADS * min(tile, 8), LANES), f32),
            row_scratch[1],
        ],
        compiler_params=pltpu.CompilerParams(
            dimension_semantics=("arbitrary", "arbitrary"), vmem_limit_bytes=VMEM_LIMIT),
        name="mixer_decode" if decode else "mixer_prompt",
    )(*args)


def _out_proj_kernel(mix_ref, w_ref, x_ref, y_ref, wb_ref):
    @pl.when(pl.program_id(1) == 0)
    def _():
        wb_ref[...] = w_ref[...].astype(bf16)

    y_ref[...] = x_ref[...] + _dot(mix_ref[...].astype(bf16), wb_ref[...])


def _out_proj(mix, w, x, l, tm, tn):
    m, k = mix.shape
    n = w.shape[2]
    return pl.pallas_call(
        _out_proj_kernel,
        grid=(n // tn, m // tm),
        in_specs=[
            pl.BlockSpec((tm, k), lambda j, i: (i, 0)),
            pl.BlockSpec((None, k, tn), lambda j, i: (l, 0, j)),
            pl.BlockSpec((tm, tn), lambda j, i: (i, j)),
        ],
        out_specs=pl.BlockSpec((tm, tn), lambda j, i: (i, j)),
        out_shape=jax.ShapeDtypeStruct((m, n), f32),
        scratch_shapes=[pltpu.VMEM((k, tn), bf16)],
        compiler_params=pltpu.CompilerParams(
            dimension_semantics=("arbitrary", "arbitrary"), vmem_limit_bytes=VMEM_LIMIT),
        name="out_proj",
    )(mix, w, x)


PROMPT_TILE = 512
DECODE_SEQS_PER_STEP = 8
PROJ_TN = 1536
OUT_TM, OUT_TN = 1024, 1024

_LANE = np.arange(GROUP_W)
HEAD_BLOCK_DIAG = _LANE[:, None] // SWA_HD == _LANE[None, :] // SWA_HD


def kernel(x_prompt, x_sample, mem_prompt, state_conv, state_gla, cache_swa_k, cache_swa_v,
           cache_mem_k, cache_mem_v, g_norm, w_in, conv_w, w_gla_a_up, b_gla_a, g_gla_o,
           g_swa_q, g_swa_k, swa_sinks, g_mem, w_mem_kv, g_mem_q, g_mem_k, w_out):
    depth = w_in.shape[0]
    bp, lp, _ = x_prompt.shape
    bs, ls, _ = x_sample.shape
    hp = x_prompt.reshape(bp * lp, D_MODEL)
    hs = x_sample.reshape(bs * ls, D_MODEL)

    def row(a):
        return a[:, None, :]

    params = (conv_w, w_gla_a_up, row(b_gla_a), row(g_gla_o),
              row(jnp.tile(g_swa_q, (1, SWA_HEADS))), row(jnp.tile(g_swa_k, (1, SWA_KV_HEADS))),
              jnp.asarray(HEAD_BLOCK_DIAG, bf16), swa_sinks, row(g_mem_q))
    g_n, g_m, g_mk = row(g_norm), row(g_mem), row(g_mem_k)
    w_in_t = jnp.swapaxes(w_in, 1, 2).astype(bf16)
    kv_w = SWA_KV_HEADS * SWA_HD
    state = (state_conv, state_gla, cache_swa_k.reshape(depth, bs, WINDOW, kv_w),
             cache_swa_v.reshape(depth, bs, WINDOW, kv_w))
    mem_k_s = cache_mem_k.reshape(depth, bs, N_MEM * MEM_HEADS, MEM_HD)
    mem_v_s = cache_mem_v.reshape(depth, bs, N_MEM * MEM_HEADS, MEM_HD)

    outs = [[] for _ in range(10)]
    for l in range(depth):
        mk, mv = _memory_kv(mem_prompt, g_m, w_mem_kv, g_mk, l)
        mix, c, s, kb, vb = _mixer(hp.reshape(bp, lp, D_MODEL), (g_n, w_in_t), mk[None], mv[None], 0,
                                   None, params, l, PROMPT_TILE, 1, decode=False)
        hp = _out_proj(mix.reshape(bp * lp, 4 * GROUP_W), w_out, hp, l, OUT_TM, OUT_TN)
        for lst, a in zip(outs[:6], (
                c, s, kb.reshape(bp, WINDOW, SWA_KV_HEADS, SWA_HD),
                vb.reshape(bp, WINDOW, SWA_KV_HEADS, SWA_HD),
                mk.reshape(bp, N_MEM, MEM_HEADS, MEM_HD), mv.reshape(bp, N_MEM, MEM_HEADS, MEM_HD))):
            lst.append(a)

        proj = _norm_matmul(hs, g_n, w_in_t, l, bs * ls, PROJ_TN)
        mix, c, s, kb, vb = _mixer(proj, None, mem_k_s, mem_v_s, l, state, params, l, ls,
                                   DECODE_SEQS_PER_STEP, decode=True)
        hs = _out_proj(mix, w_out, hs, l, bs * ls, OUT_TN)
        for lst, a in zip(outs[6:], (
                c, s, kb.reshape(bs, WINDOW, SWA_KV_HEADS, SWA_HD),
                vb.reshape(bs, WINDOW, SWA_KV_HEADS, SWA_HD))):
            lst.append(a)

    return (hp.reshape(bp, lp, D_MODEL), hs.reshape(bs, ls, D_MODEL),
            *[jnp.stack(o) for o in outs])
```

```python
import functools
import itertools

import jax
import jax.numpy as jnp
import numpy as np
from jax import lax
from jax.experimental import pallas as pl
from jax.experimental.pallas import tpu as pltpu

f32 = jnp.float32
bf16 = jnp.bfloat16

D_MODEL = 2048
GROUP_W = 512
GLA_HEADS = 4
GLA_DK = 64
GLA_DV = 128
GLA_RANK = 16
GLA_TAU = 16.0
GLA_CHUNK = 64
SWA_HEADS = 8
SWA_KV_HEADS = 2
SWA_HD = 64
SWA_GROUP = SWA_HEADS // SWA_KV_HEADS
WINDOW = 128
N_MEM = 256
MEM_HEADS = 4
MEM_HD = 128
CONV_W = 3
EPS = 1e-6

LANES = 128
MXU_CHUNK = 256

D_IN = 5904
OFF_AB, OFF_AC, OFF_AH, OFF_AZ = 0, 512, 1024, 1536
OFF_GQ, OFF_GK, OFF_GV, OFF_GA, OFF_GZ = 2048, 2304, 2560, 3072, 3088
OFF_SQ, OFF_SK, OFF_SV, OFF_SZ = 3600, 4112, 4240, 4368
OFF_MQ, OFF_MZ = 4880, 5392

VMEM_LIMIT = 56 * 1024 * 1024


def _dot(a, b):
    return jnp.dot(a, b, preferred_element_type=f32)


def _dot_nt(a, b):
    return lax.dot_general(a, b, (((1,), (1,)), ((), ())), preferred_element_type=f32)


def _dot_tn(a, b):
    return lax.dot_general(a, b, (((0,), (0,)), ((), ())), preferred_element_type=f32)


def _split3(x):
    hi = x.astype(bf16)
    r = x - hi.astype(f32)
    mid = r.astype(bf16)
    lo = (r - mid.astype(f32)).astype(bf16)
    return hi, mid, lo


def _silu(x):
    return x * jax.nn.sigmoid(x)


def _log_sigmoid(x):
    return jnp.minimum(x, 0.0) - jnp.log1p(jnp.exp(-jnp.abs(x)))


def _norm_matmul_kernel(x_ref, g_ref, wt_ref, o_ref, wb_ref, hn_ref):
    x = x_ref[...]
    y = x * lax.rsqrt(jnp.mean(x * x, axis=-1, keepdims=True) + EPS)
    hn_ref[...] = (y * g_ref[...]).astype(bf16)
    wb_ref[...] = wt_ref[...].astype(bf16)
    o_ref[...] = _dot_nt(hn_ref[...], wb_ref[...])


def _norm_matmul(x, g, wt, l, tn):
    m, k = x.shape
    n = wt.shape[1]
    return pl.pallas_call(
        _norm_matmul_kernel,
        grid=(pl.cdiv(n, tn),),
        in_specs=[
            pl.BlockSpec((m, k), lambda j: (0, 0)),
            pl.BlockSpec((None, 1, k), lambda j: (l, 0, 0)),
            pl.BlockSpec((None, tn, k), lambda j: (l, j, 0)),
        ],
        out_specs=[pl.BlockSpec((m, tn), lambda j: (0, j)), pl.BlockSpec((tn, k), lambda j: (j, 0))],
        out_shape=[jax.ShapeDtypeStruct((m, n), f32), jax.ShapeDtypeStruct((n, k), bf16)],
        scratch_shapes=[pltpu.VMEM((m, k), bf16)],
        compiler_params=pltpu.CompilerParams(
            dimension_semantics=("arbitrary",), vmem_limit_bytes=VMEM_LIMIT),
        name="norm_in_proj",
    )(x, g, wt)


def _memory_kv_kernel(x_ref, g_ref, w_ref, gk_ref, k_ref, v_ref, wb_ref):
    @pl.when(pl.program_id(0) == 0)
    def _():
        wb_ref[...] = w_ref[...].astype(bf16)

    x = x_ref[...]
    y = x * lax.rsqrt(jnp.mean(x * x, axis=-1, keepdims=True) + EPS)
    kv = _dot((y * g_ref[...]).astype(bf16), wb_ref[...])
    for h in range(MEM_HEADS):
        kh = kv[:, h * MEM_HD:(h + 1) * MEM_HD]
        kh = kh * lax.rsqrt(jnp.mean(kh * kh, axis=-1, keepdims=True) + EPS)
        k_ref[:, h * MEM_HD:(h + 1) * MEM_HD] = kh * gk_ref[...]
    v_ref[...] = kv[:, GROUP_W:]


def _memory_kv(mem, g, w, gk, l):
    b = mem.shape[0]
    out = jax.ShapeDtypeStruct((b, N_MEM, GROUP_W), f32)
    return pl.pallas_call(
        _memory_kv_kernel,
        grid=(b,),
        in_specs=[
            pl.BlockSpec((None, N_MEM, D_MODEL), lambda i: (i, 0, 0)),
            pl.BlockSpec((None, 1, D_MODEL), lambda i: (l, 0, 0)),
            pl.BlockSpec((None, D_MODEL, 2 * GROUP_W), lambda i: (l, 0, 0)),
            pl.BlockSpec((None, 1, MEM_HD), lambda i: (l, 0, 0)),
        ],
        out_specs=[pl.BlockSpec((None, N_MEM, GROUP_W), lambda i: (i, 0, 0))] * 2,
        out_shape=[out, out],
        scratch_shapes=[pltpu.VMEM((D_MODEL, 2 * GROUP_W), bf16)],
        compiler_params=pltpu.CompilerParams(
            dimension_semantics=("arbitrary",), vmem_limit_bytes=VMEM_LIMIT),
        name="memory_kv",
    )(mem, g, w, gk)


CONV_PAD = 8
N_STACK_SLOTS = 2 + 2 * SWA_KV_HEADS
N_SEQ_IN_PROMPT, N_SEQ_IN_DECODE, N_PARAMS_PROMPT, N_PARAMS_DECODE, N_OUT = 3, 7, 11, 9, 5


def _mixer_kernel(*refs, tile, decode, layer, bb):
    n_seq = N_SEQ_IN_DECODE if decode else N_SEQ_IN_PROMPT
    n_par = N_PARAMS_DECODE if decode else N_PARAMS_PROMPT
    seq_in = refs[:n_seq]
    params = refs[n_seq:n_seq + n_par]
    outs = refs[n_seq + n_par:n_seq + n_par + N_OUT]
    scratch = refs[n_seq + n_par + N_OUT:]

    def view(ref, s):
        if decode and ref.ndim == 2:
            return _RowWindow(ref, s * tile, tile)
        return ref.at[s]

    if decode:
        p_ref, tail_ref, qkn_ref = seq_in[0], scratch[4], scratch[6]
        bd_ref, gsq_ref, gsk_ref = params[-3], params[-5], params[-4]
        tail_ref[...] = p_ref[:, OFF_GZ:D_IN]
        sq0, sk0 = OFF_SQ - OFF_GZ, OFF_SK - OFF_GZ
        qkn_ref[:, 0:GROUP_W] = _head_norm(tail_ref[:, sq0:sq0 + GROUP_W], gsq_ref[...], bd_ref)
        qkn_ref[:, GROUP_W:GROUP_W + LANES] = _head_norm(tail_ref[:, sk0:sk0 + LANES], gsk_ref[...],
                                                         bd_ref)

        def seg_rows(off, width):
            if off < OFF_GZ:
                return p_ref[:, off:off + width]
            return tail_ref[:, off - OFF_GZ:off - OFF_GZ + width]

        gla_in_ref, gla_out_ref = seq_in[4], outs[2]
        _gla_group(seg_rows, bb * tile, tile, params[-8:-5], outs[0],
                   lambda c: _gla_block_diag(gla_in_ref.at[c]),
                   lambda c, state: _store_gla_state(gla_out_ref.at[c], state), carry=False)

    stages = [_mixer_seq([view(r, s) for r in seq_in], params, [view(r, s) for r in outs],
                         [view(r, s) for r in scratch], tile=tile, decode=decode, layer=layer)
              for s in range(bb)]
    for _ in itertools.zip_longest(*stages):
        pass


class _RowWindow:
    def __init__(self, ref, start, size):
        self.ref, self.start, self.size, self.dtype = ref, start, size, ref.dtype

    def _index(self, idx):
        rows, cols = (slice(None), slice(None)) if idx is Ellipsis else idx
        lo, hi, _ = rows.indices(self.size)
        return slice(self.start + lo, self.start + hi), cols

    def __getitem__(self, idx):
        return self.ref[self._index(idx)]

    def __setitem__(self, idx, value):
        self.ref[self._index(idx)] = value


GLA_INTRA_ROWS = MXU_CHUNK
GLA_K_W = GLA_HEADS * GLA_DK
GLA_V_W = GLA_HEADS * GLA_DV


def _gla_block_diag(state_ref):
    rows = []
    for h in range(GLA_HEADS):
        blocks = [state_ref[h] if j == h else jnp.zeros((GLA_DK, GLA_DV), f32)
                  for j in range(GLA_HEADS)]
        rows.append(jnp.concatenate(blocks, axis=1))
    return jnp.concatenate(rows, axis=0)


def _store_gla_state(state_ref, state):
    for h in range(GLA_HEADS):
        state_ref[h] = state[h * GLA_DK:(h + 1) * GLA_DK, h * GLA_DV:(h + 1) * GLA_DV]


def _gla_group(seg, T, C, params, mix_ref, state_in, state_out, carry):
    wup_ref, bga_ref, ggo_ref = params
    n_chunk = T // C
    G = min(T, GLA_INTRA_ROWS)
    groups = [slice(i * G, (i + 1) * G) for i in range(T // G)]
    row = lax.broadcasted_iota(jnp.int32, (G, G), 0)
    col = lax.broadcasted_iota(jnp.int32, (G, G), 1)
    causal = (row // C == col // C) & (row >= col)
    g_a = seg(OFF_GA, GLA_RANK).astype(bf16)
    log_a = _log_sigmoid(_dot(g_a, wup_ref[...].astype(bf16)) + bga_ref[...]) * (1.0 / GLA_TAU)
    la3 = _split3(log_a)
    tril = jnp.where(causal, 1.0, 0.0).astype(bf16)
    in_chunk = jnp.where(lax.broadcasted_iota(jnp.int32, (T, LANES), 0) // C
                         == lax.broadcasted_iota(jnp.int32, (T, LANES), 1), 1.0, 0.0).astype(bf16)
    cum = jnp.concatenate(
        [_dot(tril, la3[0][r]) + _dot(tril, la3[1][r]) + _dot(tril, la3[2][r]) for r in groups],
        axis=0)
    tot_t = (_dot_tn(la3[0], in_chunk) + _dot_tn(la3[1], in_chunk)
             + _dot_tn(la3[2], in_chunk))
    decay_t = jnp.exp(tot_t)
    g_k = seg(OFF_GK, GLA_K_W)
    qd = ((seg(OFF_GQ, GLA_K_W) * (GLA_DK ** -0.5)) * jnp.exp(cum)).astype(bf16)
    kd = (g_k * jnp.exp(-cum)).astype(bf16)
    k_tail = jnp.concatenate(
        [g_k[c * C:(c + 1) * C] * jnp.exp(cum[(c + 1) * C - 1:(c + 1) * C] - cum[c * C:(c + 1) * C])
         for c in range(n_chunk)], axis=0) if n_chunk > 1 else g_k * jnp.exp(cum[T - 1:T] - cum)
    kt = k_tail.astype(bf16)
    v_b = seg(OFF_GV, GLA_V_W).astype(bf16)
    g_z = seg(OFF_GZ, GROUP_W)

    o_intra = []
    for r in groups:
        o_heads = []
        for h in range(GLA_HEADS):
            ks = slice(h * GLA_DK, (h + 1) * GLA_DK)
            attn = jnp.where(causal, _dot_nt(qd[r, ks], kd[r, ks]), 0.0).astype(bf16)
            o_heads.append(_dot(attn, v_b[r, h * GLA_DV:(h + 1) * GLA_DV]))
        o_intra.append(jnp.concatenate(o_heads, axis=1))
    o_intra = jnp.concatenate(o_intra, axis=0)

    shape = (GLA_K_W, GLA_V_W)
    on_diag = (lax.broadcasted_iota(jnp.int32, shape, 0) // GLA_DK
               == lax.broadcasted_iota(jnp.int32, shape, 1) // GLA_DV)
    o_chunks = []
    state = None
    for c in range(n_chunk):
        rs = slice(c * C, (c + 1) * C)
        if c == 0 or not carry:
            state = state_in(c)
        o_chunks.append(o_intra[rs] + _dot(qd[rs], state.astype(bf16)))
        update = jnp.where(on_diag, _dot_tn(kt[rs], v_b[rs]), 0.0)
        state = decay_t[:, c:c + 1] * state + update
        state_out(c, state)
    o = jnp.concatenate(o_chunks, axis=0) if n_chunk > 1 else o_chunks[0]
    for h in range(GLA_HEADS):
        vs = slice(h * GLA_DV, (h + 1) * GLA_DV)
        o_h = o[:, vs]
        o_h = o_h * lax.rsqrt(jnp.mean(o_h * o_h, axis=-1, keepdims=True) + EPS) * ggo_ref[...]
        mix_ref[:, GROUP_W + h * GLA_DV:GROUP_W + (h + 1) * GLA_DV] = (
            o_h * _silu(g_z[:, vs])).astype(mix_ref.dtype)


def _head_norm(x, g, bd_ref):
    rows, n_lanes = x.shape
    w = min(n_lanes, MXU_CHUNK)
    pieces = n_lanes // w
    bd = bd_ref[0:w, 0:w]

    def head_sums(v):
        stacked = jnp.concatenate([v[:, i * w:(i + 1) * w] for i in range(pieces)], axis=0)
        r = _dot(stacked, bd)
        return jnp.concatenate([r[i * rows:(i + 1) * rows] for i in range(pieces)], axis=1)

    sq = x * x
    hi = sq.astype(bf16)
    lo = (sq - hi.astype(f32)).astype(bf16)
    ms = (head_sums(hi) + head_sums(lo)) * (1.0 / SWA_HD)
    return x * lax.rsqrt(ms + EPS) * g


def _mixer_seq(seq_in, params, outs, scratch, *, tile, decode, layer):
    if decode:
        p_ref, mk_ref, mv_ref, conv_in_ref, gla_in_ref, kc_ref, vc_ref = seq_in
    else:
        x_ref, mk_ref, mv_ref = seq_in
        gn_ref, wt_ref = params[:2]
    (convw_ref, wup_ref, bga_ref, ggo_ref, gsq_ref, gsk_ref, bd_ref, sinks_ref,
     gmq_ref) = params[-N_PARAMS_DECODE:]
    mix_ref, conv_out_ref, gla_out_ref, kbuf_ref, vbuf_ref = outs
    ext_ref, s_ref, kprev_ref, vprev_ref, tail_ref, stk_ref, qkn_ref = scratch

    T = tile
    t = pl.program_id(1)

    def init_state():
        ext_ref[0:CONV_PAD, :] = jnp.zeros((CONV_PAD, GROUP_W), f32)
        if decode:
            ext_ref[CONV_PAD - (CONV_W - 1):CONV_PAD, :] = conv_in_ref[...]
            kprev_ref[...] = kc_ref[...]
            vprev_ref[...] = vc_ref[...]
        else:
            s_ref[...] = jnp.zeros_like(s_ref)
            kprev_ref[...] = jnp.zeros_like(kprev_ref)
            vprev_ref[...] = jnp.zeros_like(vprev_ref)

    if decode:
        init_state()
    else:
        pl.when(t == 0)(init_state)
    yield

    if decode:
        def seg(off, width):
            if off < OFF_GZ:
                return p_ref[:, off:off + width]
            return tail_ref[:, off - OFF_GZ:off - OFF_GZ + width]
    else:
        x = x_ref[...]
        hn = (x * lax.rsqrt(jnp.mean(x * x, axis=-1, keepdims=True) + EPS) * gn_ref[...]).astype(bf16)

        def seg(off, width):
            return _dot_nt(hn, wt_ref[off:off + width, :])

    def stack_rows(pieces, slot):
        r, w = pieces[0].shape
        if r % 8 == 0:
            return jnp.concatenate(pieces, axis=0)
        for j, piece in enumerate(pieces):
            stk_ref[slot, j * r:(j + 1) * r, 0:w] = piece
        return stk_ref[slot, 0:len(pieces) * r, 0:w]

    def unstack_rows(x, n, slot):
        r, w = x.shape[0] // n, x.shape[1]
        if r % 8 == 0:
            return [x[j * r:(j + 1) * r] for j in range(n)]
        stk_ref[slot, 0:n * r, 0:w] = x
        return [stk_ref[slot, j * r:(j + 1) * r, 0:w] for j in range(n)]

    u = seg(OFF_AC, GROUP_W) * seg(OFF_AH, GROUP_W)
    ext_ref[CONV_PAD:CONV_PAD + T, :] = u
    conv = (convw_ref[0:1, :] * ext_ref[CONV_PAD - 2:CONV_PAD - 2 + T, :]
            + convw_ref[1:2, :] * ext_ref[CONV_PAD - 1:CONV_PAD - 1 + T, :]
            + convw_ref[2:3, :] * u)
    mix_ref[:, 0:GROUP_W] = (seg(OFF_AB, GROUP_W) * conv
                             * _silu(seg(OFF_AZ, GROUP_W))).astype(mix_ref.dtype)
    conv_state = ext_ref[CONV_PAD + T - 2:CONV_PAD + T, :]
    ext_ref[CONV_PAD - 2:CONV_PAD, :] = conv_state
    conv_out_ref[...] = conv_state
    yield

    if not decode:
        def keep_state(c, state):
            if c == T // GLA_CHUNK - 1:
                s_ref[...] = state
                _store_gla_state(gla_out_ref, state)

        _gla_group(seg, T, GLA_CHUNK, (wup_ref, bga_ref, ggo_ref), mix_ref,
                   lambda c: s_ref[...], keep_state, carry=True)
        yield

    s_z = seg(OFF_SZ, GROUP_W)
    if decode:
        q_n, k_n = qkn_ref[:, 0:GROUP_W], qkn_ref[:, GROUP_W:GROUP_W + LANES]
        v_n = seg(OFF_SV, LANES)
    else:
        s_q = seg(OFF_SQ, GROUP_W)
        s_kv = seg(OFF_SK, 2 * LANES)
        q_n = _head_norm(s_q, gsq_ref[...], bd_ref)
        k_n = _head_norm(s_kv[:, 0:LANES], gsk_ref[...], bd_ref)
        v_n = s_kv[:, LANES:2 * LANES]
    yield

    BQ = min(WINDOW, T)
    n_blk = T // BQ
    stack = SWA_GROUP
    nk = WINDOW + BQ
    qi = lax.broadcasted_iota(jnp.int32, (stack * BQ, nk), 0) % BQ
    kj = lax.broadcasted_iota(jnp.int32, (stack * BQ, nk), 1)
    dist = qi + WINDOW - kj
    band = (dist >= 0) & (dist < WINDOW)
    srow = lax.broadcasted_iota(jnp.int32, (stack * BQ, 1), 0) // BQ
    for blk in range(n_blk):
        rs = slice(blk * BQ, (blk + 1) * BQ)
        if blk == 0:
            k_prev, v_prev = kprev_ref[...], vprev_ref[...]
            valid = band if decode else band & ((kj >= WINDOW) | (t > 0))
        else:
            ps = slice((blk - 1) * BQ, blk * BQ)
            k_prev, v_prev = k_n[ps], v_n[ps]
            valid = band
        k_cat = jnp.concatenate([k_prev, k_n[rs]], axis=0)
        v_cat = jnp.concatenate([v_prev, v_n[rs]], axis=0)
        for g in range(SWA_KV_HEADS):
            kg = k_cat[:, g * SWA_HD:(g + 1) * SWA_HD].astype(bf16)
            vg = v_cat[:, g * SWA_HD:(g + 1) * SWA_HD].astype(bf16)
            heads = [g * SWA_GROUP + j for j in range(stack)]
            qg = stack_rows([q_n[rs, hd * SWA_HD:(hd + 1) * SWA_HD] for hd in heads],
                            2 + 2 * g).astype(bf16)
            sink = jnp.full((stack * BQ, 1), sinks_ref[layer, heads[0]], f32)
            for j in range(1, stack):
                sink = jnp.where(srow == j, sinks_ref[layer, heads[j]], sink)
            s = _dot_nt(qg, kg) * (SWA_HD ** -0.5)
            s = jnp.where(valid, s, -jnp.inf)
            m = jnp.maximum(jnp.max(s, axis=-1, keepdims=True), sink)
            e = jnp.exp(s - m)
            prob = e / (jnp.sum(e, axis=-1, keepdims=True) + jnp.exp(sink - m))
            o = _dot(prob.astype(bf16), vg)
            for hd, o_hd in zip(heads, unstack_rows(o, stack, 3 + 2 * g)):
                z = s_z[rs, hd * SWA_HD:(hd + 1) * SWA_HD]
                mix_ref[rs, 2 * GROUP_W + hd * SWA_HD:2 * GROUP_W + (hd + 1) * SWA_HD] = (
                    o_hd * _silu(z)).astype(mix_ref.dtype)
            yield

    if decode:
        kbuf_ref[0:WINDOW - T, :] = kc_ref[T:WINDOW, :]
        kbuf_ref[WINDOW - T:WINDOW, :] = k_n
        vbuf_ref[0:WINDOW - T, :] = vc_ref[T:WINDOW, :]
        vbuf_ref[WINDOW - T:WINDOW, :] = v_n
    else:
        kprev_ref[...] = k_n[T - WINDOW:T]
        vprev_ref[...] = v_n[T - WINDOW:T]
        kbuf_ref[...] = k_n[T - WINDOW:T]
        vbuf_ref[...] = v_n[T - WINDOW:T]
    yield

    m_q = seg(OFF_MQ, GROUP_W)
    m_z = seg(OFF_MZ, GROUP_W)
    yield

    def mem_q(h):
        qh = m_q[:, h * MEM_HD:(h + 1) * MEM_HD]
        return qh * lax.rsqrt(jnp.mean(qh * qh, axis=-1, keepdims=True) + EPS) * gmq_ref[...]

    def softmax(s):
        e = jnp.exp(s - jnp.max(s, axis=-1, keepdims=True))
        return e / jnp.sum(e, axis=-1, keepdims=True)

    if decode:
        qs = stack_rows([mem_q(h) for h in range(MEM_HEADS)], 0).astype(bf16)
        s = _dot_nt(qs, mk_ref[...].astype(bf16)) * (MEM_HD ** -0.5)
        shape = (MEM_HEADS * T, MEM_HEADS * N_MEM)
        same_head = (lax.broadcasted_iota(jnp.int32, shape, 0) // T
                     == lax.broadcasted_iota(jnp.int32, shape, 1) % MEM_HEADS)
        yield
        prob = softmax(jnp.where(same_head, s, -jnp.inf))
        o_all = unstack_rows(_dot(prob.astype(bf16), mv_ref[...].astype(bf16)), MEM_HEADS, 1)
    else:
        o_all = []
        for h in range(MEM_HEADS):
            hs = slice(h * MEM_HD, (h + 1) * MEM_HD)
            s = _dot_nt(mem_q(h).astype(bf16), mk_ref[:, hs].astype(bf16)) * (MEM_HD ** -0.5)
            o_all.append(_dot(softmax(s).astype(bf16), mv_ref[:, hs].astype(bf16)))
    for h in range(MEM_HEADS):
        mix_ref[:, 3 * GROUP_W + h * MEM_HD:3 * GROUP_W + (h + 1) * MEM_HD] = (
            o_all[h] * _silu(m_z[:, h * MEM_HD:(h + 1) * MEM_HD])).astype(mix_ref.dtype)


def _mixer(tokens, norm_w, mem_k, mem_v, mem_layer, state, params, layer, tile, bb, decode):
    if decode:
        b, width = state[0].shape[1], tokens.shape[1]
        L = tokens.shape[0] // b
    else:
        b, L, width = tokens.shape
    nt = L // tile
    assert nt == 1 or not decode, "a decode call covers each sequence with a single tile"
    conv_w, w_up, b_ga, g_go, g_sq, g_sk, bd, sinks, g_mq = params

    def tok(width):
        if decode:
            return pl.BlockSpec((bb * tile, width), lambda i, t: (i, 0))
        return pl.BlockSpec((bb, tile, width), lambda i, t: (i, t, 0))

    def per_seq(*shape):
        return pl.BlockSpec((bb,) + shape, lambda i, t: (i,) + (0,) * len(shape))

    def per_seq_at(lyr, *shape):
        return pl.BlockSpec((None, bb) + shape, lambda i, t: (lyr, i) + (0,) * len(shape))

    def param(a):
        return pl.BlockSpec((None,) + a.shape[1:], lambda i, t: (layer,) + (0,) * (a.ndim - 1))

    kv_w = SWA_KV_HEADS * SWA_HD
    state_shapes = [(CONV_W - 1, GROUP_W), (GLA_HEADS, GLA_DK, GLA_DV), (WINDOW, kv_w),
                    (WINDOW, kv_w)]
    in_specs = [tok(width), per_seq_at(mem_layer, *mem_k.shape[2:]),
                per_seq_at(mem_layer, *mem_v.shape[2:])]
    args = [tokens, mem_k, mem_v]
    if decode:
        in_specs += [per_seq_at(layer, *s) for s in state_shapes]
        args += list(state)
    else:
        g_n, w_t = norm_w
        in_specs += [param(g_n), pl.BlockSpec(w_t.shape, lambda i, t: (0, 0),
                                              pipeline_mode=pl.Buffered(1))]
        args += [g_n, w_t]
    in_specs += [param(conv_w), param(w_up), param(b_ga), param(g_go), param(g_sq), param(g_sk),
                 pl.BlockSpec(bd.shape, lambda i, t: (0, 0)),
                 pl.BlockSpec(memory_space=pltpu.SMEM), param(g_mq)]
    args += [conv_w, w_up, b_ga, g_go, g_sq, g_sk, bd, sinks, g_mq]
    if decode:
        out_shape = [jax.ShapeDtypeStruct((b * L, 4 * GROUP_W), f32)]
        row_scratch = [pltpu.VMEM((bb * tile, D_IN - OFF_GZ), f32),
                       pltpu.VMEM((bb * tile, GROUP_W + LANES), f32)]
    else:
        out_shape = [jax.ShapeDtypeStruct((b, L, 4 * GROUP_W), bf16)]
        row_scratch = [pltpu.VMEM((bb, 8, LANES), f32)] * 2
    out_shape += [jax.ShapeDtypeStruct((b,) + s, f32) for s in state_shapes]
    out_specs = [tok(4 * GROUP_W)] + [per_seq(*s) for s in state_shapes]
    return pl.pallas_call(
        functools.partial(_mixer_kernel, tile=tile, decode=decode, layer=layer, bb=bb),
        grid=(b // bb, nt),
        in_specs=in_specs,
        out_specs=out_specs,
        out_shape=out_shape,
        scratch_shapes=[
            pltpu.VMEM((bb, CONV_PAD + tile, GROUP_W), f32),
            pltpu.VMEM((bb, 8, LANES) if decode else (bb, GLA_K_W, GLA_V_W), f32),
            pltpu.VMEM((bb, WINDOW, kv_w), f32),
            pltpu.VMEM((bb, WINDOW, kv_w), f32),
            row_scratch[0],
            pltpu.VMEM((bb, N_STACK_SLOTS, MEM_HEADS * min(tile, 8), LANES), f32),
            row_scratch[1],
        ],
        compiler_params=pltpu.CompilerParams(
            dimension_semantics=("arbitrary", "arbitrary"), vmem_limit_bytes=VMEM_LIMIT),
        name="mixer_decode" if decode else "mixer_prompt",
    )(*args)


def _out_proj_kernel(mix_ref, w_ref, x_ref, y_ref, wb_ref):
    @pl.when(pl.program_id(1) == 0)
    def _():
        wb_ref[...] = w_ref[...].astype(bf16)

    y_ref[...] = x_ref[...] + _dot(mix_ref[...].astype(bf16), wb_ref[...])


def _out_proj(mix, w, x, l, tm, tn):
    m, k = mix.shape
    n = w.shape[2]
    return pl.pallas_call(
        _out_proj_kernel,
        grid=(n // tn, m // tm),
        in_specs=[
            pl.BlockSpec((tm, k), lambda j, i: (i, 0)),
            pl.BlockSpec((None, k, tn), lambda j, i: (l, 0, j)),
            pl.BlockSpec((tm, tn), lambda j, i: (i, j)),
        ],
        out_specs=pl.BlockSpec((tm, tn), lambda j, i: (i, j)),
        out_shape=jax.ShapeDtypeStruct((m, n), f32),
        scratch_shapes=[pltpu.VMEM((k, tn), bf16)],
        compiler_params=pltpu.CompilerParams(
            dimension_semantics=("arbitrary", "arbitrary"), vmem_limit_bytes=VMEM_LIMIT),
        name="out_proj",
    )(mix, w, x)


PROMPT_TILE = 512
DECODE_SEQS_PER_STEP = 8
PROJ_TN = 1536
OUT_TM, OUT_TN = 1024, 1024

_LANE = np.arange(GROUP_W)
HEAD_BLOCK_DIAG = _LANE[:, None] // SWA_HD == _LANE[None, :] // SWA_HD


def kernel(x_prompt, x_sample, mem_prompt, state_conv, state_gla, cache_swa_k, cache_swa_v,
           cache_mem_k, cache_mem_v, g_norm, w_in, conv_w, w_gla_a_up, b_gla_a, g_gla_o,
           g_swa_q, g_swa_k, swa_sinks, g_mem, w_mem_kv, g_mem_q, g_mem_k, w_out):
    depth = w_in.shape[0]
    bp, lp, _ = x_prompt.shape
    bs, ls, _ = x_sample.shape
    hp = x_prompt.reshape(bp * lp, D_MODEL)
    hs = x_sample.reshape(bs * ls, D_MODEL)

    def row(a):
        return a[:, None, :]

    params = (conv_w, w_gla_a_up, row(b_gla_a), row(g_gla_o),
              row(jnp.tile(g_swa_q, (1, SWA_HEADS))), row(jnp.tile(g_swa_k, (1, SWA_KV_HEADS))),
              jnp.asarray(HEAD_BLOCK_DIAG, bf16), swa_sinks, row(g_mem_q))
    g_n, g_m, g_mk = row(g_norm), row(g_mem), row(g_mem_k)
    w_in_t = jnp.swapaxes(w_in, 1, 2)
    kv_w = SWA_KV_HEADS * SWA_HD
    state = (state_conv, state_gla, cache_swa_k.reshape(depth, bs, WINDOW, kv_w),
             cache_swa_v.reshape(depth, bs, WINDOW, kv_w))
    mem_k_s = cache_mem_k.reshape(depth, bs, N_MEM * MEM_HEADS, MEM_HD)
    mem_v_s = cache_mem_v.reshape(depth, bs, N_MEM * MEM_HEADS, MEM_HD)

    outs = [[] for _ in range(10)]
    for l in range(depth):
        proj, w_bf = _norm_matmul(hs, g_n, w_in_t, l, PROJ_TN)

        mk, mv = _memory_kv(mem_prompt, g_m, w_mem_kv, g_mk, l)
        mix, c, s, kb, vb = _mixer(hp.reshape(bp, lp, D_MODEL), (g_n, w_bf), mk[None], mv[None], 0,
                                   None, params, l, PROMPT_TILE, 1, decode=False)
        hp = _out_proj(mix.reshape(bp * lp, 4 * GROUP_W), w_out, hp, l, OUT_TM, OUT_TN)
        for lst, a in zip(outs[:6], (
                c, s, kb.reshape(bp, WINDOW, SWA_KV_HEADS, SWA_HD),
                vb.reshape(bp, WINDOW, SWA_KV_HEADS, SWA_HD),
                mk.reshape(bp, N_MEM, MEM_HEADS, MEM_HD), mv.reshape(bp, N_MEM, MEM_HEADS, MEM_HD))):
            lst.append(a)

        mix, c, s, kb, vb = _mixer(proj, None, mem_k_s, mem_v_s, l, state, params, l, ls,
                                   DECODE_SEQS_PER_STEP, decode=True)
        hs = _out_proj(mix, w_out, hs, l, bs * ls, OUT_TN)
        for lst, a in zip(outs[6:], (
                c, s, kb.reshape(bs, WINDOW, SWA_KV_HEADS, SWA_HD),
                vb.reshape(bs, WINDOW, SWA_KV_HEADS, SWA_HD))):
            lst.append(a)

    return (hp.reshape(bp, lp, D_MODEL), hs.reshape(bs, ls, D_MODEL),
            *[jnp.stack(o) for o in outs])
```

```python
import functools
import itertools

import jax
import jax.numpy as jnp
import numpy as np
from jax import lax
from jax.experimental import pallas as pl
from jax.experimental.pallas import tpu as pltpu

f32 = jnp.float32
bf16 = jnp.bfloat16

D_MODEL = 2048
GROUP_W = 512
GLA_HEADS = 4
GLA_DK = 64
GLA_DV = 128
GLA_RANK = 16
GLA_TAU = 16.0
GLA_CHUNK = 64
SWA_HEADS = 8
SWA_KV_HEADS = 2
SWA_HD = 64
SWA_GROUP = SWA_HEADS // SWA_KV_HEADS
WINDOW = 128
N_MEM = 256
MEM_HEADS = 4
MEM_HD = 128
CONV_W = 3
EPS = 1e-6

LANES = 128
MXU_CHUNK = 256

D_IN = 5904
OFF_AB, OFF_AC, OFF_AH, OFF_AZ = 0, 512, 1024, 1536
OFF_GQ, OFF_GK, OFF_GV, OFF_GA, OFF_GZ = 2048, 2304, 2560, 3072, 3088
OFF_SQ, OFF_SK, OFF_SV, OFF_SZ = 3600, 4112, 4240, 4368
OFF_MQ, OFF_MZ = 4880, 5392

VMEM_LIMIT = 56 * 1024 * 1024


def _dot(a, b):
    return jnp.dot(a, b, preferred_element_type=f32)


def _dot_nt(a, b):
    return lax.dot_general(a, b, (((1,), (1,)), ((), ())), preferred_element_type=f32)


def _dot_tn(a, b):
    return lax.dot_general(a, b, (((0,), (0,)), ((), ())), preferred_element_type=f32)


def _split3(x):
    hi = x.astype(bf16)
    r = x - hi.astype(f32)
    mid = r.astype(bf16)
    lo = (r - mid.astype(f32)).astype(bf16)
    return hi, mid, lo


def _silu(x):
    return x * jax.nn.sigmoid(x)


def _log_sigmoid(x):
    return jnp.minimum(x, 0.0) - jnp.log1p(jnp.exp(-jnp.abs(x)))


def _norm_matmul_kernel(x_ref, g_ref, wt_ref, o_ref, wb_ref, hn_ref):
    x = x_ref[...]
    y = x * lax.rsqrt(jnp.mean(x * x, axis=-1, keepdims=True) + EPS)
    hn_ref[...] = (y * g_ref[...]).astype(bf16)
    wb_ref[...] = wt_ref[...].astype(bf16)
    o_ref[...] = _dot_nt(hn_ref[...], wb_ref[...])


def _norm_matmul(x, g, wt, l, tn):
    m, k = x.shape
    n = wt.shape[1]
    return pl.pallas_call(
        _norm_matmul_kernel,
        grid=(pl.cdiv(n, tn),),
        in_specs=[
            pl.BlockSpec((m, k), lambda j: (0, 0)),
            pl.BlockSpec((None, 1, k), lambda j: (l, 0, 0)),
            pl.BlockSpec((None, tn, k), lambda j: (l, j, 0)),
        ],
        out_specs=[pl.BlockSpec((m, tn), lambda j: (0, j)), pl.BlockSpec((tn, k), lambda j: (j, 0))],
        out_shape=[jax.ShapeDtypeStruct((m, n), f32), jax.ShapeDtypeStruct((n, k), bf16)],
        scratch_shapes=[pltpu.VMEM((m, k), bf16)],
        compiler_params=pltpu.CompilerParams(
            dimension_semantics=("arbitrary",), vmem_limit_bytes=VMEM_LIMIT),
        name="norm_in_proj",
    )(x, g, wt)


def _memory_kv_kernel(x_ref, g_ref, w_ref, gk_ref, k_ref, v_ref, k4_ref, v4_ref, wb_ref):
    @pl.when(pl.program_id(1) == 0)
    def _():
        wb_ref[...] = w_ref[...].astype(bf16)

    x = x_ref[...]
    y = x * lax.rsqrt(jnp.mean(x * x, axis=-1, keepdims=True) + EPS)
    kv = _dot((y * g_ref[...]).astype(bf16), wb_ref[...])
    for h in range(MEM_HEADS):
        kh = kv[:, h * MEM_HD:(h + 1) * MEM_HD]
        kh = kh * lax.rsqrt(jnp.mean(kh * kh, axis=-1, keepdims=True) + EPS) * gk_ref[...]
        vh = kv[:, GROUP_W + h * MEM_HD:GROUP_W + (h + 1) * MEM_HD]
        k_ref[:, h * MEM_HD:(h + 1) * MEM_HD] = kh
        k4_ref[:, h, :] = kh
        v4_ref[:, h, :] = vh
    v_ref[...] = kv[:, GROUP_W:]


def _memory_kv(mem, g, w, gk):
    depth, b = w.shape[0], mem.shape[0]
    flat = jax.ShapeDtypeStruct((depth, b, N_MEM, GROUP_W), f32)
    split = jax.ShapeDtypeStruct((depth, b, N_MEM, MEM_HEADS, MEM_HD), f32)
    return pl.pallas_call(
        _memory_kv_kernel,
        grid=(depth, b),
        in_specs=[
            pl.BlockSpec((None, N_MEM, D_MODEL), lambda l, i: (i, 0, 0)),
            pl.BlockSpec((None, 1, D_MODEL), lambda l, i: (l, 0, 0)),
            pl.BlockSpec((None, D_MODEL, 2 * GROUP_W), lambda l, i: (l, 0, 0)),
            pl.BlockSpec((None, 1, MEM_HD), lambda l, i: (l, 0, 0)),
        ],
        out_specs=[pl.BlockSpec((None, None, N_MEM, GROUP_W), lambda l, i: (l, i, 0, 0))] * 2
        + [pl.BlockSpec((None, None, N_MEM, MEM_HEADS, MEM_HD), lambda l, i: (l, i, 0, 0, 0))] * 2,
        out_shape=[flat, flat, split, split],
        scratch_shapes=[pltpu.VMEM((D_MODEL, 2 * GROUP_W), bf16)],
        compiler_params=pltpu.CompilerParams(
            dimension_semantics=("arbitrary", "arbitrary"), vmem_limit_bytes=VMEM_LIMIT),
        name="memory_kv",
    )(mem, g, w, gk)


CONV_PAD = 8
N_STACK_SLOTS = 2 + 2 * SWA_KV_HEADS
N_SEQ_IN_PROMPT, N_SEQ_IN_DECODE, N_PARAMS_PROMPT, N_PARAMS_DECODE, N_OUT = 3, 7, 11, 9, 5


def _mixer_kernel(*refs, tile, decode, layer, bb):
    n_seq = N_SEQ_IN_DECODE if decode else N_SEQ_IN_PROMPT
    n_par = N_PARAMS_DECODE if decode else N_PARAMS_PROMPT
    seq_in = refs[:n_seq]
    params = refs[n_seq:n_seq + n_par]
    outs = refs[n_seq + n_par:n_seq + n_par + N_OUT]
    scratch = refs[n_seq + n_par + N_OUT:]

    def view(ref, s):
        if decode and ref.ndim == 2:
            return _RowWindow(ref, s * tile, tile)
        return ref.at[s]

    if decode:
        p_ref, tail_ref, qkn_ref = seq_in[0], scratch[4], scratch[6]
        bd_ref, gsq_ref, gsk_ref = params[-3], params[-5], params[-4]
        tail_ref[...] = p_ref[:, OFF_GZ:D_IN]
        sq0, sk0 = OFF_SQ - OFF_GZ, OFF_SK - OFF_GZ
        qkn_ref[:, 0:GROUP_W] = _head_norm(tail_ref[:, sq0:sq0 + GROUP_W], gsq_ref[...], bd_ref)
        qkn_ref[:, GROUP_W:GROUP_W + LANES] = _head_norm(tail_ref[:, sk0:sk0 + LANES], gsk_ref[...],
                                                         bd_ref)

        def seg_rows(off, width):
            if off < OFF_GZ:
                return p_ref[:, off:off + width]
            return tail_ref[:, off - OFF_GZ:off - OFF_GZ + width]

        gla_in_ref, gla_out_ref = seq_in[4], outs[2]
        _gla_group(seg_rows, bb * tile, tile, params[-8:-5], outs[0],
                   lambda c: _gla_block_diag(gla_in_ref.at[c]),
                   lambda c, state: _store_gla_state(gla_out_ref.at[c], state), carry=False)

    stages = [_mixer_seq([view(r, s) for r in seq_in], params, [view(r, s) for r in outs],
                         [view(r, s) for r in scratch], tile=tile, decode=decode, layer=layer)
              for s in range(bb)]
    for _ in itertools.zip_longest(*stages):
        pass


class _RowWindow:
    def __init__(self, ref, start, size):
        self.ref, self.start, self.size, self.dtype = ref, start, size, ref.dtype

    def _index(self, idx):
        rows, cols = (slice(None), slice(None)) if idx is Ellipsis else idx
        lo, hi, _ = rows.indices(self.size)
        return slice(self.start + lo, self.start + hi), cols

    def __getitem__(self, idx):
        return self.ref[self._index(idx)]

    def __setitem__(self, idx, value):
        self.ref[self._index(idx)] = value


GLA_INTRA_ROWS = MXU_CHUNK
GLA_K_W = GLA_HEADS * GLA_DK
GLA_V_W = GLA_HEADS * GLA_DV


def _gla_block_diag(state_ref):
    rows = []
    for h in range(GLA_HEADS):
        blocks = [state_ref[h] if j == h else jnp.zeros((GLA_DK, GLA_DV), f32)
                  for j in range(GLA_HEADS)]
        rows.append(jnp.concatenate(blocks, axis=1))
    return jnp.concatenate(rows, axis=0)


def _store_gla_state(state_ref, state):
    for h in range(GLA_HEADS):
        state_ref[h] = state[h * GLA_DK:(h + 1) * GLA_DK, h * GLA_DV:(h + 1) * GLA_DV]


def _gla_group(seg, T, C, params, mix_ref, state_in, state_out, carry):
    wup_ref, bga_ref, ggo_ref = params
    n_chunk = T // C
    G = min(T, GLA_INTRA_ROWS)
    groups = [slice(i * G, (i + 1) * G) for i in range(T // G)]
    row = lax.broadcasted_iota(jnp.int32, (G, G), 0)
    col = lax.broadcasted_iota(jnp.int32, (G, G), 1)
    causal = (row // C == col // C) & (row >= col)
    g_a = seg(OFF_GA, GLA_RANK).astype(bf16)
    log_a = _log_sigmoid(_dot(g_a, wup_ref[...].astype(bf16)) + bga_ref[...]) * (1.0 / GLA_TAU)
    la3 = _split3(log_a)
    tril = jnp.where(causal, 1.0, 0.0).astype(bf16)
    in_chunk = jnp.where(lax.broadcasted_iota(jnp.int32, (T, LANES), 0) // C
                         == lax.broadcasted_iota(jnp.int32, (T, LANES), 1), 1.0, 0.0).astype(bf16)
    cum = jnp.concatenate(
        [_dot(tril, la3[0][r]) + _dot(tril, la3[1][r]) + _dot(tril, la3[2][r]) for r in groups],
        axis=0)
    tot_t = (_dot_tn(la3[0], in_chunk) + _dot_tn(la3[1], in_chunk)
             + _dot_tn(la3[2], in_chunk))
    decay_t = jnp.exp(tot_t)
    g_k = seg(OFF_GK, GLA_K_W)
    qd = ((seg(OFF_GQ, GLA_K_W) * (GLA_DK ** -0.5)) * jnp.exp(cum)).astype(bf16)
    kd = (g_k * jnp.exp(-cum)).astype(bf16)
    k_tail = jnp.concatenate(
        [g_k[c * C:(c + 1) * C] * jnp.exp(cum[(c + 1) * C - 1:(c + 1) * C] - cum[c * C:(c + 1) * C])
         for c in range(n_chunk)], axis=0) if n_chunk > 1 else g_k * jnp.exp(cum[T - 1:T] - cum)
    kt = k_tail.astype(bf16)
    v_b = seg(OFF_GV, GLA_V_W).astype(bf16)
    g_z = seg(OFF_GZ, GROUP_W)

    o_intra = []
    for r in groups:
        o_heads = []
        for h in range(GLA_HEADS):
            ks = slice(h * GLA_DK, (h + 1) * GLA_DK)
            attn = jnp.where(causal, _dot_nt(qd[r, ks], kd[r, ks]), 0.0).astype(bf16)
            o_heads.append(_dot(attn, v_b[r, h * GLA_DV:(h + 1) * GLA_DV]))
        o_intra.append(jnp.concatenate(o_heads, axis=1))
    o_intra = jnp.concatenate(o_intra, axis=0)

    shape = (GLA_K_W, GLA_V_W)
    on_diag = (lax.broadcasted_iota(jnp.int32, shape, 0) // GLA_DK
               == lax.broadcasted_iota(jnp.int32, shape, 1) // GLA_DV)
    o_chunks = []
    state = None
    for c in range(n_chunk):
        rs = slice(c * C, (c + 1) * C)
        if c == 0 or not carry:
            state = state_in(c)
        o_chunks.append(o_intra[rs] + _dot(qd[rs], state.astype(bf16)))
        update = jnp.where(on_diag, _dot_tn(kt[rs], v_b[rs]), 0.0)
        state = decay_t[:, c:c + 1] * state + update
        state_out(c, state)
    o = jnp.concatenate(o_chunks, axis=0) if n_chunk > 1 else o_chunks[0]
    for h in range(GLA_HEADS):
        vs = slice(h * GLA_DV, (h + 1) * GLA_DV)
        o_h = o[:, vs]
        o_h = o_h * lax.rsqrt(jnp.mean(o_h * o_h, axis=-1, keepdims=True) + EPS) * ggo_ref[...]
        mix_ref[:, GROUP_W + h * GLA_DV:GROUP_W + (h + 1) * GLA_DV] = (
            o_h * _silu(g_z[:, vs])).astype(mix_ref.dtype)


def _head_norm(x, g, bd_ref):
    rows, n_lanes = x.shape
    w = min(n_lanes, MXU_CHUNK)
    pieces = n_lanes // w
    bd = bd_ref[0:w, 0:w]

    def head_sums(v):
        stacked = jnp.concatenate([v[:, i * w:(i + 1) * w] for i in range(pieces)], axis=0)
        r = _dot(stacked, bd)
        return jnp.concatenate([r[i * rows:(i + 1) * rows] for i in range(pieces)], axis=1)

    sq = x * x
    hi = sq.astype(bf16)
    lo = (sq - hi.astype(f32)).astype(bf16)
    ms = (head_sums(hi) + head_sums(lo)) * (1.0 / SWA_HD)
    return x * lax.rsqrt(ms + EPS) * g


def _mixer_seq(seq_in, params, outs, scratch, *, tile, decode, layer):
    if decode:
        p_ref, mk_ref, mv_ref, conv_in_ref, gla_in_ref, kc_ref, vc_ref = seq_in
    else:
        x_ref, mk_ref, mv_ref = seq_in
        gn_ref, wt_ref = params[:2]
    (convw_ref, wup_ref, bga_ref, ggo_ref, gsq_ref, gsk_ref, bd_ref, sinks_ref,
     gmq_ref) = params[-N_PARAMS_DECODE:]
    mix_ref, conv_out_ref, gla_out_ref, kbuf_ref, vbuf_ref = outs
    ext_ref, s_ref, kprev_ref, vprev_ref, tail_ref, stk_ref, qkn_ref = scratch

    T = tile
    t = pl.program_id(1)

    def init_state():
        ext_ref[0:CONV_PAD, :] = jnp.zeros((CONV_PAD, GROUP_W), f32)
        if decode:
            ext_ref[CONV_PAD - (CONV_W - 1):CONV_PAD, :] = conv_in_ref[...]
            kprev_ref[...] = kc_ref[...]
            vprev_ref[...] = vc_ref[...]
        else:
            s_ref[...] = jnp.zeros_like(s_ref)
            kprev_ref[...] = jnp.zeros_like(kprev_ref)
            vprev_ref[...] = jnp.zeros_like(vprev_ref)

    if decode:
        init_state()
    else:
        pl.when(t == 0)(init_state)
    yield

    if decode:
        def seg(off, width):
            if off < OFF_GZ:
                return p_ref[:, off:off + width]
            return tail_ref[:, off - OFF_GZ:off - OFF_GZ + width]
    else:
        x = x_ref[...]
        hn = (x * lax.rsqrt(jnp.mean(x * x, axis=-1, keepdims=True) + EPS) * gn_ref[...]).astype(bf16)

        def seg(off, width):
            return _dot_nt(hn, wt_ref[off:off + width, :])

    def stack_rows(pieces, slot):
        r, w = pieces[0].shape
        if r % 8 == 0:
            return jnp.concatenate(pieces, axis=0)
        for j, piece in enumerate(pieces):
            stk_ref[slot, j * r:(j + 1) * r, 0:w] = piece
        return stk_ref[slot, 0:len(pieces) * r, 0:w]

    def unstack_rows(x, n, slot):
        r, w = x.shape[0] // n, x.shape[1]
        if r % 8 == 0:
            return [x[j * r:(j + 1) * r] for j in range(n)]
        stk_ref[slot, 0:n * r, 0:w] = x
        return [stk_ref[slot, j * r:(j + 1) * r, 0:w] for j in range(n)]

    u = seg(OFF_AC, GROUP_W) * seg(OFF_AH, GROUP_W)
    ext_ref[CONV_PAD:CONV_PAD + T, :] = u
    conv = (convw_ref[0:1, :] * ext_ref[CONV_PAD - 2:CONV_PAD - 2 + T, :]
            + convw_ref[1:2, :] * ext_ref[CONV_PAD - 1:CONV_PAD - 1 + T, :]
            + convw_ref[2:3, :] * u)
    mix_ref[:, 0:GROUP_W] = (seg(OFF_AB, GROUP_W) * conv
                             * _silu(seg(OFF_AZ, GROUP_W))).astype(mix_ref.dtype)
    conv_state = ext_ref[CONV_PAD + T - 2:CONV_PAD + T, :]
    ext_ref[CONV_PAD - 2:CONV_PAD, :] = conv_state
    conv_out_ref[...] = conv_state
    yield

    if not decode:
        def keep_state(c, state):
            if c == T // GLA_CHUNK - 1:
                s_ref[...] = state
                _store_gla_state(gla_out_ref, state)

        _gla_group(seg, T, GLA_CHUNK, (wup_ref, bga_ref, ggo_ref), mix_ref,
                   lambda c: s_ref[...], keep_state, carry=True)
        yield

    s_z = seg(OFF_SZ, GROUP_W)
    if decode:
        q_n, k_n = qkn_ref[:, 0:GROUP_W], qkn_ref[:, GROUP_W:GROUP_W + LANES]
        v_n = seg(OFF_SV, LANES)
    else:
        s_q = seg(OFF_SQ, GROUP_W)
        s_kv = seg(OFF_SK, 2 * LANES)
        q_n = _head_norm(s_q, gsq_ref[...], bd_ref)
        k_n = _head_norm(s_kv[:, 0:LANES], gsk_ref[...], bd_ref)
        v_n = s_kv[:, LANES:2 * LANES]
    yield

    BQ = min(WINDOW, T)
    n_blk = T // BQ
    stack = SWA_GROUP
    nk = WINDOW + BQ
    qi = lax.broadcasted_iota(jnp.int32, (stack * BQ, nk), 0) % BQ
    kj = lax.broadcasted_iota(jnp.int32, (stack * BQ, nk), 1)
    dist = qi + WINDOW - kj
    band = (dist >= 0) & (dist < WINDOW)
    srow = lax.broadcasted_iota(jnp.int32, (stack * BQ, 1), 0) // BQ
    for blk in range(n_blk):
        rs = slice(blk * BQ, (blk + 1) * BQ)
        if blk == 0:
            k_prev, v_prev = kprev_ref[...], vprev_ref[...]
            valid = band if decode else band & ((kj >= WINDOW) | (t > 0))
        else:
            ps = slice((blk - 1) * BQ, blk * BQ)
            k_prev, v_prev = k_n[ps], v_n[ps]
            valid = band
        k_cat = jnp.concatenate([k_prev, k_n[rs]], axis=0)
        v_cat = jnp.concatenate([v_prev, v_n[rs]], axis=0)
        for g in range(SWA_KV_HEADS):
            kg = k_cat[:, g * SWA_HD:(g + 1) * SWA_HD].astype(bf16)
            vg = v_cat[:, g * SWA_HD:(g + 1) * SWA_HD].astype(bf16)
            heads = [g * SWA_GROUP + j for j in range(stack)]
            qg = stack_rows([q_n[rs, hd * SWA_HD:(hd + 1) * SWA_HD] for hd in heads],
                            2 + 2 * g).astype(bf16)
            sink = jnp.full((stack * BQ, 1), sinks_ref[layer, heads[0]], f32)
            for j in range(1, stack):
                sink = jnp.where(srow == j, sinks_ref[layer, heads[j]], sink)
            s = _dot_nt(qg, kg) * (SWA_HD ** -0.5)
            s = jnp.where(valid, s, -jnp.inf)
            m = jnp.maximum(jnp.max(s, axis=-1, keepdims=True), sink)
            e = jnp.exp(s - m)
            prob = e / (jnp.sum(e, axis=-1, keepdims=True) + jnp.exp(sink - m))
            o = _dot(prob.astype(bf16), vg)
            for hd, o_hd in zip(heads, unstack_rows(o, stack, 3 + 2 * g)):
                z = s_z[rs, hd * SWA_HD:(hd + 1) * SWA_HD]
                mix_ref[rs, 2 * GROUP_W + hd * SWA_HD:2 * GROUP_W + (hd + 1) * SWA_HD] = (
                    o_hd * _silu(z)).astype(mix_ref.dtype)
            yield

    if decode:
        kbuf_ref[0:WINDOW - T, :] = kc_ref[T:WINDOW, :]
        kbuf_ref[WINDOW - T:WINDOW, :] = k_n
        vbuf_ref[0:WINDOW - T, :] = vc_ref[T:WINDOW, :]
        vbuf_ref[WINDOW - T:WINDOW, :] = v_n
    else:
        kprev_ref[...] = k_n[T - WINDOW:T]
        vprev_ref[...] = v_n[T - WINDOW:T]
        kbuf_ref[...] = k_n[T - WINDOW:T]
        vbuf_ref[...] = v_n[T - WINDOW:T]
    yield

    m_q = seg(OFF_MQ, GROUP_W)
    m_z = seg(OFF_MZ, GROUP_W)
    yield

    def mem_q(h):
        qh = m_q[:, h * MEM_HD:(h + 1) * MEM_HD]
        return qh * lax.rsqrt(jnp.mean(qh * qh, axis=-1, keepdims=True) + EPS) * gmq_ref[...]

    def softmax(s):
        e = jnp.exp(s - jnp.max(s, axis=-1, keepdims=True))
        return e / jnp.sum(e, axis=-1, keepdims=True)

    if decode:
        qs = stack_rows([mem_q(h) for h in range(MEM_HEADS)], 0).astype(bf16)
        s = _dot_nt(qs, mk_ref[...].astype(bf16)) * (MEM_HD ** -0.5)
        shape = (MEM_HEADS * T, MEM_HEADS * N_MEM)
        same_head = (lax.broadcasted_iota(jnp.int32, shape, 0) // T
                     == lax.broadcasted_iota(jnp.int32, shape, 1) % MEM_HEADS)
        yield
        prob = softmax(jnp.where(same_head, s, -jnp.inf))
        o_all = unstack_rows(_dot(prob.astype(bf16), mv_ref[...].astype(bf16)), MEM_HEADS, 1)
    else:
        o_all = []
        for h in range(MEM_HEADS):
            hs = slice(h * MEM_HD, (h + 1) * MEM_HD)
            s = _dot_nt(mem_q(h).astype(bf16), mk_ref[:, hs].astype(bf16)) * (MEM_HD ** -0.5)
            o_all.append(_dot(softmax(s).astype(bf16), mv_ref[:, hs].astype(bf16)))
    for h in range(MEM_HEADS):
        mix_ref[:, 3 * GROUP_W + h * MEM_HD:3 * GROUP_W + (h + 1) * MEM_HD] = (
            o_all[h] * _silu(m_z[:, h * MEM_HD:(h + 1) * MEM_HD])).astype(mix_ref.dtype)


def _mixer(tokens, norm_w, mem_k, mem_v, mem_layer, state, params, layer, tile, bb, decode):
    if decode:
        b, width = state[0].shape[1], tokens.shape[1]
        L = tokens.shape[0] // b
    else:
        b, L, width = tokens.shape
    nt = L // tile
    assert nt == 1 or not decode, "a decode call covers each sequence with a single tile"
    conv_w, w_up, b_ga, g_go, g_sq, g_sk, bd, sinks, g_mq = params

    def tok(width):
        if decode:
            return pl.BlockSpec((bb * tile, width), lambda i, t: (i, 0))
        return pl.BlockSpec((bb, tile, width), lambda i, t: (i, t, 0))

    def per_seq(*shape):
        return pl.BlockSpec((bb,) + shape, lambda i, t: (i,) + (0,) * len(shape))

    def per_seq_at(lyr, *shape):
        return pl.BlockSpec((None, bb) + shape, lambda i, t: (lyr, i) + (0,) * len(shape))

    def param(a):
        return pl.BlockSpec((None,) + a.shape[1:], lambda i, t: (layer,) + (0,) * (a.ndim - 1))

    kv_w = SWA_KV_HEADS * SWA_HD
    state_shapes = [(CONV_W - 1, GROUP_W), (GLA_HEADS, GLA_DK, GLA_DV), (WINDOW, kv_w),
                    (WINDOW, kv_w)]
    in_specs = [tok(width), per_seq_at(mem_layer, *mem_k.shape[2:]),
                per_seq_at(mem_layer, *mem_v.shape[2:])]
    args = [tokens, mem_k, mem_v]
    if decode:
        in_specs += [per_seq_at(layer, *s) for s in state_shapes]
        args += list(state)
    else:
        g_n, w_t = norm_w
        in_specs += [param(g_n), pl.BlockSpec(w_t.shape, lambda i, t: (0, 0),
                                              pipeline_mode=pl.Buffered(1))]
        args += [g_n, w_t]
    in_specs += [param(conv_w), param(w_up), param(b_ga), param(g_go), param(g_sq), param(g_sk),
                 pl.BlockSpec(bd.shape, lambda i, t: (0, 0)),
                 pl.BlockSpec(memory_space=pltpu.SMEM), param(g_mq)]
    args += [conv_w, w_up, b_ga, g_go, g_sq, g_sk, bd, sinks, g_mq]
    if decode:
        out_shape = [jax.ShapeDtypeStruct((b * L, 4 * GROUP_W), f32)]
        row_scratch = [pltpu.VMEM((bb * tile, D_IN - OFF_GZ), f32),
                       pltpu.VMEM((bb * tile, GROUP_W + LANES), f32)]
    else:
        out_shape = [jax.ShapeDtypeStruct((b, L, 4 * GROUP_W), bf16)]
        row_scratch = [pltpu.VMEM((bb, 8, LANES), f32)] * 2
    out_shape += [jax.ShapeDtypeStruct((b,) + s, f32) for s in state_shapes]
    out_specs = [tok(4 * GROUP_W)] + [per_seq(*s) for s in state_shapes]
    return pl.pallas_call(
        functools.partial(_mixer_kernel, tile=tile, decode=decode, layer=layer, bb=bb),
        grid=(b // bb, nt),
        in_specs=in_specs,
        out_specs=out_specs,
        out_shape=out_shape,
        scratch_shapes=[
            pltpu.VMEM((bb, CONV_PAD + tile, GROUP_W), f32),
            pltpu.VMEM((bb, 8, LANES) if decode else (bb, GLA_K_W, GLA_V_W), f32),
            pltpu.VMEM((bb, WINDOW, kv_w), f32),
            pltpu.VMEM((bb, WINDOW, kv_w), f32),
            row_scratch[0],
            pltpu.VMEM((bb, N_STACK_SLOTS, MEM_HEADS * min(tile, 8), LANES), f32),
            row_scratch[1],
        ],
        compiler_params=pltpu.CompilerParams(
            dimension_semantics=("arbitrary", "arbitrary"), vmem_limit_bytes=VMEM_LIMIT),
        name="mixer_decode" if decode else "mixer_prompt",
    )(*args)


def _out_proj_kernel(mix_ref, w_ref, x_ref, y_ref, wb_ref):
    @pl.when(pl.program_id(1) == 0)
    def _():
        wb_ref[...] = w_ref[...].astype(bf16)

    y_ref[...] = x_ref[...] + _dot(mix_ref[...].astype(bf16), wb_ref[...])


def _out_proj(mix, w, x, l, tm, tn):
    m, k = mix.shape
    n = w.shape[2]
    return pl.pallas_call(
        _out_proj_kernel,
        grid=(n // tn, m // tm),
        in_specs=[
            pl.BlockSpec((tm, k), lambda j, i: (i, 0)),
            pl.BlockSpec((None, k, tn), lambda j, i: (l, 0, j)),
            pl.BlockSpec((tm, tn), lambda j, i: (i, j)),
        ],
        out_specs=pl.BlockSpec((tm, tn), lambda j, i: (i, j)),
        out_shape=jax.ShapeDtypeStruct((m, n), f32),
        scratch_shapes=[pltpu.VMEM((k, tn), bf16)],
        compiler_params=pltpu.CompilerParams(
            dimension_semantics=("arbitrary", "arbitrary"), vmem_limit_bytes=VMEM_LIMIT),
        name="out_proj",
    )(mix, w, x)


PROMPT_TILE = 512
DECODE_SEQS_PER_STEP = 8
PROJ_TN = 1536
OUT_TM, OUT_TN = 1024, 1024

_LANE = np.arange(GROUP_W)
HEAD_BLOCK_DIAG = _LANE[:, None] // SWA_HD == _LANE[None, :] // SWA_HD


def kernel(x_prompt, x_sample, mem_prompt, state_conv, state_gla, cache_swa_k, cache_swa_v,
           cache_mem_k, cache_mem_v, g_norm, w_in, conv_w, w_gla_a_up, b_gla_a, g_gla_o,
           g_swa_q, g_swa_k, swa_sinks, g_mem, w_mem_kv, g_mem_q, g_mem_k, w_out):
    depth = w_in.shape[0]
    bp, lp, _ = x_prompt.shape
    bs, ls, _ = x_sample.shape
    hp = x_prompt.reshape(bp * lp, D_MODEL)
    hs = x_sample.reshape(bs * ls, D_MODEL)

    def row(a):
        return a[:, None, :]

    params = (conv_w, w_gla_a_up, row(b_gla_a), row(g_gla_o),
              row(jnp.tile(g_swa_q, (1, SWA_HEADS))), row(jnp.tile(g_swa_k, (1, SWA_KV_HEADS))),
              jnp.asarray(HEAD_BLOCK_DIAG, bf16), swa_sinks, row(g_mem_q))
    g_n, g_m, g_mk = row(g_norm), row(g_mem), row(g_mem_k)
    w_in_t = jnp.swapaxes(w_in, 1, 2)
    kv_w = SWA_KV_HEADS * SWA_HD
    state = (state_conv, state_gla, cache_swa_k.reshape(depth, bs, WINDOW, kv_w),
             cache_swa_v.reshape(depth, bs, WINDOW, kv_w))
    mem_k_s = cache_mem_k.reshape(depth, bs, N_MEM * MEM_HEADS, MEM_HD)
    mem_v_s = cache_mem_v.reshape(depth, bs, N_MEM * MEM_HEADS, MEM_HD)

    mk, mv, mem_k_p, mem_v_p = _memory_kv(mem_prompt, g_m, w_mem_kv, g_mk)
    outs = [[] for _ in range(8)]
    for l in range(depth):
        proj, w_bf = _norm_matmul(hs, g_n, w_in_t, l, PROJ_TN)

        mix, c, s, kb, vb = _mixer(hp.reshape(bp, lp, D_MODEL), (g_n, w_bf), mk, mv, l,
                                   None, params, l, PROMPT_TILE, 1, decode=False)
        hp = _out_proj(mix.reshape(bp * lp, 4 * GROUP_W), w_out, hp, l, OUT_TM, OUT_TN)
        for lst, a in zip(outs[:4], (
                c, s, kb.reshape(bp, WINDOW, SWA_KV_HEADS, SWA_HD),
                vb.reshape(bp, WINDOW, SWA_KV_HEADS, SWA_HD))):
            lst.append(a)

        mix, c, s, kb, vb = _mixer(proj, None, mem_k_s, mem_v_s, l, state, params, l, ls,
                                   DECODE_SEQS_PER_STEP, decode=True)
        hs = _out_proj(mix, w_out, hs, l, bs * ls, OUT_TN)
        for lst, a in zip(outs[4:], (
                c, s, kb.reshape(bs, WINDOW, SWA_KV_HEADS, SWA_HD),
                vb.reshape(bs, WINDOW, SWA_KV_HEADS, SWA_HD))):
            lst.append(a)

    stacked = [jnp.stack(o) for o in outs]
    return (hp.reshape(bp, lp, D_MODEL), hs.reshape(bs, ls, D_MODEL),
            *stacked[:4], mem_k_p, mem_v_p, *stacked[4:])
```

```python
import functools
import itertools

import jax
import jax.numpy as jnp
import numpy as np
from jax import lax
from jax.experimental import pallas as pl
from jax.experimental.pallas import tpu as pltpu

f32 = jnp.float32
bf16 = jnp.bfloat16

D_MODEL = 2048
GROUP_W = 512
GLA_HEADS = 4
GLA_DK = 64
GLA_DV = 128
GLA_RANK = 16
GLA_TAU = 16.0
GLA_CHUNK = 64
SWA_HEADS = 8
SWA_KV_HEADS = 2
SWA_HD = 64
SWA_GROUP = SWA_HEADS // SWA_KV_HEADS
WINDOW = 128
N_MEM = 256
MEM_HEADS = 4
MEM_HD = 128
CONV_W = 3
EPS = 1e-6

LANES = 128
MXU_CHUNK = 256

D_IN = 5904
OFF_AB, OFF_AC, OFF_AH, OFF_AZ = 0, 512, 1024, 1536
OFF_GQ, OFF_GK, OFF_GV, OFF_GA, OFF_GZ = 2048, 2304, 2560, 3072, 3088
OFF_SQ, OFF_SK, OFF_SV, OFF_SZ = 3600, 4112, 4240, 4368
OFF_MQ, OFF_MZ = 4880, 5392

VMEM_LIMIT = 56 * 1024 * 1024


def _dot(a, b):
    return jnp.dot(a, b, preferred_element_type=f32)


def _dot_nt(a, b):
    return lax.dot_general(a, b, (((1,), (1,)), ((), ())), preferred_element_type=f32)


def _dot_tn(a, b):
    return lax.dot_general(a, b, (((0,), (0,)), ((), ())), preferred_element_type=f32)


def _split3(x):
    hi = x.astype(bf16)
    r = x - hi.astype(f32)
    mid = r.astype(bf16)
    lo = (r - mid.astype(f32)).astype(bf16)
    return hi, mid, lo


def _silu(x):
    return x * jax.nn.sigmoid(x)


def _log_sigmoid(x):
    return jnp.minimum(x, 0.0) - jnp.log1p(jnp.exp(-jnp.abs(x)))


def _norm_matmul_kernel(x_ref, g_ref, wt_ref, o_ref, wb_ref, hn_ref):
    x = x_ref[...]
    y = x * lax.rsqrt(jnp.mean(x * x, axis=-1, keepdims=True) + EPS)
    hn_ref[...] = (y * g_ref[...]).astype(bf16)
    wb_ref[...] = wt_ref[...].astype(bf16)
    o_ref[...] = _dot_nt(hn_ref[...], wb_ref[...])


def _norm_matmul(x, g, wt, l, tn):
    m, k = x.shape
    n = wt.shape[1]
    return pl.pallas_call(
        _norm_matmul_kernel,
        grid=(pl.cdiv(n, tn),),
        in_specs=[
            pl.BlockSpec((m, k), lambda j: (0, 0)),
            pl.BlockSpec((None, 1, k), lambda j: (l, 0, 0)),
            pl.BlockSpec((None, tn, k), lambda j: (l, j, 0)),
        ],
        out_specs=[pl.BlockSpec((m, tn), lambda j: (0, j)), pl.BlockSpec((tn, k), lambda j: (j, 0))],
        out_shape=[jax.ShapeDtypeStruct((m, n), f32), jax.ShapeDtypeStruct((n, k), bf16)],
        scratch_shapes=[pltpu.VMEM((m, k), bf16)],
        compiler_params=pltpu.CompilerParams(
            dimension_semantics=("arbitrary",), vmem_limit_bytes=VMEM_LIMIT),
        name="norm_in_proj",
    )(x, g, wt)


def _memory_kv_kernel(x_ref, g_ref, w_ref, gk_ref, k_ref, v_ref, k4_ref, v4_ref, wb_ref):
    @pl.when(pl.program_id(1) == 0)
    def _():
        wb_ref[...] = w_ref[...].astype(bf16)

    x = x_ref[...]
    y = x * lax.rsqrt(jnp.mean(x * x, axis=-1, keepdims=True) + EPS)
    kv = _dot((y * g_ref[...]).astype(bf16), wb_ref[...])
    for h in range(MEM_HEADS):
        kh = kv[:, h * MEM_HD:(h + 1) * MEM_HD]
        kh = kh * lax.rsqrt(jnp.mean(kh * kh, axis=-1, keepdims=True) + EPS) * gk_ref[...]
        vh = kv[:, GROUP_W + h * MEM_HD:GROUP_W + (h + 1) * MEM_HD]
        k_ref[:, h * MEM_HD:(h + 1) * MEM_HD] = kh
        k4_ref[:, h, :] = kh
        v4_ref[:, h, :] = vh
    v_ref[...] = kv[:, GROUP_W:]


def _memory_kv(mem, g, w, gk):
    depth, b = w.shape[0], mem.shape[0]
    flat = jax.ShapeDtypeStruct((depth, b, N_MEM, GROUP_W), f32)
    split = jax.ShapeDtypeStruct((depth, b, N_MEM, MEM_HEADS, MEM_HD), f32)
    return pl.pallas_call(
        _memory_kv_kernel,
        grid=(depth, b),
        in_specs=[
            pl.BlockSpec((None, N_MEM, D_MODEL), lambda l, i: (i, 0, 0)),
            pl.BlockSpec((None, 1, D_MODEL), lambda l, i: (l, 0, 0)),
            pl.BlockSpec((None, D_MODEL, 2 * GROUP_W), lambda l, i: (l, 0, 0)),
            pl.BlockSpec((None, 1, MEM_HD), lambda l, i: (l, 0, 0)),
        ],
        out_specs=[pl.BlockSpec((None, None, N_MEM, GROUP_W), lambda l, i: (l, i, 0, 0))] * 2
        + [pl.BlockSpec((None, None, N_MEM, MEM_HEADS, MEM_HD), lambda l, i: (l, i, 0, 0, 0))] * 2,
        out_shape=[flat, flat, split, split],
        scratch_shapes=[pltpu.VMEM((D_MODEL, 2 * GROUP_W), bf16)],
        compiler_params=pltpu.CompilerParams(
            dimension_semantics=("arbitrary", "arbitrary"), vmem_limit_bytes=VMEM_LIMIT),
        name="memory_kv",
    )(mem, g, w, gk)


CONV_PAD = 8
N_STACK_SLOTS = 2 + 2 * SWA_KV_HEADS
N_SEQ_IN_PROMPT, N_SEQ_IN_DECODE, N_PARAMS_PROMPT, N_PARAMS_DECODE, N_OUT = 3, 7, 11, 9, 5


def _mixer_kernel(*refs, tile, decode, layer, bb):
    n_seq = N_SEQ_IN_DECODE if decode else N_SEQ_IN_PROMPT
    n_par = N_PARAMS_DECODE if decode else N_PARAMS_PROMPT
    seq_in = refs[:n_seq]
    params = refs[n_seq:n_seq + n_par]
    outs = refs[n_seq + n_par:n_seq + n_par + N_OUT]
    scratch = refs[n_seq + n_par + N_OUT:]

    def view(ref, s):
        if decode and ref.ndim == 2:
            return _RowWindow(ref, s * tile, tile)
        return ref.at[s]

    if decode:
        p_ref, tail_ref, qkn_ref = seq_in[0], scratch[4], scratch[6]
        bd_ref, gsq_ref, gsk_ref = params[-3], params[-5], params[-4]
        tail_ref[...] = p_ref[:, OFF_GZ:D_IN]
        sq0, sk0 = OFF_SQ - OFF_GZ, OFF_SK - OFF_GZ
        qkn_ref[:, 0:GROUP_W] = _head_norm(tail_ref[:, sq0:sq0 + GROUP_W], gsq_ref[...], bd_ref)
        qkn_ref[:, GROUP_W:GROUP_W + LANES] = _head_norm(tail_ref[:, sk0:sk0 + LANES], gsk_ref[...],
                                                         bd_ref)

        def seg_rows(off, width):
            if off < OFF_GZ:
                return p_ref[:, off:off + width]
            return tail_ref[:, off - OFF_GZ:off - OFF_GZ + width]

        gla_in_ref, gla_out_ref = seq_in[4], outs[2]
        _gla_group(seg_rows, bb * tile, tile, params[-8:-5], outs[0],
                   lambda c: _gla_block_diag(gla_in_ref.at[c]),
                   lambda c, state: _store_gla_state(gla_out_ref.at[c], state), carry=False)

    stages = [_mixer_seq([view(r, s) for r in seq_in], params, [view(r, s) for r in outs],
                         [view(r, s) for r in scratch], tile=tile, decode=decode, layer=layer)
              for s in range(bb)]
    for _ in itertools.zip_longest(*stages):
        pass


class _RowWindow:
    def __init__(self, ref, start, size):
        self.ref, self.start, self.size, self.dtype = ref, start, size, ref.dtype

    def _index(self, idx):
        rows, cols = (slice(None), slice(None)) if idx is Ellipsis else idx
        lo, hi, _ = rows.indices(self.size)
        return slice(self.start + lo, self.start + hi), cols

    def __getitem__(self, idx):
        return self.ref[self._index(idx)]

    def __setitem__(self, idx, value):
        self.ref[self._index(idx)] = value


GLA_INTRA_ROWS = MXU_CHUNK
GLA_K_W = GLA_HEADS * GLA_DK
GLA_V_W = GLA_HEADS * GLA_DV


def _gla_block_diag(state_ref):
    rows = []
    for h in range(GLA_HEADS):
        blocks = [state_ref[h] if j == h else jnp.zeros((GLA_DK, GLA_DV), f32)
                  for j in range(GLA_HEADS)]
        rows.append(jnp.concatenate(blocks, axis=1))
    return jnp.concatenate(rows, axis=0)


def _store_gla_state(state_ref, state):
    for h in range(GLA_HEADS):
        state_ref[h] = state[h * GLA_DK:(h + 1) * GLA_DK, h * GLA_DV:(h + 1) * GLA_DV]


def _gla_group(seg, T, C, params, mix_ref, state_in, state_out, carry):
    wup_ref, bga_ref, ggo_ref = params
    n_chunk = T // C
    G = min(T, GLA_INTRA_ROWS)
    groups = [slice(i * G, (i + 1) * G) for i in range(T // G)]
    row = lax.broadcasted_iota(jnp.int32, (G, G), 0)
    col = lax.broadcasted_iota(jnp.int32, (G, G), 1)
    causal = (row // C == col // C) & (row >= col)
    g_a = seg(OFF_GA, GLA_RANK).astype(bf16)
    log_a = _log_sigmoid(_dot(g_a, wup_ref[...].astype(bf16)) + bga_ref[...]) * (1.0 / GLA_TAU)
    la3 = _split3(log_a)
    tril = jnp.where(causal, 1.0, 0.0).astype(bf16)
    in_chunk = jnp.where(lax.broadcasted_iota(jnp.int32, (T, LANES), 0) // C
                         == lax.broadcasted_iota(jnp.int32, (T, LANES), 1), 1.0, 0.0).astype(bf16)
    cum = jnp.concatenate(
        [_dot(tril, la3[0][r]) + _dot(tril, la3[1][r]) + _dot(tril, la3[2][r]) for r in groups],
        axis=0)
    tot_t = (_dot_tn(la3[0], in_chunk) + _dot_tn(la3[1], in_chunk)
             + _dot_tn(la3[2], in_chunk))
    decay_t = jnp.exp(tot_t)
    g_k = seg(OFF_GK, GLA_K_W)
    qd = ((seg(OFF_GQ, GLA_K_W) * (GLA_DK ** -0.5)) * jnp.exp(cum)).astype(bf16)
    kd = (g_k * jnp.exp(-cum)).astype(bf16)
    k_tail = jnp.concatenate(
        [g_k[c * C:(c + 1) * C] * jnp.exp(cum[(c + 1) * C - 1:(c + 1) * C] - cum[c * C:(c + 1) * C])
         for c in range(n_chunk)], axis=0) if n_chunk > 1 else g_k * jnp.exp(cum[T - 1:T] - cum)
    kt = k_tail.astype(bf16)
    v_b = seg(OFF_GV, GLA_V_W).astype(bf16)
    g_z = seg(OFF_GZ, GROUP_W)

    o_intra = []
    for r in groups:
        o_heads = []
        for h in range(GLA_HEADS):
            ks = slice(h * GLA_DK, (h + 1) * GLA_DK)
            attn = jnp.where(causal, _dot_nt(qd[r, ks], kd[r, ks]), 0.0).astype(bf16)
            o_heads.append(_dot(attn, v_b[r, h * GLA_DV:(h + 1) * GLA_DV]))
        o_intra.append(jnp.concatenate(o_heads, axis=1))
    o_intra = jnp.concatenate(o_intra, axis=0)

    shape = (GLA_K_W, GLA_V_W)
    on_diag = (lax.broadcasted_iota(jnp.int32, shape, 0) // GLA_DK
               == lax.broadcasted_iota(jnp.int32, shape, 1) // GLA_DV)
    o_chunks = []
    state = None
    for c in range(n_chunk):
        rs = slice(c * C, (c + 1) * C)
        if c == 0 or not carry:
            state = state_in(c)
        o_chunks.append(o_intra[rs] + _dot(qd[rs], state.astype(bf16)))
        update = jnp.where(on_diag, _dot_tn(kt[rs], v_b[rs]), 0.0)
        state = decay_t[:, c:c + 1] * state + update
        state_out(c, state)
    o = jnp.concatenate(o_chunks, axis=0) if n_chunk > 1 else o_chunks[0]
    for h in range(GLA_HEADS):
        vs = slice(h * GLA_DV, (h + 1) * GLA_DV)
        o_h = o[:, vs]
        o_h = o_h * lax.rsqrt(jnp.mean(o_h * o_h, axis=-1, keepdims=True) + EPS) * ggo_ref[...]
        mix_ref[:, GROUP_W + h * GLA_DV:GROUP_W + (h + 1) * GLA_DV] = (
            o_h * _silu(g_z[:, vs])).astype(mix_ref.dtype)


def _head_norm(x, g, bd_ref):
    rows, n_lanes = x.shape
    w = min(n_lanes, MXU_CHUNK)
    pieces = n_lanes // w
    bd = bd_ref[0:w, 0:w]

    def head_sums(v):
        stacked = jnp.concatenate([v[:, i * w:(i + 1) * w] for i in range(pieces)], axis=0)
        r = _dot(stacked, bd)
        return jnp.concatenate([r[i * rows:(i + 1) * rows] for i in range(pieces)], axis=1)

    sq = x * x
    hi = sq.astype(bf16)
    lo = (sq - hi.astype(f32)).astype(bf16)
    ms = (head_sums(hi) + head_sums(lo)) * (1.0 / SWA_HD)
    return x * lax.rsqrt(ms + EPS) * g


def _mixer_seq(seq_in, params, outs, scratch, *, tile, decode, layer):
    if decode:
        p_ref, mk_ref, mv_ref, conv_in_ref, gla_in_ref, kc_ref, vc_ref = seq_in
    else:
        x_ref, mk_ref, mv_ref = seq_in
        gn_ref, wt_ref = params[:2]
    (convw_ref, wup_ref, bga_ref, ggo_ref, gsq_ref, gsk_ref, bd_ref, sinks_ref,
     gmq_ref) = params[-N_PARAMS_DECODE:]
    mix_ref, conv_out_ref, gla_out_ref, kbuf_ref, vbuf_ref = outs
    ext_ref, s_ref, kprev_ref, vprev_ref, tail_ref, stk_ref, qkn_ref = scratch

    T = tile
    t = pl.program_id(1)

    def init_state():
        ext_ref[0:CONV_PAD, :] = jnp.zeros((CONV_PAD, GROUP_W), f32)
        if decode:
            ext_ref[CONV_PAD - (CONV_W - 1):CONV_PAD, :] = conv_in_ref[...]
            kprev_ref[...] = kc_ref[...]
            vprev_ref[...] = vc_ref[...]
        else:
            s_ref[...] = jnp.zeros_like(s_ref)
            kprev_ref[...] = jnp.zeros_like(kprev_ref)
            vprev_ref[...] = jnp.zeros_like(vprev_ref)

    if decode:
        init_state()
    else:
        pl.when(t == 0)(init_state)
    yield

    if decode:
        def seg(off, width):
            if off < OFF_GZ:
                return p_ref[:, off:off + width]
            return tail_ref[:, off - OFF_GZ:off - OFF_GZ + width]
    else:
        x = x_ref[...]
        hn = (x * lax.rsqrt(jnp.mean(x * x, axis=-1, keepdims=True) + EPS) * gn_ref[...]).astype(bf16)

        def seg(off, width):
            return _dot_nt(hn, wt_ref[off:off + width, :])

    def stack_rows(pieces, slot):
        r, w = pieces[0].shape
        if r % 8 == 0:
            return jnp.concatenate(pieces, axis=0)
        for j, piece in enumerate(pieces):
            stk_ref[slot, j * r:(j + 1) * r, 0:w] = piece
        return stk_ref[slot, 0:len(pieces) * r, 0:w]

    def unstack_rows(x, n, slot):
        r, w = x.shape[0] // n, x.shape[1]
        if r % 8 == 0:
            return [x[j * r:(j + 1) * r] for j in range(n)]
        stk_ref[slot, 0:n * r, 0:w] = x
        return [stk_ref[slot, j * r:(j + 1) * r, 0:w] for j in range(n)]

    u = seg(OFF_AC, GROUP_W) * seg(OFF_AH, GROUP_W)
    ext_ref[CONV_PAD:CONV_PAD + T, :] = u
    conv = (convw_ref[0:1, :] * ext_ref[CONV_PAD - 2:CONV_PAD - 2 + T, :]
            + convw_ref[1:2, :] * ext_ref[CONV_PAD - 1:CONV_PAD - 1 + T, :]
            + convw_ref[2:3, :] * u)
    mix_ref[:, 0:GROUP_W] = (seg(OFF_AB, GROUP_W) * conv
                             * _silu(seg(OFF_AZ, GROUP_W))).astype(mix_ref.dtype)
    conv_state = ext_ref[CONV_PAD + T - 2:CONV_PAD + T, :]
    ext_ref[CONV_PAD - 2:CONV_PAD, :] = conv_state
    conv_out_ref[...] = conv_state
    yield

    if not decode:
        def keep_state(c, state):
            if c == T // GLA_CHUNK - 1:
                s_ref[...] = state
                _store_gla_state(gla_out_ref, state)

        _gla_group(seg, T, GLA_CHUNK, (wup_ref, bga_ref, ggo_ref), mix_ref,
                   lambda c: s_ref[...], keep_state, carry=True)
        yield

    s_z = seg(OFF_SZ, GROUP_W)
    if decode:
        q_n, k_n = qkn_ref[:, 0:GROUP_W], qkn_ref[:, GROUP_W:GROUP_W + LANES]
        v_n = seg(OFF_SV, LANES)
    else:
        s_q = seg(OFF_SQ, GROUP_W)
        s_kv = seg(OFF_SK, 2 * LANES)
        q_n = _head_norm(s_q, gsq_ref[...], bd_ref)
        k_n = _head_norm(s_kv[:, 0:LANES], gsk_ref[...], bd_ref)
        v_n = s_kv[:, LANES:2 * LANES]
    yield

    BQ = min(WINDOW, T)
    n_blk = T // BQ
    stack = SWA_GROUP
    nk = WINDOW + BQ
    qi = lax.broadcasted_iota(jnp.int32, (stack * BQ, nk), 0) % BQ
    kj = lax.broadcasted_iota(jnp.int32, (stack * BQ, nk), 1)
    dist = qi + WINDOW - kj
    band = (dist >= 0) & (dist < WINDOW)
    srow = lax.broadcasted_iota(jnp.int32, (stack * BQ, 1), 0) // BQ
    for blk in range(n_blk):
        rs = slice(blk * BQ, (blk + 1) * BQ)
        if blk == 0:
            k_prev, v_prev = kprev_ref[...], vprev_ref[...]
            valid = band if decode else band & ((kj >= WINDOW) | (t > 0))
        else:
            ps = slice((blk - 1) * BQ, blk * BQ)
            k_prev, v_prev = k_n[ps], v_n[ps]
            valid = band
        k_cat = jnp.concatenate([k_prev, k_n[rs]], axis=0)
        v_cat = jnp.concatenate([v_prev, v_n[rs]], axis=0)
        for g in range(SWA_KV_HEADS):
            kg = k_cat[:, g * SWA_HD:(g + 1) * SWA_HD].astype(bf16)
            vg = v_cat[:, g * SWA_HD:(g + 1) * SWA_HD].astype(bf16)
            heads = [g * SWA_GROUP + j for j in range(stack)]
            qg = stack_rows([q_n[rs, hd * SWA_HD:(hd + 1) * SWA_HD] for hd in heads],
                            2 + 2 * g).astype(bf16)
            sink = jnp.full((stack * BQ, 1), sinks_ref[layer, heads[0]], f32)
            for j in range(1, stack):
                sink = jnp.where(srow == j, sinks_ref[layer, heads[j]], sink)
            s = _dot_nt(qg, kg) * (SWA_HD ** -0.5)
            s = jnp.where(valid, s, -jnp.inf)
            m = jnp.maximum(jnp.max(s, axis=-1, keepdims=True), sink)
            e = jnp.exp(s - m)
            prob = e / (jnp.sum(e, axis=-1, keepdims=True) + jnp.exp(sink - m))
            o = _dot(prob.astype(bf16), vg)
            for hd, o_hd in zip(heads, unstack_rows(o, stack, 3 + 2 * g)):
                z = s_z[rs, hd * SWA_HD:(hd + 1) * SWA_HD]
                mix_ref[rs, 2 * GROUP_W + hd * SWA_HD:2 * GROUP_W + (hd + 1) * SWA_HD] = (
                    o_hd * _silu(z)).astype(mix_ref.dtype)
            yield

    if decode:
        kbuf_ref[0:WINDOW - T, :] = kc_ref[T:WINDOW, :]
        kbuf_ref[WINDOW - T:WINDOW, :] = k_n
        vbuf_ref[0:WINDOW - T, :] = vc_ref[T:WINDOW, :]
        vbuf_ref[WINDOW - T:WINDOW, :] = v_n
    else:
        kprev_ref[...] = k_n[T - WINDOW:T]
        vprev_ref[...] = v_n[T - WINDOW:T]
        kbuf_ref[...] = k_n[T - WINDOW:T]
        vbuf_ref[...] = v_n[T - WINDOW:T]
    yield

    m_q = seg(OFF_MQ, GROUP_W)
    m_z = seg(OFF_MZ, GROUP_W)
    yield

    def mem_q(h):
        qh = m_q[:, h * MEM_HD:(h + 1) * MEM_HD]
        return qh * lax.rsqrt(jnp.mean(qh * qh, axis=-1, keepdims=True) + EPS) * gmq_ref[...]

    def softmax(s):
        e = jnp.exp(s - jnp.max(s, axis=-1, keepdims=True))
        return e / jnp.sum(e, axis=-1, keepdims=True)

    if decode:
        qs = stack_rows([mem_q(h) for h in range(MEM_HEADS)], 0).astype(bf16)
        s = _dot_nt(qs, mk_ref[...].astype(bf16)) * (MEM_HD ** -0.5)
        shape = (MEM_HEADS * T, MEM_HEADS * N_MEM)
        same_head = (lax.broadcasted_iota(jnp.int32, shape, 0) // T
                     == lax.broadcasted_iota(jnp.int32, shape, 1) % MEM_HEADS)
        yield
        prob = softmax(jnp.where(same_head, s, -jnp.inf))
        o_all = unstack_rows(_dot(prob.astype(bf16), mv_ref[...].astype(bf16)), MEM_HEADS, 1)
    else:
        o_all = []
        for h in range(MEM_HEADS):
            hs = slice(h * MEM_HD, (h + 1) * MEM_HD)
            s = _dot_nt(mem_q(h).astype(bf16), mk_ref[:, hs].astype(bf16)) * (MEM_HD ** -0.5)
            o_all.append(_dot(softmax(s).astype(bf16), mv_ref[:, hs].astype(bf16)))
    for h in range(MEM_HEADS):
        mix_ref[:, 3 * GROUP_W + h * MEM_HD:3 * GROUP_W + (h + 1) * MEM_HD] = (
            o_all[h] * _silu(m_z[:, h * MEM_HD:(h + 1) * MEM_HD])).astype(mix_ref.dtype)


def _mixer(tokens, norm_w, mem_k, mem_v, mem_layer, state, params, layer, tile, bb, decode):
    if decode:
        b, width = state[0].shape[1], tokens.shape[1]
        L = tokens.shape[0] // b
    else:
        b, L, width = tokens.shape
    nt = L // tile
    assert nt == 1 or not decode, "a decode call covers each sequence with a single tile"
    conv_w, w_up, b_ga, g_go, g_sq, g_sk, bd, sinks, g_mq = params

    def tok(width):
        if decode:
            return pl.BlockSpec((bb * tile, width), lambda i, t: (i, 0))
        return pl.BlockSpec((bb, tile, width), lambda i, t: (i, t, 0))

    def per_seq(*shape):
        return pl.BlockSpec((bb,) + shape, lambda i, t: (i,) + (0,) * len(shape))

    def per_seq_at(lyr, *shape):
        return pl.BlockSpec((None, bb) + shape, lambda i, t: (lyr, i) + (0,) * len(shape))

    def param(a):
        return pl.BlockSpec((None,) + a.shape[1:], lambda i, t: (layer,) + (0,) * (a.ndim - 1))

    kv_w = SWA_KV_HEADS * SWA_HD
    state_shapes = [(CONV_W - 1, GROUP_W), (GLA_HEADS, GLA_DK, GLA_DV), (WINDOW, kv_w),
                    (WINDOW, kv_w)]
    in_specs = [tok(width), per_seq_at(mem_layer, *mem_k.shape[2:]),
                per_seq_at(mem_layer, *mem_v.shape[2:])]
    args = [tokens, mem_k, mem_v]
    if decode:
        in_specs += [per_seq_at(layer, *s) for s in state_shapes]
        args += list(state)
    else:
        g_n, w_t = norm_w
        in_specs += [param(g_n), pl.BlockSpec(w_t.shape, lambda i, t: (0, 0),
                                              pipeline_mode=pl.Buffered(1))]
        args += [g_n, w_t]
    in_specs += [param(conv_w), param(w_up), param(b_ga), param(g_go), param(g_sq), param(g_sk),
                 pl.BlockSpec(bd.shape, lambda i, t: (0, 0)),
                 pl.BlockSpec(memory_space=pltpu.SMEM), param(g_mq)]
    args += [conv_w, w_up, b_ga, g_go, g_sq, g_sk, bd, sinks, g_mq]
    if decode:
        out_shape = [jax.ShapeDtypeStruct((b * L, 4 * GROUP_W), f32)]
        row_scratch = [pltpu.VMEM((bb * tile, D_IN - OFF_GZ), f32),
                       pltpu.VMEM((bb * tile, GROUP_W + LANES), f32)]
    else:
        out_shape = [jax.ShapeDtypeStruct((b, L, 4 * GROUP_W), bf16)]
        row_scratch = [pltpu.VMEM((bb, 8, LANES), f32)] * 2
    out_shape += [jax.ShapeDtypeStruct((b,) + s, f32) for s in state_shapes]
    out_specs = [tok(4 * GROUP_W)] + [per_seq(*s) for s in state_shapes]
    return pl.pallas_call(
        functools.partial(_mixer_kernel, tile=tile, decode=decode, layer=layer, bb=bb),
        grid=(b // bb, nt),
        in_specs=in_specs,
        out_specs=out_specs,
        out_shape=out_shape,
        scratch_shapes=[
            pltpu.VMEM((bb, CONV_PAD + tile, GROUP_W), f32),
            pltpu.VMEM((bb, 8, LANES) if decode else (bb, GLA_K_W, GLA_V_W), f32),
            pltpu.VMEM((bb, WINDOW, kv_w), f32),
            pltpu.VMEM((bb, WINDOW, kv_w), f32),
            row_scratch[0],
            pltpu.VMEM((bb, N_STACK_SLOTS, MEM_HEADS * min(tile, 8), LANES), f32),
            row_scratch[1],
        ],
        compiler_params=pltpu.CompilerParams(
            dimension_semantics=("arbitrary", "arbitrary"), vmem_limit_bytes=VMEM_LIMIT),
        name="mixer_decode" if decode else "mixer_prompt",
    )(*args)


def _out_proj_kernel(mix_ref, w_ref, x_ref, y_ref, wb_ref):
    @pl.when(pl.program_id(1) == 0)
    def _():
        wb_ref[...] = w_ref[...].astype(bf16)

    y_ref[...] = x_ref[...] + _dot(mix_ref[...].astype(bf16), wb_ref[...])


def _out_proj(mix, w, x, l, tm, tn):
    m, k = mix.shape
    n = w.shape[2]
    return pl.pallas_call(
        _out_proj_kernel,
        grid=(n // tn, m // tm),
        in_specs=[
            pl.BlockSpec((tm, k), lambda j, i: (i, 0)),
            pl.BlockSpec((None, k, tn), lambda j, i: (l, 0, j),
                         pipeline_mode=pl.Buffered(1) if tn == n else None),
            pl.BlockSpec((tm, tn), lambda j, i: (i, j)),
        ],
        out_specs=pl.BlockSpec((tm, tn), lambda j, i: (i, j)),
        out_shape=jax.ShapeDtypeStruct((m, n), f32),
        scratch_shapes=[pltpu.VMEM((k, tn), bf16)],
        compiler_params=pltpu.CompilerParams(
            dimension_semantics=("arbitrary", "arbitrary"), vmem_limit_bytes=VMEM_LIMIT),
        name="out_proj",
    )(mix, w, x)


PROMPT_TILE = 512
DECODE_SEQS_PER_STEP = 8
PROJ_TN = 1536
OUT_TM, OUT_TN = 512, 2048

_LANE = np.arange(GROUP_W)
HEAD_BLOCK_DIAG = _LANE[:, None] // SWA_HD == _LANE[None, :] // SWA_HD


def kernel(x_prompt, x_sample, mem_prompt, state_conv, state_gla, cache_swa_k, cache_swa_v,
           cache_mem_k, cache_mem_v, g_norm, w_in, conv_w, w_gla_a_up, b_gla_a, g_gla_o,
           g_swa_q, g_swa_k, swa_sinks, g_mem, w_mem_kv, g_mem_q, g_mem_k, w_out):
    depth = w_in.shape[0]
    bp, lp, _ = x_prompt.shape
    bs, ls, _ = x_sample.shape
    hp = x_prompt.reshape(bp * lp, D_MODEL)
    hs = x_sample.reshape(bs * ls, D_MODEL)

    def row(a):
        return a[:, None, :]

    params = (conv_w, w_gla_a_up, row(b_gla_a), row(g_gla_o),
              row(jnp.tile(g_swa_q, (1, SWA_HEADS))), row(jnp.tile(g_swa_k, (1, SWA_KV_HEADS))),
              jnp.asarray(HEAD_BLOCK_DIAG, bf16), swa_sinks, row(g_mem_q))
    g_n, g_m, g_mk = row(g_norm), row(g_mem), row(g_mem_k)
    w_in_t = jnp.swapaxes(w_in, 1, 2)
    kv_w = SWA_KV_HEADS * SWA_HD
    state = (state_conv, state_gla, cache_swa_k.reshape(depth, bs, WINDOW, kv_w),
             cache_swa_v.reshape(depth, bs, WINDOW, kv_w))
    mem_k_s = cache_mem_k.reshape(depth, bs, N_MEM * MEM_HEADS, MEM_HD)
    mem_v_s = cache_mem_v.reshape(depth, bs, N_MEM * MEM_HEADS, MEM_HD)

    mk, mv, mem_k_p, mem_v_p = _memory_kv(mem_prompt, g_m, w_mem_kv, g_mk)
    outs = [[] for _ in range(8)]
    for l in range(depth):
        proj, w_bf = _norm_matmul(hs, g_n, w_in_t, l, PROJ_TN)

        mix, c, s, kb, vb = _mixer(hp.reshape(bp, lp, D_MODEL), (g_n, w_bf), mk, mv, l,
                                   None, params, l, PROMPT_TILE, 1, decode=False)
        hp = _out_proj(mix.reshape(bp * lp, 4 * GROUP_W), w_out, hp, l, OUT_TM, OUT_TN)
        for lst, a in zip(outs[:4], (
                c, s, kb.reshape(bp, WINDOW, SWA_KV_HEADS, SWA_HD),
                vb.reshape(bp, WINDOW, SWA_KV_HEADS, SWA_HD))):
            lst.append(a)

        mix, c, s, kb, vb = _mixer(proj, None, mem_k_s, mem_v_s, l, state, params, l, ls,
                                   DECODE_SEQS_PER_STEP, decode=True)
        hs = _out_proj(mix, w_out, hs, l, bs * ls, OUT_TN)
        for lst, a in zip(outs[4:], (
                c, s, kb.reshape(bs, WINDOW, SWA_KV_HEADS, SWA_HD),
                vb.reshape(bs, WINDOW, SWA_KV_HEADS, SWA_HD))):
            lst.append(a)

    stacked = [jnp.stack(o) for o in outs]
    return (hp.reshape(bp, lp, D_MODEL), hs.reshape(bs, ls, D_MODEL),
            *stacked[:4], mem_k_p, mem_v_p, *stacked[4:])
```

```python
import functools
import itertools

import jax
import jax.numpy as jnp
import numpy as np
from jax import lax
from jax.experimental import pallas as pl
from jax.experimental.pallas import tpu as pltpu

f32 = jnp.float32
bf16 = jnp.bfloat16

D_MODEL = 2048
GROUP_W = 512
GLA_HEADS = 4
GLA_DK = 64
GLA_DV = 128
GLA_RANK = 16
GLA_TAU = 16.0
GLA_CHUNK = 64
SWA_HEADS = 8
SWA_KV_HEADS = 2
SWA_HD = 64
SWA_GROUP = SWA_HEADS // SWA_KV_HEADS
WINDOW = 128
N_MEM = 256
MEM_HEADS = 4
MEM_HD = 128
CONV_W = 3
EPS = 1e-6

LANES = 128
MXU_CHUNK = 256

D_IN = 5904
OFF_AB, OFF_AC, OFF_AH, OFF_AZ = 0, 512, 1024, 1536
OFF_GQ, OFF_GK, OFF_GV, OFF_GA, OFF_GZ = 2048, 2304, 2560, 3072, 3088
OFF_SQ, OFF_SK, OFF_SV, OFF_SZ = 3600, 4112, 4240, 4368
OFF_MQ, OFF_MZ = 4880, 5392

VMEM_LIMIT = 56 * 1024 * 1024


def _dot(a, b):
    return jnp.dot(a, b, preferred_element_type=f32)


def _dot_nt(a, b):
    return lax.dot_general(a, b, (((1,), (1,)), ((), ())), preferred_element_type=f32)


def _dot_tn(a, b):
    return lax.dot_general(a, b, (((0,), (0,)), ((), ())), preferred_element_type=f32)


def _split3(x):
    hi = x.astype(bf16)
    r = x - hi.astype(f32)
    mid = r.astype(bf16)
    lo = (r - mid.astype(f32)).astype(bf16)
    return hi, mid, lo


def _silu(x):
    return x * jax.nn.sigmoid(x)


def _log_sigmoid(x):
    return jnp.minimum(x, 0.0) - jnp.log1p(jnp.exp(-jnp.abs(x)))


def _norm_matmul_kernel(x_ref, g_ref, wt_ref, o_ref, wb_ref, hn_ref):
    x = x_ref[...]
    y = x * lax.rsqrt(jnp.mean(x * x, axis=-1, keepdims=True) + EPS)
    hn_ref[...] = (y * g_ref[...]).astype(bf16)
    wb_ref[...] = wt_ref[...].astype(bf16)
    o_ref[...] = _dot_nt(hn_ref[...], wb_ref[...])


def _norm_matmul(x, g, wt, l, tn):
    m, k = x.shape
    n = wt.shape[1]
    return pl.pallas_call(
        _norm_matmul_kernel,
        grid=(pl.cdiv(n, tn),),
        in_specs=[
            pl.BlockSpec((m, k), lambda j: (0, 0)),
            pl.BlockSpec((None, 1, k), lambda j: (l, 0, 0)),
            pl.BlockSpec((None, tn, k), lambda j: (l, j, 0)),
        ],
        out_specs=[pl.BlockSpec((m, tn), lambda j: (0, j)), pl.BlockSpec((tn, k), lambda j: (j, 0))],
        out_shape=[jax.ShapeDtypeStruct((m, n), f32), jax.ShapeDtypeStruct((n, k), bf16)],
        scratch_shapes=[pltpu.VMEM((m, k), bf16)],
        compiler_params=pltpu.CompilerParams(
            dimension_semantics=("arbitrary",), vmem_limit_bytes=VMEM_LIMIT),
        name="norm_in_proj",
    )(x, g, wt)


def _memory_kv_kernel(x_ref, g_ref, w_ref, gk_ref, k_ref, v_ref, k4_ref, v4_ref, wb_ref):
    @pl.when(pl.program_id(1) == 0)
    def _():
        wb_ref[...] = w_ref[...].astype(bf16)

    x = x_ref[...]
    y = x * lax.rsqrt(jnp.mean(x * x, axis=-1, keepdims=True) + EPS)
    kv = _dot((y * g_ref[...]).astype(bf16), wb_ref[...])
    for h in range(MEM_HEADS):
        kh = kv[:, h * MEM_HD:(h + 1) * MEM_HD]
        kh = kh * lax.rsqrt(jnp.mean(kh * kh, axis=-1, keepdims=True) + EPS) * gk_ref[...]
        vh = kv[:, GROUP_W + h * MEM_HD:GROUP_W + (h + 1) * MEM_HD]
        k_ref[:, h * MEM_HD:(h + 1) * MEM_HD] = kh
        k4_ref[:, h, :] = kh
        v4_ref[:, h, :] = vh
    v_ref[...] = kv[:, GROUP_W:]


def _memory_kv(mem, g, w, gk):
    depth, b = w.shape[0], mem.shape[0]
    flat = jax.ShapeDtypeStruct((depth, b, N_MEM, GROUP_W), f32)
    split = jax.ShapeDtypeStruct((depth, b, N_MEM, MEM_HEADS, MEM_HD), f32)
    return pl.pallas_call(
        _memory_kv_kernel,
        grid=(depth, b),
        in_specs=[
            pl.BlockSpec((None, N_MEM, D_MODEL), lambda l, i: (i, 0, 0)),
            pl.BlockSpec((None, 1, D_MODEL), lambda l, i: (l, 0, 0)),
            pl.BlockSpec((None, D_MODEL, 2 * GROUP_W), lambda l, i: (l, 0, 0)),
            pl.BlockSpec((None, 1, MEM_HD), lambda l, i: (l, 0, 0)),
        ],
        out_specs=[pl.BlockSpec((None, None, N_MEM, GROUP_W), lambda l, i: (l, i, 0, 0))] * 2
        + [pl.BlockSpec((None, None, N_MEM, MEM_HEADS, MEM_HD), lambda l, i: (l, i, 0, 0, 0))] * 2,
        out_shape=[flat, flat, split, split],
        scratch_shapes=[pltpu.VMEM((D_MODEL, 2 * GROUP_W), bf16)],
        compiler_params=pltpu.CompilerParams(
            dimension_semantics=("arbitrary", "arbitrary"), vmem_limit_bytes=VMEM_LIMIT),
        name="memory_kv",
    )(mem, g, w, gk)


CONV_PAD = 8
N_STACK_SLOTS = 2 + 2 * SWA_KV_HEADS
N_SEQ_IN_PROMPT, N_SEQ_IN_DECODE, N_PARAMS_PROMPT, N_PARAMS_DECODE, N_OUT = 3, 7, 11, 9, 5


def _mixer_kernel(*refs, tile, decode, layer, bb):
    n_seq = N_SEQ_IN_DECODE if decode else N_SEQ_IN_PROMPT
    n_par = N_PARAMS_DECODE if decode else N_PARAMS_PROMPT
    seq_in = refs[:n_seq]
    params = refs[n_seq:n_seq + n_par]
    outs = refs[n_seq + n_par:n_seq + n_par + N_OUT]
    scratch = refs[n_seq + n_par + N_OUT:]

    def view(ref, s):
        if decode and ref.ndim == 2:
            return _RowWindow(ref, s * tile, tile)
        return ref.at[s]

    if decode:
        p_ref, tail_ref, qkn_ref = seq_in[0], scratch[4], scratch[6]
        bd_ref, gsq_ref, gsk_ref = params[-3], params[-5], params[-4]
        tail_ref[...] = p_ref[:, OFF_GZ:D_IN]
        sq0, sk0 = OFF_SQ - OFF_GZ, OFF_SK - OFF_GZ
        qkn_ref[:, 0:GROUP_W] = _head_norm(tail_ref[:, sq0:sq0 + GROUP_W], gsq_ref[...], bd_ref)
        qkn_ref[:, GROUP_W:GROUP_W + LANES] = _head_norm(tail_ref[:, sk0:sk0 + LANES], gsk_ref[...],
                                                         bd_ref)

        def seg_rows(off, width):
            if off < OFF_GZ:
                return p_ref[:, off:off + width]
            return tail_ref[:, off - OFF_GZ:off - OFF_GZ + width]

        gla_in_ref, gla_out_ref = seq_in[4], outs[2]
        _gla_group(seg_rows, bb * tile, tile, params[-8:-5], outs[0],
                   lambda c: _gla_block_diag(gla_in_ref.at[c]),
                   lambda c, state: _store_gla_state(gla_out_ref.at[c], state), carry=False)

    stages = [_mixer_seq([view(r, s) for r in seq_in], params, [view(r, s) for r in outs],
                         [view(r, s) for r in scratch], tile=tile, decode=decode, layer=layer)
              for s in range(bb)]
    for _ in itertools.zip_longest(*stages):
        pass


class _RowWindow:
    def __init__(self, ref, start, size):
        self.ref, self.start, self.size, self.dtype = ref, start, size, ref.dtype

    def _index(self, idx):
        rows, cols = (slice(None), slice(None)) if idx is Ellipsis else idx
        lo, hi, _ = rows.indices(self.size)
        return slice(self.start + lo, self.start + hi), cols

    def __getitem__(self, idx):
        return self.ref[self._index(idx)]

    def __setitem__(self, idx, value):
        self.ref[self._index(idx)] = value


GLA_INTRA_ROWS = MXU_CHUNK
GLA_K_W = GLA_HEADS * GLA_DK
GLA_V_W = GLA_HEADS * GLA_DV


def _gla_block_diag(state_ref):
    rows = []
    for h in range(GLA_HEADS):
        blocks = [state_ref[h] if j == h else jnp.zeros((GLA_DK, GLA_DV), f32)
                  for j in range(GLA_HEADS)]
        rows.append(jnp.concatenate(blocks, axis=1))
    return jnp.concatenate(rows, axis=0)


def _store_gla_state(state_ref, state):
    for h in range(GLA_HEADS):
        state_ref[h] = state[h * GLA_DK:(h + 1) * GLA_DK, h * GLA_DV:(h + 1) * GLA_DV]


def _gla_group(seg, T, C, params, mix_ref, state_in, state_out, carry, seg_t=None):
    wup_ref, bga_ref, ggo_ref = params
    n_chunk = T // C
    G = min(T, GLA_INTRA_ROWS)
    groups = [slice(i * G, (i + 1) * G) for i in range(T // G)]
    row = lax.broadcasted_iota(jnp.int32, (G, G), 0)
    col = lax.broadcasted_iota(jnp.int32, (G, G), 1)
    causal = (row // C == col // C) & (row >= col)
    if seg_t is None:
        a_up = _dot(seg(OFF_GA, GLA_RANK).astype(bf16), wup_ref[...].astype(bf16))
    else:
        a_up = _dot_tn(seg_t(OFF_GA, GLA_RANK).astype(bf16), wup_ref[...].astype(bf16))
    log_a = _log_sigmoid(a_up + bga_ref[...]) * (1.0 / GLA_TAU)
    la3 = _split3(log_a)
    tril = jnp.where(causal, 1.0, 0.0).astype(bf16)
    in_chunk = jnp.where(lax.broadcasted_iota(jnp.int32, (T, LANES), 0) // C
                         == lax.broadcasted_iota(jnp.int32, (T, LANES), 1), 1.0, 0.0).astype(bf16)
    cum = jnp.concatenate(
        [_dot(tril, la3[0][r]) + _dot(tril, la3[1][r]) + _dot(tril, la3[2][r]) for r in groups],
        axis=0)
    tot_t = (_dot_tn(la3[0], in_chunk) + _dot_tn(la3[1], in_chunk)
             + _dot_tn(la3[2], in_chunk))
    decay_t = jnp.exp(tot_t)
    g_k = seg(OFF_GK, GLA_K_W)
    qd = ((seg(OFF_GQ, GLA_K_W) * (GLA_DK ** -0.5)) * jnp.exp(cum)).astype(bf16)
    kd = (g_k * jnp.exp(-cum)).astype(bf16)
    k_tail = jnp.concatenate(
        [g_k[c * C:(c + 1) * C] * jnp.exp(cum[(c + 1) * C - 1:(c + 1) * C] - cum[c * C:(c + 1) * C])
         for c in range(n_chunk)], axis=0) if n_chunk > 1 else g_k * jnp.exp(cum[T - 1:T] - cum)
    kt = k_tail.astype(bf16)
    v_b = seg(OFF_GV, GLA_V_W).astype(bf16)
    g_z = seg(OFF_GZ, GROUP_W)

    o_intra = []
    for r in groups:
        o_heads = []
        for h in range(GLA_HEADS):
            ks = slice(h * GLA_DK, (h + 1) * GLA_DK)
            attn = jnp.where(causal, _dot_nt(qd[r, ks], kd[r, ks]), 0.0).astype(bf16)
            o_heads.append(_dot(attn, v_b[r, h * GLA_DV:(h + 1) * GLA_DV]))
        o_intra.append(jnp.concatenate(o_heads, axis=1))
    o_intra = jnp.concatenate(o_intra, axis=0)

    shape = (GLA_K_W, GLA_V_W)
    on_diag = (lax.broadcasted_iota(jnp.int32, shape, 0) // GLA_DK
               == lax.broadcasted_iota(jnp.int32, shape, 1) // GLA_DV)
    o_chunks = []
    state = None
    for c in range(n_chunk):
        rs = slice(c * C, (c + 1) * C)
        if c == 0 or not carry:
            state = state_in(c)
        o_chunks.append(o_intra[rs] + _dot(qd[rs], state.astype(bf16)))
        update = jnp.where(on_diag, _dot_tn(kt[rs], v_b[rs]), 0.0)
        state = decay_t[:, c:c + 1] * state + update
        state_out(c, state)
    o = jnp.concatenate(o_chunks, axis=0) if n_chunk > 1 else o_chunks[0]
    for h in range(GLA_HEADS):
        vs = slice(h * GLA_DV, (h + 1) * GLA_DV)
        o_h = o[:, vs]
        o_h = o_h * lax.rsqrt(jnp.mean(o_h * o_h, axis=-1, keepdims=True) + EPS) * ggo_ref[...]
        mix_ref[:, GROUP_W + h * GLA_DV:GROUP_W + (h + 1) * GLA_DV] = (
            o_h * _silu(g_z[:, vs])).astype(mix_ref.dtype)


def _head_norm(x, g, bd_ref):
    rows, n_lanes = x.shape
    w = min(n_lanes, MXU_CHUNK)
    pieces = n_lanes // w
    bd = bd_ref[0:w, 0:w]

    def head_sums(v):
        stacked = jnp.concatenate([v[:, i * w:(i + 1) * w] for i in range(pieces)], axis=0)
        r = _dot(stacked, bd)
        return jnp.concatenate([r[i * rows:(i + 1) * rows] for i in range(pieces)], axis=1)

    sq = x * x
    hi = sq.astype(bf16)
    lo = (sq - hi.astype(f32)).astype(bf16)
    ms = (head_sums(hi) + head_sums(lo)) * (1.0 / SWA_HD)
    return x * lax.rsqrt(ms + EPS) * g


def _mixer_seq(seq_in, params, outs, scratch, *, tile, decode, layer):
    if decode:
        p_ref, mk_ref, mv_ref, conv_in_ref, gla_in_ref, kc_ref, vc_ref = seq_in
    else:
        x_ref, mk_ref, mv_ref = seq_in
        gn_ref, wt_ref = params[:2]
    (convw_ref, wup_ref, bga_ref, ggo_ref, gsq_ref, gsk_ref, bd_ref, sinks_ref,
     gmq_ref) = params[-N_PARAMS_DECODE:]
    mix_ref, conv_out_ref, gla_out_ref, kbuf_ref, vbuf_ref = outs
    ext_ref, s_ref, kprev_ref, vprev_ref, tail_ref, stk_ref, qkn_ref = scratch

    T = tile
    t = pl.program_id(1)

    def init_state():
        ext_ref[0:CONV_PAD, :] = jnp.zeros((CONV_PAD, GROUP_W), f32)
        if decode:
            ext_ref[CONV_PAD - (CONV_W - 1):CONV_PAD, :] = conv_in_ref[...]
            kprev_ref[...] = kc_ref[...]
            vprev_ref[...] = vc_ref[...]
        else:
            s_ref[...] = jnp.zeros_like(s_ref)
            kprev_ref[...] = jnp.zeros_like(kprev_ref)
            vprev_ref[...] = jnp.zeros_like(vprev_ref)

    if decode:
        init_state()
    else:
        pl.when(t == 0)(init_state)
    yield

    if decode:
        def seg(off, width):
            if off < OFF_GZ:
                return p_ref[:, off:off + width]
            return tail_ref[:, off - OFF_GZ:off - OFF_GZ + width]
    else:
        x = x_ref[...]
        hn = (x * lax.rsqrt(jnp.mean(x * x, axis=-1, keepdims=True) + EPS) * gn_ref[...]).astype(bf16)

        def seg(off, width):
            return _dot_nt(hn, wt_ref[off:off + width, :])

        def seg_t(off, width):
            return _dot_nt(wt_ref[off:off + width, :], hn)

    def stack_rows(pieces, slot):
        r, w = pieces[0].shape
        if r % 8 == 0:
            return jnp.concatenate(pieces, axis=0)
        for j, piece in enumerate(pieces):
            stk_ref[slot, j * r:(j + 1) * r, 0:w] = piece
        return stk_ref[slot, 0:len(pieces) * r, 0:w]

    def unstack_rows(x, n, slot):
        r, w = x.shape[0] // n, x.shape[1]
        if r % 8 == 0:
            return [x[j * r:(j + 1) * r] for j in range(n)]
        stk_ref[slot, 0:n * r, 0:w] = x
        return [stk_ref[slot, j * r:(j + 1) * r, 0:w] for j in range(n)]

    u = seg(OFF_AC, GROUP_W) * seg(OFF_AH, GROUP_W)
    ext_ref[CONV_PAD:CONV_PAD + T, :] = u
    conv = (convw_ref[0:1, :] * ext_ref[CONV_PAD - 2:CONV_PAD - 2 + T, :]
            + convw_ref[1:2, :] * ext_ref[CONV_PAD - 1:CONV_PAD - 1 + T, :]
            + convw_ref[2:3, :] * u)
    mix_ref[:, 0:GROUP_W] = (seg(OFF_AB, GROUP_W) * conv
                             * _silu(seg(OFF_AZ, GROUP_W))).astype(mix_ref.dtype)
    conv_state = ext_ref[CONV_PAD + T - 2:CONV_PAD + T, :]
    ext_ref[CONV_PAD - 2:CONV_PAD, :] = conv_state
    conv_out_ref[...] = conv_state
    yield

    if not decode:
        def keep_state(c, state):
            if c == T // GLA_CHUNK - 1:
                s_ref[...] = state
                _store_gla_state(gla_out_ref, state)

        _gla_group(seg, T, GLA_CHUNK, (wup_ref, bga_ref, ggo_ref), mix_ref,
                   lambda c: s_ref[...], keep_state, carry=True, seg_t=seg_t)
        yield

    s_z = seg(OFF_SZ, GROUP_W)
    if decode:
        q_n, k_n = qkn_ref[:, 0:GROUP_W], qkn_ref[:, GROUP_W:GROUP_W + LANES]
        v_n = seg(OFF_SV, LANES)
    else:
        s_q = seg(OFF_SQ, GROUP_W)
        s_kv = seg(OFF_SK, 2 * LANES)
        q_n = _head_norm(s_q, gsq_ref[...], bd_ref)
        k_n = _head_norm(s_kv[:, 0:LANES], gsk_ref[...], bd_ref)
        v_n = s_kv[:, LANES:2 * LANES]
    yield

    BQ = min(WINDOW, T)
    n_blk = T // BQ
    stack = SWA_GROUP
    nk = WINDOW + BQ
    qi = lax.broadcasted_iota(jnp.int32, (stack * BQ, nk), 0) % BQ
    kj = lax.broadcasted_iota(jnp.int32, (stack * BQ, nk), 1)
    dist = qi + WINDOW - kj
    band = (dist >= 0) & (dist < WINDOW)
    srow = lax.broadcasted_iota(jnp.int32, (stack * BQ, 1), 0) // BQ
    for blk in range(n_blk):
        rs = slice(blk * BQ, (blk + 1) * BQ)
        if blk == 0:
            k_prev, v_prev = kprev_ref[...], vprev_ref[...]
            valid = band if decode else band & ((kj >= WINDOW) | (t > 0))
        else:
            ps = slice((blk - 1) * BQ, blk * BQ)
            k_prev, v_prev = k_n[ps], v_n[ps]
            valid = band
        k_cat = jnp.concatenate([k_prev, k_n[rs]], axis=0)
        v_cat = jnp.concatenate([v_prev, v_n[rs]], axis=0)
        for g in range(SWA_KV_HEADS):
            kg = k_cat[:, g * SWA_HD:(g + 1) * SWA_HD].astype(bf16)
            vg = v_cat[:, g * SWA_HD:(g + 1) * SWA_HD].astype(bf16)
            heads = [g * SWA_GROUP + j for j in range(stack)]
            qg = stack_rows([q_n[rs, hd * SWA_HD:(hd + 1) * SWA_HD] for hd in heads],
                            2 + 2 * g).astype(bf16)
            sink = jnp.full((stack * BQ, 1), sinks_ref[layer, heads[0]], f32)
            for j in range(1, stack):
                sink = jnp.where(srow == j, sinks_ref[layer, heads[j]], sink)
            s = _dot_nt(qg, kg) * (SWA_HD ** -0.5)
            s = jnp.where(valid, s, -jnp.inf)
            m = jnp.maximum(jnp.max(s, axis=-1, keepdims=True), sink)
            e = jnp.exp(s - m)
            prob = e / (jnp.sum(e, axis=-1, keepdims=True) + jnp.exp(sink - m))
            o = _dot(prob.astype(bf16), vg)
            for hd, o_hd in zip(heads, unstack_rows(o, stack, 3 + 2 * g)):
                z = s_z[rs, hd * SWA_HD:(hd + 1) * SWA_HD]
                mix_ref[rs, 2 * GROUP_W + hd * SWA_HD:2 * GROUP_W + (hd + 1) * SWA_HD] = (
                    o_hd * _silu(z)).astype(mix_ref.dtype)
            yield

    if decode:
        kbuf_ref[0:WINDOW - T, :] = kc_ref[T:WINDOW, :]
        kbuf_ref[WINDOW - T:WINDOW, :] = k_n
        vbuf_ref[0:WINDOW - T, :] = vc_ref[T:WINDOW, :]
        vbuf_ref[WINDOW - T:WINDOW, :] = v_n
    else:
        kprev_ref[...] = k_n[T - WINDOW:T]
        vprev_ref[...] = v_n[T - WINDOW:T]
        kbuf_ref[...] = k_n[T - WINDOW:T]
        vbuf_ref[...] = v_n[T - WINDOW:T]
    yield

    m_q = seg(OFF_MQ, GROUP_W)
    m_z = seg(OFF_MZ, GROUP_W)
    yield

    def mem_q(h):
        qh = m_q[:, h * MEM_HD:(h + 1) * MEM_HD]
        return qh * lax.rsqrt(jnp.mean(qh * qh, axis=-1, keepdims=True) + EPS) * gmq_ref[...]

    def softmax(s):
        e = jnp.exp(s - jnp.max(s, axis=-1, keepdims=True))
        return e / jnp.sum(e, axis=-1, keepdims=True)

    if decode:
        qs = stack_rows([mem_q(h) for h in range(MEM_HEADS)], 0).astype(bf16)
        s = _dot_nt(qs, mk_ref[...].astype(bf16)) * (MEM_HD ** -0.5)
        shape = (MEM_HEADS * T, MEM_HEADS * N_MEM)
        same_head = (lax.broadcasted_iota(jnp.int32, shape, 0) // T
                     == lax.broadcasted_iota(jnp.int32, shape, 1) % MEM_HEADS)
        yield
        prob = softmax(jnp.where(same_head, s, -jnp.inf))
        o_all = unstack_rows(_dot(prob.astype(bf16), mv_ref[...].astype(bf16)), MEM_HEADS, 1)
    else:
        o_all = []
        for h in range(MEM_HEADS):
            hs = slice(h * MEM_HD, (h + 1) * MEM_HD)
            s = _dot_nt(mem_q(h).astype(bf16), mk_ref[:, hs].astype(bf16)) * (MEM_HD ** -0.5)
            o_all.append(_dot(softmax(s).astype(bf16), mv_ref[:, hs].astype(bf16)))
    for h in range(MEM_HEADS):
        mix_ref[:, 3 * GROUP_W + h * MEM_HD:3 * GROUP_W + (h + 1) * MEM_HD] = (
            o_all[h] * _silu(m_z[:, h * MEM_HD:(h + 1) * MEM_HD])).astype(mix_ref.dtype)


def _mixer(tokens, norm_w, mem_k, mem_v, mem_layer, state, params, layer, tile, bb, decode):
    if decode:
        b, width = state[0].shape[1], tokens.shape[1]
        L = tokens.shape[0] // b
    else:
        b, L, width = tokens.shape
    nt = L // tile
    assert nt == 1 or not decode, "a decode call covers each sequence with a single tile"
    conv_w, w_up, b_ga, g_go, g_sq, g_sk, bd, sinks, g_mq = params

    def tok(width):
        if decode:
            return pl.BlockSpec((bb * tile, width), lambda i, t: (i, 0))
        return pl.BlockSpec((bb, tile, width), lambda i, t: (i, t, 0))

    def per_seq(*shape):
        return pl.BlockSpec((bb,) + shape, lambda i, t: (i,) + (0,) * len(shape))

    def per_seq_at(lyr, *shape):
        return pl.BlockSpec((None, bb) + shape, lambda i, t: (lyr, i) + (0,) * len(shape))

    def param(a):
        return pl.BlockSpec((None,) + a.shape[1:], lambda i, t: (layer,) + (0,) * (a.ndim - 1))

    kv_w = SWA_KV_HEADS * SWA_HD
    state_shapes = [(CONV_W - 1, GROUP_W), (GLA_HEADS, GLA_DK, GLA_DV), (WINDOW, kv_w),
                    (WINDOW, kv_w)]
    in_specs = [tok(width), per_seq_at(mem_layer, *mem_k.shape[2:]),
                per_seq_at(mem_layer, *mem_v.shape[2:])]
    args = [tokens, mem_k, mem_v]
    if decode:
        in_specs += [per_seq_at(layer, *s) for s in state_shapes]
        args += list(state)
    else:
        g_n, w_t = norm_w
        in_specs += [param(g_n), pl.BlockSpec(w_t.shape, lambda i, t: (0, 0),
                                              pipeline_mode=pl.Buffered(1))]
        args += [g_n, w_t]
    in_specs += [param(conv_w), param(w_up), param(b_ga), param(g_go), param(g_sq), param(g_sk),
                 pl.BlockSpec(bd.shape, lambda i, t: (0, 0)),
                 pl.BlockSpec(memory_space=pltpu.SMEM), param(g_mq)]
    args += [conv_w, w_up, b_ga, g_go, g_sq, g_sk, bd, sinks, g_mq]
    if decode:
        out_shape = [jax.ShapeDtypeStruct((b * L, 4 * GROUP_W), f32)]
        row_scratch = [pltpu.VMEM((bb * tile, D_IN - OFF_GZ), f32),
                       pltpu.VMEM((bb * tile, GROUP_W + LANES), f32)]
    else:
        out_shape = [jax.ShapeDtypeStruct((b, L, 4 * GROUP_W), bf16)]
        row_scratch = [pltpu.VMEM((bb, 8, LANES), f32)] * 2
    out_shape += [jax.ShapeDtypeStruct((b,) + s, f32) for s in state_shapes]
    out_specs = [tok(4 * GROUP_W)] + [per_seq(*s) for s in state_shapes]
    return pl.pallas_call(
        functools.partial(_mixer_kernel, tile=tile, decode=decode, layer=layer, bb=bb),
        grid=(b // bb, nt),
        in_specs=in_specs,
        out_specs=out_specs,
        out_shape=out_shape,
        scratch_shapes=[
            pltpu.VMEM((bb, CONV_PAD + tile, GROUP_W), f32),
            pltpu.VMEM((bb, 8, LANES) if decode else (bb, GLA_K_W, GLA_V_W), f32),
            pltpu.VMEM((bb, WINDOW, kv_w), f32),
            pltpu.VMEM((bb, WINDOW, kv_w), f32),
            row_scratch[0],
            pltpu.VMEM((bb, N_STACK_SLOTS, MEM_HEADS * min(tile, 8), LANES), f32),
            row_scratch[1],
        ],
        compiler_params=pltpu.CompilerParams(
            dimension_semantics=("arbitrary", "arbitrary"), vmem_limit_bytes=VMEM_LIMIT),
        name="mixer_decode" if decode else "mixer_prompt",
    )(*args)


def _out_proj_kernel(mix_ref, w_ref, x_ref, y_ref, wb_ref):
    @pl.when(pl.program_id(1) == 0)
    def _():
        wb_ref[...] = w_ref[...].astype(bf16)

    y_ref[...] = x_ref[...] + _dot(mix_ref[...].astype(bf16), wb_ref[...])


def _out_proj(mix, w, x, l, tm, tn):
    m, k = mix.shape
    n = w.shape[2]
    return pl.pallas_call(
        _out_proj_kernel,
        grid=(n // tn, m // tm),
        in_specs=[
            pl.BlockSpec((tm, k), lambda j, i: (i, 0)),
            pl.BlockSpec((None, k, tn), lambda j, i: (l, 0, j),
                         pipeline_mode=pl.Buffered(1) if tn == n else None),
            pl.BlockSpec((tm, tn), lambda j, i: (i, j)),
        ],
        out_specs=pl.BlockSpec((tm, tn), lambda j, i: (i, j)),
        out_shape=jax.ShapeDtypeStruct((m, n), f32),
        scratch_shapes=[pltpu.VMEM((k, tn), bf16)],
        compiler_params=pltpu.CompilerParams(
            dimension_semantics=("arbitrary", "arbitrary"), vmem_limit_bytes=VMEM_LIMIT),
        name="out_proj",
    )(mix, w, x)


PROMPT_TILE = 512
DECODE_SEQS_PER_STEP = 8
PROJ_TN = 1536
OUT_TM, OUT_TN = 512, 2048

_LANE = np.arange(GROUP_W)
HEAD_BLOCK_DIAG = _LANE[:, None] // SWA_HD == _LANE[None, :] // SWA_HD


def kernel(x_prompt, x_sample, mem_prompt, state_conv, state_gla, cache_swa_k, cache_swa_v,
           cache_mem_k, cache_mem_v, g_norm, w_in, conv_w, w_gla_a_up, b_gla_a, g_gla_o,
           g_swa_q, g_swa_k, swa_sinks, g_mem, w_mem_kv, g_mem_q, g_mem_k, w_out):
    depth = w_in.shape[0]
    bp, lp, _ = x_prompt.shape
    bs, ls, _ = x_sample.shape
    hp = x_prompt.reshape(bp * lp, D_MODEL)
    hs = x_sample.reshape(bs * ls, D_MODEL)

    def row(a):
        return a[:, None, :]

    params = (conv_w, w_gla_a_up, row(b_gla_a), row(g_gla_o),
              row(jnp.tile(g_swa_q, (1, SWA_HEADS))), row(jnp.tile(g_swa_k, (1, SWA_KV_HEADS))),
              jnp.asarray(HEAD_BLOCK_DIAG, bf16), swa_sinks, row(g_mem_q))
    g_n, g_m, g_mk = row(g_norm), row(g_mem), row(g_mem_k)
    w_in_t = jnp.swapaxes(w_in, 1, 2)
    kv_w = SWA_KV_HEADS * SWA_HD
    state = (state_conv, state_gla, cache_swa_k.reshape(depth, bs, WINDOW, kv_w),
             cache_swa_v.reshape(depth, bs, WINDOW, kv_w))
    mem_k_s = cache_mem_k.reshape(depth, bs, N_MEM * MEM_HEADS, MEM_HD)
    mem_v_s = cache_mem_v.reshape(depth, bs, N_MEM * MEM_HEADS, MEM_HD)

    mk, mv, mem_k_p, mem_v_p = _memory_kv(mem_prompt, g_m, w_mem_kv, g_mk)
    outs = [[] for _ in range(8)]
    for l in range(depth):
        proj, w_bf = _norm_matmul(hs, g_n, w_in_t, l, PROJ_TN)

        mix, c, s, kb, vb = _mixer(hp.reshape(bp, lp, D_MODEL), (g_n, w_bf), mk, mv, l,
                                   None, params, l, PROMPT_TILE, 1, decode=False)
        hp = _out_proj(mix.reshape(bp * lp, 4 * GROUP_W), w_out, hp, l, OUT_TM, OUT_TN)
        for lst, a in zip(outs[:4], (
                c, s, kb.reshape(bp, WINDOW, SWA_KV_HEADS, SWA_HD),
                vb.reshape(bp, WINDOW, SWA_KV_HEADS, SWA_HD))):
            lst.append(a)

        mix, c, s, kb, vb = _mixer(proj, None, mem_k_s, mem_v_s, l, state, params, l, ls,
                                   DECODE_SEQS_PER_STEP, decode=True)
        hs = _out_proj(mix, w_out, hs, l, bs * ls, OUT_TN)
        for lst, a in zip(outs[4:], (
                c, s, kb.reshape(bs, WINDOW, SWA_KV_HEADS, SWA_HD),
                vb.reshape(bs, WINDOW, SWA_KV_HEADS, SWA_HD))):
            lst.append(a)

    stacked = [jnp.stack(o) for o in outs]
    return (hp.reshape(bp, lp, D_MODEL), hs.reshape(bs, ls, D_MODEL),
            *stacked[:4], mem_k_p, mem_v_p, *stacked[4:])
```

```python
import functools
import itertools

import jax
import jax.numpy as jnp
import numpy as np
from jax import lax
from jax.experimental import pallas as pl
from jax.experimental.pallas import tpu as pltpu

f32 = jnp.float32
bf16 = jnp.bfloat16

D_MODEL = 2048
GROUP_W = 512
GLA_HEADS = 4
GLA_DK = 64
GLA_DV = 128
GLA_RANK = 16
GLA_TAU = 16.0
GLA_CHUNK = 64
SWA_HEADS = 8
SWA_KV_HEADS = 2
SWA_HD = 64
SWA_GROUP = SWA_HEADS // SWA_KV_HEADS
WINDOW = 128
N_MEM = 256
MEM_HEADS = 4
MEM_HD = 128
CONV_W = 3
EPS = 1e-6

LANES = 128
MXU_CHUNK = 256

D_IN = 5904
OFF_AB, OFF_AC, OFF_AH, OFF_AZ = 0, 512, 1024, 1536
OFF_GQ, OFF_GK, OFF_GV, OFF_GA, OFF_GZ = 2048, 2304, 2560, 3072, 3088
OFF_SQ, OFF_SK, OFF_SV, OFF_SZ = 3600, 4112, 4240, 4368
OFF_MQ, OFF_MZ = 4880, 5392

VMEM_LIMIT = 56 * 1024 * 1024


def _dot(a, b):
    return jnp.dot(a, b, preferred_element_type=f32)


def _dot_nt(a, b):
    return lax.dot_general(a, b, (((1,), (1,)), ((), ())), preferred_element_type=f32)


def _dot_tn(a, b):
    return lax.dot_general(a, b, (((0,), (0,)), ((), ())), preferred_element_type=f32)


def _split3(x):
    hi = x.astype(bf16)
    r = x - hi.astype(f32)
    mid = r.astype(bf16)
    lo = (r - mid.astype(f32)).astype(bf16)
    return hi, mid, lo


def _silu(x):
    return x * jax.nn.sigmoid(x)


def _log_sigmoid(x):
    return jnp.minimum(x, 0.0) - jnp.log1p(jnp.exp(-jnp.abs(x)))


def _norm_matmul_kernel(x_ref, g_ref, wt_ref, o_ref, wb_ref, hn_ref):
    x = x_ref[...]
    y = x * lax.rsqrt(jnp.mean(x * x, axis=-1, keepdims=True) + EPS)
    hn_ref[...] = (y * g_ref[...]).astype(bf16)
    wb_ref[...] = wt_ref[...].astype(bf16)
    o_ref[...] = _dot_nt(hn_ref[...], wb_ref[...])


def _norm_matmul(x, g, wt, l, tn):
    m, k = x.shape
    n = wt.shape[1]
    return pl.pallas_call(
        _norm_matmul_kernel,
        grid=(pl.cdiv(n, tn),),
        in_specs=[
            pl.BlockSpec((m, k), lambda j: (0, 0)),
            pl.BlockSpec((None, 1, k), lambda j: (l, 0, 0)),
            pl.BlockSpec((None, tn, k), lambda j: (l, j, 0)),
        ],
        out_specs=[pl.BlockSpec((m, tn), lambda j: (0, j)), pl.BlockSpec((tn, k), lambda j: (j, 0))],
        out_shape=[jax.ShapeDtypeStruct((m, n), f32), jax.ShapeDtypeStruct((n, k), bf16)],
        scratch_shapes=[pltpu.VMEM((m, k), bf16)],
        compiler_params=pltpu.CompilerParams(
            dimension_semantics=("arbitrary",), vmem_limit_bytes=VMEM_LIMIT),
        name="norm_in_proj",
    )(x, g, wt)


def _memory_kv_kernel(x_ref, g_ref, w_ref, gk_ref, k_ref, v_ref, k4_ref, v4_ref, wb_ref):
    @pl.when(pl.program_id(1) == 0)
    def _():
        wb_ref[...] = w_ref[...].astype(bf16)

    x = x_ref[...]
    y = x * lax.rsqrt(jnp.mean(x * x, axis=-1, keepdims=True) + EPS)
    kv = _dot((y * g_ref[...]).astype(bf16), wb_ref[...])
    for h in range(MEM_HEADS):
        kh = kv[:, h * MEM_HD:(h + 1) * MEM_HD]
        kh = kh * lax.rsqrt(jnp.mean(kh * kh, axis=-1, keepdims=True) + EPS) * gk_ref[...]
        vh = kv[:, GROUP_W + h * MEM_HD:GROUP_W + (h + 1) * MEM_HD]
        k_ref[:, h * MEM_HD:(h + 1) * MEM_HD] = kh
        k4_ref[:, h, :] = kh
        v4_ref[:, h, :] = vh
    v_ref[...] = kv[:, GROUP_W:]


def _memory_kv(mem, g, w, gk):
    depth, b = w.shape[0], mem.shape[0]
    flat = jax.ShapeDtypeStruct((depth, b, N_MEM, GROUP_W), f32)
    split = jax.ShapeDtypeStruct((depth, b, N_MEM, MEM_HEADS, MEM_HD), f32)
    return pl.pallas_call(
        _memory_kv_kernel,
        grid=(depth, b),
        in_specs=[
            pl.BlockSpec((None, N_MEM, D_MODEL), lambda l, i: (i, 0, 0)),
            pl.BlockSpec((None, 1, D_MODEL), lambda l, i: (l, 0, 0)),
            pl.BlockSpec((None, D_MODEL, 2 * GROUP_W), lambda l, i: (l, 0, 0)),
            pl.BlockSpec((None, 1, MEM_HD), lambda l, i: (l, 0, 0)),
        ],
        out_specs=[pl.BlockSpec((None, None, N_MEM, GROUP_W), lambda l, i: (l, i, 0, 0))] * 2
        + [pl.BlockSpec((None, None, N_MEM, MEM_HEADS, MEM_HD), lambda l, i: (l, i, 0, 0, 0))] * 2,
        out_shape=[flat, flat, split, split],
        scratch_shapes=[pltpu.VMEM((D_MODEL, 2 * GROUP_W), bf16)],
        compiler_params=pltpu.CompilerParams(
            dimension_semantics=("arbitrary", "arbitrary"), vmem_limit_bytes=VMEM_LIMIT),
        name="memory_kv",
    )(mem, g, w, gk)


CONV_PAD = 8
N_STACK_SLOTS = 2
N_SEQ_IN_PROMPT, N_SEQ_IN_DECODE, N_PARAMS_PROMPT, N_PARAMS_DECODE, N_OUT = 3, 7, 11, 9, 5


def _mixer_kernel(*refs, tile, decode, layer, bb):
    n_seq = N_SEQ_IN_DECODE if decode else N_SEQ_IN_PROMPT
    n_par = N_PARAMS_DECODE if decode else N_PARAMS_PROMPT
    seq_in = refs[:n_seq]
    params = refs[n_seq:n_seq + n_par]
    outs = refs[n_seq + n_par:n_seq + n_par + N_OUT]
    scratch = refs[n_seq + n_par + N_OUT:]

    def view(ref, s):
        if decode and ref.ndim == 2:
            return _RowWindow(ref, s * tile, tile)
        return ref.at[s]

    if decode:
        p_ref, tail_ref, qkn_ref = seq_in[0], scratch[4], scratch[6]
        bd_ref, gsq_ref, gsk_ref = params[-3], params[-5], params[-4]
        tail_ref[...] = p_ref[:, OFF_GZ:D_IN]
        sq0, sk0 = OFF_SQ - OFF_GZ, OFF_SK - OFF_GZ
        qkn_ref[:, 0:GROUP_W] = _head_norm(tail_ref[:, sq0:sq0 + GROUP_W], gsq_ref[...], bd_ref)
        qkn_ref[:, GROUP_W:GROUP_W + LANES] = _head_norm(tail_ref[:, sk0:sk0 + LANES], gsk_ref[...],
                                                         bd_ref)

        def seg_rows(off, width):
            if off < OFF_GZ:
                return p_ref[:, off:off + width]
            return tail_ref[:, off - OFF_GZ:off - OFF_GZ + width]

        gla_in_ref, gla_out_ref = seq_in[4], outs[2]
        _gla_group(seg_rows, bb * tile, tile, params[-8:-5], outs[0],
                   lambda c: _gla_block_diag(gla_in_ref.at[c]),
                   lambda c, state: _store_gla_state(gla_out_ref.at[c], state), carry=False)
        _swa_decode_rows(seg_rows, qkn_ref, seq_in[5], seq_in[6], params[-2], layer, outs[0], bb, tile)

    stages = [_mixer_seq([view(r, s) for r in seq_in], params, [view(r, s) for r in outs],
                         [view(r, s) for r in scratch], tile=tile, decode=decode, layer=layer)
              for s in range(bb)]
    for _ in itertools.zip_longest(*stages):
        pass


class _RowWindow:
    def __init__(self, ref, start, size):
        self.ref, self.start, self.size, self.dtype = ref, start, size, ref.dtype

    def _index(self, idx):
        rows, cols = (slice(None), slice(None)) if idx is Ellipsis else idx
        lo, hi, _ = rows.indices(self.size)
        return slice(self.start + lo, self.start + hi), cols

    def __getitem__(self, idx):
        return self.ref[self._index(idx)]

    def __setitem__(self, idx, value):
        self.ref[self._index(idx)] = value


GLA_INTRA_ROWS = MXU_CHUNK
GLA_K_W = GLA_HEADS * GLA_DK
GLA_V_W = GLA_HEADS * GLA_DV


def _gla_block_diag(state_ref):
    rows = []
    for h in range(GLA_HEADS):
        blocks = [state_ref[h] if j == h else jnp.zeros((GLA_DK, GLA_DV), f32)
                  for j in range(GLA_HEADS)]
        rows.append(jnp.concatenate(blocks, axis=1))
    return jnp.concatenate(rows, axis=0)


def _store_gla_state(state_ref, state):
    for h in range(GLA_HEADS):
        state_ref[h] = state[h * GLA_DK:(h + 1) * GLA_DK, h * GLA_DV:(h + 1) * GLA_DV]


def _gla_group(seg, T, C, params, mix_ref, state_in, state_out, carry, seg_t=None):
    wup_ref, bga_ref, ggo_ref = params
    n_chunk = T // C
    G = min(T, GLA_INTRA_ROWS)
    groups = [slice(i * G, (i + 1) * G) for i in range(T // G)]
    row = lax.broadcasted_iota(jnp.int32, (G, G), 0)
    col = lax.broadcasted_iota(jnp.int32, (G, G), 1)
    causal = (row // C == col // C) & (row >= col)
    if seg_t is None:
        a_up = _dot(seg(OFF_GA, GLA_RANK).astype(bf16), wup_ref[...].astype(bf16))
    else:
        a_up = _dot_tn(seg_t(OFF_GA, GLA_RANK).astype(bf16), wup_ref[...].astype(bf16))
    log_a = _log_sigmoid(a_up + bga_ref[...]) * (1.0 / GLA_TAU)
    la3 = _split3(log_a)
    tril = jnp.where(causal, 1.0, 0.0).astype(bf16)
    in_chunk = jnp.where(lax.broadcasted_iota(jnp.int32, (T, LANES), 0) // C
                         == lax.broadcasted_iota(jnp.int32, (T, LANES), 1), 1.0, 0.0).astype(bf16)
    cum = jnp.concatenate(
        [_dot(tril, la3[0][r]) + _dot(tril, la3[1][r]) + _dot(tril, la3[2][r]) for r in groups],
        axis=0)
    tot_t = (_dot_tn(la3[0], in_chunk) + _dot_tn(la3[1], in_chunk)
             + _dot_tn(la3[2], in_chunk))
    decay_t = jnp.exp(tot_t)
    g_k = seg(OFF_GK, GLA_K_W)
    qd = ((seg(OFF_GQ, GLA_K_W) * (GLA_DK ** -0.5)) * jnp.exp(cum)).astype(bf16)
    kd = (g_k * jnp.exp(-cum)).astype(bf16)
    k_tail = jnp.concatenate(
        [g_k[c * C:(c + 1) * C] * jnp.exp(cum[(c + 1) * C - 1:(c + 1) * C] - cum[c * C:(c + 1) * C])
         for c in range(n_chunk)], axis=0) if n_chunk > 1 else g_k * jnp.exp(cum[T - 1:T] - cum)
    kt = k_tail.astype(bf16)
    v_b = seg(OFF_GV, GLA_V_W).astype(bf16)
    g_z = seg(OFF_GZ, GROUP_W)

    o_intra = []
    for r in groups:
        o_heads = []
        for h in range(GLA_HEADS):
            ks = slice(h * GLA_DK, (h + 1) * GLA_DK)
            attn = jnp.where(causal, _dot_nt(qd[r, ks], kd[r, ks]), 0.0).astype(bf16)
            o_heads.append(_dot(attn, v_b[r, h * GLA_DV:(h + 1) * GLA_DV]))
        o_intra.append(jnp.concatenate(o_heads, axis=1))
    o_intra = jnp.concatenate(o_intra, axis=0)

    shape = (GLA_K_W, GLA_V_W)
    on_diag = (lax.broadcasted_iota(jnp.int32, shape, 0) // GLA_DK
               == lax.broadcasted_iota(jnp.int32, shape, 1) // GLA_DV)
    o_chunks = []
    state = None
    for c in range(n_chunk):
        rs = slice(c * C, (c + 1) * C)
        if c == 0 or not carry:
            state = state_in(c)
        o_chunks.append(o_intra[rs] + _dot(qd[rs], state.astype(bf16)))
        update = jnp.where(on_diag, _dot_tn(kt[rs], v_b[rs]), 0.0)
        state = decay_t[:, c:c + 1] * state + update
        state_out(c, state)
    o = jnp.concatenate(o_chunks, axis=0) if n_chunk > 1 else o_chunks[0]
    for h in range(GLA_HEADS):
        vs = slice(h * GLA_DV, (h + 1) * GLA_DV)
        o_h = o[:, vs]
        o_h = o_h * lax.rsqrt(jnp.mean(o_h * o_h, axis=-1, keepdims=True) + EPS) * ggo_ref[...]
        mix_ref[:, GROUP_W + h * GLA_DV:GROUP_W + (h + 1) * GLA_DV] = (
            o_h * _silu(g_z[:, vs])).astype(mix_ref.dtype)


def _head_norm(x, g, bd_ref):
    rows, n_lanes = x.shape
    w = min(n_lanes, MXU_CHUNK)
    pieces = n_lanes // w
    bd = bd_ref[0:w, 0:w]

    def head_sums(v):
        stacked = jnp.concatenate([v[:, i * w:(i + 1) * w] for i in range(pieces)], axis=0)
        r = _dot(stacked, bd)
        return jnp.concatenate([r[i * rows:(i + 1) * rows] for i in range(pieces)], axis=1)

    sq = x * x
    hi = sq.astype(bf16)
    lo = (sq - hi.astype(f32)).astype(bf16)
    ms = (head_sums(hi) + head_sums(lo)) * (1.0 / SWA_HD)
    return x * lax.rsqrt(ms + EPS) * g


def _mixer_seq(seq_in, params, outs, scratch, *, tile, decode, layer):
    if decode:
        p_ref, mk_ref, mv_ref, conv_in_ref, gla_in_ref, kc_ref, vc_ref = seq_in
    else:
        x_ref, mk_ref, mv_ref = seq_in
        gn_ref, wt_ref = params[:2]
    (convw_ref, wup_ref, bga_ref, ggo_ref, gsq_ref, gsk_ref, bd_ref, sinks_ref,
     gmq_ref) = params[-N_PARAMS_DECODE:]
    mix_ref, conv_out_ref, gla_out_ref, kbuf_ref, vbuf_ref = outs
    ext_ref, s_ref, kprev_ref, vprev_ref, tail_ref, stk_ref, qkn_ref = scratch

    T = tile
    t = pl.program_id(1)

    def init_state():
        ext_ref[0:CONV_PAD, :] = jnp.zeros((CONV_PAD, GROUP_W), f32)
        if decode:
            ext_ref[CONV_PAD - (CONV_W - 1):CONV_PAD, :] = conv_in_ref[...]
        else:
            s_ref[...] = jnp.zeros_like(s_ref)
            kprev_ref[...] = jnp.zeros_like(kprev_ref)
            vprev_ref[...] = jnp.zeros_like(vprev_ref)

    if decode:
        init_state()
    else:
        pl.when(t == 0)(init_state)
    yield

    if decode:
        def seg(off, width):
            if off < OFF_GZ:
                return p_ref[:, off:off + width]
            return tail_ref[:, off - OFF_GZ:off - OFF_GZ + width]
    else:
        x = x_ref[...]
        hn = (x * lax.rsqrt(jnp.mean(x * x, axis=-1, keepdims=True) + EPS) * gn_ref[...]).astype(bf16)

        def seg(off, width):
            return _dot_nt(hn, wt_ref[off:off + width, :])

        def seg_t(off, width):
            return _dot_nt(wt_ref[off:off + width, :], hn)

    u = seg(OFF_AC, GROUP_W) * seg(OFF_AH, GROUP_W)
    ext_ref[CONV_PAD:CONV_PAD + T, :] = u
    conv = (convw_ref[0:1, :] * ext_ref[CONV_PAD - 2:CONV_PAD - 2 + T, :]
            + convw_ref[1:2, :] * ext_ref[CONV_PAD - 1:CONV_PAD - 1 + T, :]
            + convw_ref[2:3, :] * u)
    mix_ref[:, 0:GROUP_W] = (seg(OFF_AB, GROUP_W) * conv
                             * _silu(seg(OFF_AZ, GROUP_W))).astype(mix_ref.dtype)
    conv_state = ext_ref[CONV_PAD + T - 2:CONV_PAD + T, :]
    ext_ref[CONV_PAD - 2:CONV_PAD, :] = conv_state
    conv_out_ref[...] = conv_state
    yield

    if not decode:
        def keep_state(c, state):
            if c == T // GLA_CHUNK - 1:
                s_ref[...] = state
                _store_gla_state(gla_out_ref, state)

        _gla_group(seg, T, GLA_CHUNK, (wup_ref, bga_ref, ggo_ref), mix_ref,
                   lambda c: s_ref[...], keep_state, carry=True, seg_t=seg_t)
        yield

    if decode:
        kbuf_ref[0:WINDOW - T, :] = kc_ref[T:WINDOW, :]
        kbuf_ref[WINDOW - T:WINDOW, :] = qkn_ref[:, GROUP_W:GROUP_W + LANES]
        vbuf_ref[0:WINDOW - T, :] = vc_ref[T:WINDOW, :]
        vbuf_ref[WINDOW - T:WINDOW, :] = seg(OFF_SV, LANES)
        yield
    else:
        yield from _swa_prompt_tile(seg, T, t, layer, (gsq_ref, gsk_ref, bd_ref, sinks_ref), mix_ref,
                                    kprev_ref, vprev_ref, kbuf_ref, vbuf_ref)

    m_q = seg(OFF_MQ, GROUP_W)
    m_z = seg(OFF_MZ, GROUP_W)
    yield
    yield from _memory_attention(m_q, m_z, T, decode, gmq_ref, mk_ref, mv_ref, mix_ref, stk_ref)


def _swa_prompt_tile(seg, T, t, layer, params, mix_ref, kprev_ref, vprev_ref, kbuf_ref, vbuf_ref):
    gsq_ref, gsk_ref, bd_ref, sinks_ref = params
    s_z = seg(OFF_SZ, GROUP_W)
    s_q = seg(OFF_SQ, GROUP_W)
    s_kv = seg(OFF_SK, 2 * LANES)
    q_n = _head_norm(s_q, gsq_ref[...], bd_ref)
    k_n = _head_norm(s_kv[:, 0:LANES], gsk_ref[...], bd_ref)
    v_n = s_kv[:, LANES:2 * LANES]
    yield

    BQ = WINDOW
    n_blk = T // BQ
    stack = SWA_GROUP
    nk = WINDOW + BQ
    qi = lax.broadcasted_iota(jnp.int32, (stack * BQ, nk), 0) % BQ
    kj = lax.broadcasted_iota(jnp.int32, (stack * BQ, nk), 1)
    dist = qi + WINDOW - kj
    band = (dist >= 0) & (dist < WINDOW)
    srow = lax.broadcasted_iota(jnp.int32, (stack * BQ, 1), 0) // BQ
    for blk in range(n_blk):
        rs = slice(blk * BQ, (blk + 1) * BQ)
        if blk == 0:
            k_prev, v_prev = kprev_ref[...], vprev_ref[...]
            valid = band & ((kj >= WINDOW) | (t > 0))
        else:
            ps = slice((blk - 1) * BQ, blk * BQ)
            k_prev, v_prev = k_n[ps], v_n[ps]
            valid = band
        k_cat = jnp.concatenate([k_prev, k_n[rs]], axis=0)
        v_cat = jnp.concatenate([v_prev, v_n[rs]], axis=0)
        for g in range(SWA_KV_HEADS):
            kg = k_cat[:, g * SWA_HD:(g + 1) * SWA_HD].astype(bf16)
            vg = v_cat[:, g * SWA_HD:(g + 1) * SWA_HD].astype(bf16)
            heads = [g * SWA_GROUP + j for j in range(stack)]
            qg = jnp.concatenate([q_n[rs, hd * SWA_HD:(hd + 1) * SWA_HD] for hd in heads],
                                 axis=0).astype(bf16)
            sink = jnp.full((stack * BQ, 1), sinks_ref[layer, heads[0]], f32)
            for j in range(1, stack):
                sink = jnp.where(srow == j, sinks_ref[layer, heads[j]], sink)
            s = _dot_nt(qg, kg) * (SWA_HD ** -0.5)
            s = jnp.where(valid, s, -jnp.inf)
            m = jnp.maximum(jnp.max(s, axis=-1, keepdims=True), sink)
            e = jnp.exp(s - m)
            prob = e / (jnp.sum(e, axis=-1, keepdims=True) + jnp.exp(sink - m))
            o = _dot(prob.astype(bf16), vg)
            for j, hd in enumerate(heads):
                z = s_z[rs, hd * SWA_HD:(hd + 1) * SWA_HD]
                mix_ref[rs, 2 * GROUP_W + hd * SWA_HD:2 * GROUP_W + (hd + 1) * SWA_HD] = (
                    o[j * BQ:(j + 1) * BQ] * _silu(z)).astype(mix_ref.dtype)
            yield

    kprev_ref[...] = k_n[T - WINDOW:T]
    vprev_ref[...] = v_n[T - WINDOW:T]
    kbuf_ref[...] = k_n[T - WINDOW:T]
    vbuf_ref[...] = v_n[T - WINDOW:T]
    yield


def _swa_decode_rows(seg, qkn_ref, kc_ref, vc_ref, sinks_ref, layer, mix_ref, bb, T):
    R = bb * T
    n_cache = bb * WINDOW
    nk = n_cache + R
    q_n, k_n = qkn_ref[:, 0:GROUP_W], qkn_ref[:, GROUP_W:GROUP_W + LANES]
    s_z = seg(OFF_SZ, GROUP_W)
    k_all = jnp.concatenate([kc_ref[s] for s in range(bb)] + [k_n], axis=0)
    v_all = jnp.concatenate([vc_ref[s] for s in range(bb)] + [seg(OFF_SV, LANES)], axis=0)
    rows = SWA_GROUP * R
    r = lax.broadcasted_iota(jnp.int32, (rows, nk), 0) % R
    c = lax.broadcasted_iota(jnp.int32, (rows, nk), 1)
    cached = c < n_cache
    key_seq = jnp.where(cached, c // WINDOW, (c - n_cache) // T)
    key_pos = jnp.where(cached, c % WINDOW, WINDOW + (c - n_cache) % T)
    dist = r % T + WINDOW - key_pos
    valid = (r // T == key_seq) & (dist >= 0) & (dist < WINDOW)
    srow = lax.broadcasted_iota(jnp.int32, (rows, 1), 0) // R
    for g in range(SWA_KV_HEADS):
        heads = [g * SWA_GROUP + j for j in range(SWA_GROUP)]
        qg = jnp.concatenate([q_n[:, hd * SWA_HD:(hd + 1) * SWA_HD] for hd in heads],
                             axis=0).astype(bf16)
        kg = k_all[:, g * SWA_HD:(g + 1) * SWA_HD].astype(bf16)
        vg = v_all[:, g * SWA_HD:(g + 1) * SWA_HD].astype(bf16)
        sink = jnp.full((rows, 1), sinks_ref[layer, heads[0]], f32)
        for j in range(1, SWA_GROUP):
            sink = jnp.where(srow == j, sinks_ref[layer, heads[j]], sink)
        s = _dot_nt(qg, kg) * (SWA_HD ** -0.5)
        s = jnp.where(valid, s, -jnp.inf)
        m = jnp.maximum(jnp.max(s, axis=-1, keepdims=True), sink)
        e = jnp.exp(s - m)
        prob = e / (jnp.sum(e, axis=-1, keepdims=True) + jnp.exp(sink - m))
        o = _dot(prob.astype(bf16), vg)
        for j, hd in enumerate(heads):
            mix_ref[:, 2 * GROUP_W + hd * SWA_HD:2 * GROUP_W + (hd + 1) * SWA_HD] = (
                o[j * R:(j + 1) * R] * _silu(s_z[:, hd * SWA_HD:(hd + 1) * SWA_HD])).astype(mix_ref.dtype)


def _memory_attention(m_q, m_z, T, decode, gmq_ref, mk_ref, mv_ref, mix_ref, stk_ref):
    def stack_rows(pieces, slot):
        r, w = pieces[0].shape
        if r % 8 == 0:
            return jnp.concatenate(pieces, axis=0)
        for j, piece in enumerate(pieces):
            stk_ref[slot, j * r:(j + 1) * r, 0:w] = piece
        return stk_ref[slot, 0:len(pieces) * r, 0:w]

    def unstack_rows(x, n, slot):
        r, w = x.shape[0] // n, x.shape[1]
        if r % 8 == 0:
            return [x[j * r:(j + 1) * r] for j in range(n)]
        stk_ref[slot, 0:n * r, 0:w] = x
        return [stk_ref[slot, j * r:(j + 1) * r, 0:w] for j in range(n)]

    def mem_q(h):
        qh = m_q[:, h * MEM_HD:(h + 1) * MEM_HD]
        return qh * lax.rsqrt(jnp.mean(qh * qh, axis=-1, keepdims=True) + EPS) * gmq_ref[...]

    def softmax(s):
        e = jnp.exp(s - jnp.max(s, axis=-1, keepdims=True))
        return e / jnp.sum(e, axis=-1, keepdims=True)

    if decode:
        qs = stack_rows([mem_q(h) for h in range(MEM_HEADS)], 0).astype(bf16)
        s = _dot_nt(qs, mk_ref[...].astype(bf16)) * (MEM_HD ** -0.5)
        shape = (MEM_HEADS * T, MEM_HEADS * N_MEM)
        same_head = (lax.broadcasted_iota(jnp.int32, shape, 0) // T
                     == lax.broadcasted_iota(jnp.int32, shape, 1) % MEM_HEADS)
        yield
        prob = softmax(jnp.where(same_head, s, -jnp.inf))
        o_all = unstack_rows(_dot(prob.astype(bf16), mv_ref[...].astype(bf16)), MEM_HEADS, 1)
    else:
        o_all = []
        for h in range(MEM_HEADS):
            hs = slice(h * MEM_HD, (h + 1) * MEM_HD)
            s = _dot_nt(mem_q(h).astype(bf16), mk_ref[:, hs].astype(bf16)) * (MEM_HD ** -0.5)
            o_all.append(_dot(softmax(s).astype(bf16), mv_ref[:, hs].astype(bf16)))
    for h in range(MEM_HEADS):
        mix_ref[:, 3 * GROUP_W + h * MEM_HD:3 * GROUP_W + (h + 1) * MEM_HD] = (
            o_all[h] * _silu(m_z[:, h * MEM_HD:(h + 1) * MEM_HD])).astype(mix_ref.dtype)
    yield


def _mixer(tokens, norm_w, mem_k, mem_v, mem_layer, state, params, layer, tile, bb, decode):
    if decode:
        b, width = state[0].shape[1], tokens.shape[1]
        L = tokens.shape[0] // b
    else:
        b, L, width = tokens.shape
    nt = L // tile
    assert nt == 1 or not decode, "a decode call covers each sequence with a single tile"
    conv_w, w_up, b_ga, g_go, g_sq, g_sk, bd, sinks, g_mq = params

    def tok(width):
        if decode:
            return pl.BlockSpec((bb * tile, width), lambda i, t: (i, 0))
        return pl.BlockSpec((bb, tile, width), lambda i, t: (i, t, 0))

    def per_seq(*shape):
        return pl.BlockSpec((bb,) + shape, lambda i, t: (i,) + (0,) * len(shape))

    def per_seq_at(lyr, *shape):
        return pl.BlockSpec((None, bb) + shape, lambda i, t: (lyr, i) + (0,) * len(shape))

    def param(a):
        return pl.BlockSpec((None,) + a.shape[1:], lambda i, t: (layer,) + (0,) * (a.ndim - 1))

    kv_w = SWA_KV_HEADS * SWA_HD
    state_shapes = [(CONV_W - 1, GROUP_W), (GLA_HEADS, GLA_DK, GLA_DV), (WINDOW, kv_w),
                    (WINDOW, kv_w)]
    in_specs = [tok(width), per_seq_at(mem_layer, *mem_k.shape[2:]),
                per_seq_at(mem_layer, *mem_v.shape[2:])]
    args = [tokens, mem_k, mem_v]
    if decode:
        in_specs += [per_seq_at(layer, *s) for s in state_shapes]
        args += list(state)
    else:
        g_n, w_t = norm_w
        in_specs += [param(g_n), pl.BlockSpec(w_t.shape, lambda i, t: (0, 0),
                                              pipeline_mode=pl.Buffered(1))]
        args += [g_n, w_t]
    in_specs += [param(conv_w), param(w_up), param(b_ga), param(g_go), param(g_sq), param(g_sk),
                 pl.BlockSpec(bd.shape, lambda i, t: (0, 0)),
                 pl.BlockSpec(memory_space=pltpu.SMEM), param(g_mq)]
    args += [conv_w, w_up, b_ga, g_go, g_sq, g_sk, bd, sinks, g_mq]
    if decode:
        out_shape = [jax.ShapeDtypeStruct((b * L, 4 * GROUP_W), f32)]
        row_scratch = [pltpu.VMEM((bb * tile, D_IN - OFF_GZ), f32),
                       pltpu.VMEM((bb * tile, GROUP_W + LANES), f32)]
    else:
        out_shape = [jax.ShapeDtypeStruct((b, L, 4 * GROUP_W), bf16)]
        row_scratch = [pltpu.VMEM((bb, 8, LANES), f32)] * 2
    out_shape += [jax.ShapeDtypeStruct((b,) + s, f32) for s in state_shapes]
    out_specs = [tok(4 * GROUP_W)] + [per_seq(*s) for s in state_shapes]
    return pl.pallas_call(
        functools.partial(_mixer_kernel, tile=tile, decode=decode, layer=layer, bb=bb),
        grid=(b // bb, nt),
        in_specs=in_specs,
        out_specs=out_specs,
        out_shape=out_shape,
        scratch_shapes=[
            pltpu.VMEM((bb, CONV_PAD + tile, GROUP_W), f32),
            pltpu.VMEM((bb, 8, LANES) if decode else (bb, GLA_K_W, GLA_V_W), f32),
            pltpu.VMEM((bb, WINDOW, kv_w), f32),
            pltpu.VMEM((bb, WINDOW, kv_w), f32),
            row_scratch[0],
            pltpu.VMEM((bb, N_STACK_SLOTS, MEM_HEADS * min(tile, 8), LANES), f32),
            row_scratch[1],
        ],
        compiler_params=pltpu.CompilerParams(
            dimension_semantics=("arbitrary", "arbitrary"), vmem_limit_bytes=VMEM_LIMIT),
        name="mixer_decode" if decode else "mixer_prompt",
    )(*args)


def _out_proj_kernel(mix_ref, w_ref, x_ref, y_ref, wb_ref):
    @pl.when(pl.program_id(1) == 0)
    def _():
        wb_ref[...] = w_ref[...].astype(bf16)

    y_ref[...] = x_ref[...] + _dot(mix_ref[...].astype(bf16), wb_ref[...])


def _out_proj(mix, w, x, l, tm, tn):
    m, k = mix.shape
    n = w.shape[2]
    return pl.pallas_call(
        _out_proj_kernel,
        grid=(n // tn, m // tm),
        in_specs=[
            pl.BlockSpec((tm, k), lambda j, i: (i, 0)),
            pl.BlockSpec((None, k, tn), lambda j, i: (l, 0, j),
                         pipeline_mode=pl.Buffered(1) if tn == n else None),
            pl.BlockSpec((tm, tn), lambda j, i: (i, j)),
        ],
        out_specs=pl.BlockSpec((tm, tn), lambda j, i: (i, j)),
        out_shape=jax.ShapeDtypeStruct((m, n), f32),
        scratch_shapes=[pltpu.VMEM((k, tn), bf16)],
        compiler_params=pltpu.CompilerParams(
            dimension_semantics=("arbitrary", "arbitrary"), vmem_limit_bytes=VMEM_LIMIT),
        name="out_proj",
    )(mix, w, x)


PROMPT_TILE = 512
DECODE_SEQS_PER_STEP = 8
PROJ_TN = 1536
OUT_TM, OUT_TN = 512, 2048

_LANE = np.arange(GROUP_W)
HEAD_BLOCK_DIAG = _LANE[:, None] // SWA_HD == _LANE[None, :] // SWA_HD


def kernel(x_prompt, x_sample, mem_prompt, state_conv, state_gla, cache_swa_k, cache_swa_v,
           cache_mem_k, cache_mem_v, g_norm, w_in, conv_w, w_gla_a_up, b_gla_a, g_gla_o,
           g_swa_q, g_swa_k, swa_sinks, g_mem, w_mem_kv, g_mem_q, g_mem_k, w_out):
    depth = w_in.shape[0]
    bp, lp, _ = x_prompt.shape
    bs, ls, _ = x_sample.shape
    hp = x_prompt.reshape(bp * lp, D_MODEL)
    hs = x_sample.reshape(bs * ls, D_MODEL)

    def row(a):
        return a[:, None, :]

    params = (conv_w, w_gla_a_up, row(b_gla_a), row(g_gla_o),
              row(jnp.tile(g_swa_q, (1, SWA_HEADS))), row(jnp.tile(g_swa_k, (1, SWA_KV_HEADS))),
              jnp.asarray(HEAD_BLOCK_DIAG, bf16), swa_sinks, row(g_mem_q))
    g_n, g_m, g_mk = row(g_norm), row(g_mem), row(g_mem_k)
    w_in_t = jnp.swapaxes(w_in, 1, 2)
    kv_w = SWA_KV_HEADS * SWA_HD
    state = (state_conv, state_gla, cache_swa_k.reshape(depth, bs, WINDOW, kv_w),
             cache_swa_v.reshape(depth, bs, WINDOW, kv_w))
    mem_k_s = cache_mem_k.reshape(depth, bs, N_MEM * MEM_HEADS, MEM_HD)
    mem_v_s = cache_mem_v.reshape(depth, bs, N_MEM * MEM_HEADS, MEM_HD)

    mk, mv, mem_k_p, mem_v_p = _memory_kv(mem_prompt, g_m, w_mem_kv, g_mk)
    outs = [[] for _ in range(8)]
    for l in range(depth):
        proj, w_bf = _norm_matmul(hs, g_n, w_in_t, l, PROJ_TN)

        mix, c, s, kb, vb = _mixer(hp.reshape(bp, lp, D_MODEL), (g_n, w_bf), mk, mv, l,
                                   None, params, l, PROMPT_TILE, 1, decode=False)
        hp = _out_proj(mix.reshape(bp * lp, 4 * GROUP_W), w_out, hp, l, OUT_TM, OUT_TN)
        for lst, a in zip(outs[:4], (
                c, s, kb.reshape(bp, WINDOW, SWA_KV_HEADS, SWA_HD),
                vb.reshape(bp, WINDOW, SWA_KV_HEADS, SWA_HD))):
            lst.append(a)

        mix, c, s, kb, vb = _mixer(proj, None, mem_k_s, mem_v_s, l, state, params, l, ls,
                                   DECODE_SEQS_PER_STEP, decode=True)
        hs = _out_proj(mix, w_out, hs, l, bs * ls, OUT_TN)
        for lst, a in zip(outs[4:], (
                c, s, kb.reshape(bs, WINDOW, SWA_KV_HEADS, SWA_HD),
                vb.reshape(bs, WINDOW, SWA_KV_HEADS, SWA_HD))):
            lst.append(a)

    stacked = [jnp.stack(o) for o in outs]
    return (hp.reshape(bp, lp, D_MODEL), hs.reshape(bs, ls, D_MODEL),
            *stacked[:4], mem_k_p, mem_v_p, *stacked[4:])
```

```python
import functools
import itertools

import jax
import jax.numpy as jnp
import numpy as np
from jax import lax
from jax.experimental import pallas as pl
from jax.experimental.pallas import tpu as pltpu

f32 = jnp.float32
bf16 = jnp.bfloat16

D_MODEL = 2048
GROUP_W = 512
GLA_HEADS = 4
GLA_DK = 64
GLA_DV = 128
GLA_RANK = 16
GLA_TAU = 16.0
GLA_CHUNK = 64
SWA_HEADS = 8
SWA_KV_HEADS = 2
SWA_HD = 64
SWA_GROUP = SWA_HEADS // SWA_KV_HEADS
WINDOW = 128
N_MEM = 256
MEM_HEADS = 4
MEM_HD = 128
CONV_W = 3
EPS = 1e-6

LANES = 128
MXU_CHUNK = 256

D_IN = 5904
OFF_AB, OFF_AC, OFF_AH, OFF_AZ = 0, 512, 1024, 1536
OFF_GQ, OFF_GK, OFF_GV, OFF_GA, OFF_GZ = 2048, 2304, 2560, 3072, 3088
OFF_SQ, OFF_SK, OFF_SV, OFF_SZ = 3600, 4112, 4240, 4368
OFF_MQ, OFF_MZ = 4880, 5392

VMEM_LIMIT = 56 * 1024 * 1024


def _dot(a, b):
    return jnp.dot(a, b, preferred_element_type=f32)


def _dot_nt(a, b):
    return lax.dot_general(a, b, (((1,), (1,)), ((), ())), preferred_element_type=f32)


def _dot_tn(a, b):
    return lax.dot_general(a, b, (((0,), (0,)), ((), ())), preferred_element_type=f32)


def _split3(x):
    hi = x.astype(bf16)
    r = x - hi.astype(f32)
    mid = r.astype(bf16)
    lo = (r - mid.astype(f32)).astype(bf16)
    return hi, mid, lo


LOG2_E = 1.4426950408889634


def _attend(scores, valid, scale, values, sink=None):
    if valid is not None:
        scores = jnp.where(valid, scores, -jnp.inf)
    m = jnp.max(scores, axis=-1, keepdims=True)
    if sink is not None:
        sink = sink * (1.0 / scale)
        m = jnp.maximum(m, sink)
    e = jnp.exp2((scores - m) * (scale * LOG2_E))
    denom = jnp.sum(e, axis=-1, keepdims=True)
    if sink is not None:
        denom = denom + jnp.exp2((sink - m) * (scale * LOG2_E))
    return _dot(e.astype(bf16), values) / denom


def _silu(x):
    return x * jax.nn.sigmoid(x)


def _log_sigmoid(x):
    return jnp.minimum(x, 0.0) - jnp.log1p(jnp.exp(-jnp.abs(x)))


def _norm_matmul_kernel(x_ref, g_ref, wt_ref, o_ref, wb_ref, hn_ref):
    x = x_ref[...]
    y = x * lax.rsqrt(jnp.mean(x * x, axis=-1, keepdims=True) + EPS)
    hn_ref[...] = (y * g_ref[...]).astype(bf16)
    wb_ref[...] = wt_ref[...].astype(bf16)
    o_ref[...] = _dot_nt(hn_ref[...], wb_ref[...])


def _norm_matmul(x, g, wt, l, tn):
    m, k = x.shape
    n = wt.shape[1]
    return pl.pallas_call(
        _norm_matmul_kernel,
        grid=(pl.cdiv(n, tn),),
        in_specs=[
            pl.BlockSpec((m, k), lambda j: (0, 0)),
            pl.BlockSpec((None, 1, k), lambda j: (l, 0, 0)),
            pl.BlockSpec((None, tn, k), lambda j: (l, j, 0)),
        ],
        out_specs=[pl.BlockSpec((m, tn), lambda j: (0, j)), pl.BlockSpec((tn, k), lambda j: (j, 0))],
        out_shape=[jax.ShapeDtypeStruct((m, n), f32), jax.ShapeDtypeStruct((n, k), bf16)],
        scratch_shapes=[pltpu.VMEM((m, k), bf16)],
        compiler_params=pltpu.CompilerParams(
            dimension_semantics=("arbitrary",), vmem_limit_bytes=VMEM_LIMIT),
        name="norm_in_proj",
    )(x, g, wt)


def _memory_kv_kernel(x_ref, g_ref, w_ref, gk_ref, k_ref, v_ref, k4_ref, v4_ref, wb_ref):
    @pl.when(pl.program_id(1) == 0)
    def _():
        wb_ref[...] = w_ref[...].astype(bf16)

    x = x_ref[...]
    y = x * lax.rsqrt(jnp.mean(x * x, axis=-1, keepdims=True) + EPS)
    kv = _dot((y * g_ref[...]).astype(bf16), wb_ref[...])
    for h in range(MEM_HEADS):
        kh = kv[:, h * MEM_HD:(h + 1) * MEM_HD]
        kh = kh * lax.rsqrt(jnp.mean(kh * kh, axis=-1, keepdims=True) + EPS) * gk_ref[...]
        vh = kv[:, GROUP_W + h * MEM_HD:GROUP_W + (h + 1) * MEM_HD]
        k_ref[:, h * MEM_HD:(h + 1) * MEM_HD] = kh
        k4_ref[:, h, :] = kh
        v4_ref[:, h, :] = vh
    v_ref[...] = kv[:, GROUP_W:]


def _memory_kv(mem, g, w, gk):
    depth, b = w.shape[0], mem.shape[0]
    flat = jax.ShapeDtypeStruct((depth, b, N_MEM, GROUP_W), f32)
    split = jax.ShapeDtypeStruct((depth, b, N_MEM, MEM_HEADS, MEM_HD), f32)
    return pl.pallas_call(
        _memory_kv_kernel,
        grid=(depth, b),
        in_specs=[
            pl.BlockSpec((None, N_MEM, D_MODEL), lambda l, i: (i, 0, 0)),
            pl.BlockSpec((None, 1, D_MODEL), lambda l, i: (l, 0, 0)),
            pl.BlockSpec((None, D_MODEL, 2 * GROUP_W), lambda l, i: (l, 0, 0)),
            pl.BlockSpec((None, 1, MEM_HD), lambda l, i: (l, 0, 0)),
        ],
        out_specs=[pl.BlockSpec((None, None, N_MEM, GROUP_W), lambda l, i: (l, i, 0, 0))] * 2
        + [pl.BlockSpec((None, None, N_MEM, MEM_HEADS, MEM_HD), lambda l, i: (l, i, 0, 0, 0))] * 2,
        out_shape=[flat, flat, split, split],
        scratch_shapes=[pltpu.VMEM((D_MODEL, 2 * GROUP_W), bf16)],
        compiler_params=pltpu.CompilerParams(
            dimension_semantics=("arbitrary", "arbitrary"), vmem_limit_bytes=VMEM_LIMIT),
        name="memory_kv",
    )(mem, g, w, gk)


CONV_PAD = 8
N_STACK_SLOTS = 2
N_SEQ_IN_PROMPT, N_SEQ_IN_DECODE, N_PARAMS_PROMPT, N_PARAMS_DECODE, N_OUT = 3, 7, 11, 9, 5


def _mixer_kernel(*refs, tile, decode, layer, bb):
    n_seq = N_SEQ_IN_DECODE if decode else N_SEQ_IN_PROMPT
    n_par = N_PARAMS_DECODE if decode else N_PARAMS_PROMPT
    seq_in = refs[:n_seq]
    params = refs[n_seq:n_seq + n_par]
    outs = refs[n_seq + n_par:n_seq + n_par + N_OUT]
    scratch = refs[n_seq + n_par + N_OUT:]

    def view(ref, s):
        if decode and ref.ndim == 2:
            return _RowWindow(ref, s * tile, tile)
        return ref.at[s]

    if decode:
        p_ref, tail_ref, qkn_ref = seq_in[0], scratch[4], scratch[6]
        bd_ref, gsq_ref, gsk_ref = params[-3], params[-5], params[-4]
        tail_ref[...] = p_ref[:, OFF_GZ:D_IN]
        sq0, sk0 = OFF_SQ - OFF_GZ, OFF_SK - OFF_GZ
        qkn_ref[:, 0:GROUP_W] = _head_norm(tail_ref[:, sq0:sq0 + GROUP_W], gsq_ref[...], bd_ref)
        qkn_ref[:, GROUP_W:GROUP_W + LANES] = _head_norm(tail_ref[:, sk0:sk0 + LANES], gsk_ref[...],
                                                         bd_ref)

        def seg_rows(off, width):
            if off < OFF_GZ:
                return p_ref[:, off:off + width]
            return tail_ref[:, off - OFF_GZ:off - OFF_GZ + width]

        gla_in_ref, gla_out_ref = seq_in[4], outs[2]
        _gla_group(seg_rows, bb * tile, tile, params[-8:-5], outs[0],
                   lambda c: _gla_block_diag(gla_in_ref.at[c]),
                   lambda c, state: _store_gla_state(gla_out_ref.at[c], state), carry=False)
        _swa_decode_rows(seg_rows, qkn_ref, seq_in[5], seq_in[6], params[-2], layer, outs[0], bb, tile)

    stages = [_mixer_seq([view(r, s) for r in seq_in], params, [view(r, s) for r in outs],
                         [view(r, s) for r in scratch], tile=tile, decode=decode, layer=layer)
              for s in range(bb)]
    for _ in itertools.zip_longest(*stages):
        pass


class _RowWindow:
    def __init__(self, ref, start, size):
        self.ref, self.start, self.size, self.dtype = ref, start, size, ref.dtype

    def _index(self, idx):
        rows, cols = (slice(None), slice(None)) if idx is Ellipsis else idx
        lo, hi, _ = rows.indices(self.size)
        return slice(self.start + lo, self.start + hi), cols

    def __getitem__(self, idx):
        return self.ref[self._index(idx)]

    def __setitem__(self, idx, value):
        self.ref[self._index(idx)] = value


GLA_INTRA_ROWS = MXU_CHUNK
GLA_K_W = GLA_HEADS * GLA_DK
GLA_V_W = GLA_HEADS * GLA_DV


def _gla_block_diag(state_ref):
    rows = []
    for h in range(GLA_HEADS):
        blocks = [state_ref[h] if j == h else jnp.zeros((GLA_DK, GLA_DV), f32)
                  for j in range(GLA_HEADS)]
        rows.append(jnp.concatenate(blocks, axis=1))
    return jnp.concatenate(rows, axis=0)


def _store_gla_state(state_ref, state):
    for h in range(GLA_HEADS):
        state_ref[h] = state[h * GLA_DK:(h + 1) * GLA_DK, h * GLA_DV:(h + 1) * GLA_DV]


def _gla_group(seg, T, C, params, mix_ref, state_in, state_out, carry, seg_t=None):
    wup_ref, bga_ref, ggo_ref = params
    n_chunk = T // C
    G = min(T, GLA_INTRA_ROWS)
    groups = [slice(i * G, (i + 1) * G) for i in range(T // G)]
    row = lax.broadcasted_iota(jnp.int32, (G, G), 0)
    col = lax.broadcasted_iota(jnp.int32, (G, G), 1)
    causal = (row // C == col // C) & (row >= col)
    if seg_t is None:
        a_up = _dot(seg(OFF_GA, GLA_RANK).astype(bf16), wup_ref[...].astype(bf16))
    else:
        a_up = _dot_tn(seg_t(OFF_GA, GLA_RANK).astype(bf16), wup_ref[...].astype(bf16))
    log_a = _log_sigmoid(a_up + bga_ref[...]) * (1.0 / GLA_TAU)
    la3 = _split3(log_a)
    tril = jnp.where(causal, 1.0, 0.0).astype(bf16)
    in_chunk = jnp.where(lax.broadcasted_iota(jnp.int32, (T, LANES), 0) // C
                         == lax.broadcasted_iota(jnp.int32, (T, LANES), 1), 1.0, 0.0).astype(bf16)
    cum = jnp.concatenate(
        [_dot(tril, la3[0][r]) + _dot(tril, la3[1][r]) + _dot(tril, la3[2][r]) for r in groups],
        axis=0)
    tot_t = (_dot_tn(la3[0], in_chunk) + _dot_tn(la3[1], in_chunk)
             + _dot_tn(la3[2], in_chunk))
    decay_t = jnp.exp(tot_t)
    g_k = seg(OFF_GK, GLA_K_W)
    qd = ((seg(OFF_GQ, GLA_K_W) * (GLA_DK ** -0.5)) * jnp.exp(cum)).astype(bf16)
    kd = (g_k * jnp.exp(-cum)).astype(bf16)
    k_tail = jnp.concatenate(
        [g_k[c * C:(c + 1) * C] * jnp.exp(cum[(c + 1) * C - 1:(c + 1) * C] - cum[c * C:(c + 1) * C])
         for c in range(n_chunk)], axis=0) if n_chunk > 1 else g_k * jnp.exp(cum[T - 1:T] - cum)
    kt = k_tail.astype(bf16)
    v_b = seg(OFF_GV, GLA_V_W).astype(bf16)
    g_z = seg(OFF_GZ, GROUP_W)

    o_intra = []
    for r in groups:
        o_heads = []
        for h in range(GLA_HEADS):
            ks = slice(h * GLA_DK, (h + 1) * GLA_DK)
            attn = jnp.where(causal, _dot_nt(qd[r, ks], kd[r, ks]), 0.0).astype(bf16)
            o_heads.append(_dot(attn, v_b[r, h * GLA_DV:(h + 1) * GLA_DV]))
        o_intra.append(jnp.concatenate(o_heads, axis=1))
    o_intra = jnp.concatenate(o_intra, axis=0)

    shape = (GLA_K_W, GLA_V_W)
    on_diag = (lax.broadcasted_iota(jnp.int32, shape, 0) // GLA_DK
               == lax.broadcasted_iota(jnp.int32, shape, 1) // GLA_DV)
    o_chunks = []
    state = None
    for c in range(n_chunk):
        rs = slice(c * C, (c + 1) * C)
        if c == 0 or not carry:
            state = state_in(c)
        o_chunks.append(o_intra[rs] + _dot(qd[rs], state.astype(bf16)))
        update = jnp.where(on_diag, _dot_tn(kt[rs], v_b[rs]), 0.0)
        state = decay_t[:, c:c + 1] * state + update
        state_out(c, state)
    o = jnp.concatenate(o_chunks, axis=0) if n_chunk > 1 else o_chunks[0]
    for h in range(GLA_HEADS):
        vs = slice(h * GLA_DV, (h + 1) * GLA_DV)
        o_h = o[:, vs]
        o_h = o_h * lax.rsqrt(jnp.mean(o_h * o_h, axis=-1, keepdims=True) + EPS) * ggo_ref[...]
        mix_ref[:, GROUP_W + h * GLA_DV:GROUP_W + (h + 1) * GLA_DV] = (
            o_h * _silu(g_z[:, vs])).astype(mix_ref.dtype)


def _head_norm(x, g, bd_ref):
    rows, n_lanes = x.shape
    w = min(n_lanes, MXU_CHUNK)
    pieces = n_lanes // w
    bd = bd_ref[0:w, 0:w]

    def head_sums(v):
        stacked = jnp.concatenate([v[:, i * w:(i + 1) * w] for i in range(pieces)], axis=0)
        r = _dot(stacked, bd)
        return jnp.concatenate([r[i * rows:(i + 1) * rows] for i in range(pieces)], axis=1)

    sq = x * x
    hi = sq.astype(bf16)
    lo = (sq - hi.astype(f32)).astype(bf16)
    ms = (head_sums(hi) + head_sums(lo)) * (1.0 / SWA_HD)
    return x * lax.rsqrt(ms + EPS) * g


def _mixer_seq(seq_in, params, outs, scratch, *, tile, decode, layer):
    if decode:
        p_ref, mk_ref, mv_ref, conv_in_ref, gla_in_ref, kc_ref, vc_ref = seq_in
    else:
        x_ref, mk_ref, mv_ref = seq_in
        gn_ref, wt_ref = params[:2]
    (convw_ref, wup_ref, bga_ref, ggo_ref, gsq_ref, gsk_ref, bd_ref, sinks_ref,
     gmq_ref) = params[-N_PARAMS_DECODE:]
    mix_ref, conv_out_ref, gla_out_ref, kbuf_ref, vbuf_ref = outs
    ext_ref, s_ref, kprev_ref, vprev_ref, tail_ref, stk_ref, qkn_ref = scratch

    T = tile
    t = pl.program_id(1)

    def init_state():
        ext_ref[0:CONV_PAD, :] = jnp.zeros((CONV_PAD, GROUP_W), f32)
        if decode:
            ext_ref[CONV_PAD - (CONV_W - 1):CONV_PAD, :] = conv_in_ref[...]
        else:
            s_ref[...] = jnp.zeros_like(s_ref)
            kprev_ref[...] = jnp.zeros_like(kprev_ref)
            vprev_ref[...] = jnp.zeros_like(vprev_ref)

    if decode:
        init_state()
    else:
        pl.when(t == 0)(init_state)
    yield

    if decode:
        def seg(off, width):
            if off < OFF_GZ:
                return p_ref[:, off:off + width]
            return tail_ref[:, off - OFF_GZ:off - OFF_GZ + width]
    else:
        x = x_ref[...]
        hn = (x * lax.rsqrt(jnp.mean(x * x, axis=-1, keepdims=True) + EPS) * gn_ref[...]).astype(bf16)

        def seg(off, width):
            return _dot_nt(hn, wt_ref[off:off + width, :])

        def seg_t(off, width):
            return _dot_nt(wt_ref[off:off + width, :], hn)

    u = seg(OFF_AC, GROUP_W) * seg(OFF_AH, GROUP_W)
    ext_ref[CONV_PAD:CONV_PAD + T, :] = u
    conv = (convw_ref[0:1, :] * ext_ref[CONV_PAD - 2:CONV_PAD - 2 + T, :]
            + convw_ref[1:2, :] * ext_ref[CONV_PAD - 1:CONV_PAD - 1 + T, :]
            + convw_ref[2:3, :] * u)
    mix_ref[:, 0:GROUP_W] = (seg(OFF_AB, GROUP_W) * conv
                             * _silu(seg(OFF_AZ, GROUP_W))).astype(mix_ref.dtype)
    conv_state = ext_ref[CONV_PAD + T - 2:CONV_PAD + T, :]
    ext_ref[CONV_PAD - 2:CONV_PAD, :] = conv_state
    conv_out_ref[...] = conv_state
    yield

    if not decode:
        def keep_state(c, state):
            if c == T // GLA_CHUNK - 1:
                s_ref[...] = state
                _store_gla_state(gla_out_ref, state)

        _gla_group(seg, T, GLA_CHUNK, (wup_ref, bga_ref, ggo_ref), mix_ref,
                   lambda c: s_ref[...], keep_state, carry=True, seg_t=seg_t)
        yield

    if decode:
        kbuf_ref[0:WINDOW - T, :] = kc_ref[T:WINDOW, :]
        kbuf_ref[WINDOW - T:WINDOW, :] = qkn_ref[:, GROUP_W:GROUP_W + LANES]
        vbuf_ref[0:WINDOW - T, :] = vc_ref[T:WINDOW, :]
        vbuf_ref[WINDOW - T:WINDOW, :] = seg(OFF_SV, LANES)
        yield
    else:
        yield from _swa_prompt_tile(seg, T, t, layer, (gsq_ref, gsk_ref, bd_ref, sinks_ref), mix_ref,
                                    kprev_ref, vprev_ref, kbuf_ref, vbuf_ref)

    m_q = seg(OFF_MQ, GROUP_W)
    m_z = seg(OFF_MZ, GROUP_W)
    yield
    yield from _memory_attention(m_q, m_z, T, decode, gmq_ref, mk_ref, mv_ref, mix_ref, stk_ref)


def _swa_prompt_tile(seg, T, t, layer, params, mix_ref, kprev_ref, vprev_ref, kbuf_ref, vbuf_ref):
    gsq_ref, gsk_ref, bd_ref, sinks_ref = params
    s_z = seg(OFF_SZ, GROUP_W)
    s_q = seg(OFF_SQ, GROUP_W)
    s_kv = seg(OFF_SK, 2 * LANES)
    q_n = _head_norm(s_q, gsq_ref[...], bd_ref)
    k_n = _head_norm(s_kv[:, 0:LANES], gsk_ref[...], bd_ref)
    v_n = s_kv[:, LANES:2 * LANES]
    yield

    BQ = WINDOW
    n_blk = T // BQ
    stack = SWA_GROUP
    nk = WINDOW + BQ
    qi = lax.broadcasted_iota(jnp.int32, (stack * BQ, nk), 0) % BQ
    kj = lax.broadcasted_iota(jnp.int32, (stack * BQ, nk), 1)
    dist = qi + WINDOW - kj
    band = (dist >= 0) & (dist < WINDOW)
    srow = lax.broadcasted_iota(jnp.int32, (stack * BQ, 1), 0) // BQ
    for blk in range(n_blk):
        rs = slice(blk * BQ, (blk + 1) * BQ)
        if blk == 0:
            k_prev, v_prev = kprev_ref[...], vprev_ref[...]
            valid = band & ((kj >= WINDOW) | (t > 0))
        else:
            ps = slice((blk - 1) * BQ, blk * BQ)
            k_prev, v_prev = k_n[ps], v_n[ps]
            valid = band
        k_cat = jnp.concatenate([k_prev, k_n[rs]], axis=0)
        v_cat = jnp.concatenate([v_prev, v_n[rs]], axis=0)
        for g in range(SWA_KV_HEADS):
            kg = k_cat[:, g * SWA_HD:(g + 1) * SWA_HD].astype(bf16)
            vg = v_cat[:, g * SWA_HD:(g + 1) * SWA_HD].astype(bf16)
            heads = [g * SWA_GROUP + j for j in range(stack)]
            qg = jnp.concatenate([q_n[rs, hd * SWA_HD:(hd + 1) * SWA_HD] for hd in heads],
                                 axis=0).astype(bf16)
            sink = jnp.full((stack * BQ, 1), sinks_ref[layer, heads[0]], f32)
            for j in range(1, stack):
                sink = jnp.where(srow == j, sinks_ref[layer, heads[j]], sink)
            o = _attend(_dot_nt(qg, kg), valid, SWA_HD ** -0.5, vg, sink)
            for j, hd in enumerate(heads):
                z = s_z[rs, hd * SWA_HD:(hd + 1) * SWA_HD]
                mix_ref[rs, 2 * GROUP_W + hd * SWA_HD:2 * GROUP_W + (hd + 1) * SWA_HD] = (
                    o[j * BQ:(j + 1) * BQ] * _silu(z)).astype(mix_ref.dtype)
            yield

    kprev_ref[...] = k_n[T - WINDOW:T]
    vprev_ref[...] = v_n[T - WINDOW:T]
    kbuf_ref[...] = k_n[T - WINDOW:T]
    vbuf_ref[...] = v_n[T - WINDOW:T]
    yield


def _swa_decode_rows(seg, qkn_ref, kc_ref, vc_ref, sinks_ref, layer, mix_ref, bb, T):
    R = bb * T
    n_cache = bb * WINDOW
    nk = n_cache + R
    q_n, k_n = qkn_ref[:, 0:GROUP_W], qkn_ref[:, GROUP_W:GROUP_W + LANES]
    s_z = seg(OFF_SZ, GROUP_W)
    k_all = jnp.concatenate([kc_ref[s] for s in range(bb)] + [k_n], axis=0)
    v_all = jnp.concatenate([vc_ref[s] for s in range(bb)] + [seg(OFF_SV, LANES)], axis=0)
    rows = SWA_GROUP * R
    r = lax.broadcasted_iota(jnp.int32, (rows, nk), 0) % R
    c = lax.broadcasted_iota(jnp.int32, (rows, nk), 1)
    cached = c < n_cache
    key_seq = jnp.where(cached, c // WINDOW, (c - n_cache) // T)
    key_pos = jnp.where(cached, c % WINDOW, WINDOW + (c - n_cache) % T)
    dist = r % T + WINDOW - key_pos
    valid = (r // T == key_seq) & (dist >= 0) & (dist < WINDOW)
    srow = lax.broadcasted_iota(jnp.int32, (rows, 1), 0) // R
    for g in range(SWA_KV_HEADS):
        heads = [g * SWA_GROUP + j for j in range(SWA_GROUP)]
        qg = jnp.concatenate([q_n[:, hd * SWA_HD:(hd + 1) * SWA_HD] for hd in heads],
                             axis=0).astype(bf16)
        kg = k_all[:, g * SWA_HD:(g + 1) * SWA_HD].astype(bf16)
        vg = v_all[:, g * SWA_HD:(g + 1) * SWA_HD].astype(bf16)
        sink = jnp.full((rows, 1), sinks_ref[layer, heads[0]], f32)
        for j in range(1, SWA_GROUP):
            sink = jnp.where(srow == j, sinks_ref[layer, heads[j]], sink)
        o = _attend(_dot_nt(qg, kg), valid, SWA_HD ** -0.5, vg, sink)
        for j, hd in enumerate(heads):
            mix_ref[:, 2 * GROUP_W + hd * SWA_HD:2 * GROUP_W + (hd + 1) * SWA_HD] = (
                o[j * R:(j + 1) * R] * _silu(s_z[:, hd * SWA_HD:(hd + 1) * SWA_HD])).astype(mix_ref.dtype)


def _memory_attention(m_q, m_z, T, decode, gmq_ref, mk_ref, mv_ref, mix_ref, stk_ref):
    def stack_rows(pieces, slot):
        r, w = pieces[0].shape
        if r % 8 == 0:
            return jnp.concatenate(pieces, axis=0)
        for j, piece in enumerate(pieces):
            stk_ref[slot, j * r:(j + 1) * r, 0:w] = piece
        return stk_ref[slot, 0:len(pieces) * r, 0:w]

    def unstack_rows(x, n, slot):
        r, w = x.shape[0] // n, x.shape[1]
        if r % 8 == 0:
            return [x[j * r:(j + 1) * r] for j in range(n)]
        stk_ref[slot, 0:n * r, 0:w] = x
        return [stk_ref[slot, j * r:(j + 1) * r, 0:w] for j in range(n)]

    def mem_q(h):
        qh = m_q[:, h * MEM_HD:(h + 1) * MEM_HD]
        return qh * lax.rsqrt(jnp.mean(qh * qh, axis=-1, keepdims=True) + EPS) * gmq_ref[...]

    if decode:
        qs = stack_rows([mem_q(h) for h in range(MEM_HEADS)], 0).astype(bf16)
        s = _dot_nt(qs, mk_ref[...].astype(bf16))
        shape = (MEM_HEADS * T, MEM_HEADS * N_MEM)
        same_head = (lax.broadcasted_iota(jnp.int32, shape, 0) // T
                     == lax.broadcasted_iota(jnp.int32, shape, 1) % MEM_HEADS)
        yield
        o = _attend(s, same_head, MEM_HD ** -0.5, mv_ref[...].astype(bf16))
        o_all = unstack_rows(o, MEM_HEADS, 1)
    else:
        o_all = []
        for h in range(MEM_HEADS):
            hs = slice(h * MEM_HD, (h + 1) * MEM_HD)
            s = _dot_nt(mem_q(h).astype(bf16), mk_ref[:, hs].astype(bf16))
            o_all.append(_attend(s, None, MEM_HD ** -0.5, mv_ref[:, hs].astype(bf16)))
    for h in range(MEM_HEADS):
        mix_ref[:, 3 * GROUP_W + h * MEM_HD:3 * GROUP_W + (h + 1) * MEM_HD] = (
            o_all[h] * _silu(m_z[:, h * MEM_HD:(h + 1) * MEM_HD])).astype(mix_ref.dtype)
    yield


def _mixer(tokens, norm_w, mem_k, mem_v, mem_layer, state, params, layer, tile, bb, decode):
    if decode:
        b, width = state[0].shape[1], tokens.shape[1]
        L = tokens.shape[0] // b
    else:
        b, L, width = tokens.shape
    nt = L // tile
    assert nt == 1 or not decode, "a decode call covers each sequence with a single tile"
    conv_w, w_up, b_ga, g_go, g_sq, g_sk, bd, sinks, g_mq = params

    def tok(width):
        if decode:
            return pl.BlockSpec((bb * tile, width), lambda i, t: (i, 0))
        return pl.BlockSpec((bb, tile, width), lambda i, t: (i, t, 0))

    def per_seq(*shape):
        return pl.BlockSpec((bb,) + shape, lambda i, t: (i,) + (0,) * len(shape))

    def per_seq_at(lyr, *shape):
        return pl.BlockSpec((None, bb) + shape, lambda i, t: (lyr, i) + (0,) * len(shape))

    def param(a):
        return pl.BlockSpec((None,) + a.shape[1:], lambda i, t: (layer,) + (0,) * (a.ndim - 1))

    kv_w = SWA_KV_HEADS * SWA_HD
    state_shapes = [(CONV_W - 1, GROUP_W), (GLA_HEADS, GLA_DK, GLA_DV), (WINDOW, kv_w),
                    (WINDOW, kv_w)]
    in_specs = [tok(width), per_seq_at(mem_layer, *mem_k.shape[2:]),
                per_seq_at(mem_layer, *mem_v.shape[2:])]
    args = [tokens, mem_k, mem_v]
    if decode:
        in_specs += [per_seq_at(layer, *s) for s in state_shapes]
        args += list(state)
    else:
        g_n, w_t = norm_w
        in_specs += [param(g_n), pl.BlockSpec(w_t.shape, lambda i, t: (0, 0),
                                              pipeline_mode=pl.Buffered(1))]
        args += [g_n, w_t]
    in_specs += [param(conv_w), param(w_up), param(b_ga), param(g_go), param(g_sq), param(g_sk),
                 pl.BlockSpec(bd.shape, lambda i, t: (0, 0)),
                 pl.BlockSpec(memory_space=pltpu.SMEM), param(g_mq)]
    args += [conv_w, w_up, b_ga, g_go, g_sq, g_sk, bd, sinks, g_mq]
    if decode:
        out_shape = [jax.ShapeDtypeStruct((b * L, 4 * GROUP_W), f32)]
        row_scratch = [pltpu.VMEM((bb * tile, D_IN - OFF_GZ), f32),
                       pltpu.VMEM((bb * tile, GROUP_W + LANES), f32)]
    else:
        out_shape = [jax.ShapeDtypeStruct((b, L, 4 * GROUP_W), bf16)]
        row_scratch = [pltpu.VMEM((bb, 8, LANES), f32)] * 2
    out_shape += [jax.ShapeDtypeStruct((b,) + s, f32) for s in state_shapes]
    out_specs = [tok(4 * GROUP_W)] + [per_seq(*s) for s in state_shapes]
    return pl.pallas_call(
        functools.partial(_mixer_kernel, tile=tile, decode=decode, layer=layer, bb=bb),
        grid=(b // bb, nt),
        in_specs=in_specs,
        out_specs=out_specs,
        out_shape=out_shape,
        scratch_shapes=[
            pltpu.VMEM((bb, CONV_PAD + tile, GROUP_W), f32),
            pltpu.VMEM((bb, 8, LANES) if decode else (bb, GLA_K_W, GLA_V_W), f32),
            pltpu.VMEM((bb, WINDOW, kv_w), f32),
            pltpu.VMEM((bb, WINDOW, kv_w), f32),
            row_scratch[0],
            pltpu.VMEM((bb, N_STACK_SLOTS, MEM_HEADS * min(tile, 8), LANES), f32),
            row_scratch[1],
        ],
        compiler_params=pltpu.CompilerParams(
            dimension_semantics=("arbitrary", "arbitrary"), vmem_limit_bytes=VMEM_LIMIT),
        name="mixer_decode" if decode else "mixer_prompt",
    )(*args)


def _out_proj_kernel(mix_ref, w_ref, x_ref, y_ref, wb_ref):
    @pl.when(pl.program_id(1) == 0)
    def _():
        wb_ref[...] = w_ref[...].astype(bf16)

    y_ref[...] = x_ref[...] + _dot(mix_ref[...].astype(bf16), wb_ref[...])


def _out_proj(mix, w, x, l, tm, tn):
    m, k = mix.shape
    n = w.shape[2]
    return pl.pallas_call(
        _out_proj_kernel,
        grid=(n // tn, m // tm),
        in_specs=[
            pl.BlockSpec((tm, k), lambda j, i: (i, 0)),
            pl.BlockSpec((None, k, tn), lambda j, i: (l, 0, j),
                         pipeline_mode=pl.Buffered(1) if tn == n else None),
            pl.BlockSpec((tm, tn), lambda j, i: (i, j)),
        ],
        out_specs=pl.BlockSpec((tm, tn), lambda j, i: (i, j)),
        out_shape=jax.ShapeDtypeStruct((m, n), f32),
        scratch_shapes=[pltpu.VMEM((k, tn), bf16)],
        compiler_params=pltpu.CompilerParams(
            dimension_semantics=("arbitrary", "arbitrary"), vmem_limit_bytes=VMEM_LIMIT),
        name="out_proj",
    )(mix, w, x)


PROMPT_TILE = 512
DECODE_SEQS_PER_STEP = 8
PROJ_TN = 1536
OUT_TM, OUT_TN = 512, 2048

_LANE = np.arange(GROUP_W)
HEAD_BLOCK_DIAG = _LANE[:, None] // SWA_HD == _LANE[None, :] // SWA_HD


def kernel(x_prompt, x_sample, mem_prompt, state_conv, state_gla, cache_swa_k, cache_swa_v,
           cache_mem_k, cache_mem_v, g_norm, w_in, conv_w, w_gla_a_up, b_gla_a, g_gla_o,
           g_swa_q, g_swa_k, swa_sinks, g_mem, w_mem_kv, g_mem_q, g_mem_k, w_out):
    depth = w_in.shape[0]
    bp, lp, _ = x_prompt.shape
    bs, ls, _ = x_sample.shape
    hp = x_prompt.reshape(bp * lp, D_MODEL)
    hs = x_sample.reshape(bs * ls, D_MODEL)

    def row(a):
        return a[:, None, :]

    params = (conv_w, w_gla_a_up, row(b_gla_a), row(g_gla_o),
              row(jnp.tile(g_swa_q, (1, SWA_HEADS))), row(jnp.tile(g_swa_k, (1, SWA_KV_HEADS))),
              jnp.asarray(HEAD_BLOCK_DIAG, bf16), swa_sinks, row(g_mem_q))
    g_n, g_m, g_mk = row(g_norm), row(g_mem), row(g_mem_k)
    w_in_t = jnp.swapaxes(w_in, 1, 2)
    kv_w = SWA_KV_HEADS * SWA_HD
    state = (state_conv, state_gla, cache_swa_k.reshape(depth, bs, WINDOW, kv_w),
             cache_swa_v.reshape(depth, bs, WINDOW, kv_w))
    mem_k_s = cache_mem_k.reshape(depth, bs, N_MEM * MEM_HEADS, MEM_HD)
    mem_v_s = cache_mem_v.reshape(depth, bs, N_MEM * MEM_HEADS, MEM_HD)

    mk, mv, mem_k_p, mem_v_p = _memory_kv(mem_prompt, g_m, w_mem_kv, g_mk)
    outs = [[] for _ in range(8)]
    for l in range(depth):
        proj, w_bf = _norm_matmul(hs, g_n, w_in_t, l, PROJ_TN)

        mix, c, s, kb, vb = _mixer(hp.reshape(bp, lp, D_MODEL), (g_n, w_bf), mk, mv, l,
                                   None, params, l, PROMPT_TILE, 1, decode=False)
        hp = _out_proj(mix.reshape(bp * lp, 4 * GROUP_W), w_out, hp, l, OUT_TM, OUT_TN)
        for lst, a in zip(outs[:4], (
                c, s, kb.reshape(bp, WINDOW, SWA_KV_HEADS, SWA_HD),
                vb.reshape(bp, WINDOW, SWA_KV_HEADS, SWA_HD))):
            lst.append(a)

        mix, c, s, kb, vb = _mixer(proj, None, mem_k_s, mem_v_s, l, state, params, l, ls,
                                   DECODE_SEQS_PER_STEP, decode=True)
        hs = _out_proj(mix, w_out, hs, l, bs * ls, OUT_TN)
        for lst, a in zip(outs[4:], (
                c, s, kb.reshape(bs, WINDOW, SWA_KV_HEADS, SWA_HD),
                vb.reshape(bs, WINDOW, SWA_KV_HEADS, SWA_HD))):
            lst.append(a)

    stacked = [jnp.stack(o) for o in outs]
    return (hp.reshape(bp, lp, D_MODEL), hs.reshape(bs, ls, D_MODEL),
            *stacked[:4], mem_k_p, mem_v_p, *stacked[4:])
```

```python
import functools
import itertools

import jax
import jax.numpy as jnp
import numpy as np
from jax import lax
from jax.experimental import pallas as pl
from jax.experimental.pallas import tpu as pltpu

f32 = jnp.float32
bf16 = jnp.bfloat16

D_MODEL = 2048
GROUP_W = 512
GLA_HEADS = 4
GLA_DK = 64
GLA_DV = 128
GLA_RANK = 16
GLA_TAU = 16.0
GLA_CHUNK = 64
SWA_HEADS = 8
SWA_KV_HEADS = 2
SWA_HD = 64
SWA_GROUP = SWA_HEADS // SWA_KV_HEADS
WINDOW = 128
N_MEM = 256
MEM_HEADS = 4
MEM_HD = 128
CONV_W = 3
EPS = 1e-6

LANES = 128
MXU_CHUNK = 256

D_IN = 5904
OFF_AB, OFF_AC, OFF_AH, OFF_AZ = 0, 512, 1024, 1536
OFF_GQ, OFF_GK, OFF_GV, OFF_GA, OFF_GZ = 2048, 2304, 2560, 3072, 3088
OFF_SQ, OFF_SK, OFF_SV, OFF_SZ = 3600, 4112, 4240, 4368
OFF_MQ, OFF_MZ = 4880, 5392

VMEM_LIMIT = 56 * 1024 * 1024


def _dot(a, b):
    return jnp.dot(a, b, preferred_element_type=f32)


def _dot_nt(a, b):
    return lax.dot_general(a, b, (((1,), (1,)), ((), ())), preferred_element_type=f32)


def _dot_tn(a, b):
    return lax.dot_general(a, b, (((0,), (0,)), ((), ())), preferred_element_type=f32)


def _split3(x):
    hi = x.astype(bf16)
    r = x - hi.astype(f32)
    mid = r.astype(bf16)
    lo = (r - mid.astype(f32)).astype(bf16)
    return hi, mid, lo


LOG2_E = 1.4426950408889634


def _attend(scores, valid, scale, values, sink=None):
    if valid is not None:
        scores = jnp.where(valid, scores, -jnp.inf)
    m = jnp.max(scores, axis=-1, keepdims=True)
    if sink is not None:
        sink = sink * (1.0 / scale)
        m = jnp.maximum(m, sink)
    e = jnp.exp2((scores - m) * (scale * LOG2_E))
    denom = jnp.sum(e, axis=-1, keepdims=True)
    if sink is not None:
        denom = denom + jnp.exp2((sink - m) * (scale * LOG2_E))
    return _dot(e.astype(bf16), values) / denom


def _silu(x):
    return x * jax.nn.sigmoid(x)


def _log_sigmoid(x):
    return jnp.minimum(x, 0.0) - jnp.log1p(jnp.exp(-jnp.abs(x)))


def _norm_matmul_kernel(x_ref, g_ref, wt_ref, o_ref, wb_ref, hn_ref):
    x = x_ref[...]
    y = x * lax.rsqrt(jnp.mean(x * x, axis=-1, keepdims=True) + EPS)
    hn_ref[...] = (y * g_ref[...]).astype(bf16)
    wb_ref[...] = wt_ref[...].astype(bf16)
    o_ref[...] = _dot_nt(hn_ref[...], wb_ref[...])


def _norm_matmul(x, g, wt, l, tn):
    m, k = x.shape
    n = wt.shape[1]
    return pl.pallas_call(
        _norm_matmul_kernel,
        grid=(pl.cdiv(n, tn),),
        in_specs=[
            pl.BlockSpec((m, k), lambda j: (0, 0)),
            pl.BlockSpec((None, 1, k), lambda j: (l, 0, 0)),
            pl.BlockSpec((None, tn, k), lambda j: (l, j, 0)),
        ],
        out_specs=[pl.BlockSpec((m, tn), lambda j: (0, j)), pl.BlockSpec((tn, k), lambda j: (j, 0))],
        out_shape=[jax.ShapeDtypeStruct((m, n), f32), jax.ShapeDtypeStruct((n, k), bf16)],
        scratch_shapes=[pltpu.VMEM((m, k), bf16)],
        compiler_params=pltpu.CompilerParams(
            dimension_semantics=("arbitrary",), vmem_limit_bytes=VMEM_LIMIT),
        name="norm_in_proj",
    )(x, g, wt)


def _memory_kv_kernel(x_ref, g_ref, w_ref, gk_ref, k_ref, v_ref, k4_ref, v4_ref, wb_ref):
    @pl.when(pl.program_id(1) == 0)
    def _():
        wb_ref[...] = w_ref[...].astype(bf16)

    x = x_ref[...]
    y = x * lax.rsqrt(jnp.mean(x * x, axis=-1, keepdims=True) + EPS)
    kv = _dot((y * g_ref[...]).astype(bf16), wb_ref[...])
    for h in range(MEM_HEADS):
        kh = kv[:, h * MEM_HD:(h + 1) * MEM_HD]
        kh = kh * lax.rsqrt(jnp.mean(kh * kh, axis=-1, keepdims=True) + EPS) * gk_ref[...]
        vh = kv[:, GROUP_W + h * MEM_HD:GROUP_W + (h + 1) * MEM_HD]
        k_ref[:, h * MEM_HD:(h + 1) * MEM_HD] = kh
        k4_ref[:, h, :] = kh
        v4_ref[:, h, :] = vh
    v_ref[...] = kv[:, GROUP_W:]


def _memory_kv(mem, g, w, gk):
    depth, b = w.shape[0], mem.shape[0]
    flat = jax.ShapeDtypeStruct((depth, b, N_MEM, GROUP_W), f32)
    split = jax.ShapeDtypeStruct((depth, b, N_MEM, MEM_HEADS, MEM_HD), f32)
    return pl.pallas_call(
        _memory_kv_kernel,
        grid=(depth, b),
        in_specs=[
            pl.BlockSpec((None, N_MEM, D_MODEL), lambda l, i: (i, 0, 0)),
            pl.BlockSpec((None, 1, D_MODEL), lambda l, i: (l, 0, 0)),
            pl.BlockSpec((None, D_MODEL, 2 * GROUP_W), lambda l, i: (l, 0, 0)),
            pl.BlockSpec((None, 1, MEM_HD), lambda l, i: (l, 0, 0)),
        ],
        out_specs=[pl.BlockSpec((None, None, N_MEM, GROUP_W), lambda l, i: (l, i, 0, 0))] * 2
        + [pl.BlockSpec((None, None, N_MEM, MEM_HEADS, MEM_HD), lambda l, i: (l, i, 0, 0, 0))] * 2,
        out_shape=[flat, flat, split, split],
        scratch_shapes=[pltpu.VMEM((D_MODEL, 2 * GROUP_W), bf16)],
        compiler_params=pltpu.CompilerParams(
            dimension_semantics=("arbitrary", "arbitrary"), vmem_limit_bytes=VMEM_LIMIT),
        name="memory_kv",
    )(mem, g, w, gk)


CONV_PAD = 8
N_STACK_SLOTS = 2
N_SEQ_IN_PROMPT, N_SEQ_IN_DECODE, N_PARAMS_PROMPT, N_PARAMS_DECODE, N_OUT = 3, 7, 11, 9, 5


def _mixer_kernel(*refs, tile, decode, layer, bb):
    n_seq = N_SEQ_IN_DECODE if decode else N_SEQ_IN_PROMPT
    n_par = N_PARAMS_DECODE if decode else N_PARAMS_PROMPT
    seq_in = refs[:n_seq]
    params = refs[n_seq:n_seq + n_par]
    outs = refs[n_seq + n_par:n_seq + n_par + N_OUT]
    scratch = refs[n_seq + n_par + N_OUT:]

    def view(ref, s):
        if decode and ref.ndim == 2:
            return _RowWindow(ref, s * tile, tile)
        return ref.at[s]

    if decode:
        p_ref, tail_ref, qkn_ref = seq_in[0], scratch[4], scratch[6]
        bd_ref, gsq_ref, gsk_ref = params[-3], params[-5], params[-4]
        tail_ref[...] = p_ref[:, OFF_GZ:D_IN]
        sq0, sk0 = OFF_SQ - OFF_GZ, OFF_SK - OFF_GZ
        qkn_ref[:, 0:GROUP_W] = _head_norm(tail_ref[:, sq0:sq0 + GROUP_W], gsq_ref[...], bd_ref)
        qkn_ref[:, GROUP_W:GROUP_W + LANES] = _head_norm(tail_ref[:, sk0:sk0 + LANES], gsk_ref[...],
                                                         bd_ref)

        def seg_rows(off, width):
            if off < OFF_GZ:
                return p_ref[:, off:off + width]
            return tail_ref[:, off - OFF_GZ:off - OFF_GZ + width]

        gla_in_ref, gla_out_ref = seq_in[4], outs[2]
        for _ in _gla_group(seg_rows, bb * tile, tile, params[-8:-5], outs[0],
                            lambda c: _gla_block_diag(gla_in_ref.at[c]),
                            lambda c, state: _store_gla_state(gla_out_ref.at[c], state), carry=False):
            pass
        _swa_decode_rows(seg_rows, qkn_ref, seq_in[5], seq_in[6], params[-2], layer, outs[0], bb, tile)

    stages = [_mixer_seq([view(r, s) for r in seq_in], params, [view(r, s) for r in outs],
                         [view(r, s) for r in scratch], tile=tile, decode=decode, layer=layer)
              for s in range(bb)]
    for _ in itertools.zip_longest(*stages):
        pass


class _RowWindow:
    def __init__(self, ref, start, size):
        self.ref, self.start, self.size, self.dtype = ref, start, size, ref.dtype

    def _index(self, idx):
        rows, cols = (slice(None), slice(None)) if idx is Ellipsis else idx
        lo, hi, _ = rows.indices(self.size)
        return slice(self.start + lo, self.start + hi), cols

    def __getitem__(self, idx):
        return self.ref[self._index(idx)]

    def __setitem__(self, idx, value):
        self.ref[self._index(idx)] = value


GLA_INTRA_ROWS = MXU_CHUNK
GLA_K_W = GLA_HEADS * GLA_DK
GLA_V_W = GLA_HEADS * GLA_DV


def _gla_block_diag(state_ref):
    rows = []
    for h in range(GLA_HEADS):
        blocks = [state_ref[h] if j == h else jnp.zeros((GLA_DK, GLA_DV), f32)
                  for j in range(GLA_HEADS)]
        rows.append(jnp.concatenate(blocks, axis=1))
    return jnp.concatenate(rows, axis=0)


def _store_gla_state(state_ref, state):
    for h in range(GLA_HEADS):
        state_ref[h] = state[h * GLA_DK:(h + 1) * GLA_DK, h * GLA_DV:(h + 1) * GLA_DV]


def _gla_group(seg, T, C, params, mix_ref, state_in, state_out, carry, seg_t=None):
    wup_ref, bga_ref, ggo_ref = params
    n_chunk = T // C
    G = min(T, GLA_INTRA_ROWS)
    groups = [slice(i * G, (i + 1) * G) for i in range(T // G)]
    row = lax.broadcasted_iota(jnp.int32, (G, G), 0)
    col = lax.broadcasted_iota(jnp.int32, (G, G), 1)
    causal = (row // C == col // C) & (row >= col)
    if seg_t is None:
        a_up = _dot(seg(OFF_GA, GLA_RANK).astype(bf16), wup_ref[...].astype(bf16))
    else:
        a_up = _dot_tn(seg_t(OFF_GA, GLA_RANK).astype(bf16), wup_ref[...].astype(bf16))
    log_a = _log_sigmoid(a_up + bga_ref[...]) * (1.0 / GLA_TAU)
    la3 = _split3(log_a)
    yield
    tril = jnp.where(causal, 1.0, 0.0).astype(bf16)
    in_chunk = jnp.where(lax.broadcasted_iota(jnp.int32, (T, LANES), 0) // C
                         == lax.broadcasted_iota(jnp.int32, (T, LANES), 1), 1.0, 0.0).astype(bf16)
    cum = jnp.concatenate(
        [_dot(tril, la3[0][r]) + _dot(tril, la3[1][r]) + _dot(tril, la3[2][r]) for r in groups],
        axis=0)
    tot_t = (_dot_tn(la3[0], in_chunk) + _dot_tn(la3[1], in_chunk)
             + _dot_tn(la3[2], in_chunk))
    decay_t = jnp.exp(tot_t)
    yield
    g_k = seg(OFF_GK, GLA_K_W)
    qd = ((seg(OFF_GQ, GLA_K_W) * (GLA_DK ** -0.5)) * jnp.exp(cum)).astype(bf16)
    kd = (g_k * jnp.exp(-cum)).astype(bf16)
    k_tail = jnp.concatenate(
        [g_k[c * C:(c + 1) * C] * jnp.exp(cum[(c + 1) * C - 1:(c + 1) * C] - cum[c * C:(c + 1) * C])
         for c in range(n_chunk)], axis=0) if n_chunk > 1 else g_k * jnp.exp(cum[T - 1:T] - cum)
    kt = k_tail.astype(bf16)
    yield
    v_b = seg(OFF_GV, GLA_V_W).astype(bf16)
    g_z = seg(OFF_GZ, GROUP_W)
    yield

    o_intra = []
    for r in groups:
        o_heads = []
        for h in range(GLA_HEADS):
            ks = slice(h * GLA_DK, (h + 1) * GLA_DK)
            attn = jnp.where(causal, _dot_nt(qd[r, ks], kd[r, ks]), 0.0).astype(bf16)
            o_heads.append(_dot(attn, v_b[r, h * GLA_DV:(h + 1) * GLA_DV]))
        o_intra.append(jnp.concatenate(o_heads, axis=1))
        yield
    o_intra = jnp.concatenate(o_intra, axis=0)

    shape = (GLA_K_W, GLA_V_W)
    on_diag = (lax.broadcasted_iota(jnp.int32, shape, 0) // GLA_DK
               == lax.broadcasted_iota(jnp.int32, shape, 1) // GLA_DV)
    o_chunks = []
    state = None
    for c in range(n_chunk):
        rs = slice(c * C, (c + 1) * C)
        if c == 0 or not carry:
            state = state_in(c)
        o_chunks.append(o_intra[rs] + _dot(qd[rs], state.astype(bf16)))
        update = jnp.where(on_diag, _dot_tn(kt[rs], v_b[rs]), 0.0)
        state = decay_t[:, c:c + 1] * state + update
        state_out(c, state)
        yield
    o = jnp.concatenate(o_chunks, axis=0) if n_chunk > 1 else o_chunks[0]
    for h in range(GLA_HEADS):
        vs = slice(h * GLA_DV, (h + 1) * GLA_DV)
        o_h = o[:, vs]
        o_h = o_h * lax.rsqrt(jnp.mean(o_h * o_h, axis=-1, keepdims=True) + EPS) * ggo_ref[...]
        mix_ref[:, GROUP_W + h * GLA_DV:GROUP_W + (h + 1) * GLA_DV] = (
            o_h * _silu(g_z[:, vs])).astype(mix_ref.dtype)
        yield


def _head_norm(x, g, bd_ref):
    rows, n_lanes = x.shape
    w = min(n_lanes, MXU_CHUNK)
    pieces = n_lanes // w
    bd = bd_ref[0:w, 0:w]

    def head_sums(v):
        stacked = jnp.concatenate([v[:, i * w:(i + 1) * w] for i in range(pieces)], axis=0)
        r = _dot(stacked, bd)
        return jnp.concatenate([r[i * rows:(i + 1) * rows] for i in range(pieces)], axis=1)

    sq = x * x
    hi = sq.astype(bf16)
    lo = (sq - hi.astype(f32)).astype(bf16)
    ms = (head_sums(hi) + head_sums(lo)) * (1.0 / SWA_HD)
    return x * lax.rsqrt(ms + EPS) * g


def _mixer_seq(seq_in, params, outs, scratch, *, tile, decode, layer):
    if decode:
        p_ref, mk_ref, mv_ref, conv_in_ref, gla_in_ref, kc_ref, vc_ref = seq_in
    else:
        x_ref, mk_ref, mv_ref = seq_in
        gn_ref, wt_ref = params[:2]
    (convw_ref, wup_ref, bga_ref, ggo_ref, gsq_ref, gsk_ref, bd_ref, sinks_ref,
     gmq_ref) = params[-N_PARAMS_DECODE:]
    mix_ref, conv_out_ref, gla_out_ref, kbuf_ref, vbuf_ref = outs
    ext_ref, s_ref, kprev_ref, vprev_ref, tail_ref, stk_ref, qkn_ref = scratch

    T = tile
    t = pl.program_id(1)

    def init_state():
        ext_ref[0:CONV_PAD, :] = jnp.zeros((CONV_PAD, GROUP_W), f32)
        if decode:
            ext_ref[CONV_PAD - (CONV_W - 1):CONV_PAD, :] = conv_in_ref[...]
        else:
            s_ref[...] = jnp.zeros_like(s_ref)
            kprev_ref[...] = jnp.zeros_like(kprev_ref)
            vprev_ref[...] = jnp.zeros_like(vprev_ref)

    if decode:
        init_state()
    else:
        pl.when(t == 0)(init_state)
    yield

    if decode:
        def seg(off, width):
            if off < OFF_GZ:
                return p_ref[:, off:off + width]
            return tail_ref[:, off - OFF_GZ:off - OFF_GZ + width]
    else:
        x = x_ref[...]
        hn = (x * lax.rsqrt(jnp.mean(x * x, axis=-1, keepdims=True) + EPS) * gn_ref[...]).astype(bf16)

        def seg(off, width):
            return _dot_nt(hn, wt_ref[off:off + width, :])

        def seg_t(off, width):
            return _dot_nt(wt_ref[off:off + width, :], hn)

    def group_a():
        u = seg(OFF_AC, GROUP_W) * seg(OFF_AH, GROUP_W)
        ext_ref[CONV_PAD:CONV_PAD + T, :] = u
        yield
        conv = (convw_ref[0:1, :] * ext_ref[CONV_PAD - 2:CONV_PAD - 2 + T, :]
                + convw_ref[1:2, :] * ext_ref[CONV_PAD - 1:CONV_PAD - 1 + T, :]
                + convw_ref[2:3, :] * u)
        a_b = seg(OFF_AB, GROUP_W)
        yield
        mix_ref[:, 0:GROUP_W] = (a_b * conv * _silu(seg(OFF_AZ, GROUP_W))).astype(mix_ref.dtype)
        conv_state = ext_ref[CONV_PAD + T - 2:CONV_PAD + T, :]
        ext_ref[CONV_PAD - 2:CONV_PAD, :] = conv_state
        conv_out_ref[...] = conv_state
        yield

    def group_b():
        def keep_state(c, state):
            if c == T // GLA_CHUNK - 1:
                s_ref[...] = state
                _store_gla_state(gla_out_ref, state)

        yield from _gla_group(seg, T, GLA_CHUNK, (wup_ref, bga_ref, ggo_ref), mix_ref,
                              lambda c: s_ref[...], keep_state, carry=True, seg_t=seg_t)

    def group_d():
        m_q = seg(OFF_MQ, GROUP_W)
        m_z = seg(OFF_MZ, GROUP_W)
        yield
        yield from _memory_attention(m_q, m_z, T, decode, gmq_ref, mk_ref, mv_ref, mix_ref, stk_ref)

    if decode:
        yield from group_a()
        kbuf_ref[0:WINDOW - T, :] = kc_ref[T:WINDOW, :]
        kbuf_ref[WINDOW - T:WINDOW, :] = qkn_ref[:, GROUP_W:GROUP_W + LANES]
        vbuf_ref[0:WINDOW - T, :] = vc_ref[T:WINDOW, :]
        vbuf_ref[WINDOW - T:WINDOW, :] = seg(OFF_SV, LANES)
        yield
        yield from group_d()
    else:
        group_c = _swa_prompt_tile(seg, T, t, layer, (gsq_ref, gsk_ref, bd_ref, sinks_ref), mix_ref,
                                   kprev_ref, vprev_ref, kbuf_ref, vbuf_ref)
        for _ in itertools.zip_longest(itertools.chain(group_a(), group_b()),
                                       itertools.chain(group_c, group_d())):
            pass
        yield


def _swa_prompt_tile(seg, T, t, layer, params, mix_ref, kprev_ref, vprev_ref, kbuf_ref, vbuf_ref):
    gsq_ref, gsk_ref, bd_ref, sinks_ref = params
    s_z = seg(OFF_SZ, GROUP_W)
    s_q = seg(OFF_SQ, GROUP_W)
    s_kv = seg(OFF_SK, 2 * LANES)
    q_n = _head_norm(s_q, gsq_ref[...], bd_ref)
    k_n = _head_norm(s_kv[:, 0:LANES], gsk_ref[...], bd_ref)
    v_n = s_kv[:, LANES:2 * LANES]
    yield

    BQ = WINDOW
    n_blk = T // BQ
    stack = SWA_GROUP
    nk = WINDOW + BQ
    qi = lax.broadcasted_iota(jnp.int32, (stack * BQ, nk), 0) % BQ
    kj = lax.broadcasted_iota(jnp.int32, (stack * BQ, nk), 1)
    dist = qi + WINDOW - kj
    band = (dist >= 0) & (dist < WINDOW)
    srow = lax.broadcasted_iota(jnp.int32, (stack * BQ, 1), 0) // BQ
    for blk in range(n_blk):
        rs = slice(blk * BQ, (blk + 1) * BQ)
        if blk == 0:
            k_prev, v_prev = kprev_ref[...], vprev_ref[...]
            valid = band & ((kj >= WINDOW) | (t > 0))
        else:
            ps = slice((blk - 1) * BQ, blk * BQ)
            k_prev, v_prev = k_n[ps], v_n[ps]
            valid = band
        k_cat = jnp.concatenate([k_prev, k_n[rs]], axis=0)
        v_cat = jnp.concatenate([v_prev, v_n[rs]], axis=0)
        for g in range(SWA_KV_HEADS):
            kg = k_cat[:, g * SWA_HD:(g + 1) * SWA_HD].astype(bf16)
            vg = v_cat[:, g * SWA_HD:(g + 1) * SWA_HD].astype(bf16)
            heads = [g * SWA_GROUP + j for j in range(stack)]
            qg = jnp.concatenate([q_n[rs, hd * SWA_HD:(hd + 1) * SWA_HD] for hd in heads],
                                 axis=0).astype(bf16)
            sink = jnp.full((stack * BQ, 1), sinks_ref[layer, heads[0]], f32)
            for j in range(1, stack):
                sink = jnp.where(srow == j, sinks_ref[layer, heads[j]], sink)
            o = _attend(_dot_nt(qg, kg), valid, SWA_HD ** -0.5, vg, sink)
            for j, hd in enumerate(heads):
                z = s_z[rs, hd * SWA_HD:(hd + 1) * SWA_HD]
                mix_ref[rs, 2 * GROUP_W + hd * SWA_HD:2 * GROUP_W + (hd + 1) * SWA_HD] = (
                    o[j * BQ:(j + 1) * BQ] * _silu(z)).astype(mix_ref.dtype)
            yield

    kprev_ref[...] = k_n[T - WINDOW:T]
    vprev_ref[...] = v_n[T - WINDOW:T]
    kbuf_ref[...] = k_n[T - WINDOW:T]
    vbuf_ref[...] = v_n[T - WINDOW:T]
    yield


def _swa_decode_rows(seg, qkn_ref, kc_ref, vc_ref, sinks_ref, layer, mix_ref, bb, T):
    R = bb * T
    n_cache = bb * WINDOW
    nk = n_cache + R
    q_n, k_n = qkn_ref[:, 0:GROUP_W], qkn_ref[:, GROUP_W:GROUP_W + LANES]
    s_z = seg(OFF_SZ, GROUP_W)
    k_all = jnp.concatenate([kc_ref[s] for s in range(bb)] + [k_n], axis=0)
    v_all = jnp.concatenate([vc_ref[s] for s in range(bb)] + [seg(OFF_SV, LANES)], axis=0)
    rows = SWA_GROUP * R
    r = lax.broadcasted_iota(jnp.int32, (rows, nk), 0) % R
    c = lax.broadcasted_iota(jnp.int32, (rows, nk), 1)
    cached = c < n_cache
    key_seq = jnp.where(cached, c // WINDOW, (c - n_cache) // T)
    key_pos = jnp.where(cached, c % WINDOW, WINDOW + (c - n_cache) % T)
    dist = r % T + WINDOW - key_pos
    valid = (r // T == key_seq) & (dist >= 0) & (dist < WINDOW)
    srow = lax.broadcasted_iota(jnp.int32, (rows, 1), 0) // R
    for g in range(SWA_KV_HEADS):
        heads = [g * SWA_GROUP + j for j in range(SWA_GROUP)]
        qg = jnp.concatenate([q_n[:, hd * SWA_HD:(hd + 1) * SWA_HD] for hd in heads],
                             axis=0).astype(bf16)
        kg = k_all[:, g * SWA_HD:(g + 1) * SWA_HD].astype(bf16)
        vg = v_all[:, g * SWA_HD:(g + 1) * SWA_HD].astype(bf16)
        sink = jnp.full((rows, 1), sinks_ref[layer, heads[0]], f32)
        for j in range(1, SWA_GROUP):
            sink = jnp.where(srow == j, sinks_ref[layer, heads[j]], sink)
        o = _attend(_dot_nt(qg, kg), valid, SWA_HD ** -0.5, vg, sink)
        for j, hd in enumerate(heads):
            mix_ref[:, 2 * GROUP_W + hd * SWA_HD:2 * GROUP_W + (hd + 1) * SWA_HD] = (
                o[j * R:(j + 1) * R] * _silu(s_z[:, hd * SWA_HD:(hd + 1) * SWA_HD])).astype(mix_ref.dtype)


def _memory_attention(m_q, m_z, T, decode, gmq_ref, mk_ref, mv_ref, mix_ref, stk_ref):
    def stack_rows(pieces, slot):
        r, w = pieces[0].shape
        if r % 8 == 0:
            return jnp.concatenate(pieces, axis=0)
        for j, piece in enumerate(pieces):
            stk_ref[slot, j * r:(j + 1) * r, 0:w] = piece
        return stk_ref[slot, 0:len(pieces) * r, 0:w]

    def unstack_rows(x, n, slot):
        r, w = x.shape[0] // n, x.shape[1]
        if r % 8 == 0:
            return [x[j * r:(j + 1) * r] for j in range(n)]
        stk_ref[slot, 0:n * r, 0:w] = x
        return [stk_ref[slot, j * r:(j + 1) * r, 0:w] for j in range(n)]

    def mem_q(h):
        qh = m_q[:, h * MEM_HD:(h + 1) * MEM_HD]
        return qh * lax.rsqrt(jnp.mean(qh * qh, axis=-1, keepdims=True) + EPS) * gmq_ref[...]

    if decode:
        qs = stack_rows([mem_q(h) for h in range(MEM_HEADS)], 0).astype(bf16)
        s = _dot_nt(qs, mk_ref[...].astype(bf16))
        shape = (MEM_HEADS * T, MEM_HEADS * N_MEM)
        same_head = (lax.broadcasted_iota(jnp.int32, shape, 0) // T
                     == lax.broadcasted_iota(jnp.int32, shape, 1) % MEM_HEADS)
        yield
        o = _attend(s, same_head, MEM_HD ** -0.5, mv_ref[...].astype(bf16))
        o_all = unstack_rows(o, MEM_HEADS, 1)
    else:
        o_all = []
        for h in range(MEM_HEADS):
            hs = slice(h * MEM_HD, (h + 1) * MEM_HD)
            s = _dot_nt(mem_q(h).astype(bf16), mk_ref[:, hs].astype(bf16))
            o_all.append(_attend(s, None, MEM_HD ** -0.5, mv_ref[:, hs].astype(bf16)))
            yield
    for h in range(MEM_HEADS):
        mix_ref[:, 3 * GROUP_W + h * MEM_HD:3 * GROUP_W + (h + 1) * MEM_HD] = (
            o_all[h] * _silu(m_z[:, h * MEM_HD:(h + 1) * MEM_HD])).astype(mix_ref.dtype)
    yield


def _mixer(tokens, norm_w, mem_k, mem_v, mem_layer, state, params, layer, tile, bb, decode):
    if decode:
        b, width = state[0].shape[1], tokens.shape[1]
        L = tokens.shape[0] // b
    else:
        b, L, width = tokens.shape
    nt = L // tile
    assert nt == 1 or not decode, "a decode call covers each sequence with a single tile"
    conv_w, w_up, b_ga, g_go, g_sq, g_sk, bd, sinks, g_mq = params

    def tok(width):
        if decode:
            return pl.BlockSpec((bb * tile, width), lambda i, t: (i, 0))
        return pl.BlockSpec((bb, tile, width), lambda i, t: (i, t, 0))

    def per_seq(*shape):
        return pl.BlockSpec((bb,) + shape, lambda i, t: (i,) + (0,) * len(shape))

    def per_seq_at(lyr, *shape):
        return pl.BlockSpec((None, bb) + shape, lambda i, t: (lyr, i) + (0,) * len(shape))

    def param(a):
        return pl.BlockSpec((None,) + a.shape[1:], lambda i, t: (layer,) + (0,) * (a.ndim - 1))

    kv_w = SWA_KV_HEADS * SWA_HD
    state_shapes = [(CONV_W - 1, GROUP_W), (GLA_HEADS, GLA_DK, GLA_DV), (WINDOW, kv_w),
                    (WINDOW, kv_w)]
    in_specs = [tok(width), per_seq_at(mem_layer, *mem_k.shape[2:]),
                per_seq_at(mem_layer, *mem_v.shape[2:])]
    args = [tokens, mem_k, mem_v]
    if decode:
        in_specs += [per_seq_at(layer, *s) for s in state_shapes]
        args += list(state)
    else:
        g_n, w_t = norm_w
        in_specs += [param(g_n), pl.BlockSpec(w_t.shape, lambda i, t: (0, 0),
                                              pipeline_mode=pl.Buffered(1))]
        args += [g_n, w_t]
    in_specs += [param(conv_w), param(w_up), param(b_ga), param(g_go), param(g_sq), param(g_sk),
                 pl.BlockSpec(bd.shape, lambda i, t: (0, 0)),
                 pl.BlockSpec(memory_space=pltpu.SMEM), param(g_mq)]
    args += [conv_w, w_up, b_ga, g_go, g_sq, g_sk, bd, sinks, g_mq]
    if decode:
        out_shape = [jax.ShapeDtypeStruct((b * L, 4 * GROUP_W), f32)]
        row_scratch = [pltpu.VMEM((bb * tile, D_IN - OFF_GZ), f32),
                       pltpu.VMEM((bb * tile, GROUP_W + LANES), f32)]
    else:
        out_shape = [jax.ShapeDtypeStruct((b, L, 4 * GROUP_W), bf16)]
        row_scratch = [pltpu.VMEM((bb, 8, LANES), f32)] * 2
    out_shape += [jax.ShapeDtypeStruct((b,) + s, f32) for s in state_shapes]
    out_specs = [tok(4 * GROUP_W)] + [per_seq(*s) for s in state_shapes]
    return pl.pallas_call(
        functools.partial(_mixer_kernel, tile=tile, decode=decode, layer=layer, bb=bb),
        grid=(b // bb, nt),
        in_specs=in_specs,
        out_specs=out_specs,
        out_shape=out_shape,
        scratch_shapes=[
            pltpu.VMEM((bb, CONV_PAD + tile, GROUP_W), f32),
            pltpu.VMEM((bb, 8, LANES) if decode else (bb, GLA_K_W, GLA_V_W), f32),
            pltpu.VMEM((bb, WINDOW, kv_w), f32),
            pltpu.VMEM((bb, WINDOW, kv_w), f32),
            row_scratch[0],
            pltpu.VMEM((bb, N_STACK_SLOTS, MEM_HEADS * min(tile, 8), LANES), f32),
            row_scratch[1],
        ],
        compiler_params=pltpu.CompilerParams(
            dimension_semantics=("arbitrary", "arbitrary"), vmem_limit_bytes=VMEM_LIMIT),
        name="mixer_decode" if decode else "mixer_prompt",
    )(*args)


def _out_proj_kernel(mix_ref, w_ref, x_ref, y_ref, wb_ref):
    @pl.when(pl.program_id(1) == 0)
    def _():
        wb_ref[...] = w_ref[...].astype(bf16)

    y_ref[...] = x_ref[...] + _dot(mix_ref[...].astype(bf16), wb_ref[...])


def _out_proj(mix, w, x, l, tm, tn):
    m, k = mix.shape
    n = w.shape[2]
    return pl.pallas_call(
        _out_proj_kernel,
        grid=(n // tn, m // tm),
        in_specs=[
            pl.BlockSpec((tm, k), lambda j, i: (i, 0)),
            pl.BlockSpec((None, k, tn), lambda j, i: (l, 0, j),
                         pipeline_mode=pl.Buffered(1) if tn == n else None),
            pl.BlockSpec((tm, tn), lambda j, i: (i, j)),
        ],
        out_specs=pl.BlockSpec((tm, tn), lambda j, i: (i, j)),
        out_shape=jax.ShapeDtypeStruct((m, n), f32),
        scratch_shapes=[pltpu.VMEM((k, tn), bf16)],
        compiler_params=pltpu.CompilerParams(
            dimension_semantics=("arbitrary", "arbitrary"), vmem_limit_bytes=VMEM_LIMIT),
        name="out_proj",
    )(mix, w, x)


PROMPT_TILE = 512
DECODE_SEQS_PER_STEP = 8
PROJ_TN = 1536
OUT_TM, OUT_TN = 512, 2048

_LANE = np.arange(GROUP_W)
HEAD_BLOCK_DIAG = _LANE[:, None] // SWA_HD == _LANE[None, :] // SWA_HD


def kernel(x_prompt, x_sample, mem_prompt, state_conv, state_gla, cache_swa_k, cache_swa_v,
           cache_mem_k, cache_mem_v, g_norm, w_in, conv_w, w_gla_a_up, b_gla_a, g_gla_o,
           g_swa_q, g_swa_k, swa_sinks, g_mem, w_mem_kv, g_mem_q, g_mem_k, w_out):
    depth = w_in.shape[0]
    bp, lp, _ = x_prompt.shape
    bs, ls, _ = x_sample.shape
    hp = x_prompt.reshape(bp * lp, D_MODEL)
    hs = x_sample.reshape(bs * ls, D_MODEL)

    def row(a):
        return a[:, None, :]

    params = (conv_w, w_gla_a_up, row(b_gla_a), row(g_gla_o),
              row(jnp.tile(g_swa_q, (1, SWA_HEADS))), row(jnp.tile(g_swa_k, (1, SWA_KV_HEADS))),
              jnp.asarray(HEAD_BLOCK_DIAG, bf16), swa_sinks, row(g_mem_q))
    g_n, g_m, g_mk = row(g_norm), row(g_mem), row(g_mem_k)
    w_in_t = jnp.swapaxes(w_in, 1, 2)
    kv_w = SWA_KV_HEADS * SWA_HD
    state = (state_conv, state_gla, cache_swa_k.reshape(depth, bs, WINDOW, kv_w),
             cache_swa_v.reshape(depth, bs, WINDOW, kv_w))
    mem_k_s = cache_mem_k.reshape(depth, bs, N_MEM * MEM_HEADS, MEM_HD)
    mem_v_s = cache_mem_v.reshape(depth, bs, N_MEM * MEM_HEADS, MEM_HD)

    mk, mv, mem_k_p, mem_v_p = _memory_kv(mem_prompt, g_m, w_mem_kv, g_mk)
    outs = [[] for _ in range(8)]
    for l in range(depth):
        proj, w_bf = _norm_matmul(hs, g_n, w_in_t, l, PROJ_TN)

        mix, c, s, kb, vb = _mixer(hp.reshape(bp, lp, D_MODEL), (g_n, w_bf), mk, mv, l,
                                   None, params, l, PROMPT_TILE, 1, decode=False)
        hp = _out_proj(mix.reshape(bp * lp, 4 * GROUP_W), w_out, hp, l, OUT_TM, OUT_TN)
        for lst, a in zip(outs[:4], (
                c, s, kb.reshape(bp, WINDOW, SWA_KV_HEADS, SWA_HD),
                vb.reshape(bp, WINDOW, SWA_KV_HEADS, SWA_HD))):
            lst.append(a)

        mix, c, s, kb, vb = _mixer(proj, None, mem_k_s, mem_v_s, l, state, params, l, ls,
                                   DECODE_SEQS_PER_STEP, decode=True)
        hs = _out_proj(mix, w_out, hs, l, bs * ls, OUT_TN)
        for lst, a in zip(outs[4:], (
                c, s, kb.reshape(bs, WINDOW, SWA_KV_HEADS, SWA_HD),
                vb.reshape(bs, WINDOW, SWA_KV_HEADS, SWA_HD))):
            lst.append(a)

    stacked = [jnp.stack(o) for o in outs]
    return (hp.reshape(bp, lp, D_MODEL), hs.reshape(bs, ls, D_MODEL),
            *stacked[:4], mem_k_p, mem_v_p, *stacked[4:])
```

```python
import functools
import itertools

import jax
import jax.numpy as jnp
import numpy as np
from jax import lax
from jax.experimental import pallas as pl
from jax.experimental.pallas import tpu as pltpu

f32 = jnp.float32
bf16 = jnp.bfloat16

D_MODEL = 2048
GROUP_W = 512
GLA_HEADS = 4
GLA_DK = 64
GLA_DV = 128
GLA_RANK = 16
GLA_TAU = 16.0
GLA_CHUNK = 64
SWA_HEADS = 8
SWA_KV_HEADS = 2
SWA_HD = 64
SWA_GROUP = SWA_HEADS // SWA_KV_HEADS
WINDOW = 128
N_MEM = 256
MEM_HEADS = 4
MEM_HD = 128
CONV_W = 3
EPS = 1e-6

LANES = 128
MXU_CHUNK = 256

D_IN = 5904
OFF_AB, OFF_AC, OFF_AH, OFF_AZ = 0, 512, 1024, 1536
OFF_GQ, OFF_GK, OFF_GV, OFF_GA, OFF_GZ = 2048, 2304, 2560, 3072, 3088
OFF_SQ, OFF_SK, OFF_SV, OFF_SZ = 3600, 4112, 4240, 4368
OFF_MQ, OFF_MZ = 4880, 5392

VMEM_LIMIT = 60 * 1024 * 1024


def _dot(a, b):
    return jnp.dot(a, b, preferred_element_type=f32)


def _dot_nt(a, b):
    return lax.dot_general(a, b, (((1,), (1,)), ((), ())), preferred_element_type=f32)


def _dot_tn(a, b):
    return lax.dot_general(a, b, (((0,), (0,)), ((), ())), preferred_element_type=f32)


def _split3(x):
    hi = x.astype(bf16)
    r = x - hi.astype(f32)
    mid = r.astype(bf16)
    lo = (r - mid.astype(f32)).astype(bf16)
    return hi, mid, lo


LOG2_E = 1.4426950408889634


def _attend(scores, valid, scale, values, sink=None):
    if valid is not None:
        scores = jnp.where(valid, scores, -jnp.inf)
    m = jnp.max(scores, axis=-1, keepdims=True)
    if sink is not None:
        sink = sink * (1.0 / scale)
        m = jnp.maximum(m, sink)
    e = jnp.exp2((scores - m) * (scale * LOG2_E))
    denom = jnp.sum(e, axis=-1, keepdims=True)
    if sink is not None:
        denom = denom + jnp.exp2((sink - m) * (scale * LOG2_E))
    return _dot(e.astype(bf16), values) / denom


def _silu(x):
    return x * jax.nn.sigmoid(x)


def _log_sigmoid(x):
    return jnp.minimum(x, 0.0) - jnp.log1p(jnp.exp(-jnp.abs(x)))


def _norm_matmul_kernel(x_ref, g_ref, wt_ref, o_ref, wb_ref, hn_ref):
    x = x_ref[...]
    y = x * lax.rsqrt(jnp.mean(x * x, axis=-1, keepdims=True) + EPS)
    hn_ref[...] = (y * g_ref[...]).astype(bf16)
    wb_ref[...] = wt_ref[...].astype(bf16)
    o_ref[...] = _dot_nt(hn_ref[...], wb_ref[...])


def _norm_matmul(x, g, wt, l, tn):
    m, k = x.shape
    n = wt.shape[1]
    return pl.pallas_call(
        _norm_matmul_kernel,
        grid=(pl.cdiv(n, tn),),
        in_specs=[
            pl.BlockSpec((m, k), lambda j: (0, 0)),
            pl.BlockSpec((None, 1, k), lambda j: (l, 0, 0)),
            pl.BlockSpec((None, tn, k), lambda j: (l, j, 0)),
        ],
        out_specs=[pl.BlockSpec((m, tn), lambda j: (0, j)), pl.BlockSpec((tn, k), lambda j: (j, 0))],
        out_shape=[jax.ShapeDtypeStruct((m, n), f32), jax.ShapeDtypeStruct((n, k), bf16)],
        scratch_shapes=[pltpu.VMEM((m, k), bf16)],
        compiler_params=pltpu.CompilerParams(
            dimension_semantics=("arbitrary",), vmem_limit_bytes=VMEM_LIMIT),
        name="norm_in_proj",
    )(x, g, wt)


def _memory_kv_kernel(x_ref, g_ref, w_ref, gk_ref, k_ref, v_ref, k4_ref, v4_ref, wb_ref):
    @pl.when(pl.program_id(1) == 0)
    def _():
        wb_ref[...] = w_ref[...].astype(bf16)

    x = x_ref[...]
    y = x * lax.rsqrt(jnp.mean(x * x, axis=-1, keepdims=True) + EPS)
    kv = _dot((y * g_ref[...]).astype(bf16), wb_ref[...])
    for h in range(MEM_HEADS):
        kh = kv[:, h * MEM_HD:(h + 1) * MEM_HD]
        kh = kh * lax.rsqrt(jnp.mean(kh * kh, axis=-1, keepdims=True) + EPS) * gk_ref[...]
        vh = kv[:, GROUP_W + h * MEM_HD:GROUP_W + (h + 1) * MEM_HD]
        k_ref[:, h * MEM_HD:(h + 1) * MEM_HD] = kh
        k4_ref[:, h, :] = kh
        v4_ref[:, h, :] = vh
    v_ref[...] = kv[:, GROUP_W:]


def _memory_kv(mem, g, w, gk):
    depth, b = w.shape[0], mem.shape[0]
    flat = jax.ShapeDtypeStruct((depth, b, N_MEM, GROUP_W), f32)
    split = jax.ShapeDtypeStruct((depth, b, N_MEM, MEM_HEADS, MEM_HD), f32)
    return pl.pallas_call(
        _memory_kv_kernel,
        grid=(depth, b),
        in_specs=[
            pl.BlockSpec((None, N_MEM, D_MODEL), lambda l, i: (i, 0, 0)),
            pl.BlockSpec((None, 1, D_MODEL), lambda l, i: (l, 0, 0)),
            pl.BlockSpec((None, D_MODEL, 2 * GROUP_W), lambda l, i: (l, 0, 0)),
            pl.BlockSpec((None, 1, MEM_HD), lambda l, i: (l, 0, 0)),
        ],
        out_specs=[pl.BlockSpec((None, None, N_MEM, GROUP_W), lambda l, i: (l, i, 0, 0))] * 2
        + [pl.BlockSpec((None, None, N_MEM, MEM_HEADS, MEM_HD), lambda l, i: (l, i, 0, 0, 0))] * 2,
        out_shape=[flat, flat, split, split],
        scratch_shapes=[pltpu.VMEM((D_MODEL, 2 * GROUP_W), bf16)],
        compiler_params=pltpu.CompilerParams(
            dimension_semantics=("arbitrary", "arbitrary"), vmem_limit_bytes=VMEM_LIMIT),
        name="memory_kv",
    )(mem, g, w, gk)


CONV_PAD = 8
N_STACK_SLOTS = 2
N_SEQ_IN_PROMPT, N_SEQ_IN_DECODE, N_PARAMS_PROMPT, N_PARAMS_DECODE, N_OUT = 3, 7, 11, 9, 5


def _mixer_kernel(*refs, tile, decode, layer, bb):
    n_seq = N_SEQ_IN_DECODE if decode else N_SEQ_IN_PROMPT
    n_par = N_PARAMS_DECODE if decode else N_PARAMS_PROMPT
    seq_in = refs[:n_seq]
    params = refs[n_seq:n_seq + n_par]
    outs = refs[n_seq + n_par:n_seq + n_par + N_OUT]
    scratch = refs[n_seq + n_par + N_OUT:]

    def view(ref, s):
        if decode and ref.ndim == 2:
            return _RowWindow(ref, s * tile, tile)
        return ref.at[s]

    if decode:
        p_ref, tail_ref, qkn_ref = seq_in[0], scratch[4], scratch[6]
        bd_ref, gsq_ref, gsk_ref = params[-3], params[-5], params[-4]
        tail_ref[...] = p_ref[:, OFF_GZ:D_IN]
        sq0, sk0 = OFF_SQ - OFF_GZ, OFF_SK - OFF_GZ
        qkn_ref[:, 0:GROUP_W] = _head_norm(tail_ref[:, sq0:sq0 + GROUP_W], gsq_ref[...], bd_ref)
        qkn_ref[:, GROUP_W:GROUP_W + LANES] = _head_norm(tail_ref[:, sk0:sk0 + LANES], gsk_ref[...],
                                                         bd_ref)

        def seg_rows(off, width):
            if off < OFF_GZ:
                return p_ref[:, off:off + width]
            return tail_ref[:, off - OFF_GZ:off - OFF_GZ + width]

        gla_in_ref, gla_out_ref = seq_in[4], outs[2]
        for _ in _gla_group(seg_rows, bb * tile, tile, params[-8:-5], outs[0],
                            lambda c: _gla_block_diag(gla_in_ref.at[c]),
                            lambda c, state: _store_gla_state(gla_out_ref.at[c], state), carry=False):
            pass
        _swa_decode_rows(seg_rows, qkn_ref, seq_in[5], seq_in[6], params[-2], layer, outs[0], bb, tile)

    stages = [_mixer_seq([view(r, s) for r in seq_in], params, [view(r, s) for r in outs],
                         [view(r, s) for r in scratch], tile=tile, decode=decode, layer=layer)
              for s in range(bb)]
    for _ in itertools.zip_longest(*stages):
        pass


class _RowWindow:
    def __init__(self, ref, start, size):
        self.ref, self.start, self.size, self.dtype = ref, start, size, ref.dtype

    def _index(self, idx):
        rows, cols = (slice(None), slice(None)) if idx is Ellipsis else idx
        lo, hi, _ = rows.indices(self.size)
        return slice(self.start + lo, self.start + hi), cols

    def __getitem__(self, idx):
        return self.ref[self._index(idx)]

    def __setitem__(self, idx, value):
        self.ref[self._index(idx)] = value


GLA_INTRA_ROWS = MXU_CHUNK
GLA_K_W = GLA_HEADS * GLA_DK
GLA_V_W = GLA_HEADS * GLA_DV


def _gla_block_diag(state_ref):
    rows = []
    for h in range(GLA_HEADS):
        blocks = [state_ref[h] if j == h else jnp.zeros((GLA_DK, GLA_DV), f32)
                  for j in range(GLA_HEADS)]
        rows.append(jnp.concatenate(blocks, axis=1))
    return jnp.concatenate(rows, axis=0)


def _store_gla_state(state_ref, state):
    for h in range(GLA_HEADS):
        state_ref[h] = state[h * GLA_DK:(h + 1) * GLA_DK, h * GLA_DV:(h + 1) * GLA_DV]


def _gla_group(seg, T, C, params, mix_ref, state_in, state_out, carry, seg_t=None):
    wup_ref, bga_ref, ggo_ref = params
    n_chunk = T // C
    G = min(T, GLA_INTRA_ROWS)
    groups = [slice(i * G, (i + 1) * G) for i in range(T // G)]
    row = lax.broadcasted_iota(jnp.int32, (G, G), 0)
    col = lax.broadcasted_iota(jnp.int32, (G, G), 1)
    causal = (row // C == col // C) & (row >= col)
    if seg_t is None:
        a_up = _dot(seg(OFF_GA, GLA_RANK).astype(bf16), wup_ref[...].astype(bf16))
    else:
        a_up = _dot_tn(seg_t(OFF_GA, GLA_RANK).astype(bf16), wup_ref[...].astype(bf16))
    log_a = _log_sigmoid(a_up + bga_ref[...]) * (1.0 / GLA_TAU)
    la3 = _split3(log_a)
    yield
    tril = jnp.where(causal, 1.0, 0.0).astype(bf16)
    in_chunk = jnp.where(lax.broadcasted_iota(jnp.int32, (T, LANES), 0) // C
                         == lax.broadcasted_iota(jnp.int32, (T, LANES), 1), 1.0, 0.0).astype(bf16)
    cum = jnp.concatenate(
        [_dot(tril, la3[0][r]) + _dot(tril, la3[1][r]) + _dot(tril, la3[2][r]) for r in groups],
        axis=0)
    tot_t = (_dot_tn(la3[0], in_chunk) + _dot_tn(la3[1], in_chunk)
             + _dot_tn(la3[2], in_chunk))
    decay_t = jnp.exp(tot_t)
    yield
    g_k = seg(OFF_GK, GLA_K_W)
    qd = ((seg(OFF_GQ, GLA_K_W) * (GLA_DK ** -0.5)) * jnp.exp(cum)).astype(bf16)
    kd = (g_k * jnp.exp(-cum)).astype(bf16)
    k_tail = jnp.concatenate(
        [g_k[c * C:(c + 1) * C] * jnp.exp(cum[(c + 1) * C - 1:(c + 1) * C] - cum[c * C:(c + 1) * C])
         for c in range(n_chunk)], axis=0) if n_chunk > 1 else g_k * jnp.exp(cum[T - 1:T] - cum)
    kt = k_tail.astype(bf16)
    yield
    v_b = seg(OFF_GV, GLA_V_W).astype(bf16)
    g_z = seg(OFF_GZ, GROUP_W)
    yield

    o_intra = []
    for r in groups:
        o_heads = []
        for h in range(GLA_HEADS):
            ks = slice(h * GLA_DK, (h + 1) * GLA_DK)
            attn = jnp.where(causal, _dot_nt(qd[r, ks], kd[r, ks]), 0.0).astype(bf16)
            o_heads.append(_dot(attn, v_b[r, h * GLA_DV:(h + 1) * GLA_DV]))
        o_intra.append(jnp.concatenate(o_heads, axis=1))
        yield
    o_intra = jnp.concatenate(o_intra, axis=0)

    shape = (GLA_K_W, GLA_V_W)
    on_diag = (lax.broadcasted_iota(jnp.int32, shape, 0) // GLA_DK
               == lax.broadcasted_iota(jnp.int32, shape, 1) // GLA_DV)
    o_chunks = []
    state = None
    for c in range(n_chunk):
        rs = slice(c * C, (c + 1) * C)
        if c == 0 or not carry:
            state = state_in(c)
        o_chunks.append(o_intra[rs] + _dot(qd[rs], state.astype(bf16)))
        update = jnp.where(on_diag, _dot_tn(kt[rs], v_b[rs]), 0.0)
        state = decay_t[:, c:c + 1] * state + update
        state_out(c, state)
        yield
    o = jnp.concatenate(o_chunks, axis=0) if n_chunk > 1 else o_chunks[0]
    for h in range(GLA_HEADS):
        vs = slice(h * GLA_DV, (h + 1) * GLA_DV)
        o_h = o[:, vs]
        o_h = o_h * lax.rsqrt(jnp.mean(o_h * o_h, axis=-1, keepdims=True) + EPS) * ggo_ref[...]
        mix_ref[:, GROUP_W + h * GLA_DV:GROUP_W + (h + 1) * GLA_DV] = (
            o_h * _silu(g_z[:, vs])).astype(mix_ref.dtype)
        yield


def _head_norm(x, g, bd_ref):
    rows, n_lanes = x.shape
    w = min(n_lanes, MXU_CHUNK)
    pieces = n_lanes // w
    bd = bd_ref[0:w, 0:w]

    def head_sums(v):
        stacked = jnp.concatenate([v[:, i * w:(i + 1) * w] for i in range(pieces)], axis=0)
        r = _dot(stacked, bd)
        return jnp.concatenate([r[i * rows:(i + 1) * rows] for i in range(pieces)], axis=1)

    sq = x * x
    hi = sq.astype(bf16)
    lo = (sq - hi.astype(f32)).astype(bf16)
    ms = (head_sums(hi) + head_sums(lo)) * (1.0 / SWA_HD)
    return x * lax.rsqrt(ms + EPS) * g


def _mixer_seq(seq_in, params, outs, scratch, *, tile, decode, layer):
    if decode:
        p_ref, mk_ref, mv_ref, conv_in_ref, gla_in_ref, kc_ref, vc_ref = seq_in
    else:
        x_ref, mk_ref, mv_ref = seq_in
        gn_ref, wt_ref = params[:2]
    (convw_ref, wup_ref, bga_ref, ggo_ref, gsq_ref, gsk_ref, bd_ref, sinks_ref,
     gmq_ref) = params[-N_PARAMS_DECODE:]
    mix_ref, conv_out_ref, gla_out_ref, kbuf_ref, vbuf_ref = outs
    ext_ref, s_ref, kprev_ref, vprev_ref, tail_ref, stk_ref, qkn_ref = scratch

    T = tile
    t = pl.program_id(1)

    def init_state():
        ext_ref[0:CONV_PAD, :] = jnp.zeros((CONV_PAD, GROUP_W), f32)
        if decode:
            ext_ref[CONV_PAD - (CONV_W - 1):CONV_PAD, :] = conv_in_ref[...]
        else:
            s_ref[...] = jnp.zeros_like(s_ref)
            kprev_ref[...] = jnp.zeros_like(kprev_ref)
            vprev_ref[...] = jnp.zeros_like(vprev_ref)

    if decode:
        init_state()
    else:
        pl.when(t == 0)(init_state)
    yield

    if decode:
        def seg(off, width):
            if off < OFF_GZ:
                return p_ref[:, off:off + width]
            return tail_ref[:, off - OFF_GZ:off - OFF_GZ + width]
    else:
        x = x_ref[...]
        hn = (x * lax.rsqrt(jnp.mean(x * x, axis=-1, keepdims=True) + EPS) * gn_ref[...]).astype(bf16)

        def seg(off, width):
            return _dot_nt(hn, wt_ref[off:off + width, :])

        def seg_t(off, width):
            return _dot_nt(wt_ref[off:off + width, :], hn)

    def group_a():
        u = seg(OFF_AC, GROUP_W) * seg(OFF_AH, GROUP_W)
        ext_ref[CONV_PAD:CONV_PAD + T, :] = u
        yield
        conv = (convw_ref[0:1, :] * ext_ref[CONV_PAD - 2:CONV_PAD - 2 + T, :]
                + convw_ref[1:2, :] * ext_ref[CONV_PAD - 1:CONV_PAD - 1 + T, :]
                + convw_ref[2:3, :] * u)
        a_b = seg(OFF_AB, GROUP_W)
        yield
        mix_ref[:, 0:GROUP_W] = (a_b * conv * _silu(seg(OFF_AZ, GROUP_W))).astype(mix_ref.dtype)
        conv_state = ext_ref[CONV_PAD + T - 2:CONV_PAD + T, :]
        ext_ref[CONV_PAD - 2:CONV_PAD, :] = conv_state
        conv_out_ref[...] = conv_state
        yield

    def group_b():
        def keep_state(c, state):
            if c == T // GLA_CHUNK - 1:
                s_ref[...] = state
                _store_gla_state(gla_out_ref, state)

        yield from _gla_group(seg, T, GLA_CHUNK, (wup_ref, bga_ref, ggo_ref), mix_ref,
                              lambda c: s_ref[...], keep_state, carry=True, seg_t=seg_t)

    def group_d():
        m_q = seg(OFF_MQ, GROUP_W)
        m_z = seg(OFF_MZ, GROUP_W)
        yield
        yield from _memory_attention(m_q, m_z, T, decode, gmq_ref, mk_ref, mv_ref, mix_ref, stk_ref)

    if decode:
        yield from group_a()
        kbuf_ref[0:WINDOW - T, :] = kc_ref[T:WINDOW, :]
        kbuf_ref[WINDOW - T:WINDOW, :] = qkn_ref[:, GROUP_W:GROUP_W + LANES]
        vbuf_ref[0:WINDOW - T, :] = vc_ref[T:WINDOW, :]
        vbuf_ref[WINDOW - T:WINDOW, :] = seg(OFF_SV, LANES)
        yield
        yield from group_d()
    else:
        group_c = _swa_prompt_tile(seg, T, t, layer, (gsq_ref, gsk_ref, bd_ref, sinks_ref), mix_ref,
                                   kprev_ref, vprev_ref, kbuf_ref, vbuf_ref)
        for _ in itertools.zip_longest(itertools.chain(group_b(), group_a()),
                                       itertools.chain(group_c, group_d())):
            pass
        yield


def _swa_prompt_tile(seg, T, t, layer, params, mix_ref, kprev_ref, vprev_ref, kbuf_ref, vbuf_ref):
    gsq_ref, gsk_ref, bd_ref, sinks_ref = params
    s_z = seg(OFF_SZ, GROUP_W)
    s_q = seg(OFF_SQ, GROUP_W)
    s_kv = seg(OFF_SK, 2 * LANES)
    q_n = _head_norm(s_q, gsq_ref[...], bd_ref)
    k_n = _head_norm(s_kv[:, 0:LANES], gsk_ref[...], bd_ref)
    v_n = s_kv[:, LANES:2 * LANES]
    yield

    BQ = WINDOW
    n_blk = T // BQ
    stack = SWA_GROUP
    nk = WINDOW + BQ
    qi = lax.broadcasted_iota(jnp.int32, (stack * BQ, nk), 0) % BQ
    kj = lax.broadcasted_iota(jnp.int32, (stack * BQ, nk), 1)
    dist = qi + WINDOW - kj
    band = (dist >= 0) & (dist < WINDOW)
    srow = lax.broadcasted_iota(jnp.int32, (stack * BQ, 1), 0) // BQ
    for blk in range(n_blk):
        rs = slice(blk * BQ, (blk + 1) * BQ)
        if blk == 0:
            k_prev, v_prev = kprev_ref[...], vprev_ref[...]
            valid = band & ((kj >= WINDOW) | (t > 0))
        else:
            ps = slice((blk - 1) * BQ, blk * BQ)
            k_prev, v_prev = k_n[ps], v_n[ps]
            valid = band
        k_cat = jnp.concatenate([k_prev, k_n[rs]], axis=0)
        v_cat = jnp.concatenate([v_prev, v_n[rs]], axis=0)
        for g in range(SWA_KV_HEADS):
            kg = k_cat[:, g * SWA_HD:(g + 1) * SWA_HD].astype(bf16)
            vg = v_cat[:, g * SWA_HD:(g + 1) * SWA_HD].astype(bf16)
            heads = [g * SWA_GROUP + j for j in range(stack)]
            qg = jnp.concatenate([q_n[rs, hd * SWA_HD:(hd + 1) * SWA_HD] for hd in heads],
                                 axis=0).astype(bf16)
            sink = jnp.full((stack * BQ, 1), sinks_ref[layer, heads[0]], f32)
            for j in range(1, stack):
                sink = jnp.where(srow == j, sinks_ref[layer, heads[j]], sink)
            o = _attend(_dot_nt(qg, kg), valid, SWA_HD ** -0.5, vg, sink)
            for j, hd in enumerate(heads):
                z = s_z[rs, hd * SWA_HD:(hd + 1) * SWA_HD]
                mix_ref[rs, 2 * GROUP_W + hd * SWA_HD:2 * GROUP_W + (hd + 1) * SWA_HD] = (
                    o[j * BQ:(j + 1) * BQ] * _silu(z)).astype(mix_ref.dtype)
            yield

    kprev_ref[...] = k_n[T - WINDOW:T]
    vprev_ref[...] = v_n[T - WINDOW:T]
    kbuf_ref[...] = k_n[T - WINDOW:T]
    vbuf_ref[...] = v_n[T - WINDOW:T]
    yield


def _swa_decode_rows(seg, qkn_ref, kc_ref, vc_ref, sinks_ref, layer, mix_ref, bb, T):
    R = bb * T
    n_cache = bb * WINDOW
    nk = n_cache + R
    q_n, k_n = qkn_ref[:, 0:GROUP_W], qkn_ref[:, GROUP_W:GROUP_W + LANES]
    s_z = seg(OFF_SZ, GROUP_W)
    k_all = jnp.concatenate([kc_ref[s] for s in range(bb)] + [k_n], axis=0)
    v_all = jnp.concatenate([vc_ref[s] for s in range(bb)] + [seg(OFF_SV, LANES)], axis=0)
    rows = SWA_GROUP * R
    r = lax.broadcasted_iota(jnp.int32, (rows, nk), 0) % R
    c = lax.broadcasted_iota(jnp.int32, (rows, nk), 1)
    cached = c < n_cache
    key_seq = jnp.where(cached, c // WINDOW, (c - n_cache) // T)
    key_pos = jnp.where(cached, c % WINDOW, WINDOW + (c - n_cache) % T)
    dist = r % T + WINDOW - key_pos
    valid = (r // T == key_seq) & (dist >= 0) & (dist < WINDOW)
    srow = lax.broadcasted_iota(jnp.int32, (rows, 1), 0) // R
    for g in range(SWA_KV_HEADS):
        heads = [g * SWA_GROUP + j for j in range(SWA_GROUP)]
        qg = jnp.concatenate([q_n[:, hd * SWA_HD:(hd + 1) * SWA_HD] for hd in heads],
                             axis=0).astype(bf16)
        kg = k_all[:, g * SWA_HD:(g + 1) * SWA_HD].astype(bf16)
        vg = v_all[:, g * SWA_HD:(g + 1) * SWA_HD].astype(bf16)
        sink = jnp.full((rows, 1), sinks_ref[layer, heads[0]], f32)
        for j in range(1, SWA_GROUP):
            sink = jnp.where(srow == j, sinks_ref[layer, heads[j]], sink)
        o = _attend(_dot_nt(qg, kg), valid, SWA_HD ** -0.5, vg, sink)
        for j, hd in enumerate(heads):
            mix_ref[:, 2 * GROUP_W + hd * SWA_HD:2 * GROUP_W + (hd + 1) * SWA_HD] = (
                o[j * R:(j + 1) * R] * _silu(s_z[:, hd * SWA_HD:(hd + 1) * SWA_HD])).astype(mix_ref.dtype)


def _memory_attention(m_q, m_z, T, decode, gmq_ref, mk_ref, mv_ref, mix_ref, stk_ref):
    def stack_rows(pieces, slot):
        r, w = pieces[0].shape
        if r % 8 == 0:
            return jnp.concatenate(pieces, axis=0)
        for j, piece in enumerate(pieces):
            stk_ref[slot, j * r:(j + 1) * r, 0:w] = piece
        return stk_ref[slot, 0:len(pieces) * r, 0:w]

    def unstack_rows(x, n, slot):
        r, w = x.shape[0] // n, x.shape[1]
        if r % 8 == 0:
            return [x[j * r:(j + 1) * r] for j in range(n)]
        stk_ref[slot, 0:n * r, 0:w] = x
        return [stk_ref[slot, j * r:(j + 1) * r, 0:w] for j in range(n)]

    def mem_q(h):
        qh = m_q[:, h * MEM_HD:(h + 1) * MEM_HD]
        return qh * lax.rsqrt(jnp.mean(qh * qh, axis=-1, keepdims=True) + EPS) * gmq_ref[...]

    if decode:
        qs = stack_rows([mem_q(h) for h in range(MEM_HEADS)], 0).astype(bf16)
        s = _dot_nt(qs, mk_ref[...].astype(bf16))
        shape = (MEM_HEADS * T, MEM_HEADS * N_MEM)
        same_head = (lax.broadcasted_iota(jnp.int32, shape, 0) // T
                     == lax.broadcasted_iota(jnp.int32, shape, 1) % MEM_HEADS)
        yield
        o = _attend(s, same_head, MEM_HD ** -0.5, mv_ref[...].astype(bf16))
        o_all = unstack_rows(o, MEM_HEADS, 1)
    else:
        o_all = []
        for h in range(MEM_HEADS):
            hs = slice(h * MEM_HD, (h + 1) * MEM_HD)
            s = _dot_nt(mem_q(h).astype(bf16), mk_ref[:, hs].astype(bf16))
            o_all.append(_attend(s, None, MEM_HD ** -0.5, mv_ref[:, hs].astype(bf16)))
            yield
    for h in range(MEM_HEADS):
        mix_ref[:, 3 * GROUP_W + h * MEM_HD:3 * GROUP_W + (h + 1) * MEM_HD] = (
            o_all[h] * _silu(m_z[:, h * MEM_HD:(h + 1) * MEM_HD])).astype(mix_ref.dtype)
    yield


def _mixer(tokens, norm_w, mem_k, mem_v, mem_layer, state, params, layer, tile, bb, decode):
    if decode:
        b, width = state[0].shape[1], tokens.shape[1]
        L = tokens.shape[0] // b
    else:
        b, L, width = tokens.shape
    nt = L // tile
    assert nt == 1 or not decode, "a decode call covers each sequence with a single tile"
    conv_w, w_up, b_ga, g_go, g_sq, g_sk, bd, sinks, g_mq = params

    def tok(width):
        if decode:
            return pl.BlockSpec((bb * tile, width), lambda i, t: (i, 0))
        return pl.BlockSpec((bb, tile, width), lambda i, t: (i, t, 0))

    def per_seq(*shape):
        return pl.BlockSpec((bb,) + shape, lambda i, t: (i,) + (0,) * len(shape))

    def per_seq_at(lyr, *shape):
        return pl.BlockSpec((None, bb) + shape, lambda i, t: (lyr, i) + (0,) * len(shape))

    def param(a):
        return pl.BlockSpec((None,) + a.shape[1:], lambda i, t: (layer,) + (0,) * (a.ndim - 1))

    kv_w = SWA_KV_HEADS * SWA_HD
    state_shapes = [(CONV_W - 1, GROUP_W), (GLA_HEADS, GLA_DK, GLA_DV), (WINDOW, kv_w),
                    (WINDOW, kv_w)]
    in_specs = [tok(width), per_seq_at(mem_layer, *mem_k.shape[2:]),
                per_seq_at(mem_layer, *mem_v.shape[2:])]
    args = [tokens, mem_k, mem_v]
    if decode:
        in_specs += [per_seq_at(layer, *s) for s in state_shapes]
        args += list(state)
    else:
        g_n, w_t = norm_w
        in_specs += [param(g_n), pl.BlockSpec(w_t.shape, lambda i, t: (0, 0),
                                              pipeline_mode=pl.Buffered(1))]
        args += [g_n, w_t]
    in_specs += [param(conv_w), param(w_up), param(b_ga), param(g_go), param(g_sq), param(g_sk),
                 pl.BlockSpec(bd.shape, lambda i, t: (0, 0)),
                 pl.BlockSpec(memory_space=pltpu.SMEM), param(g_mq)]
    args += [conv_w, w_up, b_ga, g_go, g_sq, g_sk, bd, sinks, g_mq]
    if decode:
        out_shape = [jax.ShapeDtypeStruct((b * L, 4 * GROUP_W), f32)]
        row_scratch = [pltpu.VMEM((bb * tile, D_IN - OFF_GZ), f32),
                       pltpu.VMEM((bb * tile, GROUP_W + LANES), f32)]
    else:
        out_shape = [jax.ShapeDtypeStruct((b, L, 4 * GROUP_W), bf16)]
        row_scratch = [pltpu.VMEM((bb, 8, LANES), f32)] * 2
    out_shape += [jax.ShapeDtypeStruct((b,) + s, f32) for s in state_shapes]
    out_specs = [tok(4 * GROUP_W)] + [per_seq(*s) for s in state_shapes]
    return pl.pallas_call(
        functools.partial(_mixer_kernel, tile=tile, decode=decode, layer=layer, bb=bb),
        grid=(b // bb, nt),
        in_specs=in_specs,
        out_specs=out_specs,
        out_shape=out_shape,
        scratch_shapes=[
            pltpu.VMEM((bb, CONV_PAD + tile, GROUP_W), f32),
            pltpu.VMEM((bb, 8, LANES) if decode else (bb, GLA_K_W, GLA_V_W), f32),
            pltpu.VMEM((bb, WINDOW, kv_w), f32),
            pltpu.VMEM((bb, WINDOW, kv_w), f32),
            row_scratch[0],
            pltpu.VMEM((bb, N_STACK_SLOTS, MEM_HEADS * min(tile, 8), LANES), f32),
            row_scratch[1],
        ],
        compiler_params=pltpu.CompilerParams(
            dimension_semantics=("arbitrary", "arbitrary"), vmem_limit_bytes=VMEM_LIMIT),
        name="mixer_decode" if decode else "mixer_prompt",
    )(*args)


def _out_proj_kernel(mix_ref, w_ref, x_ref, y_ref, wb_ref):
    @pl.when(pl.program_id(1) == 0)
    def _():
        wb_ref[...] = w_ref[...].astype(bf16)

    y_ref[...] = x_ref[...] + _dot(mix_ref[...].astype(bf16), wb_ref[...])


def _out_proj(mix, w, x, l, tm, tn):
    m, k = mix.shape
    n = w.shape[2]
    return pl.pallas_call(
        _out_proj_kernel,
        grid=(n // tn, m // tm),
        in_specs=[
            pl.BlockSpec((tm, k), lambda j, i: (i, 0)),
            pl.BlockSpec((None, k, tn), lambda j, i: (l, 0, j),
                         pipeline_mode=pl.Buffered(1) if tn == n else None),
            pl.BlockSpec((tm, tn), lambda j, i: (i, j)),
        ],
        out_specs=pl.BlockSpec((tm, tn), lambda j, i: (i, j)),
        out_shape=jax.ShapeDtypeStruct((m, n), f32),
        scratch_shapes=[pltpu.VMEM((k, tn), bf16)],
        compiler_params=pltpu.CompilerParams(
            dimension_semantics=("arbitrary", "arbitrary"), vmem_limit_bytes=VMEM_LIMIT),
        name="out_proj",
    )(mix, w, x)


PROMPT_TILE = 512
DECODE_SEQS_PER_STEP = 8
PROJ_TN = 1536
OUT_TM, OUT_TN = 512, 2048

_LANE = np.arange(GROUP_W)
HEAD_BLOCK_DIAG = _LANE[:, None] // SWA_HD == _LANE[None, :] // SWA_HD


def kernel(x_prompt, x_sample, mem_prompt, state_conv, state_gla, cache_swa_k, cache_swa_v,
           cache_mem_k, cache_mem_v, g_norm, w_in, conv_w, w_gla_a_up, b_gla_a, g_gla_o,
           g_swa_q, g_swa_k, swa_sinks, g_mem, w_mem_kv, g_mem_q, g_mem_k, w_out):
    depth = w_in.shape[0]
    bp, lp, _ = x_prompt.shape
    bs, ls, _ = x_sample.shape
    hp = x_prompt.reshape(bp * lp, D_MODEL)
    hs = x_sample.reshape(bs * ls, D_MODEL)

    def row(a):
        return a[:, None, :]

    params = (conv_w, w_gla_a_up, row(b_gla_a), row(g_gla_o),
              row(jnp.tile(g_swa_q, (1, SWA_HEADS))), row(jnp.tile(g_swa_k, (1, SWA_KV_HEADS))),
              jnp.asarray(HEAD_BLOCK_DIAG, bf16), swa_sinks, row(g_mem_q))
    g_n, g_m, g_mk = row(g_norm), row(g_mem), row(g_mem_k)
    w_in_t = jnp.swapaxes(w_in, 1, 2)
    kv_w = SWA_KV_HEADS * SWA_HD
    state = (state_conv, state_gla, cache_swa_k.reshape(depth, bs, WINDOW, kv_w),
             cache_swa_v.reshape(depth, bs, WINDOW, kv_w))
    mem_k_s = cache_mem_k.reshape(depth, bs, N_MEM * MEM_HEADS, MEM_HD)
    mem_v_s = cache_mem_v.reshape(depth, bs, N_MEM * MEM_HEADS, MEM_HD)

    mk, mv, mem_k_p, mem_v_p = _memory_kv(mem_prompt, g_m, w_mem_kv, g_mk)
    outs = [[] for _ in range(8)]
    for l in range(depth):
        proj, w_bf = _norm_matmul(hs, g_n, w_in_t, l, PROJ_TN)

        mix, c, s, kb, vb = _mixer(hp.reshape(bp, lp, D_MODEL), (g_n, w_bf), mk, mv, l,
                                   None, params, l, PROMPT_TILE, 1, decode=False)
        hp = _out_proj(mix.reshape(bp * lp, 4 * GROUP_W), w_out, hp, l, OUT_TM, OUT_TN)
        for lst, a in zip(outs[:4], (
                c, s, kb.reshape(bp, WINDOW, SWA_KV_HEADS, SWA_HD),
                vb.reshape(bp, WINDOW, SWA_KV_HEADS, SWA_HD))):
            lst.append(a)

        mix, c, s, kb, vb = _mixer(proj, None, mem_k_s, mem_v_s, l, state, params, l, ls,
                                   DECODE_SEQS_PER_STEP, decode=True)
        hs = _out_proj(mix, w_out, hs, l, bs * ls, OUT_TN)
        for lst, a in zip(outs[4:], (
                c, s, kb.reshape(bs, WINDOW, SWA_KV_HEADS, SWA_HD),
                vb.reshape(bs, WINDOW, SWA_KV_HEADS, SWA_HD))):
            lst.append(a)

    stacked = [jnp.stack(o) for o in outs]
    return (hp.reshape(bp, lp, D_MODEL), hs.reshape(bs, ls, D_MODEL),
            *stacked[:4], mem_k_p, mem_v_p, *stacked[4:])
```

```python
import functools
import itertools

import jax
import jax.numpy as jnp
import numpy as np
from jax import lax
from jax.experimental import pallas as pl
from jax.experimental.pallas import tpu as pltpu

f32 = jnp.float32
bf16 = jnp.bfloat16

D_MODEL = 2048
GROUP_W = 512
GLA_HEADS = 4
GLA_DK = 64
GLA_DV = 128
GLA_RANK = 16
GLA_TAU = 16.0
GLA_CHUNK = 64
SWA_HEADS = 8
SWA_KV_HEADS = 2
SWA_HD = 64
SWA_GROUP = SWA_HEADS // SWA_KV_HEADS
WINDOW = 128
N_MEM = 256
MEM_HEADS = 4
MEM_HD = 128
CONV_W = 3
EPS = 1e-6

LANES = 128
MXU_CHUNK = 256

D_IN = 5904
OFF_AB, OFF_AC, OFF_AH, OFF_AZ = 0, 512, 1024, 1536
OFF_GQ, OFF_GK, OFF_GV, OFF_GA, OFF_GZ = 2048, 2304, 2560, 3072, 3088
OFF_SQ, OFF_SK, OFF_SV, OFF_SZ = 3600, 4112, 4240, 4368
OFF_MQ, OFF_MZ = 4880, 5392

VMEM_LIMIT = 60 * 1024 * 1024


def _dot(a, b):
    return jnp.dot(a, b, preferred_element_type=f32)


def _dot_nt(a, b):
    return lax.dot_general(a, b, (((1,), (1,)), ((), ())), preferred_element_type=f32)


def _dot_tn(a, b):
    return lax.dot_general(a, b, (((0,), (0,)), ((), ())), preferred_element_type=f32)


def _split3(x):
    hi = x.astype(bf16)
    r = x - hi.astype(f32)
    mid = r.astype(bf16)
    lo = (r - mid.astype(f32)).astype(bf16)
    return hi, mid, lo


LOG2_E = 1.4426950408889634


def _attend(scores, valid, scale, values, sink=None):
    if valid is not None:
        scores = jnp.where(valid, scores, -jnp.inf)
    m = jnp.max(scores, axis=-1, keepdims=True)
    if sink is not None:
        sink = sink * (1.0 / scale)
        m = jnp.maximum(m, sink)
    e = jnp.exp2((scores - m) * (scale * LOG2_E))
    denom = jnp.sum(e, axis=-1, keepdims=True)
    if sink is not None:
        denom = denom + jnp.exp2((sink - m) * (scale * LOG2_E))
    return _dot(e.astype(bf16), values) / denom


def _silu(x):
    return x * jax.nn.sigmoid(x)


def _log_sigmoid(x):
    return jnp.minimum(x, 0.0) - jnp.log1p(jnp.exp(-jnp.abs(x)))


def _norm_matmul_kernel(x_ref, g_ref, wt_ref, o_ref, wb_ref, hn_ref):
    x = x_ref[...]
    y = x * lax.rsqrt(jnp.mean(x * x, axis=-1, keepdims=True) + EPS)
    hn_ref[...] = (y * g_ref[...]).astype(bf16)
    wb_ref[...] = wt_ref[...].astype(bf16)
    o_ref[...] = _dot_nt(hn_ref[...], wb_ref[...])


def _norm_matmul(x, g, wt, l, tn):
    m, k = x.shape
    n = wt.shape[1]
    return pl.pallas_call(
        _norm_matmul_kernel,
        grid=(pl.cdiv(n, tn),),
        in_specs=[
            pl.BlockSpec((m, k), lambda j: (0, 0)),
            pl.BlockSpec((None, 1, k), lambda j: (l, 0, 0)),
            pl.BlockSpec((None, tn, k), lambda j: (l, j, 0)),
        ],
        out_specs=[pl.BlockSpec((m, tn), lambda j: (0, j)), pl.BlockSpec((tn, k), lambda j: (j, 0))],
        out_shape=[jax.ShapeDtypeStruct((m, n), f32), jax.ShapeDtypeStruct((n, k), bf16)],
        scratch_shapes=[pltpu.VMEM((m, k), bf16)],
        compiler_params=pltpu.CompilerParams(
            dimension_semantics=("arbitrary",), vmem_limit_bytes=VMEM_LIMIT),
        name="norm_in_proj",
    )(x, g, wt)


def _memory_kv_kernel(x_ref, g_ref, w_ref, gk_ref, k_ref, v_ref, k4_ref, v4_ref, wb_ref):
    @pl.when(pl.program_id(1) == 0)
    def _():
        wb_ref[...] = w_ref[...].astype(bf16)

    x = x_ref[...]
    y = x * lax.rsqrt(jnp.mean(x * x, axis=-1, keepdims=True) + EPS)
    kv = _dot((y * g_ref[...]).astype(bf16), wb_ref[...])
    for h in range(MEM_HEADS):
        kh = kv[:, h * MEM_HD:(h + 1) * MEM_HD]
        kh = kh * lax.rsqrt(jnp.mean(kh * kh, axis=-1, keepdims=True) + EPS) * gk_ref[...]
        vh = kv[:, GROUP_W + h * MEM_HD:GROUP_W + (h + 1) * MEM_HD]
        k_ref[:, h * MEM_HD:(h + 1) * MEM_HD] = kh
        k4_ref[:, h, :] = kh
        v4_ref[:, h, :] = vh
    v_ref[...] = kv[:, GROUP_W:]


def _memory_kv(mem, g, w, gk):
    depth, b = w.shape[0], mem.shape[0]
    flat = jax.ShapeDtypeStruct((depth, b, N_MEM, GROUP_W), f32)
    split = jax.ShapeDtypeStruct((depth, b, N_MEM, MEM_HEADS, MEM_HD), f32)
    return pl.pallas_call(
        _memory_kv_kernel,
        grid=(depth, b),
        in_specs=[
            pl.BlockSpec((None, N_MEM, D_MODEL), lambda l, i: (i, 0, 0)),
            pl.BlockSpec((None, 1, D_MODEL), lambda l, i: (l, 0, 0)),
            pl.BlockSpec((None, D_MODEL, 2 * GROUP_W), lambda l, i: (l, 0, 0)),
            pl.BlockSpec((None, 1, MEM_HD), lambda l, i: (l, 0, 0)),
        ],
        out_specs=[pl.BlockSpec((None, None, N_MEM, GROUP_W), lambda l, i: (l, i, 0, 0))] * 2
        + [pl.BlockSpec((None, None, N_MEM, MEM_HEADS, MEM_HD), lambda l, i: (l, i, 0, 0, 0))] * 2,
        out_shape=[flat, flat, split, split],
        scratch_shapes=[pltpu.VMEM((D_MODEL, 2 * GROUP_W), bf16)],
        compiler_params=pltpu.CompilerParams(
            dimension_semantics=("arbitrary", "arbitrary"), vmem_limit_bytes=VMEM_LIMIT),
        name="memory_kv",
    )(mem, g, w, gk)


CONV_PAD = 8
N_STACK_SLOTS = 2
N_SEQ_IN_PROMPT, N_SEQ_IN_DECODE, N_PARAMS_PROMPT, N_PARAMS_DECODE, N_OUT = 3, 7, 11, 9, 5


def _mixer_kernel(*refs, tile, decode, layer, bb):
    n_seq = N_SEQ_IN_DECODE if decode else N_SEQ_IN_PROMPT
    n_par = N_PARAMS_DECODE if decode else N_PARAMS_PROMPT
    seq_in = refs[:n_seq]
    params = refs[n_seq:n_seq + n_par]
    outs = refs[n_seq + n_par:n_seq + n_par + N_OUT]
    scratch = refs[n_seq + n_par + N_OUT:]

    def view(ref, s):
        if decode and ref.ndim == 2:
            return _RowWindow(ref, s * tile, tile)
        return ref.at[s]

    if decode:
        p_ref, tail_ref, qkn_ref = seq_in[0], scratch[4], scratch[6]
        bd_ref, gsq_ref, gsk_ref = params[-3], params[-5], params[-4]
        tail_ref[...] = p_ref[:, OFF_GZ:D_IN]
        sq0, sk0 = OFF_SQ - OFF_GZ, OFF_SK - OFF_GZ
        qkn_ref[:, 0:GROUP_W] = _head_norm(tail_ref[:, sq0:sq0 + GROUP_W], gsq_ref[...], bd_ref)
        qkn_ref[:, GROUP_W:GROUP_W + LANES] = _head_norm(tail_ref[:, sk0:sk0 + LANES], gsk_ref[...],
                                                         bd_ref)

        def seg_rows(off, width):
            if off < OFF_GZ:
                return p_ref[:, off:off + width]
            return tail_ref[:, off - OFF_GZ:off - OFF_GZ + width]

        gla_in_ref, gla_out_ref = seq_in[4], outs[2]
        for _ in _gla_group(seg_rows, bb * tile, tile, params[-8:-5], outs[0],
                            lambda c: _gla_block_diag(gla_in_ref.at[c]),
                            lambda c, state: _store_gla_state(gla_out_ref.at[c], state), carry=False):
            pass
        _swa_decode_rows(seg_rows, qkn_ref, seq_in[5], seq_in[6], params[-2], layer, outs[0], bb, tile)

    stages = [_mixer_seq([view(r, s) for r in seq_in], params, [view(r, s) for r in outs],
                         [view(r, s) for r in scratch], tile=tile, decode=decode, layer=layer)
              for s in range(bb)]
    for _ in itertools.zip_longest(*stages):
        pass


class _RowWindow:
    def __init__(self, ref, start, size):
        self.ref, self.start, self.size, self.dtype = ref, start, size, ref.dtype

    def _index(self, idx):
        rows, cols = (slice(None), slice(None)) if idx is Ellipsis else idx
        lo, hi, _ = rows.indices(self.size)
        return slice(self.start + lo, self.start + hi), cols

    def __getitem__(self, idx):
        return self.ref[self._index(idx)]

    def __setitem__(self, idx, value):
        self.ref[self._index(idx)] = value


GLA_INTRA_ROWS = MXU_CHUNK
GLA_K_W = GLA_HEADS * GLA_DK
GLA_V_W = GLA_HEADS * GLA_DV


def _gla_block_diag(state_ref):
    rows = []
    for h in range(GLA_HEADS):
        blocks = [state_ref[h] if j == h else jnp.zeros((GLA_DK, GLA_DV), f32)
                  for j in range(GLA_HEADS)]
        rows.append(jnp.concatenate(blocks, axis=1))
    return jnp.concatenate(rows, axis=0)


def _store_gla_state(state_ref, state):
    for h in range(GLA_HEADS):
        state_ref[h] = state[h * GLA_DK:(h + 1) * GLA_DK, h * GLA_DV:(h + 1) * GLA_DV]


def _gla_group(seg, T, C, params, mix_ref, state_in, state_out, carry, seg_t=None):
    wup_ref, bga_ref, ggo_ref = params
    n_chunk = T // C
    G = min(T, GLA_INTRA_ROWS)
    groups = [slice(i * G, (i + 1) * G) for i in range(T // G)]
    row = lax.broadcasted_iota(jnp.int32, (G, G), 0)
    col = lax.broadcasted_iota(jnp.int32, (G, G), 1)
    causal = (row // C == col // C) & (row >= col)
    if seg_t is None:
        a_up = _dot(seg(OFF_GA, GLA_RANK).astype(bf16), wup_ref[...].astype(bf16))
    else:
        a_up = _dot_tn(seg_t(OFF_GA, GLA_RANK).astype(bf16), wup_ref[...].astype(bf16))
    log_a = _log_sigmoid(a_up + bga_ref[...]) * (1.0 / GLA_TAU)
    la3 = _split3(log_a)
    yield
    tril = jnp.where(causal, 1.0, 0.0).astype(bf16)
    in_chunk = jnp.where(lax.broadcasted_iota(jnp.int32, (T, LANES), 0) // C
                         == lax.broadcasted_iota(jnp.int32, (T, LANES), 1), 1.0, 0.0).astype(bf16)
    cum = jnp.concatenate(
        [_dot(tril, la3[0][r]) + _dot(tril, la3[1][r]) + _dot(tril, la3[2][r]) for r in groups],
        axis=0)
    tot_t = (_dot_tn(la3[0], in_chunk) + _dot_tn(la3[1], in_chunk)
             + _dot_tn(la3[2], in_chunk))
    decay_t = jnp.exp(tot_t)
    yield
    g_k = seg(OFF_GK, GLA_K_W)
    qd = ((seg(OFF_GQ, GLA_K_W) * (GLA_DK ** -0.5)) * jnp.exp(cum)).astype(bf16)
    kd = (g_k * jnp.exp(-cum)).astype(bf16)
    k_tail = jnp.concatenate(
        [g_k[c * C:(c + 1) * C] * jnp.exp(cum[(c + 1) * C - 1:(c + 1) * C] - cum[c * C:(c + 1) * C])
         for c in range(n_chunk)], axis=0) if n_chunk > 1 else g_k * jnp.exp(cum[T - 1:T] - cum)
    kt = k_tail.astype(bf16)
    yield
    v_b = seg(OFF_GV, GLA_V_W).astype(bf16)
    g_z = seg(OFF_GZ, GROUP_W)
    yield

    o_intra = []
    for r in groups:
        o_heads = []
        for h in range(GLA_HEADS):
            ks = slice(h * GLA_DK, (h + 1) * GLA_DK)
            attn = jnp.where(causal, _dot_nt(qd[r, ks], kd[r, ks]), 0.0).astype(bf16)
            o_heads.append(_dot(attn, v_b[r, h * GLA_DV:(h + 1) * GLA_DV]))
        o_intra.append(jnp.concatenate(o_heads, axis=1))
        yield
    o_intra = jnp.concatenate(o_intra, axis=0)

    shape = (GLA_K_W, GLA_V_W)
    on_diag = (lax.broadcasted_iota(jnp.int32, shape, 0) // GLA_DK
               == lax.broadcasted_iota(jnp.int32, shape, 1) // GLA_DV)
    o_chunks = []
    state = None
    for c in range(n_chunk):
        rs = slice(c * C, (c + 1) * C)
        if c == 0 or not carry:
            state = state_in(c)
        o_chunks.append(o_intra[rs] + _dot(qd[rs], state.astype(bf16)))
        update = jnp.where(on_diag, _dot_tn(kt[rs], v_b[rs]), 0.0)
        state = decay_t[:, c:c + 1] * state + update
        state_out(c, state)
        yield
    o = jnp.concatenate(o_chunks, axis=0) if n_chunk > 1 else o_chunks[0]
    for h in range(GLA_HEADS):
        vs = slice(h * GLA_DV, (h + 1) * GLA_DV)
        o_h = o[:, vs]
        o_h = o_h * lax.rsqrt(jnp.mean(o_h * o_h, axis=-1, keepdims=True) + EPS) * ggo_ref[...]
        mix_ref[:, GROUP_W + h * GLA_DV:GROUP_W + (h + 1) * GLA_DV] = (
            o_h * _silu(g_z[:, vs])).astype(mix_ref.dtype)
        yield


def _head_norm(x, g, bd_ref):
    rows, n_lanes = x.shape
    w = min(n_lanes, MXU_CHUNK)
    pieces = n_lanes // w
    bd = bd_ref[0:w, 0:w]

    def head_sums(v):
        stacked = jnp.concatenate([v[:, i * w:(i + 1) * w] for i in range(pieces)], axis=0)
        r = _dot(stacked, bd)
        return jnp.concatenate([r[i * rows:(i + 1) * rows] for i in range(pieces)], axis=1)

    sq = x * x
    hi = sq.astype(bf16)
    lo = (sq - hi.astype(f32)).astype(bf16)
    ms = (head_sums(hi) + head_sums(lo)) * (1.0 / SWA_HD)
    return x * lax.rsqrt(ms + EPS) * g


def _mixer_seq(seq_in, params, outs, scratch, *, tile, decode, layer):
    if decode:
        p_ref, mk_ref, mv_ref, conv_in_ref, gla_in_ref, kc_ref, vc_ref = seq_in
    else:
        x_ref, mk_ref, mv_ref = seq_in
        gn_ref, wt_ref = params[:2]
    (convw_ref, wup_ref, bga_ref, ggo_ref, gsq_ref, gsk_ref, bd_ref, sinks_ref,
     gmq_ref) = params[-N_PARAMS_DECODE:]
    mix_ref, conv_out_ref, gla_out_ref, kbuf_ref, vbuf_ref = outs
    ext_ref, s_ref, kprev_ref, vprev_ref, tail_ref, stk_ref, qkn_ref = scratch

    T = tile
    t = pl.program_id(1)

    def init_state():
        ext_ref[0:CONV_PAD, :] = jnp.zeros((CONV_PAD, GROUP_W), f32)
        if decode:
            ext_ref[CONV_PAD - (CONV_W - 1):CONV_PAD, :] = conv_in_ref[...]
        else:
            s_ref[...] = jnp.zeros_like(s_ref)
            kprev_ref[...] = jnp.zeros_like(kprev_ref)
            vprev_ref[...] = jnp.zeros_like(vprev_ref)

    if decode:
        init_state()
    else:
        pl.when(t == 0)(init_state)
    yield

    if decode:
        def seg(off, width):
            if off < OFF_GZ:
                return p_ref[:, off:off + width]
            return tail_ref[:, off - OFF_GZ:off - OFF_GZ + width]
    else:
        x = x_ref[...]
        hn = (x * lax.rsqrt(jnp.mean(x * x, axis=-1, keepdims=True) + EPS) * gn_ref[...]).astype(bf16)

        def seg(off, width):
            return _dot_nt(hn, wt_ref[off:off + width, :])

        def seg_t(off, width):
            return _dot_nt(wt_ref[off:off + width, :], hn)

    def group_a():
        u = seg(OFF_AC, GROUP_W) * seg(OFF_AH, GROUP_W)
        ext_ref[CONV_PAD:CONV_PAD + T, :] = u
        yield
        conv = (convw_ref[0:1, :] * ext_ref[CONV_PAD - 2:CONV_PAD - 2 + T, :]
                + convw_ref[1:2, :] * ext_ref[CONV_PAD - 1:CONV_PAD - 1 + T, :]
                + convw_ref[2:3, :] * u)
        a_b = seg(OFF_AB, GROUP_W)
        yield
        mix_ref[:, 0:GROUP_W] = (a_b * conv * _silu(seg(OFF_AZ, GROUP_W))).astype(mix_ref.dtype)
        conv_state = ext_ref[CONV_PAD + T - 2:CONV_PAD + T, :]
        ext_ref[CONV_PAD - 2:CONV_PAD, :] = conv_state
        conv_out_ref[...] = conv_state
        yield

    def group_b():
        def keep_state(c, state):
            if c == T // GLA_CHUNK - 1:
                s_ref[...] = state
                _store_gla_state(gla_out_ref, state)

        yield from _gla_group(seg, T, GLA_CHUNK, (wup_ref, bga_ref, ggo_ref), mix_ref,
                              lambda c: s_ref[...], keep_state, carry=True, seg_t=seg_t)

    def group_d():
        m_q = seg(OFF_MQ, GROUP_W)
        m_z = seg(OFF_MZ, GROUP_W)
        yield
        yield from _memory_attention(m_q, m_z, T, decode, gmq_ref, mk_ref, mv_ref, mix_ref, stk_ref)

    if decode:
        yield from group_a()
        kbuf_ref[0:WINDOW - T, :] = kc_ref[T:WINDOW, :]
        kbuf_ref[WINDOW - T:WINDOW, :] = qkn_ref[:, GROUP_W:GROUP_W + LANES]
        vbuf_ref[0:WINDOW - T, :] = vc_ref[T:WINDOW, :]
        vbuf_ref[WINDOW - T:WINDOW, :] = seg(OFF_SV, LANES)
        yield
        yield from group_d()
    else:
        group_c = _swa_prompt_tile(seg, T, t, layer, (gsq_ref, gsk_ref, bd_ref, sinks_ref), mix_ref,
                                   kprev_ref, vprev_ref, kbuf_ref, vbuf_ref)
        for _ in itertools.zip_longest(group_b(), itertools.chain(group_c, group_a(), group_d())):
            pass
        yield


def _swa_prompt_tile(seg, T, t, layer, params, mix_ref, kprev_ref, vprev_ref, kbuf_ref, vbuf_ref):
    gsq_ref, gsk_ref, bd_ref, sinks_ref = params
    s_z = seg(OFF_SZ, GROUP_W)
    s_q = seg(OFF_SQ, GROUP_W)
    s_kv = seg(OFF_SK, 2 * LANES)
    q_n = _head_norm(s_q, gsq_ref[...], bd_ref)
    k_n = _head_norm(s_kv[:, 0:LANES], gsk_ref[...], bd_ref)
    v_n = s_kv[:, LANES:2 * LANES]
    yield

    BQ = WINDOW
    n_blk = T // BQ
    stack = SWA_GROUP
    nk = WINDOW + BQ
    qi = lax.broadcasted_iota(jnp.int32, (stack * BQ, nk), 0) % BQ
    kj = lax.broadcasted_iota(jnp.int32, (stack * BQ, nk), 1)
    dist = qi + WINDOW - kj
    band = (dist >= 0) & (dist < WINDOW)
    srow = lax.broadcasted_iota(jnp.int32, (stack * BQ, 1), 0) // BQ
    for blk in range(n_blk):
        rs = slice(blk * BQ, (blk + 1) * BQ)
        if blk == 0:
            k_prev, v_prev = kprev_ref[...], vprev_ref[...]
            valid = band & ((kj >= WINDOW) | (t > 0))
        else:
            ps = slice((blk - 1) * BQ, blk * BQ)
            k_prev, v_prev = k_n[ps], v_n[ps]
            valid = band
        k_cat = jnp.concatenate([k_prev, k_n[rs]], axis=0)
        v_cat = jnp.concatenate([v_prev, v_n[rs]], axis=0)
        for g in range(SWA_KV_HEADS):
            kg = k_cat[:, g * SWA_HD:(g + 1) * SWA_HD].astype(bf16)
            vg = v_cat[:, g * SWA_HD:(g + 1) * SWA_HD].astype(bf16)
            heads = [g * SWA_GROUP + j for j in range(stack)]
            qg = jnp.concatenate([q_n[rs, hd * SWA_HD:(hd + 1) * SWA_HD] for hd in heads],
                                 axis=0).astype(bf16)
            sink = jnp.full((stack * BQ, 1), sinks_ref[layer, heads[0]], f32)
            for j in range(1, stack):
                sink = jnp.where(srow == j, sinks_ref[layer, heads[j]], sink)
            o = _attend(_dot_nt(qg, kg), valid, SWA_HD ** -0.5, vg, sink)
            for j, hd in enumerate(heads):
                z = s_z[rs, hd * SWA_HD:(hd + 1) * SWA_HD]
                mix_ref[rs, 2 * GROUP_W + hd * SWA_HD:2 * GROUP_W + (hd + 1) * SWA_HD] = (
                    o[j * BQ:(j + 1) * BQ] * _silu(z)).astype(mix_ref.dtype)
            yield

    kprev_ref[...] = k_n[T - WINDOW:T]
    vprev_ref[...] = v_n[T - WINDOW:T]
    kbuf_ref[...] = k_n[T - WINDOW:T]
    vbuf_ref[...] = v_n[T - WINDOW:T]
    yield


def _swa_decode_rows(seg, qkn_ref, kc_ref, vc_ref, sinks_ref, layer, mix_ref, bb, T):
    R = bb * T
    n_cache = bb * WINDOW
    nk = n_cache + R
    q_n, k_n = qkn_ref[:, 0:GROUP_W], qkn_ref[:, GROUP_W:GROUP_W + LANES]
    s_z = seg(OFF_SZ, GROUP_W)
    k_all = jnp.concatenate([kc_ref[s] for s in range(bb)] + [k_n], axis=0)
    v_all = jnp.concatenate([vc_ref[s] for s in range(bb)] + [seg(OFF_SV, LANES)], axis=0)
    rows = SWA_GROUP * R
    r = lax.broadcasted_iota(jnp.int32, (rows, nk), 0) % R
    c = lax.broadcasted_iota(jnp.int32, (rows, nk), 1)
    cached = c < n_cache
    key_seq = jnp.where(cached, c // WINDOW, (c - n_cache) // T)
    key_pos = jnp.where(cached, c % WINDOW, WINDOW + (c - n_cache) % T)
    dist = r % T + WINDOW - key_pos
    valid = (r // T == key_seq) & (dist >= 0) & (dist < WINDOW)
    srow = lax.broadcasted_iota(jnp.int32, (rows, 1), 0) // R
    for g in range(SWA_KV_HEADS):
        heads = [g * SWA_GROUP + j for j in range(SWA_GROUP)]
        qg = jnp.concatenate([q_n[:, hd * SWA_HD:(hd + 1) * SWA_HD] for hd in heads],
                             axis=0).astype(bf16)
        kg = k_all[:, g * SWA_HD:(g + 1) * SWA_HD].astype(bf16)
        vg = v_all[:, g * SWA_HD:(g + 1) * SWA_HD].astype(bf16)
        sink = jnp.full((rows, 1), sinks_ref[layer, heads[0]], f32)
        for j in range(1, SWA_GROUP):
            sink = jnp.where(srow == j, sinks_ref[layer, heads[j]], sink)
        o = _attend(_dot_nt(qg, kg), valid, SWA_HD ** -0.5, vg, sink)
        for j, hd in enumerate(heads):
            mix_ref[:, 2 * GROUP_W + hd * SWA_HD:2 * GROUP_W + (hd + 1) * SWA_HD] = (
                o[j * R:(j + 1) * R] * _silu(s_z[:, hd * SWA_HD:(hd + 1) * SWA_HD])).astype(mix_ref.dtype)


def _memory_attention(m_q, m_z, T, decode, gmq_ref, mk_ref, mv_ref, mix_ref, stk_ref):
    def stack_rows(pieces, slot):
        r, w = pieces[0].shape
        if r % 8 == 0:
            return jnp.concatenate(pieces, axis=0)
        for j, piece in enumerate(pieces):
            stk_ref[slot, j * r:(j + 1) * r, 0:w] = piece
        return stk_ref[slot, 0:len(pieces) * r, 0:w]

    def unstack_rows(x, n, slot):
        r, w = x.shape[0] // n, x.shape[1]
        if r % 8 == 0:
            return [x[j * r:(j + 1) * r] for j in range(n)]
        stk_ref[slot, 0:n * r, 0:w] = x
        return [stk_ref[slot, j * r:(j + 1) * r, 0:w] for j in range(n)]

    def mem_q(h):
        qh = m_q[:, h * MEM_HD:(h + 1) * MEM_HD]
        return qh * lax.rsqrt(jnp.mean(qh * qh, axis=-1, keepdims=True) + EPS) * gmq_ref[...]

    if decode:
        qs = stack_rows([mem_q(h) for h in range(MEM_HEADS)], 0).astype(bf16)
        s = _dot_nt(qs, mk_ref[...].astype(bf16))
        shape = (MEM_HEADS * T, MEM_HEADS * N_MEM)
        same_head = (lax.broadcasted_iota(jnp.int32, shape, 0) // T
                     == lax.broadcasted_iota(jnp.int32, shape, 1) % MEM_HEADS)
        yield
        o = _attend(s, same_head, MEM_HD ** -0.5, mv_ref[...].astype(bf16))
        o_all = unstack_rows(o, MEM_HEADS, 1)
    else:
        o_all = []
        for h in range(MEM_HEADS):
            hs = slice(h * MEM_HD, (h + 1) * MEM_HD)
            s = _dot_nt(mem_q(h).astype(bf16), mk_ref[:, hs].astype(bf16))
            o_all.append(_attend(s, None, MEM_HD ** -0.5, mv_ref[:, hs].astype(bf16)))
            yield
    for h in range(MEM_HEADS):
        mix_ref[:, 3 * GROUP_W + h * MEM_HD:3 * GROUP_W + (h + 1) * MEM_HD] = (
            o_all[h] * _silu(m_z[:, h * MEM_HD:(h + 1) * MEM_HD])).astype(mix_ref.dtype)
    yield


def _mixer(tokens, norm_w, mem_k, mem_v, mem_layer, state, params, layer, tile, bb, decode):
    if decode:
        b, width = state[0].shape[1], tokens.shape[1]
        L = tokens.shape[0] // b
    else:
        b, L, width = tokens.shape
    nt = L // tile
    assert nt == 1 or not decode, "a decode call covers each sequence with a single tile"
    conv_w, w_up, b_ga, g_go, g_sq, g_sk, bd, sinks, g_mq = params

    def tok(width):
        if decode:
            return pl.BlockSpec((bb * tile, width), lambda i, t: (i, 0))
        return pl.BlockSpec((bb, tile, width), lambda i, t: (i, t, 0))

    def per_seq(*shape):
        return pl.BlockSpec((bb,) + shape, lambda i, t: (i,) + (0,) * len(shape))

    def per_seq_at(lyr, *shape):
        return pl.BlockSpec((None, bb) + shape, lambda i, t: (lyr, i) + (0,) * len(shape))

    def param(a):
        return pl.BlockSpec((None,) + a.shape[1:], lambda i, t: (layer,) + (0,) * (a.ndim - 1))

    kv_w = SWA_KV_HEADS * SWA_HD
    state_shapes = [(CONV_W - 1, GROUP_W), (GLA_HEADS, GLA_DK, GLA_DV), (WINDOW, kv_w),
                    (WINDOW, kv_w)]
    in_specs = [tok(width), per_seq_at(mem_layer, *mem_k.shape[2:]),
                per_seq_at(mem_layer, *mem_v.shape[2:])]
    args = [tokens, mem_k, mem_v]
    if decode:
        in_specs += [per_seq_at(layer, *s) for s in state_shapes]
        args += list(state)
    else:
        g_n, w_t = norm_w
        in_specs += [param(g_n), pl.BlockSpec(w_t.shape, lambda i, t: (0, 0),
                                              pipeline_mode=pl.Buffered(1))]
        args += [g_n, w_t]
    in_specs += [param(conv_w), param(w_up), param(b_ga), param(g_go), param(g_sq), param(g_sk),
                 pl.BlockSpec(bd.shape, lambda i, t: (0, 0)),
                 pl.BlockSpec(memory_space=pltpu.SMEM), param(g_mq)]
    args += [conv_w, w_up, b_ga, g_go, g_sq, g_sk, bd, sinks, g_mq]
    if decode:
        out_shape = [jax.ShapeDtypeStruct((b * L, 4 * GROUP_W), f32)]
        row_scratch = [pltpu.VMEM((bb * tile, D_IN - OFF_GZ), f32),
                       pltpu.VMEM((bb * tile, GROUP_W + LANES), f32)]
    else:
        out_shape = [jax.ShapeDtypeStruct((b, L, 4 * GROUP_W), bf16)]
        row_scratch = [pltpu.VMEM((bb, 8, LANES), f32)] * 2
    out_shape += [jax.ShapeDtypeStruct((b,) + s, f32) for s in state_shapes]
    out_specs = [tok(4 * GROUP_W)] + [per_seq(*s) for s in state_shapes]
    return pl.pallas_call(
        functools.partial(_mixer_kernel, tile=tile, decode=decode, layer=layer, bb=bb),
        grid=(b // bb, nt),
        in_specs=in_specs,
        out_specs=out_specs,
        out_shape=out_shape,
        scratch_shapes=[
            pltpu.VMEM((bb, CONV_PAD + tile, GROUP_W), f32),
            pltpu.VMEM((bb, 8, LANES) if decode else (bb, GLA_K_W, GLA_V_W), f32),
            pltpu.VMEM((bb, WINDOW, kv_w), f32),
            pltpu.VMEM((bb, WINDOW, kv_w), f32),
            row_scratch[0],
            pltpu.VMEM((bb, N_STACK_SLOTS, MEM_HEADS * min(tile, 8), LANES), f32),
            row_scratch[1],
        ],
        compiler_params=pltpu.CompilerParams(
            dimension_semantics=("arbitrary", "arbitrary"), vmem_limit_bytes=VMEM_LIMIT),
        name="mixer_decode" if decode else "mixer_prompt",
    )(*args)


def _out_proj_kernel(mix_ref, w_ref, x_ref, y_ref, wb_ref):
    @pl.when(pl.program_id(1) == 0)
    def _():
        wb_ref[...] = w_ref[...].astype(bf16)

    y_ref[...] = x_ref[...] + _dot(mix_ref[...].astype(bf16), wb_ref[...])


def _out_proj(mix, w, x, l, tm, tn):
    m, k = mix.shape
    n = w.shape[2]
    return pl.pallas_call(
        _out_proj_kernel,
        grid=(n // tn, m // tm),
        in_specs=[
            pl.BlockSpec((tm, k), lambda j, i: (i, 0)),
            pl.BlockSpec((None, k, tn), lambda j, i: (l, 0, j),
                         pipeline_mode=pl.Buffered(1) if tn == n else None),
            pl.BlockSpec((tm, tn), lambda j, i: (i, j)),
        ],
        out_specs=pl.BlockSpec((tm, tn), lambda j, i: (i, j)),
        out_shape=jax.ShapeDtypeStruct((m, n), f32),
        scratch_shapes=[pltpu.VMEM((k, tn), bf16)],
        compiler_params=pltpu.CompilerParams(
            dimension_semantics=("arbitrary", "arbitrary"), vmem_limit_bytes=VMEM_LIMIT),
        name="out_proj",
    )(mix, w, x)


PROMPT_TILE = 512
DECODE_SEQS_PER_STEP = 8
PROJ_TN = 1536
OUT_TM, OUT_TN = 512, 2048

_LANE = np.arange(GROUP_W)
HEAD_BLOCK_DIAG = _LANE[:, None] // SWA_HD == _LANE[None, :] // SWA_HD


def kernel(x_prompt, x_sample, mem_prompt, state_conv, state_gla, cache_swa_k, cache_swa_v,
           cache_mem_k, cache_mem_v, g_norm, w_in, conv_w, w_gla_a_up, b_gla_a, g_gla_o,
           g_swa_q, g_swa_k, swa_sinks, g_mem, w_mem_kv, g_mem_q, g_mem_k, w_out):
    depth = w_in.shape[0]
    bp, lp, _ = x_prompt.shape
    bs, ls, _ = x_sample.shape
    hp = x_prompt.reshape(bp * lp, D_MODEL)
    hs = x_sample.reshape(bs * ls, D_MODEL)

    def row(a):
        return a[:, None, :]

    params = (conv_w, w_gla_a_up, row(b_gla_a), row(g_gla_o),
              row(jnp.tile(g_swa_q, (1, SWA_HEADS))), row(jnp.tile(g_swa_k, (1, SWA_KV_HEADS))),
              jnp.asarray(HEAD_BLOCK_DIAG, bf16), swa_sinks, row(g_mem_q))
    g_n, g_m, g_mk = row(g_norm), row(g_mem), row(g_mem_k)
    w_in_t = jnp.swapaxes(w_in, 1, 2)
    kv_w = SWA_KV_HEADS * SWA_HD
    state = (state_conv, state_gla, cache_swa_k.reshape(depth, bs, WINDOW, kv_w),
             cache_swa_v.reshape(depth, bs, WINDOW, kv_w))
    mem_k_s = cache_mem_k.reshape(depth, bs, N_MEM * MEM_HEADS, MEM_HD)
    mem_v_s = cache_mem_v.reshape(depth, bs, N_MEM * MEM_HEADS, MEM_HD)

    mk, mv, mem_k_p, mem_v_p = _memory_kv(mem_prompt, g_m, w_mem_kv, g_mk)
    outs = [[] for _ in range(8)]
    for l in range(depth):
        proj, w_bf = _norm_matmul(hs, g_n, w_in_t, l, PROJ_TN)

        mix, c, s, kb, vb = _mixer(hp.reshape(bp, lp, D_MODEL), (g_n, w_bf), mk, mv, l,
                                   None, params, l, PROMPT_TILE, 1, decode=False)
        hp = _out_proj(mix.reshape(bp * lp, 4 * GROUP_W), w_out, hp, l, OUT_TM, OUT_TN)
        for lst, a in zip(outs[:4], (
                c, s, kb.reshape(bp, WINDOW, SWA_KV_HEADS, SWA_HD),
                vb.reshape(bp, WINDOW, SWA_KV_HEADS, SWA_HD))):
            lst.append(a)

        mix, c, s, kb, vb = _mixer(proj, None, mem_k_s, mem_v_s, l, state, params, l, ls,
                                   DECODE_SEQS_PER_STEP, decode=True)
        hs = _out_proj(mix, w_out, hs, l, bs * ls, OUT_TN)
        for lst, a in zip(outs[4:], (
                c, s, kb.reshape(bs, WINDOW, SWA_KV_HEADS, SWA_HD),
                vb.reshape(bs, WINDOW, SWA_KV_HEADS, SWA_HD))):
            lst.append(a)

    stacked = [jnp.stack(o) for o in outs]
    return (hp.reshape(bp, lp, D_MODEL), hs.reshape(bs, ls, D_MODEL),
            *stacked[:4], mem_k_p, mem_v_p, *stacked[4:])
```

```python
import functools
import itertools

import jax
import jax.numpy as jnp
import numpy as np
from jax import lax
from jax.experimental import pallas as pl
from jax.experimental.pallas import tpu as pltpu

f32 = jnp.float32
bf16 = jnp.bfloat16

D_MODEL = 2048
GROUP_W = 512
GLA_HEADS = 4
GLA_DK = 64
GLA_DV = 128
GLA_RANK = 16
GLA_TAU = 16.0
GLA_CHUNK = 64
SWA_HEADS = 8
SWA_KV_HEADS = 2
SWA_HD = 64
SWA_GROUP = SWA_HEADS // SWA_KV_HEADS
WINDOW = 128
N_MEM = 256
MEM_HEADS = 4
MEM_HD = 128
CONV_W = 3
EPS = 1e-6

LANES = 128
MXU_CHUNK = 256

D_IN = 5904
OFF_AB, OFF_AC, OFF_AH, OFF_AZ = 0, 512, 1024, 1536
OFF_GQ, OFF_GK, OFF_GV, OFF_GA, OFF_GZ = 2048, 2304, 2560, 3072, 3088
OFF_SQ, OFF_SK, OFF_SV, OFF_SZ = 3600, 4112, 4240, 4368
OFF_MQ, OFF_MZ = 4880, 5392

VMEM_LIMIT = 60 * 1024 * 1024


def _dot(a, b):
    return jnp.dot(a, b, preferred_element_type=f32)


def _dot_nt(a, b):
    return lax.dot_general(a, b, (((1,), (1,)), ((), ())), preferred_element_type=f32)


def _dot_tn(a, b):
    return lax.dot_general(a, b, (((0,), (0,)), ((), ())), preferred_element_type=f32)


def _split3(x):
    hi = x.astype(bf16)
    r = x - hi.astype(f32)
    mid = r.astype(bf16)
    lo = (r - mid.astype(f32)).astype(bf16)
    return hi, mid, lo


LOG2_E = 1.4426950408889634


def _attend(scores, valid, scale, values, sink=None):
    if valid is not None:
        scores = jnp.where(valid, scores, -jnp.inf)
    m = jnp.max(scores, axis=-1, keepdims=True)
    if sink is not None:
        sink = sink * (1.0 / scale)
        m = jnp.maximum(m, sink)
    e = jnp.exp2((scores - m) * (scale * LOG2_E))
    denom = jnp.sum(e, axis=-1, keepdims=True)
    if sink is not None:
        denom = denom + jnp.exp2((sink - m) * (scale * LOG2_E))
    return _dot(e.astype(bf16), values) / denom


def _silu(x):
    return x * jax.nn.sigmoid(x)


def _log_sigmoid(x):
    return jnp.minimum(x, 0.0) - jnp.log1p(jnp.exp(-jnp.abs(x)))


def _norm_matmul_kernel(x_ref, g_ref, wt_ref, o_ref, wb_ref, hn_ref):
    x = x_ref[...]
    y = x * lax.rsqrt(jnp.mean(x * x, axis=-1, keepdims=True) + EPS)
    hn_ref[...] = (y * g_ref[...]).astype(bf16)
    wb_ref[...] = wt_ref[...].astype(bf16)
    o_ref[...] = _dot_nt(hn_ref[...], wb_ref[...])


def _norm_matmul(x, g, wt, l, tn):
    m, k = x.shape
    n = wt.shape[1]
    return pl.pallas_call(
        _norm_matmul_kernel,
        grid=(pl.cdiv(n, tn),),
        in_specs=[
            pl.BlockSpec((m, k), lambda j: (0, 0)),
            pl.BlockSpec((None, 1, k), lambda j: (l, 0, 0)),
            pl.BlockSpec((None, tn, k), lambda j: (l, j, 0)),
        ],
        out_specs=[pl.BlockSpec((m, tn), lambda j: (0, j)), pl.BlockSpec((tn, k), lambda j: (j, 0))],
        out_shape=[jax.ShapeDtypeStruct((m, n), f32), jax.ShapeDtypeStruct((n, k), bf16)],
        scratch_shapes=[pltpu.VMEM((m, k), bf16)],
        compiler_params=pltpu.CompilerParams(
            dimension_semantics=("arbitrary",), vmem_limit_bytes=VMEM_LIMIT),
        name="norm_in_proj",
    )(x, g, wt)


def _memory_kv_kernel(x_ref, g_ref, w_ref, gk_ref, k_ref, v_ref, k4_ref, v4_ref, wb_ref):
    @pl.when(pl.program_id(1) == 0)
    def _():
        wb_ref[...] = w_ref[...].astype(bf16)

    x = x_ref[...]
    y = x * lax.rsqrt(jnp.mean(x * x, axis=-1, keepdims=True) + EPS)
    kv = _dot((y * g_ref[...]).astype(bf16), wb_ref[...])
    for h in range(MEM_HEADS):
        kh = kv[:, h * MEM_HD:(h + 1) * MEM_HD]
        kh = kh * lax.rsqrt(jnp.mean(kh * kh, axis=-1, keepdims=True) + EPS) * gk_ref[...]
        vh = kv[:, GROUP_W + h * MEM_HD:GROUP_W + (h + 1) * MEM_HD]
        k_ref[:, h * MEM_HD:(h + 1) * MEM_HD] = kh
        k4_ref[:, h, :] = kh
        v4_ref[:, h, :] = vh
    v_ref[...] = kv[:, GROUP_W:]


def _memory_kv(mem, g, w, gk):
    depth, b = w.shape[0], mem.shape[0]
    flat = jax.ShapeDtypeStruct((depth, b, N_MEM, GROUP_W), f32)
    split = jax.ShapeDtypeStruct((depth, b, N_MEM, MEM_HEADS, MEM_HD), f32)
    return pl.pallas_call(
        _memory_kv_kernel,
        grid=(depth, b),
        in_specs=[
            pl.BlockSpec((None, N_MEM, D_MODEL), lambda l, i: (i, 0, 0)),
            pl.BlockSpec((None, 1, D_MODEL), lambda l, i: (l, 0, 0)),
            pl.BlockSpec((None, D_MODEL, 2 * GROUP_W), lambda l, i: (l, 0, 0)),
            pl.BlockSpec((None, 1, MEM_HD), lambda l, i: (l, 0, 0)),
        ],
        out_specs=[pl.BlockSpec((None, None, N_MEM, GROUP_W), lambda l, i: (l, i, 0, 0))] * 2
        + [pl.BlockSpec((None, None, N_MEM, MEM_HEADS, MEM_HD), lambda l, i: (l, i, 0, 0, 0))] * 2,
        out_shape=[flat, flat, split, split],
        scratch_shapes=[pltpu.VMEM((D_MODEL, 2 * GROUP_W), bf16)],
        compiler_params=pltpu.CompilerParams(
            dimension_semantics=("arbitrary", "arbitrary"), vmem_limit_bytes=VMEM_LIMIT),
        name="memory_kv",
    )(mem, g, w, gk)


CONV_PAD = 8
N_STACK_SLOTS = 2
N_SEQ_IN_PROMPT, N_SEQ_IN_DECODE, N_PARAMS_PROMPT, N_PARAMS_DECODE, N_OUT = 3, 7, 11, 9, 5


def _mixer_kernel(*refs, tile, decode, layer, bb):
    n_seq = N_SEQ_IN_DECODE if decode else N_SEQ_IN_PROMPT
    n_par = N_PARAMS_DECODE if decode else N_PARAMS_PROMPT
    seq_in = refs[:n_seq]
    params = refs[n_seq:n_seq + n_par]
    outs = refs[n_seq + n_par:n_seq + n_par + N_OUT]
    scratch = refs[n_seq + n_par + N_OUT:]

    def view(ref, s):
        if decode and ref.ndim == 2:
            return _RowWindow(ref, s * tile, tile)
        return ref.at[s]

    if decode:
        p_ref, tail_ref, qkn_ref = seq_in[0], scratch[4], scratch[6]
        bd_ref, gsq_ref, gsk_ref = params[-3], params[-5], params[-4]
        tail_ref[...] = p_ref[:, OFF_GZ:D_IN]
        sq0, sk0 = OFF_SQ - OFF_GZ, OFF_SK - OFF_GZ
        qkn_ref[:, 0:GROUP_W] = _head_norm(tail_ref[:, sq0:sq0 + GROUP_W], gsq_ref[...], bd_ref)
        qkn_ref[:, GROUP_W:GROUP_W + LANES] = _head_norm(tail_ref[:, sk0:sk0 + LANES], gsk_ref[...],
                                                         bd_ref)

        def seg_rows(off, width):
            if off < OFF_GZ:
                return p_ref[:, off:off + width]
            return tail_ref[:, off - OFF_GZ:off - OFF_GZ + width]

        gla_in_ref, gla_out_ref = seq_in[4], outs[2]
        for _ in _gla_group(seg_rows, bb * tile, tile, params[-8:-5], outs[0],
                            lambda c: _gla_block_diag(gla_in_ref.at[c]),
                            lambda c, state: _store_gla_state(gla_out_ref.at[c], state), carry=False):
            pass
        _swa_decode_rows(seg_rows, qkn_ref, seq_in[5], seq_in[6], params[-2], layer, outs[0], bb, tile)

    stages = [_mixer_seq([view(r, s) for r in seq_in], params, [view(r, s) for r in outs],
                         [view(r, s) for r in scratch], tile=tile, decode=decode, layer=layer)
              for s in range(bb)]
    for _ in itertools.zip_longest(*stages):
        pass


class _RowWindow:
    def __init__(self, ref, start, size):
        self.ref, self.start, self.size, self.dtype = ref, start, size, ref.dtype

    def _index(self, idx):
        rows, cols = (slice(None), slice(None)) if idx is Ellipsis else idx
        lo, hi, _ = rows.indices(self.size)
        return slice(self.start + lo, self.start + hi), cols

    def __getitem__(self, idx):
        return self.ref[self._index(idx)]

    def __setitem__(self, idx, value):
        self.ref[self._index(idx)] = value


GLA_INTRA_ROWS = MXU_CHUNK
GLA_K_W = GLA_HEADS * GLA_DK
GLA_V_W = GLA_HEADS * GLA_DV


def _gla_block_diag(state_ref):
    rows = []
    for h in range(GLA_HEADS):
        blocks = [state_ref[h] if j == h else jnp.zeros((GLA_DK, GLA_DV), f32)
                  for j in range(GLA_HEADS)]
        rows.append(jnp.concatenate(blocks, axis=1))
    return jnp.concatenate(rows, axis=0)


def _store_gla_state(state_ref, state):
    for h in range(GLA_HEADS):
        state_ref[h] = state[h * GLA_DK:(h + 1) * GLA_DK, h * GLA_DV:(h + 1) * GLA_DV]


def _gla_group(seg, T, C, params, mix_ref, state_in, state_out, carry, seg_t=None):
    wup_ref, bga_ref, ggo_ref = params
    n_chunk = T // C
    G = min(T, GLA_INTRA_ROWS)
    groups = [slice(i * G, (i + 1) * G) for i in range(T // G)]
    row = lax.broadcasted_iota(jnp.int32, (G, G), 0)
    col = lax.broadcasted_iota(jnp.int32, (G, G), 1)
    causal = (row // C == col // C) & (row >= col)
    if seg_t is None:
        a_up = _dot(seg(OFF_GA, GLA_RANK).astype(bf16), wup_ref[...].astype(bf16))
    else:
        a_up = _dot_tn(seg_t(OFF_GA, GLA_RANK).astype(bf16), wup_ref[...].astype(bf16))
    log_a = _log_sigmoid(a_up + bga_ref[...]) * (1.0 / GLA_TAU)
    la3 = _split3(log_a)
    yield
    tril = jnp.where(causal, 1.0, 0.0).astype(bf16)
    in_chunk = jnp.where(lax.broadcasted_iota(jnp.int32, (T, LANES), 0) // C
                         == lax.broadcasted_iota(jnp.int32, (T, LANES), 1), 1.0, 0.0).astype(bf16)
    cum = jnp.concatenate(
        [_dot(tril, la3[0][r]) + _dot(tril, la3[1][r]) + _dot(tril, la3[2][r]) for r in groups],
        axis=0)
    tot_t = (_dot_tn(la3[0], in_chunk) + _dot_tn(la3[1], in_chunk)
             + _dot_tn(la3[2], in_chunk))
    decay_t = jnp.exp(tot_t)
    yield
    g_k = seg(OFF_GK, GLA_K_W)
    qd = ((seg(OFF_GQ, GLA_K_W) * (GLA_DK ** -0.5)) * jnp.exp(cum)).astype(bf16)
    kd = (g_k * jnp.exp(-cum)).astype(bf16)
    k_tail = jnp.concatenate(
        [g_k[c * C:(c + 1) * C] * jnp.exp(cum[(c + 1) * C - 1:(c + 1) * C] - cum[c * C:(c + 1) * C])
         for c in range(n_chunk)], axis=0) if n_chunk > 1 else g_k * jnp.exp(cum[T - 1:T] - cum)
    kt = k_tail.astype(bf16)
    yield
    v_b = seg(OFF_GV, GLA_V_W).astype(bf16)
    g_z = seg(OFF_GZ, GROUP_W)
    yield

    o_intra = []
    for r in groups:
        o_heads = []
        for h in range(GLA_HEADS):
            ks = slice(h * GLA_DK, (h + 1) * GLA_DK)
            attn = jnp.where(causal, _dot_nt(qd[r, ks], kd[r, ks]), 0.0).astype(bf16)
            o_heads.append(_dot(attn, v_b[r, h * GLA_DV:(h + 1) * GLA_DV]))
        o_intra.append(jnp.concatenate(o_heads, axis=1))
        yield
    o_intra = jnp.concatenate(o_intra, axis=0)

    shape = (GLA_K_W, GLA_V_W)
    on_diag = (lax.broadcasted_iota(jnp.int32, shape, 0) // GLA_DK
               == lax.broadcasted_iota(jnp.int32, shape, 1) // GLA_DV)
    o_chunks = []
    state = None
    for c in range(n_chunk):
        rs = slice(c * C, (c + 1) * C)
        if c == 0 or not carry:
            state = state_in(c)
        o_chunks.append(o_intra[rs] + _dot(qd[rs], state.astype(bf16)))
        update = jnp.where(on_diag, _dot_tn(kt[rs], v_b[rs]), 0.0)
        state = decay_t[:, c:c + 1] * state + update
        state_out(c, state)
        yield
    o = jnp.concatenate(o_chunks, axis=0) if n_chunk > 1 else o_chunks[0]
    for h in range(GLA_HEADS):
        vs = slice(h * GLA_DV, (h + 1) * GLA_DV)
        o_h = o[:, vs]
        o_h = o_h * lax.rsqrt(jnp.mean(o_h * o_h, axis=-1, keepdims=True) + EPS) * ggo_ref[...]
        mix_ref[:, GROUP_W + h * GLA_DV:GROUP_W + (h + 1) * GLA_DV] = (
            o_h * _silu(g_z[:, vs])).astype(mix_ref.dtype)
        yield


def _head_norm(x, g, bd_ref):
    rows, n_lanes = x.shape
    w = min(n_lanes, MXU_CHUNK)
    pieces = n_lanes // w
    bd = bd_ref[0:w, 0:w]

    def head_sums(v):
        stacked = jnp.concatenate([v[:, i * w:(i + 1) * w] for i in range(pieces)], axis=0)
        r = _dot(stacked, bd)
        return jnp.concatenate([r[i * rows:(i + 1) * rows] for i in range(pieces)], axis=1)

    sq = x * x
    hi = sq.astype(bf16)
    lo = (sq - hi.astype(f32)).astype(bf16)
    ms = (head_sums(hi) + head_sums(lo)) * (1.0 / SWA_HD)
    return x * lax.rsqrt(ms + EPS) * g


def _mixer_seq(seq_in, params, outs, scratch, *, tile, decode, layer):
    if decode:
        p_ref, mk_ref, mv_ref, conv_in_ref, gla_in_ref, kc_ref, vc_ref = seq_in
    else:
        x_ref, mk_ref, mv_ref = seq_in
        gn_ref, wt_ref = params[:2]
    (convw_ref, wup_ref, bga_ref, ggo_ref, gsq_ref, gsk_ref, bd_ref, sinks_ref,
     gmq_ref) = params[-N_PARAMS_DECODE:]
    mix_ref, conv_out_ref, gla_out_ref, kbuf_ref, vbuf_ref = outs
    ext_ref, s_ref, kprev_ref, vprev_ref, tail_ref, stk_ref, qkn_ref = scratch

    T = tile
    t = pl.program_id(1)

    def init_state():
        ext_ref[0:CONV_PAD, :] = jnp.zeros((CONV_PAD, GROUP_W), f32)
        if decode:
            ext_ref[CONV_PAD - (CONV_W - 1):CONV_PAD, :] = conv_in_ref[...]
        else:
            s_ref[...] = jnp.zeros_like(s_ref)
            kprev_ref[...] = jnp.zeros_like(kprev_ref)
            vprev_ref[...] = jnp.zeros_like(vprev_ref)

    if decode:
        init_state()
    else:
        pl.when(t == 0)(init_state)
    yield

    if decode:
        def seg(off, width):
            if off < OFF_GZ:
                return p_ref[:, off:off + width]
            return tail_ref[:, off - OFF_GZ:off - OFF_GZ + width]
    else:
        x = x_ref[...]
        hn = (x * lax.rsqrt(jnp.mean(x * x, axis=-1, keepdims=True) + EPS) * gn_ref[...]).astype(bf16)

        def seg(off, width):
            return _dot_nt(hn, wt_ref[off:off + width, :])

        def seg_t(off, width):
            return _dot_nt(wt_ref[off:off + width, :], hn)

    def group_a():
        u = seg(OFF_AC, GROUP_W) * seg(OFF_AH, GROUP_W)
        ext_ref[CONV_PAD:CONV_PAD + T, :] = u
        yield
        conv = (convw_ref[0:1, :] * ext_ref[CONV_PAD - 2:CONV_PAD - 2 + T, :]
                + convw_ref[1:2, :] * ext_ref[CONV_PAD - 1:CONV_PAD - 1 + T, :]
                + convw_ref[2:3, :] * u)
        a_b = seg(OFF_AB, GROUP_W)
        yield
        mix_ref[:, 0:GROUP_W] = (a_b * conv * _silu(seg(OFF_AZ, GROUP_W))).astype(mix_ref.dtype)
        conv_state = ext_ref[CONV_PAD + T - 2:CONV_PAD + T, :]
        ext_ref[CONV_PAD - 2:CONV_PAD, :] = conv_state
        conv_out_ref[...] = conv_state
        yield

    def group_b():
        def keep_state(c, state):
            if c == T // GLA_CHUNK - 1:
                s_ref[...] = state
                _store_gla_state(gla_out_ref, state)

        yield from _gla_group(seg, T, GLA_CHUNK, (wup_ref, bga_ref, ggo_ref), mix_ref,
                              lambda c: s_ref[...], keep_state, carry=True, seg_t=seg_t)

    def group_d():
        m_q = seg(OFF_MQ, GROUP_W)
        m_z = seg(OFF_MZ, GROUP_W)
        yield
        yield from _memory_attention(m_q, m_z, T, decode, gmq_ref, mk_ref, mv_ref, mix_ref, stk_ref)

    if decode:
        yield from group_a()
        kbuf_ref[0:WINDOW - T, :] = kc_ref[T:WINDOW, :]
        kbuf_ref[WINDOW - T:WINDOW, :] = qkn_ref[:, GROUP_W:GROUP_W + LANES]
        vbuf_ref[0:WINDOW - T, :] = vc_ref[T:WINDOW, :]
        vbuf_ref[WINDOW - T:WINDOW, :] = seg(OFF_SV, LANES)
        yield
        yield from group_d()
    else:
        group_c = _swa_prompt_tile(seg, T, t, layer, (gsq_ref, gsk_ref, bd_ref, sinks_ref), mix_ref,
                                   kprev_ref, vprev_ref, kbuf_ref, vbuf_ref)
        for _ in itertools.zip_longest(group_b(), itertools.chain(group_c, group_a(), group_d())):
            pass
        yield


def _swa_prompt_tile(seg, T, t, layer, params, mix_ref, kprev_ref, vprev_ref, kbuf_ref, vbuf_ref):
    gsq_ref, gsk_ref, bd_ref, sinks_ref = params
    s_z = seg(OFF_SZ, GROUP_W)
    s_q = seg(OFF_SQ, GROUP_W)
    s_kv = seg(OFF_SK, 2 * LANES)
    q_n = _head_norm(s_q, gsq_ref[...], bd_ref)
    k_n = _head_norm(s_kv[:, 0:LANES], gsk_ref[...], bd_ref)
    v_n = s_kv[:, LANES:2 * LANES]
    yield

    BQ = WINDOW
    n_blk = T // BQ
    stack = SWA_GROUP
    nk = WINDOW + BQ
    qi = lax.broadcasted_iota(jnp.int32, (stack * BQ, nk), 0) % BQ
    kj = lax.broadcasted_iota(jnp.int32, (stack * BQ, nk), 1)
    dist = qi + WINDOW - kj
    band = (dist >= 0) & (dist < WINDOW)
    srow = lax.broadcasted_iota(jnp.int32, (stack * BQ, 1), 0) // BQ
    for blk in range(n_blk):
        rs = slice(blk * BQ, (blk + 1) * BQ)
        if blk == 0:
            k_prev, v_prev = kprev_ref[...], vprev_ref[...]
            valid = band & ((kj >= WINDOW) | (t > 0))
        else:
            ps = slice((blk - 1) * BQ, blk * BQ)
            k_prev, v_prev = k_n[ps], v_n[ps]
            valid = band
        k_cat = jnp.concatenate([k_prev, k_n[rs]], axis=0)
        v_cat = jnp.concatenate([v_prev, v_n[rs]], axis=0)
        for g in range(SWA_KV_HEADS):
            kg = k_cat[:, g * SWA_HD:(g + 1) * SWA_HD].astype(bf16)
            vg = v_cat[:, g * SWA_HD:(g + 1) * SWA_HD].astype(bf16)
            heads = [g * SWA_GROUP + j for j in range(stack)]
            qg = jnp.concatenate([q_n[rs, hd * SWA_HD:(hd + 1) * SWA_HD] for hd in heads],
                                 axis=0).astype(bf16)
            sink = jnp.full((stack * BQ, 1), sinks_ref[layer, heads[0]], f32)
            for j in range(1, stack):
                sink = jnp.where(srow == j, sinks_ref[layer, heads[j]], sink)
            o = _attend(_dot_nt(qg, kg), valid, SWA_HD ** -0.5, vg, sink)
            for j, hd in enumerate(heads):
                z = s_z[rs, hd * SWA_HD:(hd + 1) * SWA_HD]
                mix_ref[rs, 2 * GROUP_W + hd * SWA_HD:2 * GROUP_W + (hd + 1) * SWA_HD] = (
                    o[j * BQ:(j + 1) * BQ] * _silu(z)).astype(mix_ref.dtype)
            yield

    kprev_ref[...] = k_n[T - WINDOW:T]
    vprev_ref[...] = v_n[T - WINDOW:T]
    kbuf_ref[...] = k_n[T - WINDOW:T]
    vbuf_ref[...] = v_n[T - WINDOW:T]
    yield


def _swa_decode_rows(seg, qkn_ref, kc_ref, vc_ref, sinks_ref, layer, mix_ref, bb, T):
    R = bb * T
    n_cache = bb * WINDOW
    nk = n_cache + R
    q_n, k_n = qkn_ref[:, 0:GROUP_W], qkn_ref[:, GROUP_W:GROUP_W + LANES]
    s_z = seg(OFF_SZ, GROUP_W)
    k_all = jnp.concatenate([kc_ref[s] for s in range(bb)] + [k_n], axis=0)
    v_all = jnp.concatenate([vc_ref[s] for s in range(bb)] + [seg(OFF_SV, LANES)], axis=0)
    rows = SWA_GROUP * R
    r = lax.broadcasted_iota(jnp.int32, (rows, nk), 0) % R
    c = lax.broadcasted_iota(jnp.int32, (rows, nk), 1)
    cached = c < n_cache
    key_seq = jnp.where(cached, c // WINDOW, (c - n_cache) // T)
    key_pos = jnp.where(cached, c % WINDOW, WINDOW + (c - n_cache) % T)
    dist = r % T + WINDOW - key_pos
    valid = (r // T == key_seq) & (dist >= 0) & (dist < WINDOW)
    srow = lax.broadcasted_iota(jnp.int32, (rows, 1), 0) // R
    for g in range(SWA_KV_HEADS):
        heads = [g * SWA_GROUP + j for j in range(SWA_GROUP)]
        qg = jnp.concatenate([q_n[:, hd * SWA_HD:(hd + 1) * SWA_HD] for hd in heads],
                             axis=0).astype(bf16)
        kg = k_all[:, g * SWA_HD:(g + 1) * SWA_HD].astype(bf16)
        vg = v_all[:, g * SWA_HD:(g + 1) * SWA_HD].astype(bf16)
        sink = jnp.full((rows, 1), sinks_ref[layer, heads[0]], f32)
        for j in range(1, SWA_GROUP):
            sink = jnp.where(srow == j, sinks_ref[layer, heads[j]], sink)
        o = _attend(_dot_nt(qg, kg), valid, SWA_HD ** -0.5, vg, sink)
        for j, hd in enumerate(heads):
            mix_ref[:, 2 * GROUP_W + hd * SWA_HD:2 * GROUP_W + (hd + 1) * SWA_HD] = (
                o[j * R:(j + 1) * R] * _silu(s_z[:, hd * SWA_HD:(hd + 1) * SWA_HD])).astype(mix_ref.dtype)


def _memory_attention(m_q, m_z, T, decode, gmq_ref, mk_ref, mv_ref, mix_ref, stk_ref):
    def stack_rows(pieces, slot):
        r, w = pieces[0].shape
        if r % 8 == 0:
            return jnp.concatenate(pieces, axis=0)
        for j, piece in enumerate(pieces):
            stk_ref[slot, j * r:(j + 1) * r, 0:w] = piece
        return stk_ref[slot, 0:len(pieces) * r, 0:w]

    def unstack_rows(x, n, slot):
        r, w = x.shape[0] // n, x.shape[1]
        if r % 8 == 0:
            return [x[j * r:(j + 1) * r] for j in range(n)]
        stk_ref[slot, 0:n * r, 0:w] = x
        return [stk_ref[slot, j * r:(j + 1) * r, 0:w] for j in range(n)]

    def mem_q(h):
        qh = m_q[:, h * MEM_HD:(h + 1) * MEM_HD]
        return qh * lax.rsqrt(jnp.mean(qh * qh, axis=-1, keepdims=True) + EPS) * gmq_ref[...]

    if decode:
        qs = stack_rows([mem_q(h) for h in range(MEM_HEADS)], 0).astype(bf16)
        s = _dot_nt(qs, mk_ref[...].astype(bf16))
        shape = (MEM_HEADS * T, MEM_HEADS * N_MEM)
        same_head = (lax.broadcasted_iota(jnp.int32, shape, 0) // T
                     == lax.broadcasted_iota(jnp.int32, shape, 1) % MEM_HEADS)
        yield
        o = _attend(s, same_head, MEM_HD ** -0.5, mv_ref[...].astype(bf16))
        o_all = unstack_rows(o, MEM_HEADS, 1)
    else:
        o_all = []
        for h in range(MEM_HEADS):
            hs = slice(h * MEM_HD, (h + 1) * MEM_HD)
            s = _dot_nt(mem_q(h).astype(bf16), mk_ref[:, hs].astype(bf16))
            o_all.append(_attend(s, None, MEM_HD ** -0.5, mv_ref[:, hs].astype(bf16)))
            yield
    for h in range(MEM_HEADS):
        mix_ref[:, 3 * GROUP_W + h * MEM_HD:3 * GROUP_W + (h + 1) * MEM_HD] = (
            o_all[h] * _silu(m_z[:, h * MEM_HD:(h + 1) * MEM_HD])).astype(mix_ref.dtype)
    yield


def _mixer(tokens, norm_w, mem_k, mem_v, mem_layer, state, params, layer, tile, bb, decode):
    if decode:
        b, width = state[0].shape[1], tokens.shape[1]
        L = tokens.shape[0] // b
    else:
        b, L, width = tokens.shape
    nt = L // tile
    assert nt == 1 or not decode, "a decode call covers each sequence with a single tile"
    conv_w, w_up, b_ga, g_go, g_sq, g_sk, bd, sinks, g_mq = params

    def tok(width):
        if decode:
            return pl.BlockSpec((bb * tile, width), lambda i, t: (i, 0))
        return pl.BlockSpec((bb, tile, width), lambda i, t: (i, t, 0))

    def per_seq(*shape):
        return pl.BlockSpec((bb,) + shape, lambda i, t: (i,) + (0,) * len(shape))

    def per_seq_at(lyr, *shape):
        return pl.BlockSpec((None, bb) + shape, lambda i, t: (lyr, i) + (0,) * len(shape))

    def param(a):
        return pl.BlockSpec((None,) + a.shape[1:], lambda i, t: (layer,) + (0,) * (a.ndim - 1))

    kv_w = SWA_KV_HEADS * SWA_HD
    state_shapes = [(CONV_W - 1, GROUP_W), (GLA_HEADS, GLA_DK, GLA_DV), (WINDOW, kv_w),
                    (WINDOW, kv_w)]
    in_specs = [tok(width), per_seq_at(mem_layer, *mem_k.shape[2:]),
                per_seq_at(mem_layer, *mem_v.shape[2:])]
    args = [tokens, mem_k, mem_v]
    if decode:
        in_specs += [per_seq_at(layer, *s) for s in state_shapes]
        args += list(state)
    else:
        g_n, w_t = norm_w
        in_specs += [param(g_n), pl.BlockSpec(w_t.shape, lambda i, t: (0, 0),
                                              pipeline_mode=pl.Buffered(1))]
        args += [g_n, w_t]
    in_specs += [param(conv_w), param(w_up), param(b_ga), param(g_go), param(g_sq), param(g_sk),
                 pl.BlockSpec(bd.shape, lambda i, t: (0, 0)),
                 pl.BlockSpec(memory_space=pltpu.SMEM), param(g_mq)]
    args += [conv_w, w_up, b_ga, g_go, g_sq, g_sk, bd, sinks, g_mq]
    conv_window = pltpu.VMEM((bb, CONV_PAD + tile, GROUP_W), f32)
    unused = pltpu.VMEM((bb, 8, LANES), f32)
    if decode:
        out_shape = [jax.ShapeDtypeStruct((b * L, 4 * GROUP_W), f32)]
        scratch_shapes = [conv_window, unused, unused, unused,
                          pltpu.VMEM((bb * tile, D_IN - OFF_GZ), f32),
                          pltpu.VMEM((bb, N_STACK_SLOTS, MEM_HEADS * tile, LANES), f32),
                          pltpu.VMEM((bb * tile, GROUP_W + LANES), f32)]
    else:
        out_shape = [jax.ShapeDtypeStruct((b, L, 4 * GROUP_W), bf16)]
        scratch_shapes = [conv_window, pltpu.VMEM((bb, GLA_K_W, GLA_V_W), f32),
                          pltpu.VMEM((bb, WINDOW, kv_w), f32), pltpu.VMEM((bb, WINDOW, kv_w), f32),
                          unused, unused, unused]
    out_shape += [jax.ShapeDtypeStruct((b,) + s, f32) for s in state_shapes]
    out_specs = [tok(4 * GROUP_W)] + [per_seq(*s) for s in state_shapes]
    return pl.pallas_call(
        functools.partial(_mixer_kernel, tile=tile, decode=decode, layer=layer, bb=bb),
        grid=(b // bb, nt),
        in_specs=in_specs,
        out_specs=out_specs,
        out_shape=out_shape,
        scratch_shapes=scratch_shapes,
        compiler_params=pltpu.CompilerParams(
            dimension_semantics=("arbitrary", "arbitrary"), vmem_limit_bytes=VMEM_LIMIT),
        name="mixer_decode" if decode else "mixer_prompt",
    )(*args)


def _out_proj_kernel(mix_ref, w_ref, x_ref, y_ref, wb_ref):
    @pl.when(pl.program_id(1) == 0)
    def _():
        wb_ref[...] = w_ref[...].astype(bf16)

    y_ref[...] = x_ref[...] + _dot(mix_ref[...].astype(bf16), wb_ref[...])


def _out_proj(mix, w, x, l, tm, tn):
    m, k = mix.shape
    n = w.shape[2]
    return pl.pallas_call(
        _out_proj_kernel,
        grid=(n // tn, m // tm),
        in_specs=[
            pl.BlockSpec((tm, k), lambda j, i: (i, 0)),
            pl.BlockSpec((None, k, tn), lambda j, i: (l, 0, j),
                         pipeline_mode=pl.Buffered(1) if tn == n else None),
            pl.BlockSpec((tm, tn), lambda j, i: (i, j)),
        ],
        out_specs=pl.BlockSpec((tm, tn), lambda j, i: (i, j)),
        out_shape=jax.ShapeDtypeStruct((m, n), f32),
        scratch_shapes=[pltpu.VMEM((k, tn), bf16)],
        compiler_params=pltpu.CompilerParams(
            dimension_semantics=("arbitrary", "arbitrary"), vmem_limit_bytes=VMEM_LIMIT),
        name="out_proj",
    )(mix, w, x)


PROMPT_TILE = 512
DECODE_SEQS_PER_STEP = 8
PROJ_TN = 1536
OUT_TM, OUT_TN = 512, 2048

_LANE = np.arange(GROUP_W)
HEAD_BLOCK_DIAG = _LANE[:, None] // SWA_HD == _LANE[None, :] // SWA_HD


def kernel(x_prompt, x_sample, mem_prompt, state_conv, state_gla, cache_swa_k, cache_swa_v,
           cache_mem_k, cache_mem_v, g_norm, w_in, conv_w, w_gla_a_up, b_gla_a, g_gla_o,
           g_swa_q, g_swa_k, swa_sinks, g_mem, w_mem_kv, g_mem_q, g_mem_k, w_out):
    depth = w_in.shape[0]
    bp, lp, _ = x_prompt.shape
    bs, ls, _ = x_sample.shape
    hp = x_prompt.reshape(bp * lp, D_MODEL)
    hs = x_sample.reshape(bs * ls, D_MODEL)

    def row(a):
        return a[:, None, :]

    params = (conv_w, w_gla_a_up, row(b_gla_a), row(g_gla_o),
              row(jnp.tile(g_swa_q, (1, SWA_HEADS))), row(jnp.tile(g_swa_k, (1, SWA_KV_HEADS))),
              jnp.asarray(HEAD_BLOCK_DIAG, bf16), swa_sinks, row(g_mem_q))
    g_n, g_m, g_mk = row(g_norm), row(g_mem), row(g_mem_k)
    w_in_t = jnp.swapaxes(w_in, 1, 2)
    kv_w = SWA_KV_HEADS * SWA_HD
    state = (state_conv, state_gla, cache_swa_k.reshape(depth, bs, WINDOW, kv_w),
             cache_swa_v.reshape(depth, bs, WINDOW, kv_w))
    mem_k_s = cache_mem_k.reshape(depth, bs, N_MEM * MEM_HEADS, MEM_HD)
    mem_v_s = cache_mem_v.reshape(depth, bs, N_MEM * MEM_HEADS, MEM_HD)

    mk, mv, mem_k_p, mem_v_p = _memory_kv(mem_prompt, g_m, w_mem_kv, g_mk)
    outs = [[] for _ in range(8)]
    for l in range(depth):
        proj, w_bf = _norm_matmul(hs, g_n, w_in_t, l, PROJ_TN)

        mix, c, s, kb, vb = _mixer(hp.reshape(bp, lp, D_MODEL), (g_n, w_bf), mk, mv, l,
                                   None, params, l, PROMPT_TILE, 1, decode=False)
        hp = _out_proj(mix.reshape(bp * lp, 4 * GROUP_W), w_out, hp, l, OUT_TM, OUT_TN)
        for lst, a in zip(outs[:4], (
                c, s, kb.reshape(bp, WINDOW, SWA_KV_HEADS, SWA_HD),
                vb.reshape(bp, WINDOW, SWA_KV_HEADS, SWA_HD))):
            lst.append(a)

        mix, c, s, kb, vb = _mixer(proj, None, mem_k_s, mem_v_s, l, state, params, l, ls,
                                   DECODE_SEQS_PER_STEP, decode=True)
        hs = _out_proj(mix, w_out, hs, l, bs * ls, OUT_TN)
        for lst, a in zip(outs[4:], (
                c, s, kb.reshape(bs, WINDOW, SWA_KV_HEADS, SWA_HD),
                vb.reshape(bs, WINDOW, SWA_KV_HEADS, SWA_HD))):
            lst.append(a)

    stacked = [jnp.stack(o) for o in outs]
    return (hp.reshape(bp, lp, D_MODEL), hs.reshape(bs, ls, D_MODEL),
            *stacked[:4], mem_k_p, mem_v_p, *stacked[4:])
```

```python
import functools
import itertools

import jax
import jax.numpy as jnp
import numpy as np
from jax import lax
from jax.experimental import pallas as pl
from jax.experimental.pallas import tpu as pltpu

f32 = jnp.float32
bf16 = jnp.bfloat16

D_MODEL = 2048
GROUP_W = 512
GLA_HEADS = 4
GLA_DK = 64
GLA_DV = 128
GLA_RANK = 16
GLA_TAU = 16.0
GLA_CHUNK = 64
SWA_HEADS = 8
SWA_KV_HEADS = 2
SWA_HD = 64
SWA_GROUP = SWA_HEADS // SWA_KV_HEADS
WINDOW = 128
N_MEM = 256
MEM_HEADS = 4
MEM_HD = 128
CONV_W = 3
EPS = 1e-6

LANES = 128
MXU_CHUNK = 256

D_IN = 5904
OFF_AB, OFF_AC, OFF_AH, OFF_AZ = 0, 512, 1024, 1536
OFF_GQ, OFF_GK, OFF_GV, OFF_GA, OFF_GZ = 2048, 2304, 2560, 3072, 3088
OFF_SQ, OFF_SK, OFF_SV, OFF_SZ = 3600, 4112, 4240, 4368
OFF_MQ, OFF_MZ = 4880, 5392

VMEM_LIMIT = 60 * 1024 * 1024


def _dot(a, b):
    return jnp.dot(a, b, preferred_element_type=f32)


def _dot_nt(a, b):
    return lax.dot_general(a, b, (((1,), (1,)), ((), ())), preferred_element_type=f32)


def _dot_tn(a, b):
    return lax.dot_general(a, b, (((0,), (0,)), ((), ())), preferred_element_type=f32)


def _split3(x):
    hi = x.astype(bf16)
    r = x - hi.astype(f32)
    mid = r.astype(bf16)
    lo = (r - mid.astype(f32)).astype(bf16)
    return hi, mid, lo


LOG2_E = 1.4426950408889634


def _attend(scores, valid, scale, values, sink=None):
    if valid is not None:
        scores = jnp.where(valid, scores, -jnp.inf)
    m = jnp.max(scores, axis=-1, keepdims=True)
    if sink is not None:
        sink = sink * (1.0 / scale)
        m = jnp.maximum(m, sink)
    e = jnp.exp2((scores - m) * (scale * LOG2_E))
    denom = jnp.sum(e, axis=-1, keepdims=True)
    if sink is not None:
        denom = denom + jnp.exp2((sink - m) * (scale * LOG2_E))
    return _dot(e.astype(bf16), values) / denom


def _silu(x):
    return x * jax.nn.sigmoid(x)


def _log_sigmoid(x):
    return jnp.minimum(x, 0.0) - jnp.log1p(jnp.exp(-jnp.abs(x)))


def _norm_matmul_kernel(x_ref, g_ref, wt_ref, o_ref, wb_ref, hn_ref):
    x = x_ref[...]
    y = x * lax.rsqrt(jnp.mean(x * x, axis=-1, keepdims=True) + EPS)
    hn_ref[...] = (y * g_ref[...]).astype(bf16)
    wb_ref[...] = wt_ref[...].astype(bf16)
    o_ref[...] = _dot_nt(hn_ref[...], wb_ref[...])


def _norm_matmul(x, g, wt, l, tn):
    m, k = x.shape
    n = wt.shape[1]
    return pl.pallas_call(
        _norm_matmul_kernel,
        grid=(pl.cdiv(n, tn),),
        in_specs=[
            pl.BlockSpec((m, k), lambda j: (0, 0)),
            pl.BlockSpec((None, 1, k), lambda j: (l, 0, 0)),
            pl.BlockSpec((None, tn, k), lambda j: (l, j, 0)),
        ],
        out_specs=[pl.BlockSpec((m, tn), lambda j: (0, j)), pl.BlockSpec((tn, k), lambda j: (j, 0))],
        out_shape=[jax.ShapeDtypeStruct((m, n), f32), jax.ShapeDtypeStruct((n, k), bf16)],
        scratch_shapes=[pltpu.VMEM((m, k), bf16)],
        compiler_params=pltpu.CompilerParams(
            dimension_semantics=("arbitrary",), vmem_limit_bytes=VMEM_LIMIT),
        name="norm_in_proj",
    )(x, g, wt)


def _memory_kv_kernel(x_ref, g_ref, w_ref, gk_ref, k_ref, v_ref, k4_ref, v4_ref, wb_ref):
    @pl.when(pl.program_id(1) == 0)
    def _():
        wb_ref[...] = w_ref[...].astype(bf16)

    x = x_ref[...]
    y = x * lax.rsqrt(jnp.mean(x * x, axis=-1, keepdims=True) + EPS)
    kv = _dot((y * g_ref[...]).astype(bf16), wb_ref[...])
    for h in range(MEM_HEADS):
        kh = kv[:, h * MEM_HD:(h + 1) * MEM_HD]
        kh = kh * lax.rsqrt(jnp.mean(kh * kh, axis=-1, keepdims=True) + EPS) * gk_ref[...]
        vh = kv[:, GROUP_W + h * MEM_HD:GROUP_W + (h + 1) * MEM_HD]
        k_ref[:, h * MEM_HD:(h + 1) * MEM_HD] = kh
        k4_ref[:, h, :] = kh
        v4_ref[:, h, :] = vh
    v_ref[...] = kv[:, GROUP_W:]


def _memory_kv(mem, g, w, gk):
    depth, b = w.shape[0], mem.shape[0]
    flat = jax.ShapeDtypeStruct((depth, b, N_MEM, GROUP_W), f32)
    split = jax.ShapeDtypeStruct((depth, b, N_MEM, MEM_HEADS, MEM_HD), f32)
    return pl.pallas_call(
        _memory_kv_kernel,
        grid=(depth, b),
        in_specs=[
            pl.BlockSpec((None, N_MEM, D_MODEL), lambda l, i: (i, 0, 0)),
            pl.BlockSpec((None, 1, D_MODEL), lambda l, i: (l, 0, 0)),
            pl.BlockSpec((None, D_MODEL, 2 * GROUP_W), lambda l, i: (l, 0, 0)),
            pl.BlockSpec((None, 1, MEM_HD), lambda l, i: (l, 0, 0)),
        ],
        out_specs=[pl.BlockSpec((None, None, N_MEM, GROUP_W), lambda l, i: (l, i, 0, 0))] * 2
        + [pl.BlockSpec((None, None, N_MEM, MEM_HEADS, MEM_HD), lambda l, i: (l, i, 0, 0, 0))] * 2,
        out_shape=[flat, flat, split, split],
        scratch_shapes=[pltpu.VMEM((D_MODEL, 2 * GROUP_W), bf16)],
        compiler_params=pltpu.CompilerParams(
            dimension_semantics=("arbitrary", "arbitrary"), vmem_limit_bytes=VMEM_LIMIT),
        name="memory_kv",
    )(mem, g, w, gk)


CONV_PAD = 8
N_STACK_SLOTS = 2
N_SEQ_IN_PROMPT, N_SEQ_IN_DECODE, N_PARAMS_PROMPT, N_PARAMS_DECODE, N_OUT = 3, 7, 11, 9, 5


def _mixer_kernel(*refs, tile, decode, layer, bb):
    n_seq = N_SEQ_IN_DECODE if decode else N_SEQ_IN_PROMPT
    n_par = N_PARAMS_DECODE if decode else N_PARAMS_PROMPT
    seq_in = refs[:n_seq]
    params = refs[n_seq:n_seq + n_par]
    outs = refs[n_seq + n_par:n_seq + n_par + N_OUT]
    scratch = refs[n_seq + n_par + N_OUT:]

    def view(ref, s):
        if decode and ref.ndim == 2:
            return _RowWindow(ref, s * tile, tile)
        return ref.at[s]

    if decode:
        p_ref, tail_ref, qkn_ref = seq_in[0], scratch[4], scratch[6]
        bd_ref, gsq_ref, gsk_ref = params[-3], params[-5], params[-4]
        tail_ref[...] = p_ref[:, OFF_GZ:D_IN]
        sq0, sk0 = OFF_SQ - OFF_GZ, OFF_SK - OFF_GZ
        qkn_ref[:, 0:GROUP_W] = _head_norm(tail_ref[:, sq0:sq0 + GROUP_W], gsq_ref[...], bd_ref)
        qkn_ref[:, GROUP_W:GROUP_W + LANES] = _head_norm(tail_ref[:, sk0:sk0 + LANES], gsk_ref[...],
                                                         bd_ref)

        def seg_rows(off, width):
            if off < OFF_GZ:
                return p_ref[:, off:off + width]
            return tail_ref[:, off - OFF_GZ:off - OFF_GZ + width]

        gla_in_ref, gla_out_ref = seq_in[4], outs[2]
        for _ in _gla_group(seg_rows, bb * tile, tile, params[-8:-5], outs[0],
                            lambda c: _gla_block_diag(gla_in_ref.at[c]),
                            lambda c, state: _store_gla_state(gla_out_ref.at[c], state), carry=False):
            pass
        _swa_decode_rows(seg_rows, qkn_ref, seq_in[5], seq_in[6], params[-2], layer, outs[0], bb, tile)

    stages = [_mixer_seq([view(r, s) for r in seq_in], params, [view(r, s) for r in outs],
                         [view(r, s) for r in scratch], tile=tile, decode=decode, layer=layer)
              for s in range(bb)]
    for _ in itertools.zip_longest(*stages):
        pass


class _RowWindow:
    def __init__(self, ref, start, size):
        self.ref, self.start, self.size, self.dtype = ref, start, size, ref.dtype

    def _index(self, idx):
        rows, cols = (slice(None), slice(None)) if idx is Ellipsis else idx
        lo, hi, _ = rows.indices(self.size)
        return slice(self.start + lo, self.start + hi), cols

    def __getitem__(self, idx):
        return self.ref[self._index(idx)]

    def __setitem__(self, idx, value):
        self.ref[self._index(idx)] = value


GLA_INTRA_ROWS = MXU_CHUNK
GLA_K_W = GLA_HEADS * GLA_DK
GLA_V_W = GLA_HEADS * GLA_DV


def _gla_block_diag(state_ref):
    rows = []
    for h in range(GLA_HEADS):
        blocks = [state_ref[h] if j == h else jnp.zeros((GLA_DK, GLA_DV), f32)
                  for j in range(GLA_HEADS)]
        rows.append(jnp.concatenate(blocks, axis=1))
    return jnp.concatenate(rows, axis=0)


def _store_gla_state(state_ref, state):
    for h in range(GLA_HEADS):
        state_ref[h] = state[h * GLA_DK:(h + 1) * GLA_DK, h * GLA_DV:(h + 1) * GLA_DV]


def _gla_group(seg, T, C, params, mix_ref, state_in, state_out, carry, seg_t=None):
    wup_ref, bga_ref, ggo_ref = params
    n_chunk = T // C
    G = min(T, GLA_INTRA_ROWS)
    groups = [slice(i * G, (i + 1) * G) for i in range(T // G)]
    row = lax.broadcasted_iota(jnp.int32, (G, G), 0)
    col = lax.broadcasted_iota(jnp.int32, (G, G), 1)
    causal = (row // C == col // C) & (row >= col)
    if seg_t is None:
        a_up = _dot(seg(OFF_GA, GLA_RANK).astype(bf16), wup_ref[...].astype(bf16))
    else:
        a_up = _dot_tn(seg_t(OFF_GA, GLA_RANK).astype(bf16), wup_ref[...].astype(bf16))
    log_a = _log_sigmoid(a_up + bga_ref[...]) * (1.0 / GLA_TAU)
    la3 = _split3(log_a)
    yield
    tril = jnp.where(causal, 1.0, 0.0).astype(bf16)
    in_chunk = jnp.where(lax.broadcasted_iota(jnp.int32, (T, LANES), 0) // C
                         == lax.broadcasted_iota(jnp.int32, (T, LANES), 1), 1.0, 0.0).astype(bf16)
    cum = jnp.concatenate(
        [_dot(tril, la3[0][r]) + _dot(tril, la3[1][r]) + _dot(tril, la3[2][r]) for r in groups],
        axis=0)
    tot_t = (_dot_tn(la3[0], in_chunk) + _dot_tn(la3[1], in_chunk)
             + _dot_tn(la3[2], in_chunk))
    decay_t = jnp.exp(tot_t)
    yield
    g_k = seg(OFF_GK, GLA_K_W)
    qd = ((seg(OFF_GQ, GLA_K_W) * (GLA_DK ** -0.5)) * jnp.exp(cum)).astype(bf16)
    kd = (g_k * jnp.exp(-cum)).astype(bf16)
    k_tail = jnp.concatenate(
        [g_k[c * C:(c + 1) * C] * jnp.exp(cum[(c + 1) * C - 1:(c + 1) * C] - cum[c * C:(c + 1) * C])
         for c in range(n_chunk)], axis=0) if n_chunk > 1 else g_k * jnp.exp(cum[T - 1:T] - cum)
    kt = k_tail.astype(bf16)
    yield
    v_b = seg(OFF_GV, GLA_V_W).astype(bf16)
    g_z = seg(OFF_GZ, GROUP_W)
    yield

    o_intra = []
    for r in groups:
        o_heads = []
        for h in range(GLA_HEADS):
            ks = slice(h * GLA_DK, (h + 1) * GLA_DK)
            attn = jnp.where(causal, _dot_nt(qd[r, ks], kd[r, ks]), 0.0).astype(bf16)
            o_heads.append(_dot(attn, v_b[r, h * GLA_DV:(h + 1) * GLA_DV]))
        o_intra.append(jnp.concatenate(o_heads, axis=1))
        yield
    o_intra = jnp.concatenate(o_intra, axis=0)

    shape = (GLA_K_W, GLA_V_W)
    on_diag = (lax.broadcasted_iota(jnp.int32, shape, 0) // GLA_DK
               == lax.broadcasted_iota(jnp.int32, shape, 1) // GLA_DV)
    o_chunks = []
    state = None
    for c in range(n_chunk):
        rs = slice(c * C, (c + 1) * C)
        if c == 0 or not carry:
            state = state_in(c)
        o_chunks.append(o_intra[rs] + _dot(qd[rs], state.astype(bf16)))
        update = jnp.where(on_diag, _dot_tn(kt[rs], v_b[rs]), 0.0)
        state = decay_t[:, c:c + 1] * state + update
        state_out(c, state)
        yield
    o = jnp.concatenate(o_chunks, axis=0) if n_chunk > 1 else o_chunks[0]
    for h in range(GLA_HEADS):
        vs = slice(h * GLA_DV, (h + 1) * GLA_DV)
        o_h = o[:, vs]
        o_h = o_h * lax.rsqrt(jnp.mean(o_h * o_h, axis=-1, keepdims=True) + EPS) * ggo_ref[...]
        mix_ref[:, GROUP_W + h * GLA_DV:GROUP_W + (h + 1) * GLA_DV] = (
            o_h * _silu(g_z[:, vs])).astype(mix_ref.dtype)
        yield


def _head_norm(x, g, bd_ref):
    rows, n_lanes = x.shape
    w = min(n_lanes, MXU_CHUNK)
    pieces = n_lanes // w
    bd = bd_ref[0:w, 0:w]

    def head_sums(v):
        stacked = jnp.concatenate([v[:, i * w:(i + 1) * w] for i in range(pieces)], axis=0)
        r = _dot(stacked, bd)
        return jnp.concatenate([r[i * rows:(i + 1) * rows] for i in range(pieces)], axis=1)

    sq = x * x
    hi = sq.astype(bf16)
    lo = (sq - hi.astype(f32)).astype(bf16)
    ms = (head_sums(hi) + head_sums(lo)) * (1.0 / SWA_HD)
    return x * lax.rsqrt(ms + EPS) * g


def _mixer_seq(seq_in, params, outs, scratch, *, tile, decode, layer):
    if decode:
        p_ref, mk_ref, mv_ref, conv_in_ref, gla_in_ref, kc_ref, vc_ref = seq_in
    else:
        x_ref, mk_ref, mv_ref = seq_in
        gn_ref, wt_ref = params[:2]
    (convw_ref, wup_ref, bga_ref, ggo_ref, gsq_ref, gsk_ref, bd_ref, sinks_ref,
     gmq_ref) = params[-N_PARAMS_DECODE:]
    mix_ref, conv_out_ref, gla_out_ref, kbuf_ref, vbuf_ref = outs
    ext_ref, s_ref, kprev_ref, vprev_ref, tail_ref, stk_ref, qkn_ref = scratch

    T = tile
    t = pl.program_id(1)

    def init_state():
        ext_ref[0:CONV_PAD, :] = jnp.zeros((CONV_PAD, GROUP_W), f32)
        if decode:
            ext_ref[CONV_PAD - (CONV_W - 1):CONV_PAD, :] = conv_in_ref[...]
        else:
            s_ref[...] = jnp.zeros_like(s_ref)
            kprev_ref[...] = jnp.zeros_like(kprev_ref)
            vprev_ref[...] = jnp.zeros_like(vprev_ref)

    if decode:
        init_state()
    else:
        pl.when(t == 0)(init_state)
    yield

    if decode:
        def seg(off, width):
            if off < OFF_GZ:
                return p_ref[:, off:off + width]
            return tail_ref[:, off - OFF_GZ:off - OFF_GZ + width]
    else:
        x = x_ref[...]
        hn = (x * lax.rsqrt(jnp.mean(x * x, axis=-1, keepdims=True) + EPS) * gn_ref[...]).astype(bf16)

        def seg(off, width):
            return _dot_nt(hn, wt_ref[off:off + width, :])

        def seg_t(off, width):
            return _dot_nt(wt_ref[off:off + width, :], hn)

    def group_a():
        u = seg(OFF_AC, GROUP_W) * seg(OFF_AH, GROUP_W)
        ext_ref[CONV_PAD:CONV_PAD + T, :] = u
        yield
        conv = (convw_ref[0:1, :] * ext_ref[CONV_PAD - 2:CONV_PAD - 2 + T, :]
                + convw_ref[1:2, :] * ext_ref[CONV_PAD - 1:CONV_PAD - 1 + T, :]
                + convw_ref[2:3, :] * u)
        a_b = seg(OFF_AB, GROUP_W)
        yield
        mix_ref[:, 0:GROUP_W] = (a_b * conv * _silu(seg(OFF_AZ, GROUP_W))).astype(mix_ref.dtype)
        conv_state = ext_ref[CONV_PAD + T - 2:CONV_PAD + T, :]
        ext_ref[CONV_PAD - 2:CONV_PAD, :] = conv_state
        conv_out_ref[...] = conv_state
        yield

    def group_b():
        def keep_state(c, state):
            if c == T // GLA_CHUNK - 1:
                s_ref[...] = state
                _store_gla_state(gla_out_ref, state)

        yield from _gla_group(seg, T, GLA_CHUNK, (wup_ref, bga_ref, ggo_ref), mix_ref,
                              lambda c: s_ref[...], keep_state, carry=True, seg_t=seg_t)

    def group_d():
        m_q = seg(OFF_MQ, GROUP_W)
        m_z = seg(OFF_MZ, GROUP_W)
        yield
        yield from _memory_attention(m_q, m_z, T, decode, gmq_ref, mk_ref, mv_ref, mix_ref, stk_ref)

    if decode:
        yield from group_a()
        kbuf_ref[0:WINDOW - T, :] = kc_ref[T:WINDOW, :]
        kbuf_ref[WINDOW - T:WINDOW, :] = qkn_ref[:, GROUP_W:GROUP_W + LANES]
        vbuf_ref[0:WINDOW - T, :] = vc_ref[T:WINDOW, :]
        vbuf_ref[WINDOW - T:WINDOW, :] = seg(OFF_SV, LANES)
        yield
        yield from group_d()
    else:
        group_c = _swa_prompt_tile(seg, T, t, layer, (gsq_ref, gsk_ref, bd_ref, sinks_ref), mix_ref,
                                   kprev_ref, vprev_ref, kbuf_ref, vbuf_ref)
        for _ in itertools.zip_longest(group_b(), itertools.chain(group_c, group_a(), group_d())):
            pass
        yield


def _swa_prompt_tile(seg, T, t, layer, params, mix_ref, kprev_ref, vprev_ref, kbuf_ref, vbuf_ref):
    gsq_ref, gsk_ref, bd_ref, sinks_ref = params
    s_z = seg(OFF_SZ, GROUP_W)
    s_q = seg(OFF_SQ, GROUP_W)
    s_kv = seg(OFF_SK, 2 * LANES)
    q_n = _head_norm(s_q, gsq_ref[...], bd_ref)
    k_n = _head_norm(s_kv[:, 0:LANES], gsk_ref[...], bd_ref)
    v_n = s_kv[:, LANES:2 * LANES]
    yield

    BQ = WINDOW
    n_blk = T // BQ
    stack = SWA_GROUP
    nk = WINDOW + BQ
    qi = lax.broadcasted_iota(jnp.int32, (stack * BQ, nk), 0) % BQ
    kj = lax.broadcasted_iota(jnp.int32, (stack * BQ, nk), 1)
    dist = qi + WINDOW - kj
    band = (dist >= 0) & (dist < WINDOW)
    srow = lax.broadcasted_iota(jnp.int32, (stack * BQ, 1), 0) // BQ
    for blk in range(n_blk):
        rs = slice(blk * BQ, (blk + 1) * BQ)
        if blk == 0:
            k_prev, v_prev = kprev_ref[...], vprev_ref[...]
            valid = band & ((kj >= WINDOW) | (t > 0))
        else:
            ps = slice((blk - 1) * BQ, blk * BQ)
            k_prev, v_prev = k_n[ps], v_n[ps]
            valid = band
        k_cat = jnp.concatenate([k_prev, k_n[rs]], axis=0)
        v_cat = jnp.concatenate([v_prev, v_n[rs]], axis=0)
        for g in range(SWA_KV_HEADS):
            kg = k_cat[:, g * SWA_HD:(g + 1) * SWA_HD].astype(bf16)
            vg = v_cat[:, g * SWA_HD:(g + 1) * SWA_HD].astype(bf16)
            heads = [g * SWA_GROUP + j for j in range(stack)]
            qg = jnp.concatenate([q_n[rs, hd * SWA_HD:(hd + 1) * SWA_HD] for hd in heads],
                                 axis=0).astype(bf16)
            sink = jnp.full((stack * BQ, 1), sinks_ref[layer, heads[0]], f32)
            for j in range(1, stack):
                sink = jnp.where(srow == j, sinks_ref[layer, heads[j]], sink)
            o = _attend(_dot_nt(qg, kg), valid, SWA_HD ** -0.5, vg, sink)
            for j, hd in enumerate(heads):
                z = s_z[rs, hd * SWA_HD:(hd + 1) * SWA_HD]
                mix_ref[rs, 2 * GROUP_W + hd * SWA_HD:2 * GROUP_W + (hd + 1) * SWA_HD] = (
                    o[j * BQ:(j + 1) * BQ] * _silu(z)).astype(mix_ref.dtype)
            yield

    kprev_ref[...] = k_n[T - WINDOW:T]
    vprev_ref[...] = v_n[T - WINDOW:T]
    kbuf_ref[...] = k_n[T - WINDOW:T]
    vbuf_ref[...] = v_n[T - WINDOW:T]
    yield


def _swa_decode_rows(seg, qkn_ref, kc_ref, vc_ref, sinks_ref, layer, mix_ref, bb, T):
    R = bb * T
    n_cache = bb * WINDOW
    nk = n_cache + R
    q_n, k_n = qkn_ref[:, 0:GROUP_W], qkn_ref[:, GROUP_W:GROUP_W + LANES]
    s_z = seg(OFF_SZ, GROUP_W)
    k_all = jnp.concatenate([kc_ref[s] for s in range(bb)] + [k_n], axis=0)
    v_all = jnp.concatenate([vc_ref[s] for s in range(bb)] + [seg(OFF_SV, LANES)], axis=0)
    rows = SWA_GROUP * R
    r = lax.broadcasted_iota(jnp.int32, (rows, nk), 0) % R
    c = lax.broadcasted_iota(jnp.int32, (rows, nk), 1)
    cached = c < n_cache
    key_seq = jnp.where(cached, c // WINDOW, (c - n_cache) // T)
    key_pos = jnp.where(cached, c % WINDOW, WINDOW + (c - n_cache) % T)
    dist = r % T + WINDOW - key_pos
    valid = (r // T == key_seq) & (dist >= 0) & (dist < WINDOW)
    srow = lax.broadcasted_iota(jnp.int32, (rows, 1), 0) // R
    for g in range(SWA_KV_HEADS):
        heads = [g * SWA_GROUP + j for j in range(SWA_GROUP)]
        qg = jnp.concatenate([q_n[:, hd * SWA_HD:(hd + 1) * SWA_HD] for hd in heads],
                             axis=0).astype(bf16)
        kg = k_all[:, g * SWA_HD:(g + 1) * SWA_HD].astype(bf16)
        vg = v_all[:, g * SWA_HD:(g + 1) * SWA_HD].astype(bf16)
        sink = jnp.full((rows, 1), sinks_ref[layer, heads[0]], f32)
        for j in range(1, SWA_GROUP):
            sink = jnp.where(srow == j, sinks_ref[layer, heads[j]], sink)
        o = _attend(_dot_nt(qg, kg), valid, SWA_HD ** -0.5, vg, sink)
        for j, hd in enumerate(heads):
            mix_ref[:, 2 * GROUP_W + hd * SWA_HD:2 * GROUP_W + (hd + 1) * SWA_HD] = (
                o[j * R:(j + 1) * R] * _silu(s_z[:, hd * SWA_HD:(hd + 1) * SWA_HD])).astype(mix_ref.dtype)


def _memory_attention(m_q, m_z, T, decode, gmq_ref, mk_ref, mv_ref, mix_ref, stk_ref):
    def stack_rows(pieces, slot):
        r, w = pieces[0].shape
        if r % 8 == 0:
            return jnp.concatenate(pieces, axis=0)
        for j, piece in enumerate(pieces):
            stk_ref[slot, j * r:(j + 1) * r, 0:w] = piece
        return stk_ref[slot, 0:len(pieces) * r, 0:w]

    def unstack_rows(x, n, slot):
        r, w = x.shape[0] // n, x.shape[1]
        if r % 8 == 0:
            return [x[j * r:(j + 1) * r] for j in range(n)]
        stk_ref[slot, 0:n * r, 0:w] = x
        return [stk_ref[slot, j * r:(j + 1) * r, 0:w] for j in range(n)]

    def mem_q(h):
        qh = m_q[:, h * MEM_HD:(h + 1) * MEM_HD]
        return qh * lax.rsqrt(jnp.mean(qh * qh, axis=-1, keepdims=True) + EPS) * gmq_ref[...]

    if decode:
        qs = stack_rows([mem_q(h) for h in range(MEM_HEADS)], 0).astype(bf16)
        s = _dot_nt(qs, mk_ref[...].astype(bf16))
        shape = (MEM_HEADS * T, MEM_HEADS * N_MEM)
        same_head = (lax.broadcasted_iota(jnp.int32, shape, 0) // T
                     == lax.broadcasted_iota(jnp.int32, shape, 1) % MEM_HEADS)
        yield
        o = _attend(s, same_head, MEM_HD ** -0.5, mv_ref[...].astype(bf16))
        o_all = unstack_rows(o, MEM_HEADS, 1)
    else:
        o_all = []
        for h in range(MEM_HEADS):
            hs = slice(h * MEM_HD, (h + 1) * MEM_HD)
            s = _dot_nt(mem_q(h).astype(bf16), mk_ref[:, hs].astype(bf16))
            o_all.append(_attend(s, None, MEM_HD ** -0.5, mv_ref[:, hs].astype(bf16)))
            yield
    for h in range(MEM_HEADS):
        mix_ref[:, 3 * GROUP_W + h * MEM_HD:3 * GROUP_W + (h + 1) * MEM_HD] = (
            o_all[h] * _silu(m_z[:, h * MEM_HD:(h + 1) * MEM_HD])).astype(mix_ref.dtype)
    yield


def _mixer(tokens, norm_w, mem_k, mem_v, mem_layer, state, params, layer, tile, bb, decode):
    if decode:
        b, width = state[0].shape[1], tokens.shape[1]
        L = tokens.shape[0] // b
    else:
        b, L, width = tokens.shape
    nt = L // tile
    assert nt == 1 or not decode, "a decode call covers each sequence with a single tile"
    conv_w, w_up, b_ga, g_go, g_sq, g_sk, bd, sinks, g_mq = params

    def tok(width):
        if decode:
            return pl.BlockSpec((bb * tile, width), lambda i, t: (i, 0))
        return pl.BlockSpec((bb, tile, width), lambda i, t: (i, t, 0))

    def per_seq(*shape):
        return pl.BlockSpec((bb,) + shape, lambda i, t: (i,) + (0,) * len(shape))

    def per_seq_at(lyr, *shape):
        return pl.BlockSpec((None, bb) + shape, lambda i, t: (lyr, i) + (0,) * len(shape))

    def param(a):
        return pl.BlockSpec((None,) + a.shape[1:], lambda i, t: (layer,) + (0,) * (a.ndim - 1))

    kv_w = SWA_KV_HEADS * SWA_HD
    state_shapes = [(CONV_W - 1, GROUP_W), (GLA_HEADS, GLA_DK, GLA_DV), (WINDOW, kv_w),
                    (WINDOW, kv_w)]
    in_specs = [tok(width), per_seq_at(mem_layer, *mem_k.shape[2:]),
                per_seq_at(mem_layer, *mem_v.shape[2:])]
    args = [tokens, mem_k, mem_v]
    if decode:
        in_specs += [per_seq_at(layer, *s) for s in state_shapes]
        args += list(state)
    else:
        g_n, w_t = norm_w
        in_specs += [param(g_n), pl.BlockSpec(w_t.shape, lambda i, t: (0, 0),
                                              pipeline_mode=pl.Buffered(1))]
        args += [g_n, w_t]
    in_specs += [param(conv_w), param(w_up), param(b_ga), param(g_go), param(g_sq), param(g_sk),
                 pl.BlockSpec(bd.shape, lambda i, t: (0, 0)),
                 pl.BlockSpec(memory_space=pltpu.SMEM), param(g_mq)]
    args += [conv_w, w_up, b_ga, g_go, g_sq, g_sk, bd, sinks, g_mq]
    conv_window = pltpu.VMEM((bb, CONV_PAD + tile, GROUP_W), f32)
    unused = pltpu.VMEM((bb, 8, LANES), f32)
    if decode:
        out_shape = [jax.ShapeDtypeStruct((b * L, 4 * GROUP_W), f32)]
        scratch_shapes = [conv_window, unused, unused, unused,
                          pltpu.VMEM((bb * tile, D_IN - OFF_GZ), f32),
                          pltpu.VMEM((bb, N_STACK_SLOTS, MEM_HEADS * tile, LANES), f32),
                          pltpu.VMEM((bb * tile, GROUP_W + LANES), f32)]
    else:
        out_shape = [jax.ShapeDtypeStruct((b, L, 4 * GROUP_W), bf16)]
        scratch_shapes = [conv_window, pltpu.VMEM((bb, GLA_K_W, GLA_V_W), f32),
                          pltpu.VMEM((bb, WINDOW, kv_w), f32), pltpu.VMEM((bb, WINDOW, kv_w), f32),
                          unused, unused, unused]
    out_shape += [jax.ShapeDtypeStruct((b,) + s, f32) for s in state_shapes]
    out_specs = [tok(4 * GROUP_W)] + [per_seq(*s) for s in state_shapes]
    return pl.pallas_call(
        functools.partial(_mixer_kernel, tile=tile, decode=decode, layer=layer, bb=bb),
        grid=(b // bb, nt),
        in_specs=in_specs,
        out_specs=out_specs,
        out_shape=out_shape,
        scratch_shapes=scratch_shapes,
        compiler_params=pltpu.CompilerParams(
            dimension_semantics=("arbitrary", "arbitrary"), vmem_limit_bytes=VMEM_LIMIT),
        name="mixer_decode" if decode else "mixer_prompt",
    )(*args)


def _out_proj_kernel(mix_ref, w_ref, x_ref, y_ref, wb_ref):
    @pl.when(pl.program_id(1) == 0)
    def _():
        wb_ref[...] = w_ref[...].astype(bf16)

    y_ref[...] = x_ref[...] + _dot(mix_ref[...].astype(bf16), wb_ref[...])


def _out_proj(mix, w, x, l, tm, tn):
    m, k = mix.shape
    n = w.shape[2]
    return pl.pallas_call(
        _out_proj_kernel,
        grid=(n // tn, m // tm),
        in_specs=[
            pl.BlockSpec((tm, k), lambda j, i: (i, 0)),
            pl.BlockSpec((None, k, tn), lambda j, i: (l, 0, j),
                         pipeline_mode=pl.Buffered(1) if tn == n else None),
            pl.BlockSpec((tm, tn), lambda j, i: (i, j)),
        ],
        out_specs=pl.BlockSpec((tm, tn), lambda j, i: (i, j)),
        out_shape=jax.ShapeDtypeStruct((m, n), f32),
        scratch_shapes=[pltpu.VMEM((k, tn), bf16)],
        compiler_params=pltpu.CompilerParams(
            dimension_semantics=("arbitrary", "arbitrary"), vmem_limit_bytes=VMEM_LIMIT),
        name="out_proj",
    )(mix, w, x)


PROMPT_TILE = 512
DECODE_SEQS_PER_STEP = 8
PROJ_TN = 1536
OUT_TM, OUT_TN = 512, 2048
SAMPLE_OUT_TN = 512

_LANE = np.arange(GROUP_W)
HEAD_BLOCK_DIAG = _LANE[:, None] // SWA_HD == _LANE[None, :] // SWA_HD


def kernel(x_prompt, x_sample, mem_prompt, state_conv, state_gla, cache_swa_k, cache_swa_v,
           cache_mem_k, cache_mem_v, g_norm, w_in, conv_w, w_gla_a_up, b_gla_a, g_gla_o,
           g_swa_q, g_swa_k, swa_sinks, g_mem, w_mem_kv, g_mem_q, g_mem_k, w_out):
    depth = w_in.shape[0]
    bp, lp, _ = x_prompt.shape
    bs, ls, _ = x_sample.shape
    hp = x_prompt.reshape(bp * lp, D_MODEL)
    hs = x_sample.reshape(bs * ls, D_MODEL)

    def row(a):
        return a[:, None, :]

    params = (conv_w, w_gla_a_up, row(b_gla_a), row(g_gla_o),
              row(jnp.tile(g_swa_q, (1, SWA_HEADS))), row(jnp.tile(g_swa_k, (1, SWA_KV_HEADS))),
              jnp.asarray(HEAD_BLOCK_DIAG, bf16), swa_sinks, row(g_mem_q))
    g_n, g_m, g_mk = row(g_norm), row(g_mem), row(g_mem_k)
    w_in_t = jnp.swapaxes(w_in, 1, 2)
    kv_w = SWA_KV_HEADS * SWA_HD
    state = (state_conv, state_gla, cache_swa_k.reshape(depth, bs, WINDOW, kv_w),
             cache_swa_v.reshape(depth, bs, WINDOW, kv_w))
    mem_k_s = cache_mem_k.reshape(depth, bs, N_MEM * MEM_HEADS, MEM_HD)
    mem_v_s = cache_mem_v.reshape(depth, bs, N_MEM * MEM_HEADS, MEM_HD)

    mk, mv, mem_k_p, mem_v_p = _memory_kv(mem_prompt, g_m, w_mem_kv, g_mk)
    outs = [[] for _ in range(8)]
    for l in range(depth):
        proj, w_bf = _norm_matmul(hs, g_n, w_in_t, l, PROJ_TN)

        mix, c, s, kb, vb = _mixer(hp.reshape(bp, lp, D_MODEL), (g_n, w_bf), mk, mv, l,
                                   None, params, l, PROMPT_TILE, 1, decode=False)
        hp = _out_proj(mix.reshape(bp * lp, 4 * GROUP_W), w_out, hp, l, OUT_TM, OUT_TN)
        for lst, a in zip(outs[:4], (
                c, s, kb.reshape(bp, WINDOW, SWA_KV_HEADS, SWA_HD),
                vb.reshape(bp, WINDOW, SWA_KV_HEADS, SWA_HD))):
            lst.append(a)

        mix, c, s, kb, vb = _mixer(proj, None, mem_k_s, mem_v_s, l, state, params, l, ls,
                                   DECODE_SEQS_PER_STEP, decode=True)
        hs = _out_proj(mix, w_out, hs, l, bs * ls, SAMPLE_OUT_TN)
        for lst, a in zip(outs[4:], (
                c, s, kb.reshape(bs, WINDOW, SWA_KV_HEADS, SWA_HD),
                vb.reshape(bs, WINDOW, SWA_KV_HEADS, SWA_HD))):
            lst.append(a)

    stacked = [jnp.stack(o) for o in outs]
    return (hp.reshape(bp, lp, D_MODEL), hs.reshape(bs, ls, D_MODEL),
            *stacked[:4], mem_k_p, mem_v_p, *stacked[4:])
```

```python
import functools
import itertools

import jax
import jax.numpy as jnp
import numpy as np
from jax import lax
from jax.experimental import pallas as pl
from jax.experimental.pallas import tpu as pltpu

f32 = jnp.float32
bf16 = jnp.bfloat16

D_MODEL = 2048
GROUP_W = 512
GLA_HEADS = 4
GLA_DK = 64
GLA_DV = 128
GLA_RANK = 16
GLA_TAU = 16.0
GLA_CHUNK = 64
SWA_HEADS = 8
SWA_KV_HEADS = 2
SWA_HD = 64
SWA_GROUP = SWA_HEADS // SWA_KV_HEADS
WINDOW = 128
N_MEM = 256
MEM_HEADS = 4
MEM_HD = 128
CONV_W = 3
EPS = 1e-6

LANES = 128
MXU_CHUNK = 256

D_IN = 5904
OFF_AB, OFF_AC, OFF_AH, OFF_AZ = 0, 512, 1024, 1536
OFF_GQ, OFF_GK, OFF_GV, OFF_GA, OFF_GZ = 2048, 2304, 2560, 3072, 3088
OFF_SQ, OFF_SK, OFF_SV, OFF_SZ = 3600, 4112, 4240, 4368
OFF_MQ, OFF_MZ = 4880, 5392

VMEM_LIMIT = 60 * 1024 * 1024


def _dot(a, b):
    return jnp.dot(a, b, preferred_element_type=f32)


def _dot_nt(a, b):
    return lax.dot_general(a, b, (((1,), (1,)), ((), ())), preferred_element_type=f32)


def _dot_tn(a, b):
    return lax.dot_general(a, b, (((0,), (0,)), ((), ())), preferred_element_type=f32)


def _split3(x):
    hi = x.astype(bf16)
    r = x - hi.astype(f32)
    mid = r.astype(bf16)
    lo = (r - mid.astype(f32)).astype(bf16)
    return hi, mid, lo


LOG2_E = 1.4426950408889634


def _attend(scores, valid, scale, values, sink=None):
    if valid is not None:
        scores = jnp.where(valid, scores, -jnp.inf)
    m = jnp.max(scores, axis=-1, keepdims=True)
    if sink is not None:
        sink = sink * (1.0 / scale)
        m = jnp.maximum(m, sink)
    e = jnp.exp2((scores - m) * (scale * LOG2_E))
    denom = jnp.sum(e, axis=-1, keepdims=True)
    if sink is not None:
        denom = denom + jnp.exp2((sink - m) * (scale * LOG2_E))
    return _dot(e.astype(bf16), values) / denom


def _silu(x):
    return x * jax.nn.sigmoid(x)


def _log_sigmoid(x):
    return jnp.minimum(x, 0.0) - jnp.log1p(jnp.exp(-jnp.abs(x)))


def _norm_matmul_kernel(x_ref, g_ref, wt_ref, o_ref, wb_ref, hn_ref, *, layer):
    x = x_ref[...]
    y = x * lax.rsqrt(jnp.mean(x * x, axis=-1, keepdims=True) + EPS)
    hn_ref[...] = (y * g_ref[layer:layer + 1, :]).astype(bf16)
    wb_ref[...] = wt_ref[...].astype(bf16)
    o_ref[...] = _dot_nt(hn_ref[...], wb_ref[...])


def _norm_matmul(x, g, wt, l, tn):
    m, k = x.shape
    n = wt.shape[1]
    return pl.pallas_call(
        functools.partial(_norm_matmul_kernel, layer=l),
        grid=(pl.cdiv(n, tn),),
        in_specs=[
            pl.BlockSpec((m, k), lambda j: (0, 0)),
            pl.BlockSpec(g.shape, lambda j: (0, 0)),
            pl.BlockSpec((None, tn, k), lambda j: (l, j, 0)),
        ],
        out_specs=[pl.BlockSpec((m, tn), lambda j: (0, j)), pl.BlockSpec((tn, k), lambda j: (j, 0))],
        out_shape=[jax.ShapeDtypeStruct((m, n), f32), jax.ShapeDtypeStruct((n, k), bf16)],
        scratch_shapes=[pltpu.VMEM((m, k), bf16)],
        compiler_params=pltpu.CompilerParams(
            dimension_semantics=("arbitrary",), vmem_limit_bytes=VMEM_LIMIT),
        name="norm_in_proj",
    )(x, g, wt)


def _memory_kv_kernel(x_ref, g_ref, w_ref, gk_ref, k_ref, v_ref, k4_ref, v4_ref, wb_ref):
    @pl.when(pl.program_id(1) == 0)
    def _():
        wb_ref[...] = w_ref[...].astype(bf16)

    layer = pl.program_id(0)
    g, gk = g_ref[pl.ds(layer, 1), :], gk_ref[pl.ds(layer, 1), :]
    x = x_ref[...]
    y = x * lax.rsqrt(jnp.mean(x * x, axis=-1, keepdims=True) + EPS)
    kv = _dot((y * g).astype(bf16), wb_ref[...])
    for h in range(MEM_HEADS):
        kh = kv[:, h * MEM_HD:(h + 1) * MEM_HD]
        kh = kh * lax.rsqrt(jnp.mean(kh * kh, axis=-1, keepdims=True) + EPS) * gk
        vh = kv[:, GROUP_W + h * MEM_HD:GROUP_W + (h + 1) * MEM_HD]
        k_ref[:, h * MEM_HD:(h + 1) * MEM_HD] = kh
        k4_ref[:, h, :] = kh
        v4_ref[:, h, :] = vh
    v_ref[...] = kv[:, GROUP_W:]


def _memory_kv(mem, g, w, gk):
    depth, b = w.shape[0], mem.shape[0]
    flat = jax.ShapeDtypeStruct((depth, b, N_MEM, GROUP_W), f32)
    split = jax.ShapeDtypeStruct((depth, b, N_MEM, MEM_HEADS, MEM_HD), f32)
    return pl.pallas_call(
        _memory_kv_kernel,
        grid=(depth, b),
        in_specs=[
            pl.BlockSpec((None, N_MEM, D_MODEL), lambda l, i: (i, 0, 0)),
            pl.BlockSpec(g.shape, lambda l, i: (0, 0)),
            pl.BlockSpec((None, D_MODEL, 2 * GROUP_W), lambda l, i: (l, 0, 0)),
            pl.BlockSpec(gk.shape, lambda l, i: (0, 0)),
        ],
        out_specs=[pl.BlockSpec((None, None, N_MEM, GROUP_W), lambda l, i: (l, i, 0, 0))] * 2
        + [pl.BlockSpec((None, None, N_MEM, MEM_HEADS, MEM_HD), lambda l, i: (l, i, 0, 0, 0))] * 2,
        out_shape=[flat, flat, split, split],
        scratch_shapes=[pltpu.VMEM((D_MODEL, 2 * GROUP_W), bf16)],
        compiler_params=pltpu.CompilerParams(
            dimension_semantics=("arbitrary", "arbitrary"), vmem_limit_bytes=VMEM_LIMIT),
        name="memory_kv",
    )(mem, g, w, gk)


CONV_PAD = 8
N_STACK_SLOTS = 2
N_SEQ_IN_PROMPT, N_SEQ_IN_DECODE, N_PARAMS_PROMPT, N_PARAMS_DECODE, N_OUT = 3, 7, 11, 9, 5


def _mixer_kernel(*refs, tile, decode, layer, bb):
    n_seq = N_SEQ_IN_DECODE if decode else N_SEQ_IN_PROMPT
    n_par = N_PARAMS_DECODE if decode else N_PARAMS_PROMPT
    seq_in = refs[:n_seq]
    params = list(refs[n_seq:n_seq + n_par])
    outs = refs[n_seq + n_par:n_seq + n_par + N_OUT]
    scratch = refs[n_seq + n_par + N_OUT:]
    for k in (-7, -6, -5, -4, -1) + (() if decode else (0,)):
        params[k] = params[k].at[layer:layer + 1]

    def view(ref, s):
        if decode and ref.ndim == 2:
            return _RowWindow(ref, s * tile, tile)
        return ref.at[s]

    if decode:
        p_ref, tail_ref, qkn_ref = seq_in[0], scratch[4], scratch[6]
        bd_ref, gsq_ref, gsk_ref = params[-3], params[-5], params[-4]
        tail_ref[...] = p_ref[:, OFF_GZ:D_IN]
        sq0, sk0 = OFF_SQ - OFF_GZ, OFF_SK - OFF_GZ
        qkn_ref[:, 0:GROUP_W] = _head_norm(tail_ref[:, sq0:sq0 + GROUP_W], gsq_ref[...], bd_ref)
        qkn_ref[:, GROUP_W:GROUP_W + LANES] = _head_norm(tail_ref[:, sk0:sk0 + LANES], gsk_ref[...],
                                                         bd_ref)

        def seg_rows(off, width):
            if off < OFF_GZ:
                return p_ref[:, off:off + width]
            return tail_ref[:, off - OFF_GZ:off - OFF_GZ + width]

        gla_in_ref, gla_out_ref = seq_in[4], outs[2]
        for _ in _gla_group(seg_rows, bb * tile, tile, params[-8:-5], outs[0],
                            lambda c: _gla_block_diag(gla_in_ref.at[c]),
                            lambda c, state: _store_gla_state(gla_out_ref.at[c], state), carry=False):
            pass
        _swa_decode_rows(seg_rows, qkn_ref, seq_in[5], seq_in[6], params[-2], layer, outs[0], bb, tile)

    stages = [_mixer_seq([view(r, s) for r in seq_in], params, [view(r, s) for r in outs],
                         [view(r, s) for r in scratch], tile=tile, decode=decode, layer=layer)
              for s in range(bb)]
    for _ in itertools.zip_longest(*stages):
        pass


class _RowWindow:
    def __init__(self, ref, start, size):
        self.ref, self.start, self.size, self.dtype = ref, start, size, ref.dtype

    def _index(self, idx):
        rows, cols = (slice(None), slice(None)) if idx is Ellipsis else idx
        lo, hi, _ = rows.indices(self.size)
        return slice(self.start + lo, self.start + hi), cols

    def __getitem__(self, idx):
        return self.ref[self._index(idx)]

    def __setitem__(self, idx, value):
        self.ref[self._index(idx)] = value


GLA_INTRA_ROWS = MXU_CHUNK
GLA_K_W = GLA_HEADS * GLA_DK
GLA_V_W = GLA_HEADS * GLA_DV


def _gla_block_diag(state_ref):
    rows = []
    for h in range(GLA_HEADS):
        blocks = [state_ref[h] if j == h else jnp.zeros((GLA_DK, GLA_DV), f32)
                  for j in range(GLA_HEADS)]
        rows.append(jnp.concatenate(blocks, axis=1))
    return jnp.concatenate(rows, axis=0)


def _store_gla_state(state_ref, state):
    for h in range(GLA_HEADS):
        state_ref[h] = state[h * GLA_DK:(h + 1) * GLA_DK, h * GLA_DV:(h + 1) * GLA_DV]


def _gla_group(seg, T, C, params, mix_ref, state_in, state_out, carry, seg_t=None):
    wup_ref, bga_ref, ggo_ref = params
    n_chunk = T // C
    G = min(T, GLA_INTRA_ROWS)
    groups = [slice(i * G, (i + 1) * G) for i in range(T // G)]
    row = lax.broadcasted_iota(jnp.int32, (G, G), 0)
    col = lax.broadcasted_iota(jnp.int32, (G, G), 1)
    causal = (row // C == col // C) & (row >= col)
    if seg_t is None:
        a_up = _dot(seg(OFF_GA, GLA_RANK).astype(bf16), wup_ref[...].astype(bf16))
    else:
        a_up = _dot_tn(seg_t(OFF_GA, GLA_RANK).astype(bf16), wup_ref[...].astype(bf16))
    log_a = _log_sigmoid(a_up + bga_ref[...]) * (1.0 / GLA_TAU)
    la3 = _split3(log_a)
    yield
    tril = jnp.where(causal, 1.0, 0.0).astype(bf16)
    in_chunk = jnp.where(lax.broadcasted_iota(jnp.int32, (T, LANES), 0) // C
                         == lax.broadcasted_iota(jnp.int32, (T, LANES), 1), 1.0, 0.0).astype(bf16)
    cum = jnp.concatenate(
        [_dot(tril, la3[0][r]) + _dot(tril, la3[1][r]) + _dot(tril, la3[2][r]) for r in groups],
        axis=0)
    tot_t = (_dot_tn(la3[0], in_chunk) + _dot_tn(la3[1], in_chunk)
             + _dot_tn(la3[2], in_chunk))
    decay_t = jnp.exp(tot_t)
    yield
    g_k = seg(OFF_GK, GLA_K_W)
    qd = ((seg(OFF_GQ, GLA_K_W) * (GLA_DK ** -0.5)) * jnp.exp(cum)).astype(bf16)
    kd = (g_k * jnp.exp(-cum)).astype(bf16)
    k_tail = jnp.concatenate(
        [g_k[c * C:(c + 1) * C] * jnp.exp(cum[(c + 1) * C - 1:(c + 1) * C] - cum[c * C:(c + 1) * C])
         for c in range(n_chunk)], axis=0) if n_chunk > 1 else g_k * jnp.exp(cum[T - 1:T] - cum)
    kt = k_tail.astype(bf16)
    yield
    v_b = seg(OFF_GV, GLA_V_W).astype(bf16)
    g_z = seg(OFF_GZ, GROUP_W)
    yield

    o_intra = []
    for r in groups:
        o_heads = []
        for h in range(GLA_HEADS):
            ks = slice(h * GLA_DK, (h + 1) * GLA_DK)
            attn = jnp.where(causal, _dot_nt(qd[r, ks], kd[r, ks]), 0.0).astype(bf16)
            o_heads.append(_dot(attn, v_b[r, h * GLA_DV:(h + 1) * GLA_DV]))
        o_intra.append(jnp.concatenate(o_heads, axis=1))
        yield
    o_intra = jnp.concatenate(o_intra, axis=0)

    shape = (GLA_K_W, GLA_V_W)
    on_diag = (lax.broadcasted_iota(jnp.int32, shape, 0) // GLA_DK
               == lax.broadcasted_iota(jnp.int32, shape, 1) // GLA_DV)
    o_chunks = []
    state = None
    for c in range(n_chunk):
        rs = slice(c * C, (c + 1) * C)
        if c == 0 or not carry:
            state = state_in(c)
        o_chunks.append(o_intra[rs] + _dot(qd[rs], state.astype(bf16)))
        update = jnp.where(on_diag, _dot_tn(kt[rs], v_b[rs]), 0.0)
        state = decay_t[:, c:c + 1] * state + update
        state_out(c, state)
        yield
    o = jnp.concatenate(o_chunks, axis=0) if n_chunk > 1 else o_chunks[0]
    for h in range(GLA_HEADS):
        vs = slice(h * GLA_DV, (h + 1) * GLA_DV)
        o_h = o[:, vs]
        o_h = o_h * lax.rsqrt(jnp.mean(o_h * o_h, axis=-1, keepdims=True) + EPS) * ggo_ref[...]
        mix_ref[:, GROUP_W + h * GLA_DV:GROUP_W + (h + 1) * GLA_DV] = (
            o_h * _silu(g_z[:, vs])).astype(mix_ref.dtype)
        yield


def _head_norm(x, g, bd_ref):
    rows, n_lanes = x.shape
    w = min(n_lanes, MXU_CHUNK)
    pieces = n_lanes // w
    bd = bd_ref[0:w, 0:w]

    def head_sums(v):
        stacked = jnp.concatenate([v[:, i * w:(i + 1) * w] for i in range(pieces)], axis=0)
        r = _dot(stacked, bd)
        return jnp.concatenate([r[i * rows:(i + 1) * rows] for i in range(pieces)], axis=1)

    sq = x * x
    hi = sq.astype(bf16)
    lo = (sq - hi.astype(f32)).astype(bf16)
    ms = (head_sums(hi) + head_sums(lo)) * (1.0 / SWA_HD)
    return x * lax.rsqrt(ms + EPS) * g


def _mixer_seq(seq_in, params, outs, scratch, *, tile, decode, layer):
    if decode:
        p_ref, mk_ref, mv_ref, conv_in_ref, gla_in_ref, kc_ref, vc_ref = seq_in
    else:
        x_ref, mk_ref, mv_ref = seq_in
        gn_ref, wt_ref = params[:2]
    (convw_ref, wup_ref, bga_ref, ggo_ref, gsq_ref, gsk_ref, bd_ref, sinks_ref,
     gmq_ref) = params[-N_PARAMS_DECODE:]
    mix_ref, conv_out_ref, gla_out_ref, kbuf_ref, vbuf_ref = outs
    ext_ref, s_ref, kprev_ref, vprev_ref, tail_ref, stk_ref, qkn_ref = scratch

    T = tile
    t = pl.program_id(1)

    def init_state():
        ext_ref[0:CONV_PAD, :] = jnp.zeros((CONV_PAD, GROUP_W), f32)
        if decode:
            ext_ref[CONV_PAD - (CONV_W - 1):CONV_PAD, :] = conv_in_ref[...]
        else:
            s_ref[...] = jnp.zeros_like(s_ref)
            kprev_ref[...] = jnp.zeros_like(kprev_ref)
            vprev_ref[...] = jnp.zeros_like(vprev_ref)

    if decode:
        init_state()
    else:
        pl.when(t == 0)(init_state)
    yield

    if decode:
        def seg(off, width):
            if off < OFF_GZ:
                return p_ref[:, off:off + width]
            return tail_ref[:, off - OFF_GZ:off - OFF_GZ + width]
    else:
        x = x_ref[...]
        hn = (x * lax.rsqrt(jnp.mean(x * x, axis=-1, keepdims=True) + EPS) * gn_ref[...]).astype(bf16)

        def seg(off, width):
            return _dot_nt(hn, wt_ref[off:off + width, :])

        def seg_t(off, width):
            return _dot_nt(wt_ref[off:off + width, :], hn)

    def group_a():
        u = seg(OFF_AC, GROUP_W) * seg(OFF_AH, GROUP_W)
        ext_ref[CONV_PAD:CONV_PAD + T, :] = u
        yield
        conv = (convw_ref[0:1, :] * ext_ref[CONV_PAD - 2:CONV_PAD - 2 + T, :]
                + convw_ref[1:2, :] * ext_ref[CONV_PAD - 1:CONV_PAD - 1 + T, :]
                + convw_ref[2:3, :] * u)
        a_b = seg(OFF_AB, GROUP_W)
        yield
        mix_ref[:, 0:GROUP_W] = (a_b * conv * _silu(seg(OFF_AZ, GROUP_W))).astype(mix_ref.dtype)
        conv_state = ext_ref[CONV_PAD + T - 2:CONV_PAD + T, :]
        ext_ref[CONV_PAD - 2:CONV_PAD, :] = conv_state
        conv_out_ref[...] = conv_state
        yield

    def group_b():
        def keep_state(c, state):
            if c == T // GLA_CHUNK - 1:
                s_ref[...] = state
                _store_gla_state(gla_out_ref, state)

        yield from _gla_group(seg, T, GLA_CHUNK, (wup_ref, bga_ref, ggo_ref), mix_ref,
                              lambda c: s_ref[...], keep_state, carry=True, seg_t=seg_t)

    def group_d():
        m_q = seg(OFF_MQ, GROUP_W)
        m_z = seg(OFF_MZ, GROUP_W)
        yield
        yield from _memory_attention(m_q, m_z, T, decode, gmq_ref, mk_ref, mv_ref, mix_ref, stk_ref)

    if decode:
        yield from group_a()
        kbuf_ref[0:WINDOW - T, :] = kc_ref[T:WINDOW, :]
        kbuf_ref[WINDOW - T:WINDOW, :] = qkn_ref[:, GROUP_W:GROUP_W + LANES]
        vbuf_ref[0:WINDOW - T, :] = vc_ref[T:WINDOW, :]
        vbuf_ref[WINDOW - T:WINDOW, :] = seg(OFF_SV, LANES)
        yield
        yield from group_d()
    else:
        group_c = _swa_prompt_tile(seg, T, t, layer, (gsq_ref, gsk_ref, bd_ref, sinks_ref), mix_ref,
                                   kprev_ref, vprev_ref, kbuf_ref, vbuf_ref)
        for _ in itertools.zip_longest(group_b(), itertools.chain(group_c, group_a(), group_d())):
            pass
        yield


def _swa_prompt_tile(seg, T, t, layer, params, mix_ref, kprev_ref, vprev_ref, kbuf_ref, vbuf_ref):
    gsq_ref, gsk_ref, bd_ref, sinks_ref = params
    s_z = seg(OFF_SZ, GROUP_W)
    s_q = seg(OFF_SQ, GROUP_W)
    s_kv = seg(OFF_SK, 2 * LANES)
    q_n = _head_norm(s_q, gsq_ref[...], bd_ref)
    k_n = _head_norm(s_kv[:, 0:LANES], gsk_ref[...], bd_ref)
    v_n = s_kv[:, LANES:2 * LANES]
    yield

    BQ = WINDOW
    n_blk = T // BQ
    stack = SWA_GROUP
    nk = WINDOW + BQ
    qi = lax.broadcasted_iota(jnp.int32, (stack * BQ, nk), 0) % BQ
    kj = lax.broadcasted_iota(jnp.int32, (stack * BQ, nk), 1)
    dist = qi + WINDOW - kj
    band = (dist >= 0) & (dist < WINDOW)
    srow = lax.broadcasted_iota(jnp.int32, (stack * BQ, 1), 0) // BQ
    for blk in range(n_blk):
        rs = slice(blk * BQ, (blk + 1) * BQ)
        if blk == 0:
            k_prev, v_prev = kprev_ref[...], vprev_ref[...]
            valid = band & ((kj >= WINDOW) | (t > 0))
        else:
            ps = slice((blk - 1) * BQ, blk * BQ)
            k_prev, v_prev = k_n[ps], v_n[ps]
            valid = band
        k_cat = jnp.concatenate([k_prev, k_n[rs]], axis=0)
        v_cat = jnp.concatenate([v_prev, v_n[rs]], axis=0)
        for g in range(SWA_KV_HEADS):
            kg = k_cat[:, g * SWA_HD:(g + 1) * SWA_HD].astype(bf16)
            vg = v_cat[:, g * SWA_HD:(g + 1) * SWA_HD].astype(bf16)
            heads = [g * SWA_GROUP + j for j in range(stack)]
            qg = jnp.concatenate([q_n[rs, hd * SWA_HD:(hd + 1) * SWA_HD] for hd in heads],
                                 axis=0).astype(bf16)
            sink = jnp.full((stack * BQ, 1), sinks_ref[layer, heads[0]], f32)
            for j in range(1, stack):
                sink = jnp.where(srow == j, sinks_ref[layer, heads[j]], sink)
            o = _attend(_dot_nt(qg, kg), valid, SWA_HD ** -0.5, vg, sink)
            for j, hd in enumerate(heads):
                z = s_z[rs, hd * SWA_HD:(hd + 1) * SWA_HD]
                mix_ref[rs, 2 * GROUP_W + hd * SWA_HD:2 * GROUP_W + (hd + 1) * SWA_HD] = (
                    o[j * BQ:(j + 1) * BQ] * _silu(z)).astype(mix_ref.dtype)
            yield

    kprev_ref[...] = k_n[T - WINDOW:T]
    vprev_ref[...] = v_n[T - WINDOW:T]
    kbuf_ref[...] = k_n[T - WINDOW:T]
    vbuf_ref[...] = v_n[T - WINDOW:T]
    yield


def _swa_decode_rows(seg, qkn_ref, kc_ref, vc_ref, sinks_ref, layer, mix_ref, bb, T):
    R = bb * T
    n_cache = bb * WINDOW
    nk = n_cache + R
    q_n, k_n = qkn_ref[:, 0:GROUP_W], qkn_ref[:, GROUP_W:GROUP_W + LANES]
    s_z = seg(OFF_SZ, GROUP_W)
    k_all = jnp.concatenate([kc_ref[s] for s in range(bb)] + [k_n], axis=0)
    v_all = jnp.concatenate([vc_ref[s] for s in range(bb)] + [seg(OFF_SV, LANES)], axis=0)
    rows = SWA_GROUP * R
    r = lax.broadcasted_iota(jnp.int32, (rows, nk), 0) % R
    c = lax.broadcasted_iota(jnp.int32, (rows, nk), 1)
    cached = c < n_cache
    key_seq = jnp.where(cached, c // WINDOW, (c - n_cache) // T)
    key_pos = jnp.where(cached, c % WINDOW, WINDOW + (c - n_cache) % T)
    dist = r % T + WINDOW - key_pos
    valid = (r // T == key_seq) & (dist >= 0) & (dist < WINDOW)
    srow = lax.broadcasted_iota(jnp.int32, (rows, 1), 0) // R
    for g in range(SWA_KV_HEADS):
        heads = [g * SWA_GROUP + j for j in range(SWA_GROUP)]
        qg = jnp.concatenate([q_n[:, hd * SWA_HD:(hd + 1) * SWA_HD] for hd in heads],
                             axis=0).astype(bf16)
        kg = k_all[:, g * SWA_HD:(g + 1) * SWA_HD].astype(bf16)
        vg = v_all[:, g * SWA_HD:(g + 1) * SWA_HD].astype(bf16)
        sink = jnp.full((rows, 1), sinks_ref[layer, heads[0]], f32)
        for j in range(1, SWA_GROUP):
            sink = jnp.where(srow == j, sinks_ref[layer, heads[j]], sink)
        o = _attend(_dot_nt(qg, kg), valid, SWA_HD ** -0.5, vg, sink)
        for j, hd in enumerate(heads):
            mix_ref[:, 2 * GROUP_W + hd * SWA_HD:2 * GROUP_W + (hd + 1) * SWA_HD] = (
                o[j * R:(j + 1) * R] * _silu(s_z[:, hd * SWA_HD:(hd + 1) * SWA_HD])).astype(mix_ref.dtype)


def _memory_attention(m_q, m_z, T, decode, gmq_ref, mk_ref, mv_ref, mix_ref, stk_ref):
    def stack_rows(pieces, slot):
        r, w = pieces[0].shape
        if r % 8 == 0:
            return jnp.concatenate(pieces, axis=0)
        for j, piece in enumerate(pieces):
            stk_ref[slot, j * r:(j + 1) * r, 0:w] = piece
        return stk_ref[slot, 0:len(pieces) * r, 0:w]

    def unstack_rows(x, n, slot):
        r, w = x.shape[0] // n, x.shape[1]
        if r % 8 == 0:
            return [x[j * r:(j + 1) * r] for j in range(n)]
        stk_ref[slot, 0:n * r, 0:w] = x
        return [stk_ref[slot, j * r:(j + 1) * r, 0:w] for j in range(n)]

    def mem_q(h):
        qh = m_q[:, h * MEM_HD:(h + 1) * MEM_HD]
        return qh * lax.rsqrt(jnp.mean(qh * qh, axis=-1, keepdims=True) + EPS) * gmq_ref[...]

    if decode:
        qs = stack_rows([mem_q(h) for h in range(MEM_HEADS)], 0).astype(bf16)
        s = _dot_nt(qs, mk_ref[...].astype(bf16))
        shape = (MEM_HEADS * T, MEM_HEADS * N_MEM)
        same_head = (lax.broadcasted_iota(jnp.int32, shape, 0) // T
                     == lax.broadcasted_iota(jnp.int32, shape, 1) % MEM_HEADS)
        yield
        o = _attend(s, same_head, MEM_HD ** -0.5, mv_ref[...].astype(bf16))
        o_all = unstack_rows(o, MEM_HEADS, 1)
    else:
        o_all = []
        for h in range(MEM_HEADS):
            hs = slice(h * MEM_HD, (h + 1) * MEM_HD)
            s = _dot_nt(mem_q(h).astype(bf16), mk_ref[:, hs].astype(bf16))
            o_all.append(_attend(s, None, MEM_HD ** -0.5, mv_ref[:, hs].astype(bf16)))
            yield
    for h in range(MEM_HEADS):
        mix_ref[:, 3 * GROUP_W + h * MEM_HD:3 * GROUP_W + (h + 1) * MEM_HD] = (
            o_all[h] * _silu(m_z[:, h * MEM_HD:(h + 1) * MEM_HD])).astype(mix_ref.dtype)
    yield


def _mixer(tokens, norm_w, mem_k, mem_v, mem_layer, state, params, layer, tile, bb, decode):
    if decode:
        b, width = state[0].shape[1], tokens.shape[1]
        L = tokens.shape[0] // b
    else:
        b, L, width = tokens.shape
    nt = L // tile
    assert nt == 1 or not decode, "a decode call covers each sequence with a single tile"
    conv_w, w_up, b_ga, g_go, g_sq, g_sk, bd, sinks, g_mq = params

    def tok(width):
        if decode:
            return pl.BlockSpec((bb * tile, width), lambda i, t: (i, 0))
        return pl.BlockSpec((bb, tile, width), lambda i, t: (i, t, 0))

    def per_seq(*shape):
        return pl.BlockSpec((bb,) + shape, lambda i, t: (i,) + (0,) * len(shape))

    def per_seq_at(lyr, *shape):
        return pl.BlockSpec((None, bb) + shape, lambda i, t: (lyr, i) + (0,) * len(shape))

    def param(a):
        if a.ndim == 2:
            return pl.BlockSpec(a.shape, lambda i, t: (0, 0))
        return pl.BlockSpec((None,) + a.shape[1:], lambda i, t: (layer,) + (0,) * (a.ndim - 1))

    kv_w = SWA_KV_HEADS * SWA_HD
    state_shapes = [(CONV_W - 1, GROUP_W), (GLA_HEADS, GLA_DK, GLA_DV), (WINDOW, kv_w),
                    (WINDOW, kv_w)]
    in_specs = [tok(width), per_seq_at(mem_layer, *mem_k.shape[2:]),
                per_seq_at(mem_layer, *mem_v.shape[2:])]
    args = [tokens, mem_k, mem_v]
    if decode:
        in_specs += [per_seq_at(layer, *s) for s in state_shapes]
        args += list(state)
    else:
        g_n, w_t = norm_w
        in_specs += [param(g_n), pl.BlockSpec(w_t.shape, lambda i, t: (0, 0),
                                              pipeline_mode=pl.Buffered(1))]
        args += [g_n, w_t]
    in_specs += [param(conv_w), param(w_up), param(b_ga), param(g_go), param(g_sq), param(g_sk),
                 pl.BlockSpec(bd.shape, lambda i, t: (0, 0)),
                 pl.BlockSpec(memory_space=pltpu.SMEM), param(g_mq)]
    args += [conv_w, w_up, b_ga, g_go, g_sq, g_sk, bd, sinks, g_mq]
    conv_window = pltpu.VMEM((bb, CONV_PAD + tile, GROUP_W), f32)
    unused = pltpu.VMEM((bb, 8, LANES), f32)
    if decode:
        out_shape = [jax.ShapeDtypeStruct((b * L, 4 * GROUP_W), f32)]
        scratch_shapes = [conv_window, unused, unused, unused,
                          pltpu.VMEM((bb * tile, D_IN - OFF_GZ), f32),
                          pltpu.VMEM((bb, N_STACK_SLOTS, MEM_HEADS * tile, LANES), f32),
                          pltpu.VMEM((bb * tile, GROUP_W + LANES), f32)]
    else:
        out_shape = [jax.ShapeDtypeStruct((b, L, 4 * GROUP_W), bf16)]
        scratch_shapes = [conv_window, pltpu.VMEM((bb, GLA_K_W, GLA_V_W), f32),
                          pltpu.VMEM((bb, WINDOW, kv_w), f32), pltpu.VMEM((bb, WINDOW, kv_w), f32),
                          unused, unused, unused]
    out_shape += [jax.ShapeDtypeStruct((b,) + s, f32) for s in state_shapes]
    out_specs = [tok(4 * GROUP_W)] + [per_seq(*s) for s in state_shapes]
    return pl.pallas_call(
        functools.partial(_mixer_kernel, tile=tile, decode=decode, layer=layer, bb=bb),
        grid=(b // bb, nt),
        in_specs=in_specs,
        out_specs=out_specs,
        out_shape=out_shape,
        scratch_shapes=scratch_shapes,
        compiler_params=pltpu.CompilerParams(
            dimension_semantics=("arbitrary", "arbitrary"), vmem_limit_bytes=VMEM_LIMIT),
        name="mixer_decode" if decode else "mixer_prompt",
    )(*args)


def _out_proj_kernel(mix_ref, w_ref, x_ref, y_ref, wb_ref):
    @pl.when(pl.program_id(1) == 0)
    def _():
        wb_ref[...] = w_ref[...].astype(bf16)

    y_ref[...] = x_ref[...] + _dot(mix_ref[...].astype(bf16), wb_ref[...])


def _out_proj(mix, w, x, l, tm, tn):
    m, k = mix.shape
    n = w.shape[2]
    return pl.pallas_call(
        _out_proj_kernel,
        grid=(n // tn, m // tm),
        in_specs=[
            pl.BlockSpec((tm, k), lambda j, i: (i, 0)),
            pl.BlockSpec((None, k, tn), lambda j, i: (l, 0, j),
                         pipeline_mode=pl.Buffered(1) if tn == n else None),
            pl.BlockSpec((tm, tn), lambda j, i: (i, j)),
        ],
        out_specs=pl.BlockSpec((tm, tn), lambda j, i: (i, j)),
        out_shape=jax.ShapeDtypeStruct((m, n), f32),
        scratch_shapes=[pltpu.VMEM((k, tn), bf16)],
        compiler_params=pltpu.CompilerParams(
            dimension_semantics=("arbitrary", "arbitrary"), vmem_limit_bytes=VMEM_LIMIT),
        name="out_proj",
    )(mix, w, x)


PROMPT_TILE = 512
DECODE_SEQS_PER_STEP = 8
PROJ_TN = 1536
OUT_TM, OUT_TN = 512, 2048

_LANE = np.arange(GROUP_W)
HEAD_BLOCK_DIAG = _LANE[:, None] // SWA_HD == _LANE[None, :] // SWA_HD


def kernel(x_prompt, x_sample, mem_prompt, state_conv, state_gla, cache_swa_k, cache_swa_v,
           cache_mem_k, cache_mem_v, g_norm, w_in, conv_w, w_gla_a_up, b_gla_a, g_gla_o,
           g_swa_q, g_swa_k, swa_sinks, g_mem, w_mem_kv, g_mem_q, g_mem_k, w_out):
    depth = w_in.shape[0]
    bp, lp, _ = x_prompt.shape
    bs, ls, _ = x_sample.shape
    hp = x_prompt.reshape(bp * lp, D_MODEL)
    hs = x_sample.reshape(bs * ls, D_MODEL)

    params = (conv_w, w_gla_a_up, b_gla_a, g_gla_o,
              jnp.tile(g_swa_q, (1, SWA_HEADS)), jnp.tile(g_swa_k, (1, SWA_KV_HEADS)),
              jnp.asarray(HEAD_BLOCK_DIAG, bf16), swa_sinks, g_mem_q)
    g_n, g_m, g_mk = g_norm, g_mem, g_mem_k
    w_in_t = jnp.swapaxes(w_in, 1, 2)
    kv_w = SWA_KV_HEADS * SWA_HD
    state = (state_conv, state_gla, cache_swa_k.reshape(depth, bs, WINDOW, kv_w),
             cache_swa_v.reshape(depth, bs, WINDOW, kv_w))
    mem_k_s = cache_mem_k.reshape(depth, bs, N_MEM * MEM_HEADS, MEM_HD)
    mem_v_s = cache_mem_v.reshape(depth, bs, N_MEM * MEM_HEADS, MEM_HD)

    mk, mv, mem_k_p, mem_v_p = _memory_kv(mem_prompt, g_m, w_mem_kv, g_mk)
    outs = [[] for _ in range(8)]
    for l in range(depth):
        proj, w_bf = _norm_matmul(hs, g_n, w_in_t, l, PROJ_TN)

        mix, c, s, kb, vb = _mixer(hp.reshape(bp, lp, D_MODEL), (g_n, w_bf), mk, mv, l,
                                   None, params, l, PROMPT_TILE, 1, decode=False)
        hp = _out_proj(mix.reshape(bp * lp, 4 * GROUP_W), w_out, hp, l, OUT_TM, OUT_TN)
        for lst, a in zip(outs[:4], (
                c, s, kb.reshape(bp, WINDOW, SWA_KV_HEADS, SWA_HD),
                vb.reshape(bp, WINDOW, SWA_KV_HEADS, SWA_HD))):
            lst.append(a)

        mix, c, s, kb, vb = _mixer(proj, None, mem_k_s, mem_v_s, l, state, params, l, ls,
                                   DECODE_SEQS_PER_STEP, decode=True)
        hs = _out_proj(mix, w_out, hs, l, bs * ls, OUT_TN)
        for lst, a in zip(outs[4:], (
                c, s, kb.reshape(bs, WINDOW, SWA_KV_HEADS, SWA_HD),
                vb.reshape(bs, WINDOW, SWA_KV_HEADS, SWA_HD))):
            lst.append(a)

    stacked = [jnp.stack(o) for o in outs]
    return (hp.reshape(bp, lp, D_MODEL), hs.reshape(bs, ls, D_MODEL),
            *stacked[:4], mem_k_p, mem_v_p, *stacked[4:])
```

```python
import functools
import itertools

import jax
import jax.numpy as jnp
import numpy as np
from jax import lax
from jax.experimental import pallas as pl
from jax.experimental.pallas import tpu as pltpu

f32 = jnp.float32
bf16 = jnp.bfloat16

D_MODEL = 2048
GROUP_W = 512
GLA_HEADS = 4
GLA_DK = 64
GLA_DV = 128
GLA_RANK = 16
GLA_TAU = 16.0
GLA_CHUNK = 64
SWA_HEADS = 8
SWA_KV_HEADS = 2
SWA_HD = 64
SWA_GROUP = SWA_HEADS // SWA_KV_HEADS
WINDOW = 128
N_MEM = 256
MEM_HEADS = 4
MEM_HD = 128
CONV_W = 3
EPS = 1e-6

LANES = 128
MXU_CHUNK = 256

D_IN = 5904
OFF_AB, OFF_AC, OFF_AH, OFF_AZ = 0, 512, 1024, 1536
OFF_GQ, OFF_GK, OFF_GV, OFF_GA, OFF_GZ = 2048, 2304, 2560, 3072, 3088
OFF_SQ, OFF_SK, OFF_SV, OFF_SZ = 3600, 4112, 4240, 4368
OFF_MQ, OFF_MZ = 4880, 5392

VMEM_LIMIT = 60 * 1024 * 1024


def _dot(a, b):
    return jnp.dot(a, b, preferred_element_type=f32)


def _dot_nt(a, b):
    return lax.dot_general(a, b, (((1,), (1,)), ((), ())), preferred_element_type=f32)


def _dot_tn(a, b):
    return lax.dot_general(a, b, (((0,), (0,)), ((), ())), preferred_element_type=f32)


def _split3(x):
    hi = x.astype(bf16)
    r = x - hi.astype(f32)
    mid = r.astype(bf16)
    lo = (r - mid.astype(f32)).astype(bf16)
    return hi, mid, lo


LOG2_E = 1.4426950408889634


def _attend(scores, valid, scale, values, sink=None):
    if valid is not None:
        scores = jnp.where(valid, scores, -jnp.inf)
    m = jnp.max(scores, axis=-1, keepdims=True)
    if sink is not None:
        sink = sink * (1.0 / scale)
        m = jnp.maximum(m, sink)
    e = jnp.exp2((scores - m) * (scale * LOG2_E))
    denom = jnp.sum(e, axis=-1, keepdims=True)
    if sink is not None:
        denom = denom + jnp.exp2((sink - m) * (scale * LOG2_E))
    return _dot(e.astype(bf16), values) / denom


def _silu(x):
    return x * jax.nn.sigmoid(x)


def _log_sigmoid(x):
    return jnp.minimum(x, 0.0) - jnp.log1p(jnp.exp(-jnp.abs(x)))


def _norm_matmul_kernel(x_ref, g_ref, wt_ref, o_ref, wb_ref, hn_ref, *, layer):
    x = x_ref[...]
    y = x * lax.rsqrt(jnp.mean(x * x, axis=-1, keepdims=True) + EPS)
    hn_ref[...] = (y * g_ref[layer:layer + 1, :]).astype(bf16)
    wb_ref[...] = wt_ref[...].astype(bf16)
    o_ref[...] = _dot_nt(hn_ref[...], wb_ref[...])


def _norm_matmul(x, g, wt, l, tn):
    m, k = x.shape
    n = wt.shape[1]
    return pl.pallas_call(
        functools.partial(_norm_matmul_kernel, layer=l),
        grid=(pl.cdiv(n, tn),),
        in_specs=[
            pl.BlockSpec((m, k), lambda j: (0, 0)),
            pl.BlockSpec(g.shape, lambda j: (0, 0)),
            pl.BlockSpec((None, tn, k), lambda j: (l, j, 0)),
        ],
        out_specs=[pl.BlockSpec((m, tn), lambda j: (0, j)), pl.BlockSpec((tn, k), lambda j: (j, 0))],
        out_shape=[jax.ShapeDtypeStruct((m, n), f32), jax.ShapeDtypeStruct((n, k), bf16)],
        scratch_shapes=[pltpu.VMEM((m, k), bf16)],
        compiler_params=pltpu.CompilerParams(
            dimension_semantics=("arbitrary",), vmem_limit_bytes=VMEM_LIMIT),
        name="norm_in_proj",
    )(x, g, wt)


def _memory_kv_kernel(x_ref, g_ref, w_ref, gk_ref, k_ref, v_ref, k4_ref, v4_ref, wb_ref):
    @pl.when(pl.program_id(1) == 0)
    def _():
        wb_ref[...] = w_ref[...].astype(bf16)

    layer = pl.program_id(0)
    g, gk = g_ref[pl.ds(layer, 1), :], gk_ref[pl.ds(layer, 1), :]
    x = x_ref[...]
    y = x * lax.rsqrt(jnp.mean(x * x, axis=-1, keepdims=True) + EPS)
    kv = _dot((y * g).astype(bf16), wb_ref[...])
    for h in range(MEM_HEADS):
        kh = kv[:, h * MEM_HD:(h + 1) * MEM_HD]
        kh = kh * lax.rsqrt(jnp.mean(kh * kh, axis=-1, keepdims=True) + EPS) * gk
        vh = kv[:, GROUP_W + h * MEM_HD:GROUP_W + (h + 1) * MEM_HD]
        k_ref[:, h * MEM_HD:(h + 1) * MEM_HD] = kh
        k4_ref[:, h, :] = kh
        v4_ref[:, h, :] = vh
    v_ref[...] = kv[:, GROUP_W:]


def _memory_kv(mem, g, w, gk):
    depth, b = w.shape[0], mem.shape[0]
    flat = jax.ShapeDtypeStruct((depth, b, N_MEM, GROUP_W), f32)
    split = jax.ShapeDtypeStruct((depth, b, N_MEM, MEM_HEADS, MEM_HD), f32)
    return pl.pallas_call(
        _memory_kv_kernel,
        grid=(depth, b),
        in_specs=[
            pl.BlockSpec((None, N_MEM, D_MODEL), lambda l, i: (i, 0, 0)),
            pl.BlockSpec(g.shape, lambda l, i: (0, 0)),
            pl.BlockSpec((None, D_MODEL, 2 * GROUP_W), lambda l, i: (l, 0, 0)),
            pl.BlockSpec(gk.shape, lambda l, i: (0, 0)),
        ],
        out_specs=[pl.BlockSpec((None, None, N_MEM, GROUP_W), lambda l, i: (l, i, 0, 0))] * 2
        + [pl.BlockSpec((None, None, N_MEM, MEM_HEADS, MEM_HD), lambda l, i: (l, i, 0, 0, 0))] * 2,
        out_shape=[flat, flat, split, split],
        scratch_shapes=[pltpu.VMEM((D_MODEL, 2 * GROUP_W), bf16)],
        compiler_params=pltpu.CompilerParams(
            dimension_semantics=("arbitrary", "arbitrary"), vmem_limit_bytes=VMEM_LIMIT),
        name="memory_kv",
    )(mem, g, w, gk)


CONV_PAD = 8
N_STACK_SLOTS = 2
N_SEQ_IN_PROMPT, N_SEQ_IN_DECODE, N_PARAMS_PROMPT, N_PARAMS_DECODE, N_OUT = 3, 7, 11, 9, 5


def _mixer_kernel(*refs, tile, decode, layer, bb):
    n_seq = N_SEQ_IN_DECODE if decode else N_SEQ_IN_PROMPT
    n_par = N_PARAMS_DECODE if decode else N_PARAMS_PROMPT
    seq_in = refs[:n_seq]
    params = list(refs[n_seq:n_seq + n_par])
    outs = refs[n_seq + n_par:n_seq + n_par + N_OUT]
    scratch = refs[n_seq + n_par + N_OUT:]
    for k in (-7, -6, -5, -4, -1) + (() if decode else (0,)):
        params[k] = params[k].at[layer:layer + 1]

    def view(ref, s):
        if decode and ref.ndim == 2:
            return _RowWindow(ref, s * tile, tile)
        return ref.at[s]

    if decode:
        p_ref, tail_ref, qkn_ref = seq_in[0], scratch[4], scratch[6]
        bd_ref, gsq_ref, gsk_ref = params[-3], params[-5], params[-4]
        tail_ref[...] = p_ref[:, OFF_GZ:D_IN]
        sq0, sk0 = OFF_SQ - OFF_GZ, OFF_SK - OFF_GZ
        qkn_ref[:, 0:GROUP_W] = _head_norm(tail_ref[:, sq0:sq0 + GROUP_W], gsq_ref[...], bd_ref)
        qkn_ref[:, GROUP_W:GROUP_W + LANES] = _head_norm(tail_ref[:, sk0:sk0 + LANES], gsk_ref[...],
                                                         bd_ref)

        def seg_rows(off, width):
            if off < OFF_GZ:
                return p_ref[:, off:off + width]
            return tail_ref[:, off - OFF_GZ:off - OFF_GZ + width]

        gla_in_ref, gla_out_ref = seq_in[4], outs[2]
        for _ in _gla_group(seg_rows, bb * tile, tile, params[-8:-5], outs[0],
                            lambda c: _gla_block_diag(gla_in_ref.at[c]),
                            lambda c, state: _store_gla_state(gla_out_ref.at[c], state), carry=False):
            pass
        kc_ref, vc_ref = scratch[2], scratch[3]
        for s in range(bb):
            kc_ref[s] = seq_in[5][s].T
            vc_ref[s] = seq_in[6][s].T
        _swa_decode_rows(seg_rows, qkn_ref, kc_ref, vc_ref, params[-2], layer, outs[0], bb, tile)

    stages = [_mixer_seq([view(r, s) for r in seq_in], params, [view(r, s) for r in outs],
                         [view(r, s) for r in scratch], tile=tile, decode=decode, layer=layer)
              for s in range(bb)]
    for _ in itertools.zip_longest(*stages):
        pass


class _RowWindow:
    def __init__(self, ref, start, size):
        self.ref, self.start, self.size, self.dtype = ref, start, size, ref.dtype

    def _index(self, idx):
        rows, cols = (slice(None), slice(None)) if idx is Ellipsis else idx
        lo, hi, _ = rows.indices(self.size)
        return slice(self.start + lo, self.start + hi), cols

    def __getitem__(self, idx):
        return self.ref[self._index(idx)]

    def __setitem__(self, idx, value):
        self.ref[self._index(idx)] = value


GLA_INTRA_ROWS = MXU_CHUNK
GLA_K_W = GLA_HEADS * GLA_DK
GLA_V_W = GLA_HEADS * GLA_DV


def _gla_block_diag(state_ref):
    rows = []
    for h in range(GLA_HEADS):
        blocks = [state_ref[h] if j == h else jnp.zeros((GLA_DK, GLA_DV), f32)
                  for j in range(GLA_HEADS)]
        rows.append(jnp.concatenate(blocks, axis=1))
    return jnp.concatenate(rows, axis=0)


def _store_gla_state(state_ref, state):
    for h in range(GLA_HEADS):
        state_ref[h] = state[h * GLA_DK:(h + 1) * GLA_DK, h * GLA_DV:(h + 1) * GLA_DV]


def _gla_group(seg, T, C, params, mix_ref, state_in, state_out, carry, seg_t=None):
    wup_ref, bga_ref, ggo_ref = params
    n_chunk = T // C
    G = min(T, GLA_INTRA_ROWS)
    groups = [slice(i * G, (i + 1) * G) for i in range(T // G)]
    row = lax.broadcasted_iota(jnp.int32, (G, G), 0)
    col = lax.broadcasted_iota(jnp.int32, (G, G), 1)
    causal = (row // C == col // C) & (row >= col)
    if seg_t is None:
        a_up = _dot(seg(OFF_GA, GLA_RANK).astype(bf16), wup_ref[...].astype(bf16))
    else:
        a_up = _dot_tn(seg_t(OFF_GA, GLA_RANK).astype(bf16), wup_ref[...].astype(bf16))
    log_a = _log_sigmoid(a_up + bga_ref[...]) * (1.0 / GLA_TAU)
    la3 = _split3(log_a)
    yield
    tril = jnp.where(causal, 1.0, 0.0).astype(bf16)
    in_chunk = jnp.where(lax.broadcasted_iota(jnp.int32, (T, LANES), 0) // C
                         == lax.broadcasted_iota(jnp.int32, (T, LANES), 1), 1.0, 0.0).astype(bf16)
    cum = jnp.concatenate(
        [_dot(tril, la3[0][r]) + _dot(tril, la3[1][r]) + _dot(tril, la3[2][r]) for r in groups],
        axis=0)
    tot_t = (_dot_tn(la3[0], in_chunk) + _dot_tn(la3[1], in_chunk)
             + _dot_tn(la3[2], in_chunk))
    decay_t = jnp.exp(tot_t)
    yield
    g_k = seg(OFF_GK, GLA_K_W)
    qd = ((seg(OFF_GQ, GLA_K_W) * (GLA_DK ** -0.5)) * jnp.exp(cum)).astype(bf16)
    kd = (g_k * jnp.exp(-cum)).astype(bf16)
    k_tail = jnp.concatenate(
        [g_k[c * C:(c + 1) * C] * jnp.exp(cum[(c + 1) * C - 1:(c + 1) * C] - cum[c * C:(c + 1) * C])
         for c in range(n_chunk)], axis=0) if n_chunk > 1 else g_k * jnp.exp(cum[T - 1:T] - cum)
    kt = k_tail.astype(bf16)
    yield
    v_b = seg(OFF_GV, GLA_V_W).astype(bf16)
    g_z = seg(OFF_GZ, GROUP_W)
    yield

    o_intra = []
    for r in groups:
        o_heads = []
        for h in range(GLA_HEADS):
            ks = slice(h * GLA_DK, (h + 1) * GLA_DK)
            attn = jnp.where(causal, _dot_nt(qd[r, ks], kd[r, ks]), 0.0).astype(bf16)
            o_heads.append(_dot(attn, v_b[r, h * GLA_DV:(h + 1) * GLA_DV]))
        o_intra.append(jnp.concatenate(o_heads, axis=1))
        yield
    o_intra = jnp.concatenate(o_intra, axis=0)

    shape = (GLA_K_W, GLA_V_W)
    on_diag = (lax.broadcasted_iota(jnp.int32, shape, 0) // GLA_DK
               == lax.broadcasted_iota(jnp.int32, shape, 1) // GLA_DV)
    o_chunks = []
    state = None
    for c in range(n_chunk):
        rs = slice(c * C, (c + 1) * C)
        if c == 0 or not carry:
            state = state_in(c)
        o_chunks.append(o_intra[rs] + _dot(qd[rs], state.astype(bf16)))
        update = jnp.where(on_diag, _dot_tn(kt[rs], v_b[rs]), 0.0)
        state = decay_t[:, c:c + 1] * state + update
        state_out(c, state)
        yield
    o = jnp.concatenate(o_chunks, axis=0) if n_chunk > 1 else o_chunks[0]
    for h in range(GLA_HEADS):
        vs = slice(h * GLA_DV, (h + 1) * GLA_DV)
        o_h = o[:, vs]
        o_h = o_h * lax.rsqrt(jnp.mean(o_h * o_h, axis=-1, keepdims=True) + EPS) * ggo_ref[...]
        mix_ref[:, GROUP_W + h * GLA_DV:GROUP_W + (h + 1) * GLA_DV] = (
            o_h * _silu(g_z[:, vs])).astype(mix_ref.dtype)
        yield


def _head_norm(x, g, bd_ref):
    rows, n_lanes = x.shape
    w = min(n_lanes, MXU_CHUNK)
    pieces = n_lanes // w
    bd = bd_ref[0:w, 0:w]

    def head_sums(v):
        stacked = jnp.concatenate([v[:, i * w:(i + 1) * w] for i in range(pieces)], axis=0)
        r = _dot(stacked, bd)
        return jnp.concatenate([r[i * rows:(i + 1) * rows] for i in range(pieces)], axis=1)

    sq = x * x
    hi = sq.astype(bf16)
    lo = (sq - hi.astype(f32)).astype(bf16)
    ms = (head_sums(hi) + head_sums(lo)) * (1.0 / SWA_HD)
    return x * lax.rsqrt(ms + EPS) * g


def _mixer_seq(seq_in, params, outs, scratch, *, tile, decode, layer):
    if decode:
        p_ref, mk_ref, mv_ref, conv_in_ref, gla_in_ref, kc_ref, vc_ref = seq_in
    else:
        x_ref, mk_ref, mv_ref = seq_in
        gn_ref, wt_ref = params[:2]
    (convw_ref, wup_ref, bga_ref, ggo_ref, gsq_ref, gsk_ref, bd_ref, sinks_ref,
     gmq_ref) = params[-N_PARAMS_DECODE:]
    mix_ref, conv_out_ref, gla_out_ref, kbuf_ref, vbuf_ref = outs
    ext_ref, s_ref, kprev_ref, vprev_ref, tail_ref, stk_ref, qkn_ref, knew_ref, vnew_ref = scratch

    T = tile
    t = pl.program_id(1)

    def init_state():
        ext_ref[0:CONV_PAD, :] = jnp.zeros((CONV_PAD, GROUP_W), f32)
        if decode:
            ext_ref[CONV_PAD - (CONV_W - 1):CONV_PAD, :] = conv_in_ref[...]
        else:
            s_ref[...] = jnp.zeros_like(s_ref)
            kprev_ref[...] = jnp.zeros_like(kprev_ref)
            vprev_ref[...] = jnp.zeros_like(vprev_ref)

    if decode:
        init_state()
    else:
        pl.when(t == 0)(init_state)
    yield

    if decode:
        def seg(off, width):
            if off < OFF_GZ:
                return p_ref[:, off:off + width]
            return tail_ref[:, off - OFF_GZ:off - OFF_GZ + width]
    else:
        x = x_ref[...]
        hn = (x * lax.rsqrt(jnp.mean(x * x, axis=-1, keepdims=True) + EPS) * gn_ref[...]).astype(bf16)

        def seg(off, width):
            return _dot_nt(hn, wt_ref[off:off + width, :])

        def seg_t(off, width):
            return _dot_nt(wt_ref[off:off + width, :], hn)

    def group_a():
        u = seg(OFF_AC, GROUP_W) * seg(OFF_AH, GROUP_W)
        ext_ref[CONV_PAD:CONV_PAD + T, :] = u
        yield
        conv = (convw_ref[0:1, :] * ext_ref[CONV_PAD - 2:CONV_PAD - 2 + T, :]
                + convw_ref[1:2, :] * ext_ref[CONV_PAD - 1:CONV_PAD - 1 + T, :]
                + convw_ref[2:3, :] * u)
        a_b = seg(OFF_AB, GROUP_W)
        yield
        mix_ref[:, 0:GROUP_W] = (a_b * conv * _silu(seg(OFF_AZ, GROUP_W))).astype(mix_ref.dtype)
        conv_state = ext_ref[CONV_PAD + T - 2:CONV_PAD + T, :]
        ext_ref[CONV_PAD - 2:CONV_PAD, :] = conv_state
        conv_out_ref[...] = conv_state
        yield

    def group_b():
        def keep_state(c, state):
            if c == T // GLA_CHUNK - 1:
                s_ref[...] = state
                _store_gla_state(gla_out_ref, state)

        yield from _gla_group(seg, T, GLA_CHUNK, (wup_ref, bga_ref, ggo_ref), mix_ref,
                              lambda c: s_ref[...], keep_state, carry=True, seg_t=seg_t)

    def group_d():
        m_q = seg(OFF_MQ, GROUP_W)
        m_z = seg(OFF_MZ, GROUP_W)
        yield
        yield from _memory_attention(m_q, m_z, T, decode, gmq_ref, mk_ref, mv_ref, mix_ref, stk_ref)

    if decode:
        yield from group_a()
        knew_ref[0:WINDOW - T, :] = kprev_ref[T:WINDOW, :]
        knew_ref[WINDOW - T:WINDOW, :] = qkn_ref[:, GROUP_W:GROUP_W + LANES]
        vnew_ref[0:WINDOW - T, :] = vprev_ref[T:WINDOW, :]
        vnew_ref[WINDOW - T:WINDOW, :] = seg(OFF_SV, LANES)
        yield
        kbuf_ref[...] = knew_ref[...].T
        vbuf_ref[...] = vnew_ref[...].T
        yield
        yield from group_d()
    else:
        group_c = _swa_prompt_tile(seg, T, t, layer, (gsq_ref, gsk_ref, bd_ref, sinks_ref), mix_ref,
                                   kprev_ref, vprev_ref, kbuf_ref, vbuf_ref)
        for _ in itertools.zip_longest(group_b(), itertools.chain(group_c, group_a(), group_d())):
            pass
        yield


def _swa_prompt_tile(seg, T, t, layer, params, mix_ref, kprev_ref, vprev_ref, kbuf_ref, vbuf_ref):
    gsq_ref, gsk_ref, bd_ref, sinks_ref = params
    s_z = seg(OFF_SZ, GROUP_W)
    s_q = seg(OFF_SQ, GROUP_W)
    s_kv = seg(OFF_SK, 2 * LANES)
    q_n = _head_norm(s_q, gsq_ref[...], bd_ref)
    k_n = _head_norm(s_kv[:, 0:LANES], gsk_ref[...], bd_ref)
    v_n = s_kv[:, LANES:2 * LANES]
    yield

    BQ = WINDOW
    n_blk = T // BQ
    stack = SWA_GROUP
    nk = WINDOW + BQ
    qi = lax.broadcasted_iota(jnp.int32, (stack * BQ, nk), 0) % BQ
    kj = lax.broadcasted_iota(jnp.int32, (stack * BQ, nk), 1)
    dist = qi + WINDOW - kj
    band = (dist >= 0) & (dist < WINDOW)
    srow = lax.broadcasted_iota(jnp.int32, (stack * BQ, 1), 0) // BQ
    for blk in range(n_blk):
        rs = slice(blk * BQ, (blk + 1) * BQ)
        if blk == 0:
            k_prev, v_prev = kprev_ref[...], vprev_ref[...]
            valid = band & ((kj >= WINDOW) | (t > 0))
        else:
            ps = slice((blk - 1) * BQ, blk * BQ)
            k_prev, v_prev = k_n[ps], v_n[ps]
            valid = band
        k_cat = jnp.concatenate([k_prev, k_n[rs]], axis=0)
        v_cat = jnp.concatenate([v_prev, v_n[rs]], axis=0)
        for g in range(SWA_KV_HEADS):
            kg = k_cat[:, g * SWA_HD:(g + 1) * SWA_HD].astype(bf16)
            vg = v_cat[:, g * SWA_HD:(g + 1) * SWA_HD].astype(bf16)
            heads = [g * SWA_GROUP + j for j in range(stack)]
            qg = jnp.concatenate([q_n[rs, hd * SWA_HD:(hd + 1) * SWA_HD] for hd in heads],
                                 axis=0).astype(bf16)
            sink = jnp.full((stack * BQ, 1), sinks_ref[layer, heads[0]], f32)
            for j in range(1, stack):
                sink = jnp.where(srow == j, sinks_ref[layer, heads[j]], sink)
            o = _attend(_dot_nt(qg, kg), valid, SWA_HD ** -0.5, vg, sink)
            for j, hd in enumerate(heads):
                z = s_z[rs, hd * SWA_HD:(hd + 1) * SWA_HD]
                mix_ref[rs, 2 * GROUP_W + hd * SWA_HD:2 * GROUP_W + (hd + 1) * SWA_HD] = (
                    o[j * BQ:(j + 1) * BQ] * _silu(z)).astype(mix_ref.dtype)
            yield

    kprev_ref[...] = k_n[T - WINDOW:T]
    vprev_ref[...] = v_n[T - WINDOW:T]
    kbuf_ref[...] = k_n[T - WINDOW:T].T
    vbuf_ref[...] = v_n[T - WINDOW:T].T
    yield


def _swa_decode_rows(seg, qkn_ref, kc_ref, vc_ref, sinks_ref, layer, mix_ref, bb, T):
    R = bb * T
    n_cache = bb * WINDOW
    nk = n_cache + R
    q_n, k_n = qkn_ref[:, 0:GROUP_W], qkn_ref[:, GROUP_W:GROUP_W + LANES]
    s_z = seg(OFF_SZ, GROUP_W)
    k_all = jnp.concatenate([kc_ref[s] for s in range(bb)] + [k_n], axis=0)
    v_all = jnp.concatenate([vc_ref[s] for s in range(bb)] + [seg(OFF_SV, LANES)], axis=0)
    rows = SWA_GROUP * R
    r = lax.broadcasted_iota(jnp.int32, (rows, nk), 0) % R
    c = lax.broadcasted_iota(jnp.int32, (rows, nk), 1)
    cached = c < n_cache
    key_seq = jnp.where(cached, c // WINDOW, (c - n_cache) // T)
    key_pos = jnp.where(cached, c % WINDOW, WINDOW + (c - n_cache) % T)
    dist = r % T + WINDOW - key_pos
    valid = (r // T == key_seq) & (dist >= 0) & (dist < WINDOW)
    srow = lax.broadcasted_iota(jnp.int32, (rows, 1), 0) // R
    for g in range(SWA_KV_HEADS):
        heads = [g * SWA_GROUP + j for j in range(SWA_GROUP)]
        qg = jnp.concatenate([q_n[:, hd * SWA_HD:(hd + 1) * SWA_HD] for hd in heads],
                             axis=0).astype(bf16)
        kg = k_all[:, g * SWA_HD:(g + 1) * SWA_HD].astype(bf16)
        vg = v_all[:, g * SWA_HD:(g + 1) * SWA_HD].astype(bf16)
        sink = jnp.full((rows, 1), sinks_ref[layer, heads[0]], f32)
        for j in range(1, SWA_GROUP):
            sink = jnp.where(srow == j, sinks_ref[layer, heads[j]], sink)
        o = _attend(_dot_nt(qg, kg), valid, SWA_HD ** -0.5, vg, sink)
        for j, hd in enumerate(heads):
            mix_ref[:, 2 * GROUP_W + hd * SWA_HD:2 * GROUP_W + (hd + 1) * SWA_HD] = (
                o[j * R:(j + 1) * R] * _silu(s_z[:, hd * SWA_HD:(hd + 1) * SWA_HD])).astype(mix_ref.dtype)


def _memory_attention(m_q, m_z, T, decode, gmq_ref, mk_ref, mv_ref, mix_ref, stk_ref):
    def stack_rows(pieces, slot):
        r, w = pieces[0].shape
        if r % 8 == 0:
            return jnp.concatenate(pieces, axis=0)
        for j, piece in enumerate(pieces):
            stk_ref[slot, j * r:(j + 1) * r, 0:w] = piece
        return stk_ref[slot, 0:len(pieces) * r, 0:w]

    def unstack_rows(x, n, slot):
        r, w = x.shape[0] // n, x.shape[1]
        if r % 8 == 0:
            return [x[j * r:(j + 1) * r] for j in range(n)]
        stk_ref[slot, 0:n * r, 0:w] = x
        return [stk_ref[slot, j * r:(j + 1) * r, 0:w] for j in range(n)]

    def mem_q(h):
        qh = m_q[:, h * MEM_HD:(h + 1) * MEM_HD]
        return qh * lax.rsqrt(jnp.mean(qh * qh, axis=-1, keepdims=True) + EPS) * gmq_ref[...]

    if decode:
        qs = stack_rows([mem_q(h) for h in range(MEM_HEADS)], 0).astype(bf16)
        s = _dot_nt(qs, mk_ref[...].astype(bf16))
        shape = (MEM_HEADS * T, MEM_HEADS * N_MEM)
        same_head = (lax.broadcasted_iota(jnp.int32, shape, 0) // T
                     == lax.broadcasted_iota(jnp.int32, shape, 1) % MEM_HEADS)
        yield
        o = _attend(s, same_head, MEM_HD ** -0.5, mv_ref[...].astype(bf16))
        o_all = unstack_rows(o, MEM_HEADS, 1)
    else:
        o_all = []
        for h in range(MEM_HEADS):
            hs = slice(h * MEM_HD, (h + 1) * MEM_HD)
            s = _dot_nt(mem_q(h).astype(bf16), mk_ref[:, hs].astype(bf16))
            o_all.append(_attend(s, None, MEM_HD ** -0.5, mv_ref[:, hs].astype(bf16)))
            yield
    for h in range(MEM_HEADS):
        mix_ref[:, 3 * GROUP_W + h * MEM_HD:3 * GROUP_W + (h + 1) * MEM_HD] = (
            o_all[h] * _silu(m_z[:, h * MEM_HD:(h + 1) * MEM_HD])).astype(mix_ref.dtype)
    yield


def _mixer(tokens, norm_w, mem_k, mem_v, mem_layer, state, params, layer, tile, bb, decode):
    if decode:
        b, width = state[0].shape[1], tokens.shape[1]
        L = tokens.shape[0] // b
    else:
        b, L, width = tokens.shape
    nt = L // tile
    assert nt == 1 or not decode, "a decode call covers each sequence with a single tile"
    conv_w, w_up, b_ga, g_go, g_sq, g_sk, bd, sinks, g_mq = params

    def tok(width):
        if decode:
            return pl.BlockSpec((bb * tile, width), lambda i, t: (i, 0))
        return pl.BlockSpec((bb, tile, width), lambda i, t: (i, t, 0))

    def per_seq(*shape):
        return pl.BlockSpec((bb,) + shape, lambda i, t: (i,) + (0,) * len(shape))

    def per_seq_at(lyr, *shape):
        return pl.BlockSpec((None, bb) + shape, lambda i, t: (lyr, i) + (0,) * len(shape))

    def param(a):
        if a.ndim == 2:
            return pl.BlockSpec(a.shape, lambda i, t: (0, 0))
        return pl.BlockSpec((None,) + a.shape[1:], lambda i, t: (layer,) + (0,) * (a.ndim - 1))

    kv_w = SWA_KV_HEADS * SWA_HD
    state_shapes = [(CONV_W - 1, GROUP_W), (GLA_HEADS, GLA_DK, GLA_DV), (WINDOW, kv_w),
                    (WINDOW, kv_w)]
    in_specs = [tok(width), per_seq_at(mem_layer, *mem_k.shape[2:]),
                per_seq_at(mem_layer, *mem_v.shape[2:])]
    args = [tokens, mem_k, mem_v]
    if decode:
        in_specs += [per_seq_at(layer, *s) for s in state_shapes]
        args += list(state)
    else:
        g_n, w_t = norm_w
        in_specs += [param(g_n), pl.BlockSpec(w_t.shape, lambda i, t: (0, 0),
                                              pipeline_mode=pl.Buffered(1))]
        args += [g_n, w_t]
    in_specs += [param(conv_w), param(w_up), param(b_ga), param(g_go), param(g_sq), param(g_sk),
                 pl.BlockSpec(bd.shape, lambda i, t: (0, 0)),
                 pl.BlockSpec(memory_space=pltpu.SMEM), param(g_mq)]
    args += [conv_w, w_up, b_ga, g_go, g_sq, g_sk, bd, sinks, g_mq]
    conv_window = pltpu.VMEM((bb, CONV_PAD + tile, GROUP_W), f32)
    window = pltpu.VMEM((bb, WINDOW, kv_w), f32)
    unused = pltpu.VMEM((bb, 8, LANES), f32)
    if decode:
        out_shape = [jax.ShapeDtypeStruct((b * L, 4 * GROUP_W), f32)]
        scratch_shapes = [conv_window, unused, window, window,
                          pltpu.VMEM((bb * tile, D_IN - OFF_GZ), f32),
                          pltpu.VMEM((bb, N_STACK_SLOTS, MEM_HEADS * tile, LANES), f32),
                          pltpu.VMEM((bb * tile, GROUP_W + LANES), f32), window, window]
    else:
        out_shape = [jax.ShapeDtypeStruct((b, L, 4 * GROUP_W), bf16)]
        scratch_shapes = [conv_window, pltpu.VMEM((bb, GLA_K_W, GLA_V_W), f32), window, window,
                          unused, unused, unused, unused, unused]
    out_shape += [jax.ShapeDtypeStruct((b,) + s, f32) for s in state_shapes]
    out_specs = [tok(4 * GROUP_W)] + [per_seq(*s) for s in state_shapes]
    return pl.pallas_call(
        functools.partial(_mixer_kernel, tile=tile, decode=decode, layer=layer, bb=bb),
        grid=(b // bb, nt),
        in_specs=in_specs,
        out_specs=out_specs,
        out_shape=out_shape,
        scratch_shapes=scratch_shapes,
        compiler_params=pltpu.CompilerParams(
            dimension_semantics=("arbitrary", "arbitrary"), vmem_limit_bytes=VMEM_LIMIT),
        name="mixer_decode" if decode else "mixer_prompt",
    )(*args)


def _out_proj_kernel(mix_ref, w_ref, x_ref, y_ref, wb_ref):
    @pl.when(pl.program_id(1) == 0)
    def _():
        wb_ref[...] = w_ref[...].astype(bf16)

    y_ref[...] = x_ref[...] + _dot(mix_ref[...].astype(bf16), wb_ref[...])


def _out_proj(mix, w, x, l, tm, tn):
    m, k = mix.shape
    n = w.shape[2]
    return pl.pallas_call(
        _out_proj_kernel,
        grid=(n // tn, m // tm),
        in_specs=[
            pl.BlockSpec((tm, k), lambda j, i: (i, 0)),
            pl.BlockSpec((None, k, tn), lambda j, i: (l, 0, j),
                         pipeline_mode=pl.Buffered(1) if tn == n else None),
            pl.BlockSpec((tm, tn), lambda j, i: (i, j)),
        ],
        out_specs=pl.BlockSpec((tm, tn), lambda j, i: (i, j)),
        out_shape=jax.ShapeDtypeStruct((m, n), f32),
        scratch_shapes=[pltpu.VMEM((k, tn), bf16)],
        compiler_params=pltpu.CompilerParams(
            dimension_semantics=("arbitrary", "arbitrary"), vmem_limit_bytes=VMEM_LIMIT),
        name="out_proj",
    )(mix, w, x)


PROMPT_TILE = 512
DECODE_SEQS_PER_STEP = 8
PROJ_TN = 1536
OUT_TM, OUT_TN = 512, 2048

_LANE = np.arange(GROUP_W)
HEAD_BLOCK_DIAG = _LANE[:, None] // SWA_HD == _LANE[None, :] // SWA_HD


def kernel(x_prompt, x_sample, mem_prompt, state_conv, state_gla, cache_swa_k, cache_swa_v,
           cache_mem_k, cache_mem_v, g_norm, w_in, conv_w, w_gla_a_up, b_gla_a, g_gla_o,
           g_swa_q, g_swa_k, swa_sinks, g_mem, w_mem_kv, g_mem_q, g_mem_k, w_out):
    depth = w_in.shape[0]
    bp, lp, _ = x_prompt.shape
    bs, ls, _ = x_sample.shape
    hp = x_prompt.reshape(bp * lp, D_MODEL)
    hs = x_sample.reshape(bs * ls, D_MODEL)

    params = (conv_w, w_gla_a_up, b_gla_a, g_gla_o,
              jnp.tile(g_swa_q, (1, SWA_HEADS)), jnp.tile(g_swa_k, (1, SWA_KV_HEADS)),
              jnp.asarray(HEAD_BLOCK_DIAG, bf16), swa_sinks, g_mem_q)
    g_n, g_m, g_mk = g_norm, g_mem, g_mem_k
    w_in_t = jnp.swapaxes(w_in, 1, 2)
    kv_w = SWA_KV_HEADS * SWA_HD
    def window_minor(a):
        return jnp.transpose(a, (0, 1, 3, 4, 2)).reshape(depth, bs, kv_w, WINDOW)

    def window_major(a):
        return jnp.transpose(a.reshape(a.shape[0], SWA_KV_HEADS, SWA_HD, WINDOW), (0, 3, 1, 2))

    state = (state_conv, state_gla, window_minor(cache_swa_k), window_minor(cache_swa_v))
    mem_k_s = cache_mem_k.reshape(depth, bs, N_MEM * MEM_HEADS, MEM_HD)
    mem_v_s = cache_mem_v.reshape(depth, bs, N_MEM * MEM_HEADS, MEM_HD)

    mk, mv, mem_k_p, mem_v_p = _memory_kv(mem_prompt, g_m, w_mem_kv, g_mk)
    outs = [[] for _ in range(8)]
    for l in range(depth):
        proj, w_bf = _norm_matmul(hs, g_n, w_in_t, l, PROJ_TN)

        mix, c, s, kb, vb = _mixer(hp.reshape(bp, lp, D_MODEL), (g_n, w_bf), mk, mv, l,
                                   None, params, l, PROMPT_TILE, 1, decode=False)
        hp = _out_proj(mix.reshape(bp * lp, 4 * GROUP_W), w_out, hp, l, OUT_TM, OUT_TN)
        for lst, a in zip(outs[:4], (
                c, s, window_major(kb), window_major(vb))):
            lst.append(a)

        mix, c, s, kb, vb = _mixer(proj, None, mem_k_s, mem_v_s, l, state, params, l, ls,
                                   DECODE_SEQS_PER_STEP, decode=True)
        hs = _out_proj(mix, w_out, hs, l, bs * ls, OUT_TN)
        for lst, a in zip(outs[4:], (
                c, s, window_major(kb), window_major(vb))):
            lst.append(a)

    stacked = [jnp.stack(o) for o in outs]
    return (hp.reshape(bp, lp, D_MODEL), hs.reshape(bs, ls, D_MODEL),
            *stacked[:4], mem_k_p, mem_v_p, *stacked[4:])
```

```python
import functools
import itertools

import jax
import jax.numpy as jnp
import numpy as np
from jax import lax
from jax.experimental import pallas as pl
from jax.experimental.pallas import tpu as pltpu

f32 = jnp.float32
bf16 = jnp.bfloat16

D_MODEL = 2048
GROUP_W = 512
GLA_HEADS = 4
GLA_DK = 64
GLA_DV = 128
GLA_RANK = 16
GLA_TAU = 16.0
GLA_CHUNK = 64
SWA_HEADS = 8
SWA_KV_HEADS = 2
SWA_HD = 64
SWA_GROUP = SWA_HEADS // SWA_KV_HEADS
WINDOW = 128
N_MEM = 256
MEM_HEADS = 4
MEM_HD = 128
CONV_W = 3
EPS = 1e-6

LANES = 128
MXU_CHUNK = 256

D_IN = 5904
OFF_AB, OFF_AC, OFF_AH, OFF_AZ = 0, 512, 1024, 1536
OFF_GQ, OFF_GK, OFF_GV, OFF_GA, OFF_GZ = 2048, 2304, 2560, 3072, 3088
OFF_SQ, OFF_SK, OFF_SV, OFF_SZ = 3600, 4112, 4240, 4368
OFF_MQ, OFF_MZ = 4880, 5392

VMEM_LIMIT = 60 * 1024 * 1024


def _dot(a, b):
    return jnp.dot(a, b, preferred_element_type=f32)


def _dot_nt(a, b):
    return lax.dot_general(a, b, (((1,), (1,)), ((), ())), preferred_element_type=f32)


def _dot_tn(a, b):
    return lax.dot_general(a, b, (((0,), (0,)), ((), ())), preferred_element_type=f32)


def _split3(x):
    hi = x.astype(bf16)
    r = x - hi.astype(f32)
    mid = r.astype(bf16)
    lo = (r - mid.astype(f32)).astype(bf16)
    return hi, mid, lo


LOG2_E = 1.4426950408889634


def _attend(scores, valid, scale, values, sink=None):
    if valid is not None:
        scores = jnp.where(valid, scores, -jnp.inf)
    m = jnp.max(scores, axis=-1, keepdims=True)
    if sink is not None:
        sink = sink * (1.0 / scale)
        m = jnp.maximum(m, sink)
    e = jnp.exp2((scores - m) * (scale * LOG2_E))
    denom = jnp.sum(e, axis=-1, keepdims=True)
    if sink is not None:
        denom = denom + jnp.exp2((sink - m) * (scale * LOG2_E))
    return _dot(e.astype(bf16), values) / denom


def _silu(x):
    return x * jax.nn.sigmoid(x)


def _log_sigmoid(x):
    return jnp.minimum(x, 0.0) - jnp.log1p(jnp.exp(-jnp.abs(x)))


def _norm_matmul_kernel(x_ref, g_ref, wt_ref, o_ref, wb_ref, hn_ref, *, layer):
    x = x_ref[...]
    y = x * lax.rsqrt(jnp.mean(x * x, axis=-1, keepdims=True) + EPS)
    hn_ref[...] = (y * g_ref[layer:layer + 1, :]).astype(bf16)
    wb_ref[...] = wt_ref[...].astype(bf16)
    o_ref[...] = _dot_nt(hn_ref[...], wb_ref[...])


def _norm_matmul(x, g, wt, l, tn):
    m, k = x.shape
    n = wt.shape[1]
    return pl.pallas_call(
        functools.partial(_norm_matmul_kernel, layer=l),
        grid=(pl.cdiv(n, tn),),
        in_specs=[
            pl.BlockSpec((m, k), lambda j: (0, 0)),
            pl.BlockSpec(g.shape, lambda j: (0, 0)),
            pl.BlockSpec((None, tn, k), lambda j: (l, j, 0)),
        ],
        out_specs=[pl.BlockSpec((m, tn), lambda j: (0, j)), pl.BlockSpec((tn, k), lambda j: (j, 0))],
        out_shape=[jax.ShapeDtypeStruct((m, n), f32), jax.ShapeDtypeStruct((n, k), bf16)],
        scratch_shapes=[pltpu.VMEM((m, k), bf16)],
        compiler_params=pltpu.CompilerParams(
            dimension_semantics=("arbitrary",), vmem_limit_bytes=VMEM_LIMIT),
        name="norm_in_proj",
    )(x, g, wt)


def _memory_kv_kernel(x_ref, g_ref, w_ref, gk_ref, k_ref, v_ref, k4_ref, v4_ref, wb_ref):
    @pl.when(pl.program_id(1) == 0)
    def _():
        wb_ref[...] = w_ref[...].astype(bf16)

    layer = pl.program_id(0)
    g, gk = g_ref[pl.ds(layer, 1), :], gk_ref[pl.ds(layer, 1), :]
    x = x_ref[...]
    y = x * lax.rsqrt(jnp.mean(x * x, axis=-1, keepdims=True) + EPS)
    kv = _dot((y * g).astype(bf16), wb_ref[...])
    for h in range(MEM_HEADS):
        kh = kv[:, h * MEM_HD:(h + 1) * MEM_HD]
        kh = kh * lax.rsqrt(jnp.mean(kh * kh, axis=-1, keepdims=True) + EPS) * gk
        vh = kv[:, GROUP_W + h * MEM_HD:GROUP_W + (h + 1) * MEM_HD]
        k_ref[:, h * MEM_HD:(h + 1) * MEM_HD] = kh.astype(bf16)
        k4_ref[:, h, :] = kh
        v4_ref[:, h, :] = vh
    v_ref[...] = kv[:, GROUP_W:].astype(bf16)


def _memory_kv(mem, g, w, gk):
    depth, b = w.shape[0], mem.shape[0]
    flat = jax.ShapeDtypeStruct((depth, b, N_MEM, GROUP_W), bf16)
    split = jax.ShapeDtypeStruct((depth, b, N_MEM, MEM_HEADS, MEM_HD), f32)
    return pl.pallas_call(
        _memory_kv_kernel,
        grid=(depth, b),
        in_specs=[
            pl.BlockSpec((None, N_MEM, D_MODEL), lambda l, i: (i, 0, 0)),
            pl.BlockSpec(g.shape, lambda l, i: (0, 0)),
            pl.BlockSpec((None, D_MODEL, 2 * GROUP_W), lambda l, i: (l, 0, 0)),
            pl.BlockSpec(gk.shape, lambda l, i: (0, 0)),
        ],
        out_specs=[pl.BlockSpec((None, None, N_MEM, GROUP_W), lambda l, i: (l, i, 0, 0))] * 2
        + [pl.BlockSpec((None, None, N_MEM, MEM_HEADS, MEM_HD), lambda l, i: (l, i, 0, 0, 0))] * 2,
        out_shape=[flat, flat, split, split],
        scratch_shapes=[pltpu.VMEM((D_MODEL, 2 * GROUP_W), bf16)],
        compiler_params=pltpu.CompilerParams(
            dimension_semantics=("arbitrary", "arbitrary"), vmem_limit_bytes=VMEM_LIMIT),
        name="memory_kv",
    )(mem, g, w, gk)


CONV_PAD = 8
N_STACK_SLOTS = 2
N_SEQ_IN_PROMPT, N_SEQ_IN_DECODE, N_PARAMS_PROMPT, N_PARAMS_DECODE, N_OUT = 3, 7, 11, 9, 5


def _mixer_kernel(*refs, tile, decode, layer, bb):
    n_seq = N_SEQ_IN_DECODE if decode else N_SEQ_IN_PROMPT
    n_par = N_PARAMS_DECODE if decode else N_PARAMS_PROMPT
    seq_in = refs[:n_seq]
    params = list(refs[n_seq:n_seq + n_par])
    outs = refs[n_seq + n_par:n_seq + n_par + N_OUT]
    scratch = refs[n_seq + n_par + N_OUT:]
    for k in (-7, -6, -5, -4, -1) + (() if decode else (0,)):
        params[k] = params[k].at[layer:layer + 1]

    def view(ref, s):
        if decode and ref.ndim == 2:
            return _RowWindow(ref, s * tile, tile)
        return ref.at[s]

    if decode:
        p_ref, tail_ref, qkn_ref = seq_in[0], scratch[4], scratch[6]
        bd_ref, gsq_ref, gsk_ref = params[-3], params[-5], params[-4]
        tail_ref[...] = p_ref[:, OFF_GZ:D_IN]
        sq0, sk0 = OFF_SQ - OFF_GZ, OFF_SK - OFF_GZ
        qkn_ref[:, 0:GROUP_W] = _head_norm(tail_ref[:, sq0:sq0 + GROUP_W], gsq_ref[...], bd_ref)
        qkn_ref[:, GROUP_W:GROUP_W + LANES] = _head_norm(tail_ref[:, sk0:sk0 + LANES], gsk_ref[...],
                                                         bd_ref)

        def seg_rows(off, width):
            if off < OFF_GZ:
                return p_ref[:, off:off + width]
            return tail_ref[:, off - OFF_GZ:off - OFF_GZ + width]

        gla_in_ref, gla_out_ref = seq_in[4], outs[2]
        for _ in _gla_group(seg_rows, bb * tile, tile, params[-8:-5], outs[0],
                            lambda c: _gla_block_diag(gla_in_ref.at[c]),
                            lambda c, state: _store_gla_state(gla_out_ref.at[c], state), carry=False):
            pass
        kc_ref, vc_ref = scratch[2], scratch[3]
        for s in range(bb):
            kc_ref[s] = seq_in[5][s].T
            vc_ref[s] = seq_in[6][s].T
        _swa_decode_rows(seg_rows, qkn_ref, kc_ref, vc_ref, params[-2], layer, outs[0], bb, tile)

    stages = [_mixer_seq([view(r, s) for r in seq_in], params, [view(r, s) for r in outs],
                         [view(r, s) for r in scratch], tile=tile, decode=decode, layer=layer)
              for s in range(bb)]
    for _ in itertools.zip_longest(*stages):
        pass


class _RowWindow:
    def __init__(self, ref, start, size):
        self.ref, self.start, self.size, self.dtype = ref, start, size, ref.dtype

    def _index(self, idx):
        rows, cols = (slice(None), slice(None)) if idx is Ellipsis else idx
        lo, hi, _ = rows.indices(self.size)
        return slice(self.start + lo, self.start + hi), cols

    def __getitem__(self, idx):
        return self.ref[self._index(idx)]

    def __setitem__(self, idx, value):
        self.ref[self._index(idx)] = value


GLA_INTRA_ROWS = MXU_CHUNK
GLA_K_W = GLA_HEADS * GLA_DK
GLA_V_W = GLA_HEADS * GLA_DV


def _gla_block_diag(state_ref):
    rows = []
    for h in range(GLA_HEADS):
        blocks = [state_ref[h] if j == h else jnp.zeros((GLA_DK, GLA_DV), f32)
                  for j in range(GLA_HEADS)]
        rows.append(jnp.concatenate(blocks, axis=1))
    return jnp.concatenate(rows, axis=0)


def _store_gla_state(state_ref, state):
    for h in range(GLA_HEADS):
        state_ref[h] = state[h * GLA_DK:(h + 1) * GLA_DK, h * GLA_DV:(h + 1) * GLA_DV]


def _gla_group(seg, T, C, params, mix_ref, state_in, state_out, carry, seg_t=None):
    wup_ref, bga_ref, ggo_ref = params
    n_chunk = T // C
    G = min(T, GLA_INTRA_ROWS)
    groups = [slice(i * G, (i + 1) * G) for i in range(T // G)]
    row = lax.broadcasted_iota(jnp.int32, (G, G), 0)
    col = lax.broadcasted_iota(jnp.int32, (G, G), 1)
    causal = (row // C == col // C) & (row >= col)
    if seg_t is None:
        a_up = _dot(seg(OFF_GA, GLA_RANK).astype(bf16), wup_ref[...].astype(bf16))
    else:
        a_up = _dot_tn(seg_t(OFF_GA, GLA_RANK).astype(bf16), wup_ref[...].astype(bf16))
    log_a = _log_sigmoid(a_up + bga_ref[...]) * (1.0 / GLA_TAU)
    la3 = _split3(log_a)
    yield
    tril = jnp.where(causal, 1.0, 0.0).astype(bf16)
    in_chunk = jnp.where(lax.broadcasted_iota(jnp.int32, (T, LANES), 0) // C
                         == lax.broadcasted_iota(jnp.int32, (T, LANES), 1), 1.0, 0.0).astype(bf16)
    cum = jnp.concatenate(
        [_dot(tril, la3[0][r]) + _dot(tril, la3[1][r]) + _dot(tril, la3[2][r]) for r in groups],
        axis=0)
    tot_t = (_dot_tn(la3[0], in_chunk) + _dot_tn(la3[1], in_chunk)
             + _dot_tn(la3[2], in_chunk))
    decay_t = jnp.exp(tot_t)
    yield
    g_k = seg(OFF_GK, GLA_K_W)
    qd = ((seg(OFF_GQ, GLA_K_W) * (GLA_DK ** -0.5)) * jnp.exp(cum)).astype(bf16)
    kd = (g_k * jnp.exp(-cum)).astype(bf16)
    k_tail = jnp.concatenate(
        [g_k[c * C:(c + 1) * C] * jnp.exp(cum[(c + 1) * C - 1:(c + 1) * C] - cum[c * C:(c + 1) * C])
         for c in range(n_chunk)], axis=0) if n_chunk > 1 else g_k * jnp.exp(cum[T - 1:T] - cum)
    kt = k_tail.astype(bf16)
    yield
    v_b = seg(OFF_GV, GLA_V_W).astype(bf16)
    g_z = seg(OFF_GZ, GROUP_W)
    yield

    o_intra = []
    for r in groups:
        o_heads = []
        for h in range(GLA_HEADS):
            ks = slice(h * GLA_DK, (h + 1) * GLA_DK)
            attn = jnp.where(causal, _dot_nt(qd[r, ks], kd[r, ks]), 0.0).astype(bf16)
            o_heads.append(_dot(attn, v_b[r, h * GLA_DV:(h + 1) * GLA_DV]))
        o_intra.append(jnp.concatenate(o_heads, axis=1))
        yield
    o_intra = jnp.concatenate(o_intra, axis=0)

    shape = (GLA_K_W, GLA_V_W)
    on_diag = (lax.broadcasted_iota(jnp.int32, shape, 0) // GLA_DK
               == lax.broadcasted_iota(jnp.int32, shape, 1) // GLA_DV)
    o_chunks = []
    state = None
    for c in range(n_chunk):
        rs = slice(c * C, (c + 1) * C)
        if c == 0 or not carry:
            state = state_in(c)
        o_chunks.append(o_intra[rs] + _dot(qd[rs], state.astype(bf16)))
        update = jnp.where(on_diag, _dot_tn(kt[rs], v_b[rs]), 0.0)
        state = decay_t[:, c:c + 1] * state + update
        state_out(c, state)
        yield
    o = jnp.concatenate(o_chunks, axis=0) if n_chunk > 1 else o_chunks[0]
    for h in range(GLA_HEADS):
        vs = slice(h * GLA_DV, (h + 1) * GLA_DV)
        o_h = o[:, vs]
        o_h = o_h * lax.rsqrt(jnp.mean(o_h * o_h, axis=-1, keepdims=True) + EPS) * ggo_ref[...]
        mix_ref[:, GROUP_W + h * GLA_DV:GROUP_W + (h + 1) * GLA_DV] = (
            o_h * _silu(g_z[:, vs])).astype(mix_ref.dtype)
        yield


def _head_norm(x, g, bd_ref):
    rows, n_lanes = x.shape
    w = min(n_lanes, MXU_CHUNK)
    pieces = n_lanes // w
    bd = bd_ref[0:w, 0:w]

    def head_sums(v):
        stacked = jnp.concatenate([v[:, i * w:(i + 1) * w] for i in range(pieces)], axis=0)
        r = _dot(stacked, bd)
        return jnp.concatenate([r[i * rows:(i + 1) * rows] for i in range(pieces)], axis=1)

    sq = x * x
    hi = sq.astype(bf16)
    lo = (sq - hi.astype(f32)).astype(bf16)
    ms = (head_sums(hi) + head_sums(lo)) * (1.0 / SWA_HD)
    return x * lax.rsqrt(ms + EPS) * g


def _mixer_seq(seq_in, params, outs, scratch, *, tile, decode, layer):
    if decode:
        p_ref, mk_ref, mv_ref, conv_in_ref, gla_in_ref, kc_ref, vc_ref = seq_in
    else:
        x_ref, mk_ref, mv_ref = seq_in
        gn_ref, wt_ref = params[:2]
    (convw_ref, wup_ref, bga_ref, ggo_ref, gsq_ref, gsk_ref, bd_ref, sinks_ref,
     gmq_ref) = params[-N_PARAMS_DECODE:]
    mix_ref, conv_out_ref, gla_out_ref, kbuf_ref, vbuf_ref = outs
    ext_ref, s_ref, kprev_ref, vprev_ref, tail_ref, stk_ref, qkn_ref, knew_ref, vnew_ref = scratch

    T = tile
    t = pl.program_id(1)

    def init_state():
        ext_ref[0:CONV_PAD, :] = jnp.zeros((CONV_PAD, GROUP_W), f32)
        if decode:
            ext_ref[CONV_PAD - (CONV_W - 1):CONV_PAD, :] = conv_in_ref[...]
        else:
            s_ref[...] = jnp.zeros_like(s_ref)
            kprev_ref[...] = jnp.zeros_like(kprev_ref)
            vprev_ref[...] = jnp.zeros_like(vprev_ref)

    if decode:
        init_state()
    else:
        pl.when(t == 0)(init_state)
    yield

    if decode:
        def seg(off, width):
            if off < OFF_GZ:
                return p_ref[:, off:off + width]
            return tail_ref[:, off - OFF_GZ:off - OFF_GZ + width]
    else:
        x = x_ref[...]
        hn = (x * lax.rsqrt(jnp.mean(x * x, axis=-1, keepdims=True) + EPS) * gn_ref[...]).astype(bf16)

        def seg(off, width):
            return _dot_nt(hn, wt_ref[off:off + width, :])

        def seg_t(off, width):
            return _dot_nt(wt_ref[off:off + width, :], hn)

    def group_a():
        u = seg(OFF_AC, GROUP_W) * seg(OFF_AH, GROUP_W)
        ext_ref[CONV_PAD:CONV_PAD + T, :] = u
        yield
        conv = (convw_ref[0:1, :] * ext_ref[CONV_PAD - 2:CONV_PAD - 2 + T, :]
                + convw_ref[1:2, :] * ext_ref[CONV_PAD - 1:CONV_PAD - 1 + T, :]
                + convw_ref[2:3, :] * u)
        a_b = seg(OFF_AB, GROUP_W)
        yield
        mix_ref[:, 0:GROUP_W] = (a_b * conv * _silu(seg(OFF_AZ, GROUP_W))).astype(mix_ref.dtype)
        conv_state = ext_ref[CONV_PAD + T - 2:CONV_PAD + T, :]
        ext_ref[CONV_PAD - 2:CONV_PAD, :] = conv_state
        conv_out_ref[...] = conv_state
        yield

    def group_b():
        def keep_state(c, state):
            if c == T // GLA_CHUNK - 1:
                s_ref[...] = state
                _store_gla_state(gla_out_ref, state)

        yield from _gla_group(seg, T, GLA_CHUNK, (wup_ref, bga_ref, ggo_ref), mix_ref,
                              lambda c: s_ref[...], keep_state, carry=True, seg_t=seg_t)

    def group_d():
        m_q = seg(OFF_MQ, GROUP_W)
        m_z = seg(OFF_MZ, GROUP_W)
        yield
        yield from _memory_attention(m_q, m_z, T, decode, gmq_ref, mk_ref, mv_ref, mix_ref, stk_ref)

    if decode:
        yield from group_a()
        knew_ref[0:WINDOW - T, :] = kprev_ref[T:WINDOW, :]
        knew_ref[WINDOW - T:WINDOW, :] = qkn_ref[:, GROUP_W:GROUP_W + LANES]
        vnew_ref[0:WINDOW - T, :] = vprev_ref[T:WINDOW, :]
        vnew_ref[WINDOW - T:WINDOW, :] = seg(OFF_SV, LANES)
        yield
        kbuf_ref[...] = knew_ref[...].T
        vbuf_ref[...] = vnew_ref[...].T
        yield
        yield from group_d()
    else:
        group_c = _swa_prompt_tile(seg, T, t, layer, (gsq_ref, gsk_ref, bd_ref, sinks_ref), mix_ref,
                                   kprev_ref, vprev_ref, kbuf_ref, vbuf_ref)
        for _ in itertools.zip_longest(group_b(), itertools.chain(group_c, group_a(), group_d())):
            pass
        yield


def _swa_prompt_tile(seg, T, t, layer, params, mix_ref, kprev_ref, vprev_ref, kbuf_ref, vbuf_ref):
    gsq_ref, gsk_ref, bd_ref, sinks_ref = params
    s_z = seg(OFF_SZ, GROUP_W)
    s_q = seg(OFF_SQ, GROUP_W)
    s_kv = seg(OFF_SK, 2 * LANES)
    q_n = _head_norm(s_q, gsq_ref[...], bd_ref)
    k_n = _head_norm(s_kv[:, 0:LANES], gsk_ref[...], bd_ref)
    v_n = s_kv[:, LANES:2 * LANES]
    yield

    BQ = WINDOW
    n_blk = T // BQ
    stack = SWA_GROUP
    nk = WINDOW + BQ
    qi = lax.broadcasted_iota(jnp.int32, (stack * BQ, nk), 0) % BQ
    kj = lax.broadcasted_iota(jnp.int32, (stack * BQ, nk), 1)
    dist = qi + WINDOW - kj
    band = (dist >= 0) & (dist < WINDOW)
    srow = lax.broadcasted_iota(jnp.int32, (stack * BQ, 1), 0) // BQ
    for blk in range(n_blk):
        rs = slice(blk * BQ, (blk + 1) * BQ)
        if blk == 0:
            k_prev, v_prev = kprev_ref[...], vprev_ref[...]
            valid = band & ((kj >= WINDOW) | (t > 0))
        else:
            ps = slice((blk - 1) * BQ, blk * BQ)
            k_prev, v_prev = k_n[ps], v_n[ps]
            valid = band
        k_cat = jnp.concatenate([k_prev, k_n[rs]], axis=0)
        v_cat = jnp.concatenate([v_prev, v_n[rs]], axis=0)
        for g in range(SWA_KV_HEADS):
            kg = k_cat[:, g * SWA_HD:(g + 1) * SWA_HD].astype(bf16)
            vg = v_cat[:, g * SWA_HD:(g + 1) * SWA_HD].astype(bf16)
            heads = [g * SWA_GROUP + j for j in range(stack)]
            qg = jnp.concatenate([q_n[rs, hd * SWA_HD:(hd + 1) * SWA_HD] for hd in heads],
                                 axis=0).astype(bf16)
            sink = jnp.full((stack * BQ, 1), sinks_ref[layer, heads[0]], f32)
            for j in range(1, stack):
                sink = jnp.where(srow == j, sinks_ref[layer, heads[j]], sink)
            o = _attend(_dot_nt(qg, kg), valid, SWA_HD ** -0.5, vg, sink)
            for j, hd in enumerate(heads):
                z = s_z[rs, hd * SWA_HD:(hd + 1) * SWA_HD]
                mix_ref[rs, 2 * GROUP_W + hd * SWA_HD:2 * GROUP_W + (hd + 1) * SWA_HD] = (
                    o[j * BQ:(j + 1) * BQ] * _silu(z)).astype(mix_ref.dtype)
            yield

    kprev_ref[...] = k_n[T - WINDOW:T]
    vprev_ref[...] = v_n[T - WINDOW:T]
    kbuf_ref[...] = k_n[T - WINDOW:T].T
    vbuf_ref[...] = v_n[T - WINDOW:T].T
    yield


def _swa_decode_rows(seg, qkn_ref, kc_ref, vc_ref, sinks_ref, layer, mix_ref, bb, T):
    R = bb * T
    n_cache = bb * WINDOW
    nk = n_cache + R
    q_n, k_n = qkn_ref[:, 0:GROUP_W], qkn_ref[:, GROUP_W:GROUP_W + LANES]
    s_z = seg(OFF_SZ, GROUP_W)
    k_all = jnp.concatenate([kc_ref[s] for s in range(bb)] + [k_n], axis=0)
    v_all = jnp.concatenate([vc_ref[s] for s in range(bb)] + [seg(OFF_SV, LANES)], axis=0)
    rows = SWA_GROUP * R
    r = lax.broadcasted_iota(jnp.int32, (rows, nk), 0) % R
    c = lax.broadcasted_iota(jnp.int32, (rows, nk), 1)
    cached = c < n_cache
    key_seq = jnp.where(cached, c // WINDOW, (c - n_cache) // T)
    key_pos = jnp.where(cached, c % WINDOW, WINDOW + (c - n_cache) % T)
    dist = r % T + WINDOW - key_pos
    valid = (r // T == key_seq) & (dist >= 0) & (dist < WINDOW)
    srow = lax.broadcasted_iota(jnp.int32, (rows, 1), 0) // R
    for g in range(SWA_KV_HEADS):
        heads = [g * SWA_GROUP + j for j in range(SWA_GROUP)]
        qg = jnp.concatenate([q_n[:, hd * SWA_HD:(hd + 1) * SWA_HD] for hd in heads],
                             axis=0).astype(bf16)
        kg = k_all[:, g * SWA_HD:(g + 1) * SWA_HD].astype(bf16)
        vg = v_all[:, g * SWA_HD:(g + 1) * SWA_HD].astype(bf16)
        sink = jnp.full((rows, 1), sinks_ref[layer, heads[0]], f32)
        for j in range(1, SWA_GROUP):
            sink = jnp.where(srow == j, sinks_ref[layer, heads[j]], sink)
        o = _attend(_dot_nt(qg, kg), valid, SWA_HD ** -0.5, vg, sink)
        for j, hd in enumerate(heads):
            mix_ref[:, 2 * GROUP_W + hd * SWA_HD:2 * GROUP_W + (hd + 1) * SWA_HD] = (
                o[j * R:(j + 1) * R] * _silu(s_z[:, hd * SWA_HD:(hd + 1) * SWA_HD])).astype(mix_ref.dtype)


def _memory_attention(m_q, m_z, T, decode, gmq_ref, mk_ref, mv_ref, mix_ref, stk_ref):
    def stack_rows(pieces, slot):
        r, w = pieces[0].shape
        if r % 8 == 0:
            return jnp.concatenate(pieces, axis=0)
        for j, piece in enumerate(pieces):
            stk_ref[slot, j * r:(j + 1) * r, 0:w] = piece
        return stk_ref[slot, 0:len(pieces) * r, 0:w]

    def unstack_rows(x, n, slot):
        r, w = x.shape[0] // n, x.shape[1]
        if r % 8 == 0:
            return [x[j * r:(j + 1) * r] for j in range(n)]
        stk_ref[slot, 0:n * r, 0:w] = x
        return [stk_ref[slot, j * r:(j + 1) * r, 0:w] for j in range(n)]

    def mem_q(h):
        qh = m_q[:, h * MEM_HD:(h + 1) * MEM_HD]
        return qh * lax.rsqrt(jnp.mean(qh * qh, axis=-1, keepdims=True) + EPS) * gmq_ref[...]

    if decode:
        qs = stack_rows([mem_q(h) for h in range(MEM_HEADS)], 0).astype(bf16)
        s = _dot_nt(qs, mk_ref[...].astype(bf16))
        shape = (MEM_HEADS * T, MEM_HEADS * N_MEM)
        same_head = (lax.broadcasted_iota(jnp.int32, shape, 0) // T
                     == lax.broadcasted_iota(jnp.int32, shape, 1) % MEM_HEADS)
        yield
        o = _attend(s, same_head, MEM_HD ** -0.5, mv_ref[...].astype(bf16))
        o_all = unstack_rows(o, MEM_HEADS, 1)
    else:
        o_all = []
        for h in range(MEM_HEADS):
            hs = slice(h * MEM_HD, (h + 1) * MEM_HD)
            s = _dot_nt(mem_q(h).astype(bf16), mk_ref[:, hs].astype(bf16))
            o_all.append(_attend(s, None, MEM_HD ** -0.5, mv_ref[:, hs].astype(bf16)))
            yield
    for h in range(MEM_HEADS):
        mix_ref[:, 3 * GROUP_W + h * MEM_HD:3 * GROUP_W + (h + 1) * MEM_HD] = (
            o_all[h] * _silu(m_z[:, h * MEM_HD:(h + 1) * MEM_HD])).astype(mix_ref.dtype)
    yield


def _mixer(tokens, norm_w, mem_k, mem_v, mem_layer, state, params, layer, tile, bb, decode):
    if decode:
        b, width = state[0].shape[1], tokens.shape[1]
        L = tokens.shape[0] // b
    else:
        b, L, width = tokens.shape
    nt = L // tile
    assert nt == 1 or not decode, "a decode call covers each sequence with a single tile"
    conv_w, w_up, b_ga, g_go, g_sq, g_sk, bd, sinks, g_mq = params

    def tok(width):
        if decode:
            return pl.BlockSpec((bb * tile, width), lambda i, t: (i, 0))
        return pl.BlockSpec((bb, tile, width), lambda i, t: (i, t, 0))

    def per_seq(*shape):
        return pl.BlockSpec((bb,) + shape, lambda i, t: (i,) + (0,) * len(shape))

    def per_seq_at(lyr, *shape):
        return pl.BlockSpec((None, bb) + shape, lambda i, t: (lyr, i) + (0,) * len(shape))

    def param(a):
        if a.ndim == 2:
            return pl.BlockSpec(a.shape, lambda i, t: (0, 0))
        return pl.BlockSpec((None,) + a.shape[1:], lambda i, t: (layer,) + (0,) * (a.ndim - 1))

    kv_w = SWA_KV_HEADS * SWA_HD
    state_shapes = [(CONV_W - 1, GROUP_W), (GLA_HEADS, GLA_DK, GLA_DV), (WINDOW, kv_w),
                    (WINDOW, kv_w)]
    in_specs = [tok(width), per_seq_at(mem_layer, *mem_k.shape[2:]),
                per_seq_at(mem_layer, *mem_v.shape[2:])]
    args = [tokens, mem_k, mem_v]
    if decode:
        in_specs += [per_seq_at(layer, *s) for s in state_shapes]
        args += list(state)
    else:
        g_n, w_t = norm_w
        in_specs += [param(g_n), pl.BlockSpec(w_t.shape, lambda i, t: (0, 0),
                                              pipeline_mode=pl.Buffered(1))]
        args += [g_n, w_t]
    in_specs += [param(conv_w), param(w_up), param(b_ga), param(g_go), param(g_sq), param(g_sk),
                 pl.BlockSpec(bd.shape, lambda i, t: (0, 0)),
                 pl.BlockSpec(memory_space=pltpu.SMEM), param(g_mq)]
    args += [conv_w, w_up, b_ga, g_go, g_sq, g_sk, bd, sinks, g_mq]
    conv_window = pltpu.VMEM((bb, CONV_PAD + tile, GROUP_W), f32)
    window = pltpu.VMEM((bb, WINDOW, kv_w), f32)
    unused = pltpu.VMEM((bb, 8, LANES), f32)
    if decode:
        out_shape = [jax.ShapeDtypeStruct((b * L, 4 * GROUP_W), f32)]
        scratch_shapes = [conv_window, unused, window, window,
                          pltpu.VMEM((bb * tile, D_IN - OFF_GZ), f32),
                          pltpu.VMEM((bb, N_STACK_SLOTS, MEM_HEADS * tile, LANES), f32),
                          pltpu.VMEM((bb * tile, GROUP_W + LANES), f32), window, window]
    else:
        out_shape = [jax.ShapeDtypeStruct((b, L, 4 * GROUP_W), bf16)]
        scratch_shapes = [conv_window, pltpu.VMEM((bb, GLA_K_W, GLA_V_W), f32), window, window,
                          unused, unused, unused, unused, unused]
    out_shape += [jax.ShapeDtypeStruct((b,) + s, f32) for s in state_shapes]
    out_specs = [tok(4 * GROUP_W)] + [per_seq(*s) for s in state_shapes]
    return pl.pallas_call(
        functools.partial(_mixer_kernel, tile=tile, decode=decode, layer=layer, bb=bb),
        grid=(b // bb, nt),
        in_specs=in_specs,
        out_specs=out_specs,
        out_shape=out_shape,
        scratch_shapes=scratch_shapes,
        compiler_params=pltpu.CompilerParams(
            dimension_semantics=("arbitrary", "arbitrary"), vmem_limit_bytes=VMEM_LIMIT),
        name="mixer_decode" if decode else "mixer_prompt",
    )(*args)


def _out_proj_kernel(mix_ref, w_ref, x_ref, y_ref, wb_ref):
    @pl.when(pl.program_id(1) == 0)
    def _():
        wb_ref[...] = w_ref[...].astype(bf16)

    y_ref[...] = x_ref[...] + _dot(mix_ref[...].astype(bf16), wb_ref[...])


def _out_proj(mix, w, x, l, tm, tn):
    m, k = mix.shape
    n = w.shape[2]
    return pl.pallas_call(
        _out_proj_kernel,
        grid=(n // tn, m // tm),
        in_specs=[
            pl.BlockSpec((tm, k), lambda j, i: (i, 0)),
            pl.BlockSpec((None, k, tn), lambda j, i: (l, 0, j),
                         pipeline_mode=pl.Buffered(1) if tn == n else None),
            pl.BlockSpec((tm, tn), lambda j, i: (i, j)),
        ],
        out_specs=pl.BlockSpec((tm, tn), lambda j, i: (i, j)),
        out_shape=jax.ShapeDtypeStruct((m, n), f32),
        scratch_shapes=[pltpu.VMEM((k, tn), bf16)],
        compiler_params=pltpu.CompilerParams(
            dimension_semantics=("arbitrary", "arbitrary"), vmem_limit_bytes=VMEM_LIMIT),
        name="out_proj",
    )(mix, w, x)


PROMPT_TILE = 512
DECODE_SEQS_PER_STEP = 8
PROJ_TN = 1536
OUT_TM, OUT_TN = 512, 2048

_LANE = np.arange(GROUP_W)
HEAD_BLOCK_DIAG = _LANE[:, None] // SWA_HD == _LANE[None, :] // SWA_HD


def kernel(x_prompt, x_sample, mem_prompt, state_conv, state_gla, cache_swa_k, cache_swa_v,
           cache_mem_k, cache_mem_v, g_norm, w_in, conv_w, w_gla_a_up, b_gla_a, g_gla_o,
           g_swa_q, g_swa_k, swa_sinks, g_mem, w_mem_kv, g_mem_q, g_mem_k, w_out):
    depth = w_in.shape[0]
    bp, lp, _ = x_prompt.shape
    bs, ls, _ = x_sample.shape
    hp = x_prompt.reshape(bp * lp, D_MODEL)
    hs = x_sample.reshape(bs * ls, D_MODEL)

    params = (conv_w, w_gla_a_up, b_gla_a, g_gla_o,
              jnp.tile(g_swa_q, (1, SWA_HEADS)), jnp.tile(g_swa_k, (1, SWA_KV_HEADS)),
              jnp.asarray(HEAD_BLOCK_DIAG, bf16), swa_sinks, g_mem_q)
    g_n, g_m, g_mk = g_norm, g_mem, g_mem_k
    w_in_t = jnp.swapaxes(w_in, 1, 2)
    kv_w = SWA_KV_HEADS * SWA_HD
    def window_minor(a):
        return jnp.transpose(a, (0, 1, 3, 4, 2)).reshape(depth, bs, kv_w, WINDOW)

    def window_major(a):
        return jnp.transpose(a.reshape(a.shape[0], SWA_KV_HEADS, SWA_HD, WINDOW), (0, 3, 1, 2))

    state = (state_conv, state_gla, window_minor(cache_swa_k), window_minor(cache_swa_v))
    mem_k_s = cache_mem_k.reshape(depth, bs, N_MEM * MEM_HEADS, MEM_HD)
    mem_v_s = cache_mem_v.reshape(depth, bs, N_MEM * MEM_HEADS, MEM_HD)

    mk, mv, mem_k_p, mem_v_p = _memory_kv(mem_prompt, g_m, w_mem_kv, g_mk)
    outs = [[] for _ in range(8)]
    for l in range(depth):
        proj, w_bf = _norm_matmul(hs, g_n, w_in_t, l, PROJ_TN)

        mix, c, s, kb, vb = _mixer(hp.reshape(bp, lp, D_MODEL), (g_n, w_bf), mk, mv, l,
                                   None, params, l, PROMPT_TILE, 1, decode=False)
        hp = _out_proj(mix.reshape(bp * lp, 4 * GROUP_W), w_out, hp, l, OUT_TM, OUT_TN)
        for lst, a in zip(outs[:4], (
                c, s, window_major(kb), window_major(vb))):
            lst.append(a)

        mix, c, s, kb, vb = _mixer(proj, None, mem_k_s, mem_v_s, l, state, params, l, ls,
                                   DECODE_SEQS_PER_STEP, decode=True)
        hs = _out_proj(mix, w_out, hs, l, bs * ls, OUT_TN)
        for lst, a in zip(outs[4:], (
                c, s, window_major(kb), window_major(vb))):
            lst.append(a)

    stacked = [jnp.stack(o) for o in outs]
    return (hp.reshape(bp, lp, D_MODEL), hs.reshape(bs, ls, D_MODEL),
            *stacked[:4], mem_k_p, mem_v_p, *stacked[4:])
```

```python
import functools
import itertools

import jax
import jax.numpy as jnp
import numpy as np
from jax import lax
from jax.experimental import pallas as pl
from jax.experimental.pallas import tpu as pltpu

f32 = jnp.float32
bf16 = jnp.bfloat16

D_MODEL = 2048
GROUP_W = 512
GLA_HEADS = 4
GLA_DK = 64
GLA_DV = 128
GLA_RANK = 16
GLA_TAU = 16.0
GLA_CHUNK = 64
SWA_HEADS = 8
SWA_KV_HEADS = 2
SWA_HD = 64
SWA_GROUP = SWA_HEADS // SWA_KV_HEADS
WINDOW = 128
N_MEM = 256
MEM_HEADS = 4
MEM_HD = 128
CONV_W = 3
EPS = 1e-6

LANES = 128
MXU_CHUNK = 256

D_IN = 5904
OFF_AB, OFF_AC, OFF_AH, OFF_AZ = 0, 512, 1024, 1536
OFF_GQ, OFF_GK, OFF_GV, OFF_GA, OFF_GZ = 2048, 2304, 2560, 3072, 3088
OFF_SQ, OFF_SK, OFF_SV, OFF_SZ = 3600, 4112, 4240, 4368
OFF_MQ, OFF_MZ = 4880, 5392

VMEM_LIMIT = 60 * 1024 * 1024


def _dot(a, b):
    return jnp.dot(a, b, preferred_element_type=f32)


def _dot_nt(a, b):
    return lax.dot_general(a, b, (((1,), (1,)), ((), ())), preferred_element_type=f32)


def _dot_tn(a, b):
    return lax.dot_general(a, b, (((0,), (0,)), ((), ())), preferred_element_type=f32)


def _split3(x):
    hi = x.astype(bf16)
    r = x - hi.astype(f32)
    mid = r.astype(bf16)
    lo = (r - mid.astype(f32)).astype(bf16)
    return hi, mid, lo


LOG2_E = 1.4426950408889634


def _attend(scores, valid, scale, values, sink=None):
    if valid is not None:
        scores = jnp.where(valid, scores, -jnp.inf)
    m = jnp.max(scores, axis=-1, keepdims=True)
    if sink is not None:
        sink = sink * (1.0 / scale)
        m = jnp.maximum(m, sink)
    e = jnp.exp2((scores - m) * (scale * LOG2_E))
    denom = jnp.sum(e, axis=-1, keepdims=True)
    if sink is not None:
        denom = denom + jnp.exp2((sink - m) * (scale * LOG2_E))
    return _dot(e.astype(bf16), values) / denom


def _silu(x):
    return x * jax.nn.sigmoid(x)


def _log_sigmoid(x):
    return jnp.minimum(x, 0.0) - jnp.log1p(jnp.exp(-jnp.abs(x)))


def _norm_matmul_kernel(x_ref, g_ref, wt_ref, o_ref, wb_ref, hn_ref, *, layer):
    x = x_ref[...]
    y = x * lax.rsqrt(jnp.mean(x * x, axis=-1, keepdims=True) + EPS)
    hn_ref[...] = (y * g_ref[layer:layer + 1, :]).astype(bf16)
    wb_ref[...] = wt_ref[...].astype(bf16)
    o_ref[...] = _dot_nt(hn_ref[...], wb_ref[...])


def _norm_matmul(x, g, wt, l, tn):
    m, k = x.shape
    n = wt.shape[1]
    return pl.pallas_call(
        functools.partial(_norm_matmul_kernel, layer=l),
        grid=(pl.cdiv(n, tn),),
        in_specs=[
            pl.BlockSpec((m, k), lambda j: (0, 0)),
            pl.BlockSpec(g.shape, lambda j: (0, 0)),
            pl.BlockSpec((None, tn, k), lambda j: (l, j, 0)),
        ],
        out_specs=[pl.BlockSpec((m, tn), lambda j: (0, j)), pl.BlockSpec((tn, k), lambda j: (j, 0))],
        out_shape=[jax.ShapeDtypeStruct((m, n), f32), jax.ShapeDtypeStruct((n, k), bf16)],
        scratch_shapes=[pltpu.VMEM((m, k), bf16)],
        compiler_params=pltpu.CompilerParams(
            dimension_semantics=("arbitrary",), vmem_limit_bytes=VMEM_LIMIT),
        name="norm_in_proj",
    )(x, g, wt)


def _memory_kv_kernel(x_ref, g_ref, w_ref, gk_ref, k_ref, v_ref, k4_ref, v4_ref, wb_ref):
    @pl.when(pl.program_id(1) == 0)
    def _():
        wb_ref[...] = w_ref[...].astype(bf16)

    layer = pl.program_id(0)
    g, gk = g_ref[pl.ds(layer, 1), :], gk_ref[pl.ds(layer, 1), :]
    x = x_ref[...]
    y = x * lax.rsqrt(jnp.mean(x * x, axis=-1, keepdims=True) + EPS)
    kv = _dot((y * g).astype(bf16), wb_ref[...])
    for h in range(MEM_HEADS):
        kh = kv[:, h * MEM_HD:(h + 1) * MEM_HD]
        kh = kh * lax.rsqrt(jnp.mean(kh * kh, axis=-1, keepdims=True) + EPS) * gk
        vh = kv[:, GROUP_W + h * MEM_HD:GROUP_W + (h + 1) * MEM_HD]
        k_ref[:, h * MEM_HD:(h + 1) * MEM_HD] = kh.astype(bf16)
        k4_ref[:, h, :] = kh
        v4_ref[:, h, :] = vh
    v_ref[...] = kv[:, GROUP_W:].astype(bf16)


def _memory_kv(mem, g, w, gk):
    depth, b = w.shape[0], mem.shape[0]
    flat = jax.ShapeDtypeStruct((depth, b, N_MEM, GROUP_W), bf16)
    split = jax.ShapeDtypeStruct((depth, b, N_MEM, MEM_HEADS, MEM_HD), f32)
    return pl.pallas_call(
        _memory_kv_kernel,
        grid=(depth, b),
        in_specs=[
            pl.BlockSpec((None, N_MEM, D_MODEL), lambda l, i: (i, 0, 0)),
            pl.BlockSpec(g.shape, lambda l, i: (0, 0)),
            pl.BlockSpec((None, D_MODEL, 2 * GROUP_W), lambda l, i: (l, 0, 0)),
            pl.BlockSpec(gk.shape, lambda l, i: (0, 0)),
        ],
        out_specs=[pl.BlockSpec((None, None, N_MEM, GROUP_W), lambda l, i: (l, i, 0, 0))] * 2
        + [pl.BlockSpec((None, None, N_MEM, MEM_HEADS, MEM_HD), lambda l, i: (l, i, 0, 0, 0))] * 2,
        out_shape=[flat, flat, split, split],
        scratch_shapes=[pltpu.VMEM((D_MODEL, 2 * GROUP_W), bf16)],
        compiler_params=pltpu.CompilerParams(
            dimension_semantics=("arbitrary", "arbitrary"), vmem_limit_bytes=VMEM_LIMIT),
        name="memory_kv",
    )(mem, g, w, gk)


CONV_PAD = 8
N_STACK_SLOTS = 2
N_SEQ_IN_PROMPT, N_SEQ_IN_DECODE, N_PARAMS_PROMPT, N_PARAMS_DECODE, N_OUT = 3, 7, 11, 9, 5


def _mixer_kernel(*refs, tile, decode, layer, bb):
    n_seq = N_SEQ_IN_DECODE if decode else N_SEQ_IN_PROMPT
    n_par = N_PARAMS_DECODE if decode else N_PARAMS_PROMPT
    seq_in = refs[:n_seq]
    params = list(refs[n_seq:n_seq + n_par])
    outs = refs[n_seq + n_par:n_seq + n_par + N_OUT]
    scratch = refs[n_seq + n_par + N_OUT:]
    for k in (-7, -6, -5, -4, -1) + (() if decode else (0,)):
        params[k] = params[k].at[layer:layer + 1]

    def view(ref, s):
        if decode and ref.ndim == 2:
            return _RowWindow(ref, s * tile, tile)
        return ref.at[s]

    if decode:
        p_ref, tail_ref, qkn_ref = seq_in[0], scratch[4], scratch[6]
        bd_ref, gsq_ref, gsk_ref = params[-3], params[-5], params[-4]
        tail_ref[...] = p_ref[:, OFF_GZ:D_IN]
        sq0, sk0 = OFF_SQ - OFF_GZ, OFF_SK - OFF_GZ
        qkn_ref[:, 0:GROUP_W] = _head_norm(tail_ref[:, sq0:sq0 + GROUP_W], gsq_ref[...], bd_ref)
        qkn_ref[:, GROUP_W:GROUP_W + LANES] = _head_norm(tail_ref[:, sk0:sk0 + LANES], gsk_ref[...],
                                                         bd_ref)

        def seg_rows(off, width):
            if off < OFF_GZ:
                return p_ref[:, off:off + width]
            return tail_ref[:, off - OFF_GZ:off - OFF_GZ + width]

        gla_in_ref, gla_out_ref = seq_in[4], outs[2]
        for _ in _gla_group(seg_rows, bb * tile, tile, params[-8:-5], outs[0],
                            lambda c: _gla_block_diag(gla_in_ref.at[c]),
                            lambda c, state: _store_gla_state(gla_out_ref.at[c], state), carry=False):
            pass
        kc_ref, vc_ref = scratch[2], scratch[3]
        for s in range(bb):
            kc_ref[s] = seq_in[5][s].T
            vc_ref[s] = seq_in[6][s].T
        _swa_decode_rows(seg_rows, qkn_ref, kc_ref, vc_ref, params[-2], layer, outs[0], bb, tile)

    stages = [_mixer_seq([view(r, s) for r in seq_in], params, [view(r, s) for r in outs],
                         [view(r, s) for r in scratch], tile=tile, decode=decode, layer=layer)
              for s in range(bb)]
    for _ in itertools.zip_longest(*stages):
        pass


class _RowWindow:
    def __init__(self, ref, start, size):
        self.ref, self.start, self.size, self.dtype = ref, start, size, ref.dtype

    def _index(self, idx):
        rows, cols = (slice(None), slice(None)) if idx is Ellipsis else idx
        lo, hi, _ = rows.indices(self.size)
        return slice(self.start + lo, self.start + hi), cols

    def __getitem__(self, idx):
        return self.ref[self._index(idx)]

    def __setitem__(self, idx, value):
        self.ref[self._index(idx)] = value


GLA_INTRA_ROWS = MXU_CHUNK
GLA_K_W = GLA_HEADS * GLA_DK
GLA_V_W = GLA_HEADS * GLA_DV


def _gla_block_diag(state_ref):
    rows = []
    for h in range(GLA_HEADS):
        blocks = [state_ref[h] if j == h else jnp.zeros((GLA_DK, GLA_DV), f32)
                  for j in range(GLA_HEADS)]
        rows.append(jnp.concatenate(blocks, axis=1))
    return jnp.concatenate(rows, axis=0)


def _store_gla_state(state_ref, state):
    for h in range(GLA_HEADS):
        state_ref[h] = state[h * GLA_DK:(h + 1) * GLA_DK, h * GLA_DV:(h + 1) * GLA_DV]


def _gla_group(seg, T, C, params, mix_ref, state_in, state_out, carry, seg_t=None):
    wup_ref, bga_ref, ggo_ref = params
    n_chunk = T // C
    G = min(T, GLA_INTRA_ROWS)
    groups = [slice(i * G, (i + 1) * G) for i in range(T // G)]
    row = lax.broadcasted_iota(jnp.int32, (G, G), 0)
    col = lax.broadcasted_iota(jnp.int32, (G, G), 1)
    causal = (row // C == col // C) & (row >= col)
    if seg_t is None:
        a_up = _dot(seg(OFF_GA, GLA_RANK).astype(bf16), wup_ref[...].astype(bf16))
    else:
        a_up = _dot_tn(seg_t(OFF_GA, GLA_RANK).astype(bf16), wup_ref[...].astype(bf16))
    log_a = _log_sigmoid(a_up + bga_ref[...]) * (1.0 / GLA_TAU)
    la3 = _split3(log_a)
    yield
    tril = jnp.where(causal, 1.0, 0.0).astype(bf16)
    in_chunk = jnp.where(lax.broadcasted_iota(jnp.int32, (T, LANES), 0) // C
                         == lax.broadcasted_iota(jnp.int32, (T, LANES), 1), 1.0, 0.0).astype(bf16)
    cum = jnp.concatenate(
        [_dot(tril, la3[0][r]) + _dot(tril, la3[1][r]) + _dot(tril, la3[2][r]) for r in groups],
        axis=0)
    tot_t = (_dot_tn(la3[0], in_chunk) + _dot_tn(la3[1], in_chunk)
             + _dot_tn(la3[2], in_chunk))
    decay_t = jnp.exp(tot_t)
    yield
    g_k = seg(OFF_GK, GLA_K_W)
    qd = ((seg(OFF_GQ, GLA_K_W) * (GLA_DK ** -0.5)) * jnp.exp(cum)).astype(bf16)
    kd = (g_k * jnp.exp(-cum)).astype(bf16)
    k_tail = jnp.concatenate(
        [g_k[c * C:(c + 1) * C] * jnp.exp(cum[(c + 1) * C - 1:(c + 1) * C] - cum[c * C:(c + 1) * C])
         for c in range(n_chunk)], axis=0) if n_chunk > 1 else g_k * jnp.exp(cum[T - 1:T] - cum)
    kt = k_tail.astype(bf16)
    yield
    v_b = seg(OFF_GV, GLA_V_W).astype(bf16)
    g_z = seg(OFF_GZ, GROUP_W)
    yield

    o_intra = []
    for r in groups:
        o_heads = []
        for h in range(GLA_HEADS):
            ks = slice(h * GLA_DK, (h + 1) * GLA_DK)
            attn = jnp.where(causal, _dot_nt(qd[r, ks], kd[r, ks]), 0.0).astype(bf16)
            o_heads.append(_dot(attn, v_b[r, h * GLA_DV:(h + 1) * GLA_DV]))
        o_intra.append(jnp.concatenate(o_heads, axis=1))
        yield
    o_intra = jnp.concatenate(o_intra, axis=0)

    shape = (GLA_K_W, GLA_V_W)
    on_diag = (lax.broadcasted_iota(jnp.int32, shape, 0) // GLA_DK
               == lax.broadcasted_iota(jnp.int32, shape, 1) // GLA_DV)
    o_chunks = []
    state = None
    for c in range(n_chunk):
        rs = slice(c * C, (c + 1) * C)
        if c == 0 or not carry:
            state = state_in(c)
        o_chunks.append(o_intra[rs] + _dot(qd[rs], state.astype(bf16)))
        update = jnp.where(on_diag, _dot_tn(kt[rs], v_b[rs]), 0.0)
        state = decay_t[:, c:c + 1] * state + update
        state_out(c, state)
        yield
    o = jnp.concatenate(o_chunks, axis=0) if n_chunk > 1 else o_chunks[0]
    for h in range(GLA_HEADS):
        vs = slice(h * GLA_DV, (h + 1) * GLA_DV)
        o_h = o[:, vs]
        o_h = o_h * lax.rsqrt(jnp.mean(o_h * o_h, axis=-1, keepdims=True) + EPS) * ggo_ref[...]
        mix_ref[:, GROUP_W + h * GLA_DV:GROUP_W + (h + 1) * GLA_DV] = (
            o_h * _silu(g_z[:, vs])).astype(mix_ref.dtype)
        yield


def _head_norm(x, g, bd_ref):
    rows, n_lanes = x.shape
    w = min(n_lanes, MXU_CHUNK)
    pieces = n_lanes // w
    bd = bd_ref[0:w, 0:w]

    def head_sums(v):
        stacked = jnp.concatenate([v[:, i * w:(i + 1) * w] for i in range(pieces)], axis=0)
        r = _dot(stacked, bd)
        return jnp.concatenate([r[i * rows:(i + 1) * rows] for i in range(pieces)], axis=1)

    sq = x * x
    hi = sq.astype(bf16)
    lo = (sq - hi.astype(f32)).astype(bf16)
    ms = (head_sums(hi) + head_sums(lo)) * (1.0 / SWA_HD)
    return x * lax.rsqrt(ms + EPS) * g


def _mixer_seq(seq_in, params, outs, scratch, *, tile, decode, layer):
    if decode:
        p_ref, mk_ref, mv_ref, conv_in_ref, gla_in_ref, kc_ref, vc_ref = seq_in
    else:
        x_ref, mk_ref, mv_ref = seq_in
        gn_ref, wt_ref = params[:2]
    (convw_ref, wup_ref, bga_ref, ggo_ref, gsq_ref, gsk_ref, bd_ref, sinks_ref,
     gmq_ref) = params[-N_PARAMS_DECODE:]
    mix_ref, conv_out_ref, gla_out_ref, kbuf_ref, vbuf_ref = outs
    ext_ref, s_ref, kprev_ref, vprev_ref, tail_ref, stk_ref, qkn_ref, knew_ref, vnew_ref = scratch

    T = tile
    t = pl.program_id(1)

    def init_state():
        ext_ref[0:CONV_PAD, :] = jnp.zeros((CONV_PAD, GROUP_W), f32)
        if decode:
            ext_ref[CONV_PAD - (CONV_W - 1):CONV_PAD, :] = conv_in_ref[...]
        else:
            s_ref[...] = jnp.zeros_like(s_ref)
            kprev_ref[...] = jnp.zeros_like(kprev_ref)
            vprev_ref[...] = jnp.zeros_like(vprev_ref)

    if decode:
        init_state()
    else:
        pl.when(t == 0)(init_state)
    yield

    if decode:
        def seg(off, width):
            if off < OFF_GZ:
                return p_ref[:, off:off + width]
            return tail_ref[:, off - OFF_GZ:off - OFF_GZ + width]
    else:
        x = x_ref[...]
        hn = (x * lax.rsqrt(jnp.mean(x * x, axis=-1, keepdims=True) + EPS) * gn_ref[...]).astype(bf16)

        def seg(off, width):
            return _dot_nt(hn, wt_ref[off:off + width, :])

        def seg_t(off, width):
            return _dot_nt(wt_ref[off:off + width, :], hn)

    def group_a():
        w = min(GROUP_W, MXU_CHUNK) if not decode else GROUP_W
        for o in range(0, GROUP_W, w):
            cs = slice(o, o + w)
            u = seg(OFF_AC + o, w) * seg(OFF_AH + o, w)
            ext_ref[CONV_PAD:CONV_PAD + T, cs] = u
            yield
            conv = (convw_ref[0:1, cs] * ext_ref[CONV_PAD - 2:CONV_PAD - 2 + T, cs]
                    + convw_ref[1:2, cs] * ext_ref[CONV_PAD - 1:CONV_PAD - 1 + T, cs]
                    + convw_ref[2:3, cs] * u)
            a_b = seg(OFF_AB + o, w)
            yield
            mix_ref[:, cs] = (a_b * conv * _silu(seg(OFF_AZ + o, w))).astype(mix_ref.dtype)
        conv_state = ext_ref[CONV_PAD + T - 2:CONV_PAD + T, :]
        ext_ref[CONV_PAD - 2:CONV_PAD, :] = conv_state
        conv_out_ref[...] = conv_state
        yield

    def group_b():
        def keep_state(c, state):
            if c == T // GLA_CHUNK - 1:
                s_ref[...] = state
                _store_gla_state(gla_out_ref, state)

        yield from _gla_group(seg, T, GLA_CHUNK, (wup_ref, bga_ref, ggo_ref), mix_ref,
                              lambda c: s_ref[...], keep_state, carry=True, seg_t=seg_t)

    def group_d():
        m_q = seg(OFF_MQ, GROUP_W)
        m_z = seg(OFF_MZ, GROUP_W)
        yield
        yield from _memory_attention(m_q, m_z, T, decode, gmq_ref, mk_ref, mv_ref, mix_ref, stk_ref)

    if decode:
        yield from group_a()
        knew_ref[0:WINDOW - T, :] = kprev_ref[T:WINDOW, :]
        knew_ref[WINDOW - T:WINDOW, :] = qkn_ref[:, GROUP_W:GROUP_W + LANES]
        vnew_ref[0:WINDOW - T, :] = vprev_ref[T:WINDOW, :]
        vnew_ref[WINDOW - T:WINDOW, :] = seg(OFF_SV, LANES)
        yield
        kbuf_ref[...] = knew_ref[...].T
        vbuf_ref[...] = vnew_ref[...].T
        yield
        yield from group_d()
    else:
        group_c = _swa_prompt_tile(seg, T, t, layer, (gsq_ref, gsk_ref, bd_ref, sinks_ref), mix_ref,
                                   kprev_ref, vprev_ref, kbuf_ref, vbuf_ref)
        for _ in itertools.zip_longest(group_b(), itertools.chain(group_c, group_a(), group_d())):
            pass
        yield


def _swa_prompt_tile(seg, T, t, layer, params, mix_ref, kprev_ref, vprev_ref, kbuf_ref, vbuf_ref):
    gsq_ref, gsk_ref, bd_ref, sinks_ref = params
    s_z = seg(OFF_SZ, GROUP_W)
    s_q = seg(OFF_SQ, GROUP_W)
    s_kv = seg(OFF_SK, 2 * LANES)
    q_n = _head_norm(s_q, gsq_ref[...], bd_ref)
    k_n = _head_norm(s_kv[:, 0:LANES], gsk_ref[...], bd_ref)
    v_n = s_kv[:, LANES:2 * LANES]
    yield

    BQ = WINDOW
    n_blk = T // BQ
    stack = SWA_GROUP
    nk = WINDOW + BQ
    qi = lax.broadcasted_iota(jnp.int32, (stack * BQ, nk), 0) % BQ
    kj = lax.broadcasted_iota(jnp.int32, (stack * BQ, nk), 1)
    dist = qi + WINDOW - kj
    band = (dist >= 0) & (dist < WINDOW)
    srow = lax.broadcasted_iota(jnp.int32, (stack * BQ, 1), 0) // BQ
    for blk in range(n_blk):
        rs = slice(blk * BQ, (blk + 1) * BQ)
        if blk == 0:
            k_prev, v_prev = kprev_ref[...], vprev_ref[...]
            valid = band & ((kj >= WINDOW) | (t > 0))
        else:
            ps = slice((blk - 1) * BQ, blk * BQ)
            k_prev, v_prev = k_n[ps], v_n[ps]
            valid = band
        k_cat = jnp.concatenate([k_prev, k_n[rs]], axis=0)
        v_cat = jnp.concatenate([v_prev, v_n[rs]], axis=0)
        for g in range(SWA_KV_HEADS):
            kg = k_cat[:, g * SWA_HD:(g + 1) * SWA_HD].astype(bf16)
            vg = v_cat[:, g * SWA_HD:(g + 1) * SWA_HD].astype(bf16)
            heads = [g * SWA_GROUP + j for j in range(stack)]
            qg = jnp.concatenate([q_n[rs, hd * SWA_HD:(hd + 1) * SWA_HD] for hd in heads],
                                 axis=0).astype(bf16)
            sink = jnp.full((stack * BQ, 1), sinks_ref[layer, heads[0]], f32)
            for j in range(1, stack):
                sink = jnp.where(srow == j, sinks_ref[layer, heads[j]], sink)
            o = _attend(_dot_nt(qg, kg), valid, SWA_HD ** -0.5, vg, sink)
            for j, hd in enumerate(heads):
                z = s_z[rs, hd * SWA_HD:(hd + 1) * SWA_HD]
                mix_ref[rs, 2 * GROUP_W + hd * SWA_HD:2 * GROUP_W + (hd + 1) * SWA_HD] = (
                    o[j * BQ:(j + 1) * BQ] * _silu(z)).astype(mix_ref.dtype)
            yield

    kprev_ref[...] = k_n[T - WINDOW:T]
    vprev_ref[...] = v_n[T - WINDOW:T]
    kbuf_ref[...] = k_n[T - WINDOW:T].T
    vbuf_ref[...] = v_n[T - WINDOW:T].T
    yield


def _swa_decode_rows(seg, qkn_ref, kc_ref, vc_ref, sinks_ref, layer, mix_ref, bb, T):
    R = bb * T
    n_cache = bb * WINDOW
    nk = n_cache + R
    q_n, k_n = qkn_ref[:, 0:GROUP_W], qkn_ref[:, GROUP_W:GROUP_W + LANES]
    s_z = seg(OFF_SZ, GROUP_W)
    k_all = jnp.concatenate([kc_ref[s] for s in range(bb)] + [k_n], axis=0)
    v_all = jnp.concatenate([vc_ref[s] for s in range(bb)] + [seg(OFF_SV, LANES)], axis=0)
    rows = SWA_GROUP * R
    r = lax.broadcasted_iota(jnp.int32, (rows, nk), 0) % R
    c = lax.broadcasted_iota(jnp.int32, (rows, nk), 1)
    cached = c < n_cache
    key_seq = jnp.where(cached, c // WINDOW, (c - n_cache) // T)
    key_pos = jnp.where(cached, c % WINDOW, WINDOW + (c - n_cache) % T)
    dist = r % T + WINDOW - key_pos
    valid = (r // T == key_seq) & (dist >= 0) & (dist < WINDOW)
    srow = lax.broadcasted_iota(jnp.int32, (rows, 1), 0) // R
    for g in range(SWA_KV_HEADS):
        heads = [g * SWA_GROUP + j for j in range(SWA_GROUP)]
        qg = jnp.concatenate([q_n[:, hd * SWA_HD:(hd + 1) * SWA_HD] for hd in heads],
                             axis=0).astype(bf16)
        kg = k_all[:, g * SWA_HD:(g + 1) * SWA_HD].astype(bf16)
        vg = v_all[:, g * SWA_HD:(g + 1) * SWA_HD].astype(bf16)
        sink = jnp.full((rows, 1), sinks_ref[layer, heads[0]], f32)
        for j in range(1, SWA_GROUP):
            sink = jnp.where(srow == j, sinks_ref[layer, heads[j]], sink)
        o = _attend(_dot_nt(qg, kg), valid, SWA_HD ** -0.5, vg, sink)
        for j, hd in enumerate(heads):
            mix_ref[:, 2 * GROUP_W + hd * SWA_HD:2 * GROUP_W + (hd + 1) * SWA_HD] = (
                o[j * R:(j + 1) * R] * _silu(s_z[:, hd * SWA_HD:(hd + 1) * SWA_HD])).astype(mix_ref.dtype)


def _memory_attention(m_q, m_z, T, decode, gmq_ref, mk_ref, mv_ref, mix_ref, stk_ref):
    def stack_rows(pieces, slot):
        r, w = pieces[0].shape
        if r % 8 == 0:
            return jnp.concatenate(pieces, axis=0)
        for j, piece in enumerate(pieces):
            stk_ref[slot, j * r:(j + 1) * r, 0:w] = piece
        return stk_ref[slot, 0:len(pieces) * r, 0:w]

    def unstack_rows(x, n, slot):
        r, w = x.shape[0] // n, x.shape[1]
        if r % 8 == 0:
            return [x[j * r:(j + 1) * r] for j in range(n)]
        stk_ref[slot, 0:n * r, 0:w] = x
        return [stk_ref[slot, j * r:(j + 1) * r, 0:w] for j in range(n)]

    def mem_q(h):
        qh = m_q[:, h * MEM_HD:(h + 1) * MEM_HD]
        return qh * lax.rsqrt(jnp.mean(qh * qh, axis=-1, keepdims=True) + EPS) * gmq_ref[...]

    if decode:
        qs = stack_rows([mem_q(h) for h in range(MEM_HEADS)], 0).astype(bf16)
        s = _dot_nt(qs, mk_ref[...].astype(bf16))
        shape = (MEM_HEADS * T, MEM_HEADS * N_MEM)
        same_head = (lax.broadcasted_iota(jnp.int32, shape, 0) // T
                     == lax.broadcasted_iota(jnp.int32, shape, 1) % MEM_HEADS)
        yield
        o = _attend(s, same_head, MEM_HD ** -0.5, mv_ref[...].astype(bf16))
        o_all = unstack_rows(o, MEM_HEADS, 1)
    else:
        o_all = []
        for h in range(MEM_HEADS):
            hs = slice(h * MEM_HD, (h + 1) * MEM_HD)
            s = _dot_nt(mem_q(h).astype(bf16), mk_ref[:, hs].astype(bf16))
            o_all.append(_attend(s, None, MEM_HD ** -0.5, mv_ref[:, hs].astype(bf16)))
            yield
    for h in range(MEM_HEADS):
        mix_ref[:, 3 * GROUP_W + h * MEM_HD:3 * GROUP_W + (h + 1) * MEM_HD] = (
            o_all[h] * _silu(m_z[:, h * MEM_HD:(h + 1) * MEM_HD])).astype(mix_ref.dtype)
    yield


def _mixer(tokens, norm_w, mem_k, mem_v, mem_layer, state, params, layer, tile, bb, decode):
    if decode:
        b, width = state[0].shape[1], tokens.shape[1]
        L = tokens.shape[0] // b
    else:
        b, L, width = tokens.shape
    nt = L // tile
    assert nt == 1 or not decode, "a decode call covers each sequence with a single tile"
    conv_w, w_up, b_ga, g_go, g_sq, g_sk, bd, sinks, g_mq = params

    def tok(width):
        if decode:
            return pl.BlockSpec((bb * tile, width), lambda i, t: (i, 0))
        return pl.BlockSpec((bb, tile, width), lambda i, t: (i, t, 0))

    def per_seq(*shape):
        return pl.BlockSpec((bb,) + shape, lambda i, t: (i,) + (0,) * len(shape))

    def per_seq_at(lyr, *shape):
        return pl.BlockSpec((None, bb) + shape, lambda i, t: (lyr, i) + (0,) * len(shape))

    def param(a):
        if a.ndim == 2:
            return pl.BlockSpec(a.shape, lambda i, t: (0, 0))
        return pl.BlockSpec((None,) + a.shape[1:], lambda i, t: (layer,) + (0,) * (a.ndim - 1))

    kv_w = SWA_KV_HEADS * SWA_HD
    state_shapes = [(CONV_W - 1, GROUP_W), (GLA_HEADS, GLA_DK, GLA_DV), (WINDOW, kv_w),
                    (WINDOW, kv_w)]
    in_specs = [tok(width), per_seq_at(mem_layer, *mem_k.shape[2:]),
                per_seq_at(mem_layer, *mem_v.shape[2:])]
    args = [tokens, mem_k, mem_v]
    if decode:
        in_specs += [per_seq_at(layer, *s) for s in state_shapes]
        args += list(state)
    else:
        g_n, w_t = norm_w
        in_specs += [param(g_n), pl.BlockSpec(w_t.shape, lambda i, t: (0, 0),
                                              pipeline_mode=pl.Buffered(1))]
        args += [g_n, w_t]
    in_specs += [param(conv_w), param(w_up), param(b_ga), param(g_go), param(g_sq), param(g_sk),
                 pl.BlockSpec(bd.shape, lambda i, t: (0, 0)),
                 pl.BlockSpec(memory_space=pltpu.SMEM), param(g_mq)]
    args += [conv_w, w_up, b_ga, g_go, g_sq, g_sk, bd, sinks, g_mq]
    conv_window = pltpu.VMEM((bb, CONV_PAD + tile, GROUP_W), f32)
    window = pltpu.VMEM((bb, WINDOW, kv_w), f32)
    unused = pltpu.VMEM((bb, 8, LANES), f32)
    if decode:
        out_shape = [jax.ShapeDtypeStruct((b * L, 4 * GROUP_W), f32)]
        scratch_shapes = [conv_window, unused, window, window,
                          pltpu.VMEM((bb * tile, D_IN - OFF_GZ), f32),
                          pltpu.VMEM((bb, N_STACK_SLOTS, MEM_HEADS * tile, LANES), f32),
                          pltpu.VMEM((bb * tile, GROUP_W + LANES), f32), window, window]
    else:
        out_shape = [jax.ShapeDtypeStruct((b, L, 4 * GROUP_W), bf16)]
        scratch_shapes = [conv_window, pltpu.VMEM((bb, GLA_K_W, GLA_V_W), f32), window, window,
                          unused, unused, unused, unused, unused]
    out_shape += [jax.ShapeDtypeStruct((b,) + s, f32) for s in state_shapes]
    out_specs = [tok(4 * GROUP_W)] + [per_seq(*s) for s in state_shapes]
    return pl.pallas_call(
        functools.partial(_mixer_kernel, tile=tile, decode=decode, layer=layer, bb=bb),
        grid=(b // bb, nt),
        in_specs=in_specs,
        out_specs=out_specs,
        out_shape=out_shape,
        scratch_shapes=scratch_shapes,
        compiler_params=pltpu.CompilerParams(
            dimension_semantics=("arbitrary", "arbitrary"), vmem_limit_bytes=VMEM_LIMIT),
        name="mixer_decode" if decode else "mixer_prompt",
    )(*args)


def _out_proj_kernel(mix_ref, w_ref, x_ref, y_ref, wb_ref):
    @pl.when(pl.program_id(1) == 0)
    def _():
        wb_ref[...] = w_ref[...].astype(bf16)

    y_ref[...] = x_ref[...] + _dot(mix_ref[...].astype(bf16), wb_ref[...])


def _out_proj(mix, w, x, l, tm, tn):
    m, k = mix.shape
    n = w.shape[2]
    return pl.pallas_call(
        _out_proj_kernel,
        grid=(n // tn, m // tm),
        in_specs=[
            pl.BlockSpec((tm, k), lambda j, i: (i, 0)),
            pl.BlockSpec((None, k, tn), lambda j, i: (l, 0, j),
                         pipeline_mode=pl.Buffered(1) if tn == n else None),
            pl.BlockSpec((tm, tn), lambda j, i: (i, j)),
        ],
        out_specs=pl.BlockSpec((tm, tn), lambda j, i: (i, j)),
        out_shape=jax.ShapeDtypeStruct((m, n), f32),
        scratch_shapes=[pltpu.VMEM((k, tn), bf16)],
        compiler_params=pltpu.CompilerParams(
            dimension_semantics=("arbitrary", "arbitrary"), vmem_limit_bytes=VMEM_LIMIT),
        name="out_proj",
    )(mix, w, x)


PROMPT_TILE = 512
DECODE_SEQS_PER_STEP = 8
PROJ_TN = 1536
OUT_TM, OUT_TN = 512, 2048

_LANE = np.arange(GROUP_W)
HEAD_BLOCK_DIAG = _LANE[:, None] // SWA_HD == _LANE[None, :] // SWA_HD


def kernel(x_prompt, x_sample, mem_prompt, state_conv, state_gla, cache_swa_k, cache_swa_v,
           cache_mem_k, cache_mem_v, g_norm, w_in, conv_w, w_gla_a_up, b_gla_a, g_gla_o,
           g_swa_q, g_swa_k, swa_sinks, g_mem, w_mem_kv, g_mem_q, g_mem_k, w_out):
    depth = w_in.shape[0]
    bp, lp, _ = x_prompt.shape
    bs, ls, _ = x_sample.shape
    hp = x_prompt.reshape(bp * lp, D_MODEL)
    hs = x_sample.reshape(bs * ls, D_MODEL)

    params = (conv_w, w_gla_a_up, b_gla_a, g_gla_o,
              jnp.tile(g_swa_q, (1, SWA_HEADS)), jnp.tile(g_swa_k, (1, SWA_KV_HEADS)),
              jnp.asarray(HEAD_BLOCK_DIAG, bf16), swa_sinks, g_mem_q)
    g_n, g_m, g_mk = g_norm, g_mem, g_mem_k
    w_in_t = jnp.swapaxes(w_in, 1, 2)
    kv_w = SWA_KV_HEADS * SWA_HD
    def window_minor(a):
        return jnp.transpose(a, (0, 1, 3, 4, 2)).reshape(depth, bs, kv_w, WINDOW)

    def window_major(a):
        return jnp.transpose(a.reshape(a.shape[0], SWA_KV_HEADS, SWA_HD, WINDOW), (0, 3, 1, 2))

    state = (state_conv, state_gla, window_minor(cache_swa_k), window_minor(cache_swa_v))
    mem_k_s = cache_mem_k.reshape(depth, bs, N_MEM * MEM_HEADS, MEM_HD)
    mem_v_s = cache_mem_v.reshape(depth, bs, N_MEM * MEM_HEADS, MEM_HD)

    mk, mv, mem_k_p, mem_v_p = _memory_kv(mem_prompt, g_m, w_mem_kv, g_mk)
    outs = [[] for _ in range(8)]
    for l in range(depth):
        proj, w_bf = _norm_matmul(hs, g_n, w_in_t, l, PROJ_TN)

        mix, c, s, kb, vb = _mixer(hp.reshape(bp, lp, D_MODEL), (g_n, w_bf), mk, mv, l,
                                   None, params, l, PROMPT_TILE, 1, decode=False)
        hp = _out_proj(mix.reshape(bp * lp, 4 * GROUP_W), w_out, hp, l, OUT_TM, OUT_TN)
        for lst, a in zip(outs[:4], (
                c, s, window_major(kb), window_major(vb))):
            lst.append(a)

        mix, c, s, kb, vb = _mixer(proj, None, mem_k_s, mem_v_s, l, state, params, l, ls,
                                   DECODE_SEQS_PER_STEP, decode=True)
        hs = _out_proj(mix, w_out, hs, l, bs * ls, OUT_TN)
        for lst, a in zip(outs[4:], (
                c, s, window_major(kb), window_major(vb))):
            lst.append(a)

    stacked = [jnp.stack(o) for o in outs]
    return (hp.reshape(bp, lp, D_MODEL), hs.reshape(bs, ls, D_MODEL),
            *stacked[:4], mem_k_p, mem_v_p, *stacked[4:])
```
